```python
import jax
import jax.numpy as jnp
from jax import lax
import numpy as np

D_MODEL = 1024
BATCH = 16
SEQ = 2048
DEPTH = 1

EPS = 1e-6
HEAD_DIM = 64
ROPE_THETA = 10000.0

NSA_HEADS = 8
NSA_KV_GROUPS = 2
NSA_GROUP = NSA_HEADS // NSA_KV_GROUPS
NSA_WIDTH = NSA_HEADS * HEAD_DIM
KV_WIDTH = NSA_KV_GROUPS * HEAD_DIM
CMP_BLOCK = 32
CMP_STRIDE = 16
CMP_HIDDEN = 128
SEL_BLOCK = 64
SEL_TOPK = 16
WINDOW = 512
Q_CHUNK = 32
FORCE_BONUS = 1000.0

HGRN_HEADS = 4
HGRN_EXPAND = 128
HGRN_HEAD_V = 128
HGRN_WIDTH = HGRN_HEADS * HGRN_EXPAND
HGRN_V_WIDTH = HGRN_HEADS * HGRN_HEAD_V
HGRN_CHUNK = 32

PEER_HEADS = 8
PEER_NKEYS = 128
PEER_NEXPERTS = PEER_NKEYS * PEER_NKEYS
PEER_QDIM = 256
PEER_TOPK = 16
PEER_TOK_CHUNK = 128

IN_SPLITS = (NSA_WIDTH, KV_WIDTH, KV_WIDTH, KV_WIDTH, KV_WIDTH, KV_WIDTH, KV_WIDTH, 3 * NSA_HEADS, HGRN_WIDTH, HGRN_WIDTH, HGRN_V_WIDTH, HGRN_V_WIDTH, 2 * D_MODEL)
IN_COLS = NSA_WIDTH + 6 * KV_WIDTH + 3 * NSA_HEADS + 2 * HGRN_WIDTH + 2 * HGRN_V_WIDTH + 2 * D_MODEL

kernel_name = 'nsa_hgrn2_peer_hybrid_block'


def _rmsnorm(x, g):
    xf = x.astype(jnp.float32)
    y = xf * lax.rsqrt(jnp.mean(xf * xf, axis=-1, keepdims=True) + EPS)
    return (y * g.astype(jnp.float32)).astype(x.dtype)


def _masked_softmax(s, mask):
    s = jnp.where(mask, s.astype(jnp.float32), -jnp.inf)
    m = jnp.max(s, axis=-1, keepdims=True)
    m = jnp.where(jnp.isfinite(m), m, 0.0)
    p = jnp.exp(s - m)
    return p / jnp.maximum(jnp.sum(p, axis=-1, keepdims=True), 1e-30)


def _rope(x, pos):
    half = HEAD_DIM // 2
    inv = ROPE_THETA ** (-jnp.arange(half, dtype=jnp.float32) / half)
    ang = pos.astype(jnp.float32)[:, None] * inv[None, :]
    cos = jnp.cos(ang)[None, :, None, :]
    sin = jnp.sin(ang)[None, :, None, :]
    xf = x.astype(jnp.float32)
    x1, x2 = xf[..., :half], xf[..., half:]
    return jnp.concatenate([x1 * cos - x2 * sin, x2 * cos + x1 * sin], axis=-1).astype(x.dtype)


def _selection_map(n_cmp, n_sel):
    r_sel = SEL_BLOCK // CMP_STRIDE
    r_cmp = CMP_BLOCK // CMP_STRIDE
    i = np.arange(n_cmp)[:, None]
    j = np.arange(n_sel)[None, :]
    d = i - r_sel * j
    cnt = np.minimum(d, r_sel - 1) - np.maximum(d - r_cmp + 1, 0) + 1
    return np.clip(cnt, 0, None).astype(np.float32)


def _compress(k, pos_emb, w1, w2):
    B, T = k.shape[0], k.shape[1]
    r = CMP_BLOCK // CMP_STRIDE
    pieces = k.reshape(B, T // CMP_STRIDE, CMP_STRIDE, NSA_KV_GROUPS, HEAD_DIM)
    n_cmp = T // CMP_STRIDE - r + 1
    blocks = jnp.concatenate([pieces[:, j:j + n_cmp] for j in range(r)], axis=2)
    blocks = blocks + pos_emb[None, None, :, None, :]
    flat = blocks.transpose(0, 1, 3, 2, 4).reshape(B, n_cmp, NSA_KV_GROUPS, CMP_BLOCK * HEAD_DIM)
    return jax.nn.gelu(flat @ w1) @ w2


def _nsa(q, k_c, v_c, k_s, v_s, k_w, v_w, gate_logits, cmp_pos_k, cmp_pos_v, w_ck1, w_ck2, w_cv1, w_cv2):
    B, T = q.shape[0], q.shape[1]
    G, R, dh = NSA_KV_GROUPS, NSA_GROUP, HEAD_DIM
    scale = HEAD_DIM ** -0.5
    kc = _compress(k_c, cmp_pos_k, w_ck1, w_ck2)
    vc = _compress(v_c, cmp_pos_v, w_cv1, w_cv2)
    n_cmp = kc.shape[1]
    cmp_end = jnp.arange(n_cmp) * CMP_STRIDE + CMP_BLOCK - 1
    n_sel = T // SEL_BLOCK
    top_n = min(SEL_TOPK, n_sel)
    sel_map = jnp.asarray(_selection_map(n_cmp, n_sel))
    ks_blocks = k_s.reshape(B, n_sel, SEL_BLOCK, G, dh).transpose(0, 3, 1, 2, 4)
    vs_blocks = v_s.reshape(B, n_sel, SEL_BLOCK, G, dh).transpose(0, 3, 1, 2, 4)
    kw_pad = jnp.pad(k_w, ((0, 0), (WINDOW, 0), (0, 0), (0, 0)))
    vw_pad = jnp.pad(v_w, ((0, 0), (WINDOW, 0), (0, 0), (0, 0)))
    n_chunks = T // Q_CHUNK
    q_chunks = q.reshape(B, n_chunks, Q_CHUNK, G, R, dh).transpose(1, 0, 2, 3, 4, 5)
    g_chunks = gate_logits.reshape(B, n_chunks, Q_CHUNK, G, R, 3).transpose(1, 0, 2, 3, 4, 5)
    b_ix = jnp.arange(B)[:, None, None, None]
    g_ix = jnp.arange(G)[None, None, :, None]
    blk_ids = jnp.arange(n_sel)
    in_blk = jnp.arange(SEL_BLOCK)
    win_off = jnp.arange(WINDOW + Q_CHUNK)

    def chunk_fn(args):
        ci, qg, gl = args
        start = ci * Q_CHUNK
        t = start + jnp.arange(Q_CHUNK)
        s_c = jnp.einsum('bqgrd,bngd->bqgrn', qg, kc) * scale
        m_c = (cmp_end[None, :] <= t[:, None])[None, :, None, None, :]
        p_c = _masked_softmax(s_c, m_c)
        o_c = jnp.einsum('bqgrn,bngd->bqgrd', p_c.astype(vc.dtype), vc)
        imp = jnp.einsum('bqgrn,nj->bqgj', p_c, sel_map)
        cur = t // SEL_BLOCK
        forced = (blk_ids[None, :] == 0) | (blk_ids[None, :] == cur[:, None]) | (blk_ids[None, :] == cur[:, None] - 1)
        causal_blk = blk_ids[None, :] * SEL_BLOCK <= t[:, None]
        imp = jnp.where(forced[None, :, None, :], imp + FORCE_BONUS, imp)
        imp = jnp.where(causal_blk[None, :, None, :], imp, -1.0)
        _, idx = lax.top_k(imp, top_n)
        kg = ks_blocks[b_ix, g_ix, idx]
        vg = vs_blocks[b_ix, g_ix, idx]
        s_s = jnp.einsum('bqgrd,bqgnld->bqgrnl', qg, kg) * scale
        kpos = idx[..., None] * SEL_BLOCK + in_blk
        m_s = (kpos <= t[None, :, None, None, None]).reshape(B, Q_CHUNK, G, 1, top_n * SEL_BLOCK)
        p_s = _masked_softmax(s_s.reshape(B, Q_CHUNK, G, R, top_n * SEL_BLOCK), m_s)
        p_s = p_s.reshape(B, Q_CHUNK, G, R, top_n, SEL_BLOCK).astype(vg.dtype)
        o_s = jnp.einsum('bqgrnl,bqgnld->bqgrd', p_s, vg)
        kw_c = lax.dynamic_slice_in_dim(kw_pad, start, WINDOW + Q_CHUNK, axis=1)
        vw_c = lax.dynamic_slice_in_dim(vw_pad, start, WINDOW + Q_CHUNK, axis=1)
        kpos_w = start - WINDOW + win_off
        m_w = (kpos_w[None, :] <= t[:, None]) & (t[:, None] - kpos_w[None, :] < WINDOW) & (kpos_w[None, :] >= 0)
        s_w = jnp.einsum('bqgrd,bkgd->bqgrk', qg, kw_c) * scale
        p_w = _masked_softmax(s_w, m_w[None, :, None, None, :])
        o_w = jnp.einsum('bqgrk,bkgd->bqgrd', p_w.astype(vw_c.dtype), vw_c)
        g = jax.nn.sigmoid(gl.astype(jnp.float32)).astype(qg.dtype)
        o = g[..., 0:1] * o_c + g[..., 1:2] * o_s + g[..., 2:3] * o_w
        return o.reshape(B, Q_CHUNK, NSA_WIDTH)

    out = lax.map(chunk_fn, (jnp.arange(n_chunks), q_chunks, g_chunks))
    return out.transpose(1, 0, 2, 3).reshape(B, T, NSA_WIDTH)


def _hgrn2(hq, hf, hi, hg, lb, norm_g):
    B, T = hq.shape[0], hq.shape[1]
    H, dk, dv, C = HGRN_HEADS, HGRN_EXPAND, HGRN_HEAD_V, HGRN_CHUNK
    f = lb + (1.0 - lb) * jax.nn.sigmoid(hf.astype(jnp.float32))
    log_f = jnp.log(f)
    k = 1.0 - f
    q = jax.nn.silu(hq.astype(jnp.float32))
    v = hi.astype(jnp.float32)
    nc = T // C

    def to_chunks(a, d):
        return a.reshape(B, nc, C, H, d).transpose(1, 0, 3, 2, 4)

    causal = jnp.tril(jnp.ones((C, C), dtype=bool))

    def step(S, xs):
        qc, kc, vc, lfc = xs
        b = jnp.cumsum(lfc, axis=2)
        b_end = b[:, :, -1:, :]
        q_dec = qc * jnp.exp(b)
        k_inv = kc * jnp.exp(-b)
        k_end = kc * jnp.exp(b_end - b)
        A = jnp.where(causal, jnp.einsum('bhtd,bhsd->bhts', q_dec, k_inv), 0.0)
        o = jnp.einsum('bhts,bhse->bhte', A, vc) + jnp.einsum('bhtd,bhde->bhte', q_dec, S)
        S = jnp.exp(b_end)[:, :, 0, :, None] * S + jnp.einsum('bhsd,bhse->bhde', k_end, vc)
        return S, o

    S0 = jnp.zeros((B, H, dk, dv), jnp.float32)
    _, o = lax.scan(step, S0, (to_chunks(q, dk), to_chunks(k, dk), to_chunks(v, dv), to_chunks(log_f, dk)))
    o = o.transpose(1, 0, 3, 2, 4).reshape(B, T, H, dv).astype(hq.dtype)
    o = _rmsnorm(o, norm_g) * jax.nn.silu(hg.reshape(B, T, H, dv))
    return o.reshape(B, T, HGRN_V_WIDTH)


def _hybrid_mixer(h, layer_lb, w_in, cmp_pos_k, cmp_pos_v, w_ck1, w_ck2, w_cv1, w_cv2, norm_g, w_branch, w_out):
    B, T = h.shape[0], h.shape[1]
    proj = h @ w_in
    cuts = np.cumsum(IN_SPLITS)[:-1].tolist()
    q, k_c, v_c, k_s, v_s, k_w, v_w, nsa_gl, hq, hf, hi, hg, merge = jnp.split(proj, cuts, axis=-1)
    pos = jnp.arange(T)

    def heads(a, n):
        return a.reshape(B, T, n, HEAD_DIM)

    q = _rope(heads(q, NSA_HEADS), pos)
    k_c = _rope(heads(k_c, NSA_KV_GROUPS), pos)
    k_s = _rope(heads(k_s, NSA_KV_GROUPS), pos)
    k_w = _rope(heads(k_w, NSA_KV_GROUPS), pos)
    o_nsa = _nsa(q, k_c, heads(v_c, NSA_KV_GROUPS), k_s, heads(v_s, NSA_KV_GROUPS), k_w, heads(v_w, NSA_KV_GROUPS), nsa_gl.reshape(B, T, NSA_HEADS, 3), cmp_pos_k, cmp_pos_v, w_ck1, w_ck2, w_cv1, w_cv2)
    o_hgrn = _hgrn2(hq, hf, hi, hg, layer_lb, norm_g)
    g_a, g_b = jnp.split(jax.nn.sigmoid(merge.astype(jnp.float32)).astype(h.dtype), 2, axis=-1)
    y = g_a * (o_nsa @ w_branch[0]) + g_b * (o_hgrn @ w_branch[1])
    return y @ w_out


def _peer(h, w_q, sub_keys, u_tab, v_tab):
    B, T, D = h.shape
    n = B * T
    K = PEER_TOPK
    xt = h.reshape(n, D)
    q = (xt @ w_q).reshape(n, PEER_HEADS, 2, PEER_QDIM // 2)
    s = jnp.einsum('nhpd,phkd->nhpk', q, sub_keys).astype(jnp.float32)
    s1, i1 = lax.top_k(s[:, :, 0], K)
    s2, i2 = lax.top_k(s[:, :, 1], K)
    comb = (s1[..., :, None] + s2[..., None, :]).reshape(n, PEER_HEADS, K * K)
    cand = (i1[..., :, None] * PEER_NKEYS + i2[..., None, :]).reshape(n, PEER_HEADS, K * K)
    top_s, pos = lax.top_k(comb, K)
    experts = jnp.take_along_axis(cand, pos, axis=-1)
    gate = jax.nn.softmax(top_s, axis=-1).astype(h.dtype)
    n_chunks = n // PEER_TOK_CHUNK

    def chunk_fn(args):
        xc, ec, gc = args
        a = jax.nn.gelu(jnp.einsum('nd,nhkd->nhk', xc, u_tab[ec]))
        return jnp.einsum('nhk,nhkd->nd', gc * a, v_tab[ec])

    y = lax.map(chunk_fn, (xt.reshape(n_chunks, PEER_TOK_CHUNK, D), experts.reshape(n_chunks, PEER_TOK_CHUNK, PEER_HEADS, K), gate.reshape(n_chunks, PEER_TOK_CHUNK, PEER_HEADS, K)))
    return y.reshape(B, T, D)


def setup_inputs(seed: int = 0) -> dict:
    key = jax.random.key(seed)
    ks = jax.random.split(key, 22)

    def nrm(k, shape, s):
        return jax.random.normal(k, shape, jnp.float32) * s

    D = D_MODEL
    return {
        'x': nrm(ks[0], (BATCH, SEQ, D), 1.0),
        'c': nrm(ks[1], (BATCH, D), 1.0),
        'w_ada': nrm(ks[2], (DEPTH, D, 6 * D), 0.02),
        'b_ada': nrm(ks[3], (DEPTH, 6 * D), 0.02),
        'g_mix': 1.0 + nrm(ks[4], (DEPTH, D), 0.02),
        'g_ffn': 1.0 + nrm(ks[5], (DEPTH, D), 0.02),
        'w_in': nrm(ks[6], (DEPTH, D, IN_COLS), D ** -0.5),
        'cmp_pos_k': nrm(ks[7], (DEPTH, CMP_BLOCK, HEAD_DIM), 0.02),
        'cmp_pos_v': nrm(ks[8], (DEPTH, CMP_BLOCK, HEAD_DIM), 0.02),
        'w_ck1': nrm(ks[9], (DEPTH, CMP_BLOCK * HEAD_DIM, CMP_HIDDEN), (CMP_BLOCK * HEAD_DIM) ** -0.5),
        'w_ck2': nrm(ks[10], (DEPTH, CMP_HIDDEN, HEAD_DIM), CMP_HIDDEN ** -0.5),
        'w_cv1': nrm(ks[11], (DEPTH, CMP_BLOCK * HEAD_DIM, CMP_HIDDEN), (CMP_BLOCK * HEAD_DIM) ** -0.5),
        'w_cv2': nrm(ks[12], (DEPTH, CMP_HIDDEN, HEAD_DIM), CMP_HIDDEN ** -0.5),
        'hgrn_lb_logits': nrm(ks[13], (DEPTH + 1, HGRN_WIDTH), 0.5),
        'hgrn_out_norm': 1.0 + nrm(ks[14], (DEPTH, HGRN_HEAD_V), 0.02),
        'w_branch': nrm(ks[15], (DEPTH, 2, NSA_WIDTH, D), NSA_WIDTH ** -0.5),
        'w_out': nrm(ks[16], (DEPTH, D, D), D ** -0.5),
        'w_peer_q': nrm(ks[17], (DEPTH, D, PEER_HEADS * PEER_QDIM), D ** -0.5),
        'peer_sub_keys': nrm(ks[18], (DEPTH, 2, PEER_HEADS, PEER_NKEYS, PEER_QDIM // 2), (PEER_QDIM // 2) ** -0.5),
        'peer_u': nrm(ks[19], (DEPTH, PEER_NEXPERTS, D), D ** -0.5),
        'peer_v': nrm(ks[20], (DEPTH, PEER_NEXPERTS, D), 0.5),
        'g_final': 1.0 + nrm(ks[21], (D,), 0.02),
    }


def reference(x, c, w_ada, b_ada, g_mix, g_ffn, w_in, cmp_pos_k, cmp_pos_v, w_ck1, w_ck2, w_cv1, w_cv2, hgrn_lb_logits, hgrn_out_norm, w_branch, w_out, w_peer_q, peer_sub_keys, peer_u, peer_v, g_final):
    lb_all = jnp.cumsum(jax.nn.softmax(hgrn_lb_logits.astype(jnp.float32), axis=0), axis=0)
    cs = jax.nn.silu(c)
    for l in range(DEPTH):
        mod = cs @ w_ada[l] + b_ada[l]
        sh_m, sc_m, gt_m, sh_f, sc_f, gt_f = [m[:, None, :] for m in jnp.split(mod, 6, axis=-1)]
        h = _rmsnorm(x, g_mix[l]) * (1.0 + sc_m) + sh_m
        y = _hybrid_mixer(h, lb_all[l], w_in[l], cmp_pos_k[l], cmp_pos_v[l], w_ck1[l], w_ck2[l], w_cv1[l], w_cv2[l], hgrn_out_norm[l], w_branch[l], w_out[l])
        x = x + gt_m * y
        h = _rmsnorm(x, g_ffn[l]) * (1.0 + sc_f) + sh_f
        x = x + gt_f * _peer(h, w_peer_q[l], peer_sub_keys[l], peer_u[l], peer_v[l])
    return _rmsnorm(x, g_final)
```

```python
import functools

import numpy as np
import jax
import jax.numpy as jnp
from jax import lax
from jax.experimental import pallas as pl
from jax.experimental.pallas import tpu as pltpu

f32 = jnp.float32
bf16 = jnp.bfloat16
i32 = jnp.int32
_HIGHEST = lax.Precision.HIGHEST

D_MODEL = 1024
EPS = 1e-6
HEAD_DIM = 64
ROPE_THETA = 10000.0
NSA_HEADS = 8
NSA_KV_GROUPS = 2
NSA_GROUP = NSA_HEADS // NSA_KV_GROUPS
NSA_WIDTH = NSA_HEADS * HEAD_DIM
KV_WIDTH = NSA_KV_GROUPS * HEAD_DIM
CMP_BLOCK = 32
CMP_STRIDE = 16
CMP_HIDDEN = 128
SEL_BLOCK = 64
SEL_TOPK = 16
WINDOW = 512
FORCE_BONUS = 1000.0
HGRN_HEADS = 4
HGRN_EXPAND = 128
HGRN_HEAD_V = 128
HGRN_WIDTH = HGRN_HEADS * HGRN_EXPAND
HGRN_CHUNK = 32
PEER_HEADS = 8
PEER_NKEYS = 128
PEER_QDIM = 256
PEER_TOPK = 16
GATE_PAD = 128

LANES = 128
SUBLANES = 8
VMEM_LIMIT = 56 * 1024 * 1024

_NEG = -1e30


def _cparams(sem):
    return pltpu.CompilerParams(dimension_semantics=sem, vmem_limit_bytes=VMEM_LIMIT)


def _gelu(x):
    return 0.5 * x * (1.0 + jnp.tanh(0.7978845608028654 * (x + 0.044715 * (x * x * x))))


def _dot(a, b, **kw):
    return jnp.dot(a, b, preferred_element_type=f32, **kw)


def _dot_nt(a, b):
    return lax.dot_general(a, b, (((1,), (1,)), ((), ())), preferred_element_type=f32)


def _rms(x, g):
    return x * lax.rsqrt(jnp.mean(x * x, axis=-1, keepdims=True) + EPS) * g


def _adaln_kernel(c_ref, w_ref, b_ref, o_ref):
    c = c_ref[...]
    cs = c * jax.nn.sigmoid(c)
    o_ref[...] = _dot(cs, w_ref[...], precision=_HIGHEST) + b_ref[...]


def _adaln(c, w, b):
    bsz = c.shape[0]
    tn = 512
    return pl.pallas_call(
        _adaln_kernel,
        grid=(6 * D_MODEL // tn,),
        in_specs=[
            pl.BlockSpec((bsz, D_MODEL), lambda j: (0, 0)),
            pl.BlockSpec((D_MODEL, tn), lambda j: (0, j)),
            pl.BlockSpec((1, tn), lambda j: (0, j)),
        ],
        out_specs=pl.BlockSpec((bsz, tn), lambda j: (0, j)),
        out_shape=jax.ShapeDtypeStruct((bsz, 6 * D_MODEL), f32),
        compiler_params=_cparams(("arbitrary",)),
        name="adaln",
    )(c, w, b)


_C_Q = 0
_C_KC = 512
_C_VC = 640
_C_KS = 768
_C_VS = 896
_C_KW = 1024
_C_VW = 1152
_C_GL = 1280
_C_HQ = _C_GL + GATE_PAD
_C_HF = _C_HQ + 512
_C_HI = _C_HF + 512
_C_HG = _C_HI + 512
_C_MG = _C_HG + 512
_C_END = _C_MG + 2 * D_MODEL


def _inproj_kernel(x_ref, mod_ref, g_ref, w_ref, cos_ref, sin_ref,
                   q_ref, kc_ref, vc_ref, kvsw_ref, gate_ref, hq_ref, hf_ref, hi_ref, hg_ref, mg_ref):
    x = x_ref[...]
    sh = mod_ref[0, 0:1, :]
    sc = mod_ref[0, 1:2, :]
    h = (_rms(x, g_ref[...]) * (1.0 + sc) + sh).astype(bf16)

    def mm(c0, c1):
        return _dot(h, w_ref[:, c0:c1])

    cos = cos_ref[...]
    sin = sin_ref[...]

    def rope(a):
        width = a.shape[1]
        first = (lax.broadcasted_iota(i32, a.shape, 1) & (HEAD_DIM - 1)) < (HEAD_DIM // 2)
        partner = jnp.where(first, pltpu.roll(a, width - HEAD_DIM // 2, 1), pltpu.roll(a, HEAD_DIM // 2, 1))
        return a * cos[:, :width] + partner * sin[:, :width]

    q_ref[...] = (rope(mm(_C_Q, _C_KC)) * (HEAD_DIM ** -0.5)).astype(bf16)
    kc_ref[...] = rope(mm(_C_KC, _C_VC)).astype(bf16)
    vc_ref[...] = mm(_C_VC, _C_KS).astype(bf16)
    kvsw_ref[:, 0:128] = rope(mm(_C_KS, _C_VS)).astype(bf16)
    kvsw_ref[:, 128:256] = mm(_C_VS, _C_KW).astype(bf16)
    kvsw_ref[:, 256:384] = rope(mm(_C_KW, _C_VW)).astype(bf16)
    kvsw_ref[:, 384:512] = mm(_C_VW, _C_GL).astype(bf16)
    gate_ref[...] = jax.nn.sigmoid(mm(_C_GL, _C_HQ))
    hq_ref[...] = mm(_C_HQ, _C_HF).astype(bf16)
    hf_ref[...] = mm(_C_HF, _C_HI)
    hi_ref[...] = mm(_C_HI, _C_HG).astype(bf16)
    hg_ref[...] = mm(_C_HG, _C_MG).astype(bf16)
    mg_ref[...] = jax.nn.sigmoid(mm(_C_MG, _C_END)).astype(bf16)


def _inproj(x2, mod3, g_mix, w_pad, cos_t, sin_t, seq):
    n = x2.shape[0]
    tm = 256
    tiles_per_seq = seq // tm
    row = lambda i: (i, 0)
    outs = [
        (NSA_WIDTH, bf16), (KV_WIDTH, bf16), (KV_WIDTH, bf16), (4 * KV_WIDTH, bf16), (GATE_PAD, f32),
        (HGRN_WIDTH, bf16), (HGRN_WIDTH, f32), (HGRN_WIDTH, bf16), (HGRN_WIDTH, bf16), (2 * D_MODEL, bf16),
    ]
    return pl.pallas_call(
        _inproj_kernel,
        grid=(n // tm,),
        in_specs=[
            pl.BlockSpec((tm, D_MODEL), row),
            pl.BlockSpec((1, 6, D_MODEL), lambda i: (i // tiles_per_seq, 0, 0)),
            pl.BlockSpec((1, D_MODEL), lambda i: (0, 0)),
            pl.BlockSpec((D_MODEL, _C_END), lambda i: (0, 0)),
            pl.BlockSpec((tm, NSA_WIDTH), lambda i: (i % tiles_per_seq, 0)),
            pl.BlockSpec((tm, NSA_WIDTH), lambda i: (i % tiles_per_seq, 0)),
        ],
        out_specs=[pl.BlockSpec((tm, w), row) for w, _ in outs],
        out_shape=[jax.ShapeDtypeStruct((n, w), dt) for w, dt in outs],
        compiler_params=_cparams(("arbitrary",)),
        name="inproj",
    )(x2, mod3, g_mix, w_pad, cos_t, sin_t)


def _compress_kernel(kc_ref, vc_ref, wkt_ref, wkb_ref, wvt_ref, wvb_ref, pk_ref, pv_ref,
                     w1k_ref, w1v_ref, w2k_ref, w2v_ref, okc_ref, ovc_ref):
    def one(x_ref, wt_ref, wb_ref, pos_ref, w1_ref, w2_ref, o_ref):
        pieces = x_ref[0]
        top = _dot(pieces, wt_ref[...])
        bot = _dot(pieces, wb_ref[...])
        nrow = bot.shape[0]
        bot = pltpu.roll(bot, nrow - 1, 0)
        cpos = _dot(pos_ref[...], w1_ref[...], precision=_HIGHEST)
        w2 = w2_ref[...].astype(bf16)
        outs = []
        for g in range(NSA_KV_GROUPS):
            sl = slice(g * CMP_HIDDEN, (g + 1) * CMP_HIDDEN)
            hid = _gelu(top[:, sl] + bot[:, sl] + cpos)
            outs.append(_dot(hid.astype(bf16), w2))
        o_ref[0] = jnp.concatenate(outs, axis=1).astype(bf16)

    one(kc_ref, wkt_ref, wkb_ref, pk_ref, w1k_ref, w2k_ref, okc_ref)
    one(vc_ref, wvt_ref, wvb_ref, pv_ref, w1v_ref, w2v_ref, ovc_ref)


def _compress(kc3, vc3, wkt, wkb, wvt, wvb, pk, pv, w1k, w1v, w2k, w2v):
    bsz, npieces, width = kc3.shape
    full = lambda a: pl.BlockSpec(a.shape, lambda b: (0,) * a.ndim)
    per_b = pl.BlockSpec((1, npieces, width), lambda b: (b, 0, 0))
    out_b = pl.BlockSpec((1, npieces, KV_WIDTH), lambda b: (b, 0, 0))
    return pl.pallas_call(
        _compress_kernel,
        grid=(bsz,),
        in_specs=[per_b, per_b] + [full(a) for a in (wkt, wkb, wvt, wvb, pk, pv, w1k, w1v, w2k, w2v)],
        out_specs=[out_b, out_b],
        out_shape=[jax.ShapeDtypeStruct((bsz, npieces, KV_WIDTH), bf16)] * 2,
        compiler_params=_cparams(("arbitrary",)),
        name="compress",
    )(kc3, vc3, wkt, wkb, wvt, wvb, pk, pv, w1k, w1v, w2k, w2v)


_TQ = 64


def _softmax_parts(s, mask):
    sm = jnp.where(mask, s, _NEG)
    m = jnp.max(sm, axis=-1, keepdims=True)
    p = jnp.where(mask, jnp.exp(sm - m), 0.0)
    den = jnp.maximum(jnp.sum(p, axis=-1, keepdims=True), 1e-30)
    return p, den


def _nsa_kernel(q_ref, kvsw_ref, kc_ref, vc_ref, gate_ref, selmap_ref, expand_ref, o_ref, *, seq):
    tq = _TQ
    rows = NSA_GROUP * tq
    n_sel = seq // SEL_BLOCK
    n_cmp_pad = seq // CMP_STRIDE
    t0 = pl.program_id(1) * tq
    q = q_ref[...]
    gates = gate_ref[...]
    tpos = t0 + lax.broadcasted_iota(i32, (tq, 1), 0)
    t4 = t0 + (lax.broadcasted_iota(i32, (rows, 1), 0) & (tq - 1))
    wk = WINDOW + tq
    ws = pl.multiple_of(jnp.maximum(t0 - WINDOW, 0), tq)
    ks_all = kvsw_ref[0, :, 0:128]
    vs_all = kvsw_ref[0, :, 128:256]
    kw_all = kvsw_ref[0, pl.ds(ws, wk), 256:384]
    vw_all = kvsw_ref[0, pl.ds(ws, wk), 384:512]
    kc_all = kc_ref[0]
    vc_all = vc_ref[0]
    cend = lax.broadcasted_iota(i32, (1, n_cmp_pad), 1) * CMP_STRIDE + (CMP_BLOCK - 1)
    blk = lax.broadcasted_iota(i32, (1, n_sel), 1)
    kpos = lax.broadcasted_iota(i32, (1, seq), 1)
    kpos_w = ws + lax.broadcasted_iota(i32, (1, wk), 1)
    cur = tpos >> 6
    forced = (blk == 0) | (blk == cur) | (blk == cur - 1)
    causal_blk = blk * SEL_BLOCK <= tpos
    pieces = []
    for g in range(NSA_KV_GROUPS):
        gs = slice(g * HEAD_DIM, (g + 1) * HEAD_DIM)
        qg = jnp.concatenate(
            [q[:, (NSA_GROUP * g + r) * HEAD_DIM:(NSA_GROUP * g + r + 1) * HEAD_DIM] for r in range(NSA_GROUP)], axis=0)
        p_c, den_c = _softmax_parts(_dot_nt(qg, kc_all[:, gs]), cend <= t4)
        o_c = _dot(p_c.astype(bf16), vc_all[:, gs]) / den_c
        pn = p_c / den_c
        pc_sum = pn[0:tq]
        for r in range(1, NSA_GROUP):
            pc_sum = pc_sum + pn[r * tq:(r + 1) * tq]
        imp = _dot(pc_sum, selmap_ref[...], precision=_HIGHEST)
        imp = jnp.where(forced, imp + FORCE_BONUS, imp)
        imp = jnp.where(causal_blk, imp, -1.0)
        rank = jnp.zeros((tq, n_sel), f32)
        for j in range(n_sel):
            col = imp[:, j:j + 1]
            ahead = jnp.where(col > imp, 1.0, jnp.where(col == imp, jnp.where(blk > j, 1.0, 0.0), 0.0))
            rank = rank + ahead
        sel = jnp.where(rank < float(min(SEL_TOPK, n_sel)), 1.0, 0.0).astype(bf16)
        sel4 = jnp.concatenate([sel] * NSA_GROUP, axis=0)
        selx = _dot(sel4, expand_ref[...])
        m_s = jnp.where(kpos <= t4, selx, 0.0) > 0.5
        p_s, den_s = _softmax_parts(_dot_nt(qg, ks_all[:, gs]), m_s)
        o_s = _dot(p_s.astype(bf16), vs_all[:, gs]) / den_s
        dist = t4 - kpos_w
        m_w = jnp.where(dist >= 0, jnp.where(dist < WINDOW, 1.0, 0.0), 0.0) > 0.5
        p_w, den_w = _softmax_parts(_dot_nt(qg, kw_all[:, gs]), m_w)
        o_w = _dot(p_w.astype(bf16), vw_all[:, gs]) / den_w

        def gcol(br):
            return jnp.concatenate(
                [gates[:, (NSA_GROUP * g + r) * 3 + br:(NSA_GROUP * g + r) * 3 + br + 1] for r in range(NSA_GROUP)], axis=0)

        o = gcol(0) * o_c + gcol(1) * o_s + gcol(2) * o_w
        pieces += [o[r * tq:(r + 1) * tq] for r in range(NSA_GROUP)]
    o_ref[...] = jnp.concatenate(pieces, axis=1).astype(bf16)


def _nsa(q, kvsw3, kc3, vc3, gates, selmap, expand, seq):
    n = q.shape[0]
    bsz = n // seq
    tq = _TQ
    nq = seq // tq
    return pl.pallas_call(
        functools.partial(_nsa_kernel, seq=seq),
        grid=(bsz, nq),
        in_specs=[
            pl.BlockSpec((tq, NSA_WIDTH), lambda b, i: (b * nq + i, 0)),
            pl.BlockSpec((1, seq, 4 * KV_WIDTH), lambda b, i: (b, 0, 0)),
            pl.BlockSpec((1,) + kc3.shape[1:], lambda b, i: (b, 0, 0)),
            pl.BlockSpec((1,) + vc3.shape[1:], lambda b, i: (b, 0, 0)),
            pl.BlockSpec((tq, GATE_PAD), lambda b, i: (b * nq + i, 0)),
            pl.BlockSpec(selmap.shape, lambda b, i: (0, 0)),
            pl.BlockSpec(expand.shape, lambda b, i: (0, 0)),
        ],
        out_specs=pl.BlockSpec((tq, NSA_WIDTH), lambda b, i: (b * nq + i, 0)),
        out_shape=jax.ShapeDtypeStruct((n, NSA_WIDTH), bf16),
        compiler_params=_cparams(("arbitrary", "arbitrary")),
        name="nsa",
    )(q, kvsw3, kc3, vc3, gates, selmap, expand)


def _hgrn_kernel(hq_ref, hf_ref, hi_ref, hg_ref, lbl_ref, ng_ref, o_ref, st_ref, *, seq, layer):
    c = HGRN_CHUNK
    logits = lbl_ref[...]
    e = jnp.exp(logits - jnp.max(logits, axis=0, keepdims=True))
    sm = e / jnp.sum(e, axis=0, keepdims=True)
    lb = sm[0:1]
    for l in range(1, layer + 1):
        lb = lb + sm[l:l + 1]
    st_ref[...] = jnp.zeros_like(st_ref)
    r_i = lax.broadcasted_iota(i32, (c, c), 0)
    c_i = lax.broadcasted_iota(i32, (c, c), 1)
    causal = r_i >= c_i
    tri = jnp.where(causal, 1.0, 0.0)
    ng = ng_ref[...]

    def body(ci, carry):
        r0 = pl.multiple_of(ci * c, c)
        hq = hq_ref[pl.ds(r0, c), :].astype(f32)
        hf = hf_ref[pl.ds(r0, c), :]
        v = hi_ref[pl.ds(r0, c), :]
        hg = hg_ref[pl.ds(r0, c), :].astype(f32)
        f = lb + (1.0 - lb) * jax.nn.sigmoid(hf)
        k = 1.0 - f
        qv = hq * jax.nn.sigmoid(hq)
        bcum = _dot(tri, jnp.log(f), precision=_HIGHEST)
        bend = bcum[c - 1:c, :]
        q_dec = (qv * jnp.exp(bcum)).astype(bf16)
        k_inv = (k * jnp.exp(-bcum)).astype(bf16)
        k_end = (k * jnp.exp(bend - bcum)).astype(bf16)
        dec_end = jnp.exp(bend)
        outs = []
        for h in range(HGRN_HEADS):
            sl = slice(h * HGRN_EXPAND, (h + 1) * HGRN_EXPAND)
            a = jnp.where(causal, _dot_nt(q_dec[:, sl], k_inv[:, sl]), 0.0)
            st = st_ref[h]
            o = _dot(a.astype(bf16), v[:, sl]) + _dot_nt(q_dec[:, sl], st.astype(bf16))
            upd = lax.dot_general(v[:, sl], k_end[:, sl], (((0,), (0,)), ((), ())), preferred_element_type=f32)
            st_ref[h] = st * dec_end[:, sl] + upd
            y = _rms(o, ng) * (hg[:, sl] * jax.nn.sigmoid(hg[:, sl]))
            outs.append(y)
        o_ref[pl.ds(r0, c), :] = jnp.concatenate(outs, axis=1).astype(bf16)
        return carry

    lax.fori_loop(0, seq // c, body, 0)


def _hgrn(hq, hf, hi, hg, lb_logits, norm_g, seq, layer):
    n = hq.shape[0]
    bsz = n // seq
    per_b = pl.BlockSpec((seq, HGRN_WIDTH), lambda b: (b, 0))
    return pl.pallas_call(
        functools.partial(_hgrn_kernel, seq=seq, layer=layer),
        grid=(bsz,),
        in_specs=[per_b, per_b, per_b, per_b,
                  pl.BlockSpec(lb_logits.shape, lambda b: (0, 0)),
                  pl.BlockSpec(norm_g.shape, lambda b: (0, 0))],
        out_specs=per_b,
        out_shape=jax.ShapeDtypeStruct((n, HGRN_WIDTH), bf16),
        scratch_shapes=[pltpu.VMEM((HGRN_HEADS, HGRN_HEAD_V, HGRN_EXPAND), f32)],
        compiler_params=_cparams(("arbitrary",)),
        name="hgrn",
    )(hq, hf, hi, hg, lb_logits, norm_g)


def _merge_kernel(on_ref, oh_ref, mg_ref, x_ref, mod_ref, g_ref, wb0_ref, wb1_ref, wo_ref, x1_ref, h2_ref):
    a = _dot(on_ref[...], wb0_ref[...])
    b = _dot(oh_ref[...], wb1_ref[...])
    mg = mg_ref[...].astype(f32)
    y = mg[:, :D_MODEL] * a + mg[:, D_MODEL:] * b
    y2 = _dot(y.astype(bf16), wo_ref[...])
    x1 = x_ref[...] + mod_ref[0, 2:3, :] * y2
    x1_ref[...] = x1
    h2_ref[...] = _rms(x1, g_ref[...]) * (1.0 + mod_ref[0, 4:5, :]) + mod_ref[0, 3:4, :]


def _merge(o_nsa, o_hgrn, mg, x2, mod3, g_ffn, wb0, wb1, wo, seq):
    n = x2.shape[0]
    tm = 256
    tiles_per_seq = seq // tm
    row = lambda i: (i, 0)
    full = lambda a: pl.BlockSpec(a.shape, lambda i: (0,) * a.ndim)
    return pl.pallas_call(
        _merge_kernel,
        grid=(n // tm,),
        in_specs=[
            pl.BlockSpec((tm, NSA_WIDTH), row), pl.BlockSpec((tm, HGRN_WIDTH), row),
            pl.BlockSpec((tm, 2 * D_MODEL), row), pl.BlockSpec((tm, D_MODEL), row),
            pl.BlockSpec((1, 6, D_MODEL), lambda i: (i // tiles_per_seq, 0, 0)),
            full(g_ffn), full(wb0), full(wb1), full(wo),
        ],
        out_specs=[pl.BlockSpec((tm, D_MODEL), row)] * 2,
        out_shape=[jax.ShapeDtypeStruct((n, D_MODEL), f32)] * 2,
        compiler_params=_cparams(("arbitrary",)),
        name="merge",
    )(o_nsa, o_hgrn, mg, x2, mod3, g_ffn, wb0, wb1, wo)


_TR = 256


def _topk_rows(s, k):
    n = s.shape[0]
    rowid = lax.broadcasted_iota(i32, s.shape, 0)
    vals, idxs = [], []
    for _ in range(k):
        m = jnp.max(s, axis=0, keepdims=True)
        i = jnp.min(jnp.where(s == m, rowid, n), axis=0, keepdims=True)
        vals.append(m)
        idxs.append(i)
        s = jnp.where(rowid == i, -jnp.inf, s)
    return vals, idxs


def _row_order(c):
    s, i = divmod(c, 16)
    return i + 64 * (s >= 4) + 32 * ((s % 4) >= 2) + 16 * (s % 2)


def _route_kernel(h_ref, wq_ref, sk_ref, idx_ref, gate_ref):
    h = h_ref[...].astype(bf16)
    tr = h.shape[0]
    k = PEER_TOPK
    half = PEER_QDIM // 2
    experts, gates = [], []
    for hd in range(PEER_HEADS):
        tops = []
        for p in range(2):
            grp = hd * 2 + p
            q_t = _dot_nt(wq_ref[grp * half:(grp + 1) * half, :], h)
            s_t = _dot(sk_ref[grp], q_t.astype(bf16))
            tops.append(_topk_rows(s_t, k))
        (v1, i1), (v2, i2) = tops
        s2 = jnp.concatenate(v2, axis=0)
        e2 = jnp.concatenate(i2, axis=0)
        comb = jnp.concatenate([v1[a] + s2 for a in range(k)], axis=0)
        cand = jnp.concatenate([i1[a] * PEER_NKEYS + e2 for a in range(k)], axis=0)
        rowid = lax.broadcasted_iota(i32, comb.shape, 0)
        tv, tp = _topk_rows(comb, k)
        for r in range(k):
            experts.append(jnp.sum(jnp.where(rowid == tp[r], cand, 0), axis=0, keepdims=True))
        ex = [jnp.exp(tv[r] - tv[0]) for r in range(k)]
        den = ex[0]
        for r in range(1, k):
            den = den + ex[r]
        gates += [e / den for e in ex]
    ncol = PEER_HEADS * k
    inv = [0] * ncol
    for c in range(ncol):
        inv[_row_order(c)] = c
    idx_t = jnp.concatenate([experts[inv[j]] for j in range(ncol)], axis=0)
    gate_t = jnp.concatenate(gates, axis=0)
    for blk in range(tr // LANES):
        sl = slice(blk * LANES, (blk + 1) * LANES)
        idx_ref[sl, :] = idx_t[:, sl].T
        gate_ref[sl, :] = gate_t[:, sl].T


def _route(h2, wq_t, sk):
    n = h2.shape[0]
    tr = _TR
    ncol = PEER_HEADS * PEER_TOPK
    return pl.pallas_call(
        _route_kernel,
        grid=(n // tr,),
        in_specs=[
            pl.BlockSpec((tr, D_MODEL), lambda i: (i, 0)),
            pl.BlockSpec(wq_t.shape, lambda i: (0, 0)),
            pl.BlockSpec(sk.shape, lambda i: (0, 0, 0)),
        ],
        out_specs=[pl.BlockSpec((tr, ncol), lambda i: (i, 0))] * 2,
        out_shape=[jax.ShapeDtypeStruct((n, ncol), i32), jax.ShapeDtypeStruct((n, ncol), f32)],
        compiler_params=_cparams(("arbitrary",)),
        name="route",
    )(h2, wq_t, sk)


_TS = 64
_TG = 8
_NROW = PEER_HEADS * PEER_TOPK
_ROW_TILES = 2 * D_MODEL // LANES


def _fold(p, d, sub):
    n = p.shape[0] // 2
    a, b = p[:n], p[n:]
    low = (sub & (2 * d - 1)) < d
    return jnp.where(low, a + pltpu.roll(a, SUBLANES - d, 1), b + pltpu.roll(b, d, 1))


def _expert_kernel(idx_ref, gate_ref, h_ref, x1_ref, mod_ref, gfin_ref, uv_ref, o_ref, buf, sem):
    n_groups = _TS // _TG
    n_dma = _TG * _NROW

    def row_copy(e, slot, j):
        return pltpu.make_async_copy(uv_ref.at[e], buf.at[slot, j], sem.at[slot])

    def issue(grp, slot):
        def body(j, c):
            e = idx_ref[grp * _TG + j // _NROW, j % _NROW]
            row_copy(e, slot, j).start()
            return c
        lax.fori_loop(0, n_dma, body, 0, unroll=8)

    def wait(slot):
        def body(j, c):
            row_copy(0, slot, j).wait()
            return c
        lax.fori_loop(0, n_dma, body, 0, unroll=8)

    issue(0, 0)
    sub = lax.broadcasted_iota(i32, (1, SUBLANES, LANES), 1)
    gt_f = mod_ref[0, 5]
    gfin = gfin_ref[...]

    def group(grp, carry):
        slot = grp % 2

        @pl.when(grp + 1 < n_groups)
        def _():
            issue(grp + 1, 1 - slot)

        wait(slot)
        for t in range(_TG):
            tok = grp * _TG + t
            rows = buf[slot, pl.ds(t * _NROW, _NROW)]
            u = rows[:, 0:SUBLANES, :]
            v = rows[:, SUBLANES:2 * SUBLANES, :]
            z = u * h_ref[tok][None]
            z = _fold(z, 4, sub)
            z = _fold(z, 2, sub)
            z = _fold(z, 1, sub)
            act = _gelu(jnp.sum(z, axis=-1, keepdims=True))
            g_tok = gate_ref[tok]
            y = jnp.zeros((SUBLANES, LANES), f32)
            for i in range(16):
                w_i = g_tok[:, i:i + 1] * act[i]
                for s in range(SUBLANES):
                    w = jnp.broadcast_to(w_i[s:s + 1, :], (SUBLANES, LANES))
                    y = y + w * v[_row_order(s * 16 + i)]
            x2 = x1_ref[tok] + gt_f * y
            ms = jnp.sum(jnp.sum(x2 * x2, axis=1, keepdims=True), axis=0, keepdims=True) * (1.0 / D_MODEL)
            o_ref[tok] = x2 * lax.rsqrt(ms + EPS) * gfin
        return carry

    lax.fori_loop(0, n_groups, group, 0)


def _experts(idx, gate3, h3, x13, mod4, gfin, uv, seq):
    n = idx.shape[0]
    steps_per_seq = seq // _TS
    tok3 = lambda i: (i, 0, 0)
    return pl.pallas_call(
        _expert_kernel,
        grid=(n // _TS,),
        in_specs=[
            pl.BlockSpec((_TS, _NROW), lambda i: (i, 0), memory_space=pltpu.SMEM),
            pl.BlockSpec((_TS, SUBLANES, 16), tok3),
            pl.BlockSpec((_TS, SUBLANES, LANES), tok3),
            pl.BlockSpec((_TS, SUBLANES, LANES), tok3),
            pl.BlockSpec((1, 6, SUBLANES, LANES), lambda i: (i // steps_per_seq, 0, 0, 0)),
            pl.BlockSpec((SUBLANES, LANES), lambda i: (0, 0)),
            pl.BlockSpec(memory_space=pl.ANY),
        ],
        out_specs=pl.BlockSpec((_TS, SUBLANES, LANES), tok3),
        out_shape=jax.ShapeDtypeStruct((n, SUBLANES, LANES), f32),
        scratch_shapes=[pltpu.VMEM((2, _TG * _NROW, _ROW_TILES, LANES), f32), pltpu.SemaphoreType.DMA((2,))],
        compiler_params=_cparams(("arbitrary",)),
        name="experts",
    )(idx, gate3, h3, x13, mod4, gfin, uv)


def _rope_tables(seq):
    half = HEAD_DIM // 2
    inv = ROPE_THETA ** (-np.arange(half, dtype=np.float32) / half)
    ang = np.arange(seq, dtype=np.float32)[:, None] * inv[None, :].astype(np.float32)
    cos = np.cos(ang).astype(np.float32)
    sin = np.sin(ang).astype(np.float32)
    cos_t = np.tile(np.concatenate([cos, cos], axis=1), (1, NSA_HEADS))
    sin_t = np.tile(np.concatenate([-sin, sin], axis=1), (1, NSA_HEADS))
    return jnp.asarray(cos_t), jnp.asarray(sin_t)


def _selection_map(n_cmp_pad, n_sel):
    r_sel = SEL_BLOCK // CMP_STRIDE
    r_cmp = CMP_BLOCK // CMP_STRIDE
    i = np.arange(n_cmp_pad)[:, None]
    j = np.arange(n_sel)[None, :]
    d = i - r_sel * j
    cnt = np.minimum(d, r_sel - 1) - np.maximum(d - r_cmp + 1, 0) + 1
    cnt = np.clip(cnt, 0, None).astype(np.float32)
    cnt[n_cmp_pad - r_cmp + 1:] = 0.0
    return jnp.asarray(cnt)


def _block_expand(n_sel, seq):
    e = (np.arange(seq)[None, :] // SEL_BLOCK == np.arange(n_sel)[:, None]).astype(np.float32)
    return jnp.asarray(e, dtype=bf16)


def _compress_weights(w1):
    eye = jnp.eye(NSA_KV_GROUPS, dtype=w1.dtype)
    out = []
    for part in range(CMP_BLOCK // CMP_STRIDE):
        w = w1[part * CMP_STRIDE * HEAD_DIM:(part + 1) * CMP_STRIDE * HEAD_DIM].reshape(CMP_STRIDE, HEAD_DIM, CMP_HIDDEN)
        big = jnp.einsum('pdc,gh->pgdhc', w, eye).reshape(CMP_STRIDE * KV_WIDTH, NSA_KV_GROUPS * CMP_HIDDEN)
        out.append(big.astype(bf16))
    return out


def kernel(x, c, w_ada, b_ada, g_mix, g_ffn, w_in, cmp_pos_k, cmp_pos_v, w_ck1, w_ck2, w_cv1, w_cv2, hgrn_lb_logits, hgrn_out_norm, w_branch, w_out, w_peer_q, peer_sub_keys, peer_u, peer_v, g_final):
    bsz, seq, d = x.shape
    n = bsz * seq
    depth = w_ada.shape[0]
    assert depth == 1, "single-layer block only"
    n_sel = seq // SEL_BLOCK
    n_pieces = seq // CMP_STRIDE
    cos_t, sin_t = _rope_tables(seq)
    selmap = _selection_map(n_pieces, n_sel)
    expand = _block_expand(n_sel, seq)
    xcur = x.reshape(n, d)
    for l in range(depth):
        mod = _adaln(c, w_ada[l], b_ada[l].reshape(1, 6 * d))
        mod3 = mod.reshape(bsz, 6, d)
        w_pad = jnp.concatenate(
            [w_in[l][:, :_C_GL + 3 * NSA_HEADS], jnp.zeros((d, GATE_PAD - 3 * NSA_HEADS), w_in.dtype),
             w_in[l][:, _C_GL + 3 * NSA_HEADS:]], axis=1).astype(bf16)
        q, kc, vc, kvsw, gates, hq, hf, hi, hg, mg = _inproj(xcur, mod3, g_mix[l].reshape(1, d), w_pad, cos_t, sin_t, seq)
        wkt, wkb = _compress_weights(w_ck1[l])
        wvt, wvb = _compress_weights(w_cv1[l])
        kc_c, vc_c = _compress(
            kc.reshape(bsz, n_pieces, CMP_STRIDE * KV_WIDTH), vc.reshape(bsz, n_pieces, CMP_STRIDE * KV_WIDTH),
            wkt, wkb, wvt, wvb, cmp_pos_k[l].reshape(1, -1), cmp_pos_v[l].reshape(1, -1),
            w_ck1[l], w_cv1[l], w_ck2[l], w_cv2[l])
        o_nsa = _nsa(q, kvsw.reshape(bsz, seq, 4 * KV_WIDTH), kc_c, vc_c, gates, selmap, expand, seq)
        o_hgrn = _hgrn(hq, hf, hi, hg, hgrn_lb_logits, hgrn_out_norm[l].reshape(1, -1), seq, l)
        x1, h2 = _merge(o_nsa, o_hgrn, mg, xcur, mod3, g_ffn[l].reshape(1, d),
                        w_branch[l, 0].astype(bf16), w_branch[l, 1].astype(bf16), w_out[l].astype(bf16), seq)
        wq_t = w_peer_q[l].T.astype(bf16)
        sk = jnp.transpose(peer_sub_keys[l], (1, 0, 2, 3)).reshape(2 * PEER_HEADS, PEER_NKEYS, PEER_QDIM // 2).astype(bf16)
        idx, gate = _route(h2, wq_t, sk)
        uv = jnp.concatenate([peer_u[l], peer_v[l]], axis=1).reshape(-1, _ROW_TILES, LANES)
        out3 = _experts(idx, gate.reshape(n, SUBLANES, 16), h2.reshape(n, SUBLANES, LANES), x1.reshape(n, SUBLANES, LANES),
                        mod.reshape(bsz, 6, SUBLANES, LANES), g_final.reshape(SUBLANES, LANES), uv, seq)
        xcur = out3.reshape(n, d)
    return xcur.reshape(bsz, seq, d)
```

```python
import functools

import numpy as np
import jax
import jax.numpy as jnp
from jax import lax
from jax.experimental import pallas as pl
from jax.experimental.pallas import tpu as pltpu

f32 = jnp.float32
bf16 = jnp.bfloat16
i32 = jnp.int32
_HIGHEST = lax.Precision.HIGHEST

D_MODEL = 1024
EPS = 1e-6
HEAD_DIM = 64
ROPE_THETA = 10000.0
NSA_HEADS = 8
NSA_KV_GROUPS = 2
NSA_GROUP = NSA_HEADS // NSA_KV_GROUPS
NSA_WIDTH = NSA_HEADS * HEAD_DIM
KV_WIDTH = NSA_KV_GROUPS * HEAD_DIM
CMP_BLOCK = 32
CMP_STRIDE = 16
CMP_HIDDEN = 128
SEL_BLOCK = 64
SEL_TOPK = 16
WINDOW = 512
FORCE_BONUS = 1000.0
HGRN_HEADS = 4
HGRN_EXPAND = 128
HGRN_HEAD_V = 128
HGRN_WIDTH = HGRN_HEADS * HGRN_EXPAND
HGRN_CHUNK = 32
PEER_HEADS = 8
PEER_NKEYS = 128
PEER_QDIM = 256
PEER_TOPK = 16
GATE_PAD = 128

LANES = 128
SUBLANES = 8
VMEM_LIMIT = 56 * 1024 * 1024

_NEG = -1e30


def _cparams(sem):
    return pltpu.CompilerParams(dimension_semantics=sem, vmem_limit_bytes=VMEM_LIMIT)


def _gelu(x):
    return 0.5 * x * (1.0 + jnp.tanh(0.7978845608028654 * (x + 0.044715 * (x * x * x))))


def _dot(a, b, **kw):
    return jnp.dot(a, b, preferred_element_type=f32, **kw)


def _dot_nt(a, b):
    return lax.dot_general(a, b, (((1,), (1,)), ((), ())), preferred_element_type=f32)


def _rms(x, g):
    return x * lax.rsqrt(jnp.mean(x * x, axis=-1, keepdims=True) + EPS) * g


def _adaln_kernel(c_ref, w_ref, b_ref, o_ref):
    c = c_ref[...]
    cs = c * jax.nn.sigmoid(c)
    o_ref[...] = _dot(cs, w_ref[...], precision=_HIGHEST) + b_ref[...]


def _adaln(c, w, b):
    bsz = c.shape[0]
    tn = 512
    return pl.pallas_call(
        _adaln_kernel,
        grid=(6 * D_MODEL // tn,),
        in_specs=[
            pl.BlockSpec((bsz, D_MODEL), lambda j: (0, 0)),
            pl.BlockSpec((D_MODEL, tn), lambda j: (0, j)),
            pl.BlockSpec((1, tn), lambda j: (0, j)),
        ],
        out_specs=pl.BlockSpec((bsz, tn), lambda j: (0, j)),
        out_shape=jax.ShapeDtypeStruct((bsz, 6 * D_MODEL), f32),
        compiler_params=_cparams(("arbitrary",)),
        name="adaln",
    )(c, w, b)


_C_Q = 0
_C_KC = 512
_C_VC = 640
_C_KS = 768
_C_VS = 896
_C_KW = 1024
_C_VW = 1152
_C_GL = 1280
_C_HQ = _C_GL + GATE_PAD
_C_HF = _C_HQ + 512
_C_HI = _C_HF + 512
_C_HG = _C_HI + 512
_C_MG = _C_HG + 512
_C_END = _C_MG + 2 * D_MODEL


def _inproj_kernel(x_ref, mod_ref, g_ref, w_ref, cos_ref, sin_ref,
                   q_ref, kc_ref, vc_ref, kvsw_ref, gate_ref, hq_ref, hf_ref, hi_ref, hg_ref, mg_ref):
    x = x_ref[...]
    sh = mod_ref[0, 0:1, :]
    sc = mod_ref[0, 1:2, :]
    h = (_rms(x, g_ref[...]) * (1.0 + sc) + sh).astype(bf16)

    def mm(c0, c1):
        return _dot(h, w_ref[:, c0:c1])

    cos = cos_ref[...]
    sin = sin_ref[...]

    def rope(a):
        width = a.shape[1]
        first = (lax.broadcasted_iota(i32, a.shape, 1) & (HEAD_DIM - 1)) < (HEAD_DIM // 2)
        partner = jnp.where(first, pltpu.roll(a, width - HEAD_DIM // 2, 1), pltpu.roll(a, HEAD_DIM // 2, 1))
        return a * cos[:, :width] + partner * sin[:, :width]

    q_ref[...] = (rope(mm(_C_Q, _C_KC)) * (HEAD_DIM ** -0.5)).astype(bf16)
    kc_ref[...] = rope(mm(_C_KC, _C_VC)).astype(bf16)
    vc_ref[...] = mm(_C_VC, _C_KS).astype(bf16)
    kvsw_ref[:, 0:128] = rope(mm(_C_KS, _C_VS)).astype(bf16)
    kvsw_ref[:, 128:256] = mm(_C_VS, _C_KW).astype(bf16)
    kvsw_ref[:, 256:384] = rope(mm(_C_KW, _C_VW)).astype(bf16)
    kvsw_ref[:, 384:512] = mm(_C_VW, _C_GL).astype(bf16)
    gate_ref[...] = jax.nn.sigmoid(mm(_C_GL, _C_HQ))
    hq_ref[...] = mm(_C_HQ, _C_HF).astype(bf16)
    hf_ref[...] = mm(_C_HF, _C_HI)
    hi_ref[...] = mm(_C_HI, _C_HG).astype(bf16)
    hg_ref[...] = mm(_C_HG, _C_MG).astype(bf16)
    mg_ref[...] = jax.nn.sigmoid(mm(_C_MG, _C_END)).astype(bf16)


def _inproj(x2, mod3, g_mix, w_pad, cos_t, sin_t, seq):
    n = x2.shape[0]
    tm = 256
    tiles_per_seq = seq // tm
    row = lambda i: (i, 0)
    outs = [
        (NSA_WIDTH, bf16), (KV_WIDTH, bf16), (KV_WIDTH, bf16), (4 * KV_WIDTH, bf16), (GATE_PAD, f32),
        (HGRN_WIDTH, bf16), (HGRN_WIDTH, f32), (HGRN_WIDTH, bf16), (HGRN_WIDTH, bf16), (2 * D_MODEL, bf16),
    ]
    return pl.pallas_call(
        _inproj_kernel,
        grid=(n // tm,),
        in_specs=[
            pl.BlockSpec((tm, D_MODEL), row),
            pl.BlockSpec((1, 6, D_MODEL), lambda i: (i // tiles_per_seq, 0, 0)),
            pl.BlockSpec((1, D_MODEL), lambda i: (0, 0)),
            pl.BlockSpec((D_MODEL, _C_END), lambda i: (0, 0)),
            pl.BlockSpec((tm, NSA_WIDTH), lambda i: (i % tiles_per_seq, 0)),
            pl.BlockSpec((tm, NSA_WIDTH), lambda i: (i % tiles_per_seq, 0)),
        ],
        out_specs=[pl.BlockSpec((tm, w), row) for w, _ in outs],
        out_shape=[jax.ShapeDtypeStruct((n, w), dt) for w, dt in outs],
        compiler_params=_cparams(("arbitrary",)),
        name="inproj",
    )(x2, mod3, g_mix, w_pad, cos_t, sin_t)


def _compress_kernel(kc_ref, vc_ref, wkt_ref, wkb_ref, wvt_ref, wvb_ref, pk_ref, pv_ref,
                     w1k_ref, w1v_ref, w2k_ref, w2v_ref, okc_ref, ovc_ref):
    def one(x_ref, wt_ref, wb_ref, pos_ref, w1_ref, w2_ref, o_ref):
        pieces = x_ref[0]
        top = _dot(pieces, wt_ref[...])
        bot = _dot(pieces, wb_ref[...])
        nrow = bot.shape[0]
        bot = pltpu.roll(bot, nrow - 1, 0)
        cpos = _dot(pos_ref[...], w1_ref[...], precision=_HIGHEST)
        w2 = w2_ref[...].astype(bf16)
        outs = []
        for g in range(NSA_KV_GROUPS):
            sl = slice(g * CMP_HIDDEN, (g + 1) * CMP_HIDDEN)
            hid = _gelu(top[:, sl] + bot[:, sl] + cpos)
            outs.append(_dot(hid.astype(bf16), w2))
        o_ref[0] = jnp.concatenate(outs, axis=1).astype(bf16)

    one(kc_ref, wkt_ref, wkb_ref, pk_ref, w1k_ref, w2k_ref, okc_ref)
    one(vc_ref, wvt_ref, wvb_ref, pv_ref, w1v_ref, w2v_ref, ovc_ref)


def _compress(kc3, vc3, wkt, wkb, wvt, wvb, pk, pv, w1k, w1v, w2k, w2v):
    bsz, npieces, width = kc3.shape
    full = lambda a: pl.BlockSpec(a.shape, lambda b: (0,) * a.ndim)
    per_b = pl.BlockSpec((1, npieces, width), lambda b: (b, 0, 0))
    out_b = pl.BlockSpec((1, npieces, KV_WIDTH), lambda b: (b, 0, 0))
    return pl.pallas_call(
        _compress_kernel,
        grid=(bsz,),
        in_specs=[per_b, per_b] + [full(a) for a in (wkt, wkb, wvt, wvb, pk, pv, w1k, w1v, w2k, w2v)],
        out_specs=[out_b, out_b],
        out_shape=[jax.ShapeDtypeStruct((bsz, npieces, KV_WIDTH), bf16)] * 2,
        compiler_params=_cparams(("arbitrary",)),
        name="compress",
    )(kc3, vc3, wkt, wkb, wvt, wvb, pk, pv, w1k, w1v, w2k, w2v)


_TQ = 64


def _softmax_parts(s, mask):
    sm = jnp.where(mask, s, _NEG)
    m = jnp.max(sm, axis=-1, keepdims=True)
    p = jnp.where(mask, jnp.exp(sm - m), 0.0)
    den = jnp.maximum(jnp.sum(p, axis=-1, keepdims=True), 1e-30)
    return p, den


def _nsa_kernel(q_ref, kvsw_ref, kc_ref, vc_ref, gate_ref, selmap_ref, expand_ref, o_ref, *, seq):
    tq = _TQ
    rows = NSA_GROUP * tq
    n_sel = seq // SEL_BLOCK
    n_cmp_pad = seq // CMP_STRIDE
    t0 = pl.program_id(1) * tq
    q = q_ref[...]
    gates = gate_ref[...]
    tpos = t0 + lax.broadcasted_iota(i32, (tq, 1), 0)
    t4 = t0 + (lax.broadcasted_iota(i32, (rows, 1), 0) & (tq - 1))
    wk = WINDOW + tq
    ws = pl.multiple_of(jnp.maximum(t0 - WINDOW, 0), tq)
    ks_all = kvsw_ref[0, :, 0:128]
    vs_all = kvsw_ref[0, :, 128:256]
    kw_all = kvsw_ref[0, pl.ds(ws, wk), 256:384]
    vw_all = kvsw_ref[0, pl.ds(ws, wk), 384:512]
    kc_all = kc_ref[0]
    vc_all = vc_ref[0]
    cend = lax.broadcasted_iota(i32, (1, n_cmp_pad), 1) * CMP_STRIDE + (CMP_BLOCK - 1)
    blk = lax.broadcasted_iota(i32, (1, n_sel), 1)
    kpos = lax.broadcasted_iota(i32, (1, seq), 1)
    kpos_w = ws + lax.broadcasted_iota(i32, (1, wk), 1)
    cur = tpos >> 6
    forced = (blk == 0) | (blk == cur) | (blk == cur - 1)
    causal_blk = blk * SEL_BLOCK <= tpos
    pieces = []
    for g in range(NSA_KV_GROUPS):
        gs = slice(g * HEAD_DIM, (g + 1) * HEAD_DIM)
        qg = jnp.concatenate(
            [q[:, (NSA_GROUP * g + r) * HEAD_DIM:(NSA_GROUP * g + r + 1) * HEAD_DIM] for r in range(NSA_GROUP)], axis=0)
        p_c, den_c = _softmax_parts(_dot_nt(qg, kc_all[:, gs]), cend <= t4)
        o_c = _dot(p_c.astype(bf16), vc_all[:, gs]) / den_c
        pn = p_c / den_c
        pc_sum = pn[0:tq]
        for r in range(1, NSA_GROUP):
            pc_sum = pc_sum + pn[r * tq:(r + 1) * tq]
        imp = _dot(pc_sum, selmap_ref[...], precision=_HIGHEST)
        imp = jnp.where(forced, imp + FORCE_BONUS, imp)
        imp = jnp.where(causal_blk, imp, -1.0)
        rank = jnp.zeros((tq, n_sel), f32)
        for j in range(n_sel):
            col = imp[:, j:j + 1]
            ahead = jnp.where(col > imp, 1.0, jnp.where(col == imp, jnp.where(blk > j, 1.0, 0.0), 0.0))
            rank = rank + ahead
        sel = jnp.where(rank < float(min(SEL_TOPK, n_sel)), 1.0, 0.0).astype(bf16)
        sel4 = jnp.concatenate([sel] * NSA_GROUP, axis=0)
        selx = _dot(sel4, expand_ref[...])
        m_s = jnp.where(kpos <= t4, selx, 0.0) > 0.5
        p_s, den_s = _softmax_parts(_dot_nt(qg, ks_all[:, gs]), m_s)
        o_s = _dot(p_s.astype(bf16), vs_all[:, gs]) / den_s
        dist = t4 - kpos_w
        m_w = jnp.where(dist >= 0, jnp.where(dist < WINDOW, 1.0, 0.0), 0.0) > 0.5
        p_w, den_w = _softmax_parts(_dot_nt(qg, kw_all[:, gs]), m_w)
        o_w = _dot(p_w.astype(bf16), vw_all[:, gs]) / den_w

        def gcol(br):
            return jnp.concatenate(
                [gates[:, (NSA_GROUP * g + r) * 3 + br:(NSA_GROUP * g + r) * 3 + br + 1] for r in range(NSA_GROUP)], axis=0)

        o = gcol(0) * o_c + gcol(1) * o_s + gcol(2) * o_w
        pieces += [o[r * tq:(r + 1) * tq] for r in range(NSA_GROUP)]
    o_ref[...] = jnp.concatenate(pieces, axis=1).astype(bf16)


def _nsa(q, kvsw3, kc3, vc3, gates, selmap, expand, seq):
    n = q.shape[0]
    bsz = n // seq
    tq = _TQ
    nq = seq // tq
    return pl.pallas_call(
        functools.partial(_nsa_kernel, seq=seq),
        grid=(bsz, nq),
        in_specs=[
            pl.BlockSpec((tq, NSA_WIDTH), lambda b, i: (b * nq + i, 0)),
            pl.BlockSpec((1, seq, 4 * KV_WIDTH), lambda b, i: (b, 0, 0)),
            pl.BlockSpec((1,) + kc3.shape[1:], lambda b, i: (b, 0, 0)),
            pl.BlockSpec((1,) + vc3.shape[1:], lambda b, i: (b, 0, 0)),
            pl.BlockSpec((tq, GATE_PAD), lambda b, i: (b * nq + i, 0)),
            pl.BlockSpec(selmap.shape, lambda b, i: (0, 0)),
            pl.BlockSpec(expand.shape, lambda b, i: (0, 0)),
        ],
        out_specs=pl.BlockSpec((tq, NSA_WIDTH), lambda b, i: (b * nq + i, 0)),
        out_shape=jax.ShapeDtypeStruct((n, NSA_WIDTH), bf16),
        compiler_params=_cparams(("arbitrary", "arbitrary")),
        name="nsa",
    )(q, kvsw3, kc3, vc3, gates, selmap, expand)


def _hgrn_kernel(hq_ref, hf_ref, hi_ref, hg_ref, lbl_ref, ng_ref, o_ref, st_ref, *, seq, layer):
    c = HGRN_CHUNK
    logits = lbl_ref[...]
    e = jnp.exp(logits - jnp.max(logits, axis=0, keepdims=True))
    sm = e / jnp.sum(e, axis=0, keepdims=True)
    lb = sm[0:1]
    for l in range(1, layer + 1):
        lb = lb + sm[l:l + 1]
    st_ref[...] = jnp.zeros_like(st_ref)
    r_i = lax.broadcasted_iota(i32, (c, c), 0)
    c_i = lax.broadcasted_iota(i32, (c, c), 1)
    causal = r_i >= c_i
    tri = jnp.where(causal, 1.0, 0.0)
    ng = ng_ref[...]

    def body(ci, carry):
        r0 = pl.multiple_of(ci * c, c)
        hq = hq_ref[pl.ds(r0, c), :].astype(f32)
        hf = hf_ref[pl.ds(r0, c), :]
        v = hi_ref[pl.ds(r0, c), :]
        hg = hg_ref[pl.ds(r0, c), :].astype(f32)
        f = lb + (1.0 - lb) * jax.nn.sigmoid(hf)
        k = 1.0 - f
        qv = hq * jax.nn.sigmoid(hq)
        bcum = _dot(tri, jnp.log(f), precision=_HIGHEST)
        bend = bcum[c - 1:c, :]
        q_dec = (qv * jnp.exp(bcum)).astype(bf16)
        k_inv = (k * jnp.exp(-bcum)).astype(bf16)
        k_end = (k * jnp.exp(bend - bcum)).astype(bf16)
        dec_end = jnp.exp(bend)
        outs = []
        for h in range(HGRN_HEADS):
            sl = slice(h * HGRN_EXPAND, (h + 1) * HGRN_EXPAND)
            a = jnp.where(causal, _dot_nt(q_dec[:, sl], k_inv[:, sl]), 0.0)
            st = st_ref[h]
            o = _dot(a.astype(bf16), v[:, sl]) + _dot_nt(q_dec[:, sl], st.astype(bf16))
            upd = lax.dot_general(v[:, sl], k_end[:, sl], (((0,), (0,)), ((), ())), preferred_element_type=f32)
            st_ref[h] = st * dec_end[:, sl] + upd
            y = _rms(o, ng) * (hg[:, sl] * jax.nn.sigmoid(hg[:, sl]))
            outs.append(y)
        o_ref[pl.ds(r0, c), :] = jnp.concatenate(outs, axis=1).astype(bf16)
        return carry

    lax.fori_loop(0, seq // c, body, 0)


def _hgrn(hq, hf, hi, hg, lb_logits, norm_g, seq, layer):
    n = hq.shape[0]
    bsz = n // seq
    per_b = pl.BlockSpec((seq, HGRN_WIDTH), lambda b: (b, 0))
    return pl.pallas_call(
        functools.partial(_hgrn_kernel, seq=seq, layer=layer),
        grid=(bsz,),
        in_specs=[per_b, per_b, per_b, per_b,
                  pl.BlockSpec(lb_logits.shape, lambda b: (0, 0)),
                  pl.BlockSpec(norm_g.shape, lambda b: (0, 0))],
        out_specs=per_b,
        out_shape=jax.ShapeDtypeStruct((n, HGRN_WIDTH), bf16),
        scratch_shapes=[pltpu.VMEM((HGRN_HEADS, HGRN_HEAD_V, HGRN_EXPAND), f32)],
        compiler_params=_cparams(("arbitrary",)),
        name="hgrn",
    )(hq, hf, hi, hg, lb_logits, norm_g)


def _merge_kernel(on_ref, oh_ref, mg_ref, x_ref, mod_ref, g_ref, wb0_ref, wb1_ref, wo_ref, x1_ref, h2_ref):
    a = _dot(on_ref[...], wb0_ref[...])
    b = _dot(oh_ref[...], wb1_ref[...])
    mg = mg_ref[...].astype(f32)
    y = mg[:, :D_MODEL] * a + mg[:, D_MODEL:] * b
    y2 = _dot(y.astype(bf16), wo_ref[...])
    x1 = x_ref[...] + mod_ref[0, 2:3, :] * y2
    x1_ref[...] = x1
    h2_ref[...] = _rms(x1, g_ref[...]) * (1.0 + mod_ref[0, 4:5, :]) + mod_ref[0, 3:4, :]


def _merge(o_nsa, o_hgrn, mg, x2, mod3, g_ffn, wb0, wb1, wo, seq):
    n = x2.shape[0]
    tm = 256
    tiles_per_seq = seq // tm
    row = lambda i: (i, 0)
    full = lambda a: pl.BlockSpec(a.shape, lambda i: (0,) * a.ndim)
    return pl.pallas_call(
        _merge_kernel,
        grid=(n // tm,),
        in_specs=[
            pl.BlockSpec((tm, NSA_WIDTH), row), pl.BlockSpec((tm, HGRN_WIDTH), row),
            pl.BlockSpec((tm, 2 * D_MODEL), row), pl.BlockSpec((tm, D_MODEL), row),
            pl.BlockSpec((1, 6, D_MODEL), lambda i: (i // tiles_per_seq, 0, 0)),
            full(g_ffn), full(wb0), full(wb1), full(wo),
        ],
        out_specs=[pl.BlockSpec((tm, D_MODEL), row)] * 2,
        out_shape=[jax.ShapeDtypeStruct((n, D_MODEL), f32)] * 2,
        compiler_params=_cparams(("arbitrary",)),
        name="merge",
    )(o_nsa, o_hgrn, mg, x2, mod3, g_ffn, wb0, wb1, wo)


_TR = 256


def _topk_rows(s, k):
    n = s.shape[0]
    rowid = lax.broadcasted_iota(i32, s.shape, 0)
    vals, idxs = [], []
    for _ in range(k):
        m = jnp.max(s, axis=0, keepdims=True)
        i = jnp.min(jnp.where(s == m, rowid, n), axis=0, keepdims=True)
        vals.append(m)
        idxs.append(i)
        s = jnp.where(rowid == i, -jnp.inf, s)
    return vals, idxs


def _row_order(c):
    s, i = divmod(c, 16)
    return i + 64 * (s >= 4) + 32 * ((s % 4) >= 2) + 16 * (s % 2)


def _route_kernel(h_ref, wq_ref, sk_ref, idx_ref, gate_ref):
    h = h_ref[...].astype(bf16)
    tr = h.shape[0]
    k = PEER_TOPK
    half = PEER_QDIM // 2
    experts, gates = [], []
    for hd in range(PEER_HEADS):
        tops = []
        for p in range(2):
            grp = hd * 2 + p
            q_t = _dot_nt(wq_ref[grp * half:(grp + 1) * half, :], h)
            s_t = _dot(sk_ref[grp], q_t.astype(bf16))
            tops.append(_topk_rows(s_t, k))
        (v1, i1), (v2, i2) = tops
        s2 = jnp.concatenate(v2, axis=0)
        e2 = jnp.concatenate(i2, axis=0)
        comb = jnp.concatenate([v1[a] + s2 for a in range(k)], axis=0)
        cand = jnp.concatenate([i1[a] * PEER_NKEYS + e2 for a in range(k)], axis=0)
        rowid = lax.broadcasted_iota(i32, comb.shape, 0)
        tv, tp = _topk_rows(comb, k)
        for r in range(k):
            experts.append(jnp.sum(jnp.where(rowid == tp[r], cand, 0), axis=0, keepdims=True))
        ex = [jnp.exp(tv[r] - tv[0]) for r in range(k)]
        den = ex[0]
        for r in range(1, k):
            den = den + ex[r]
        gates += [e / den for e in ex]
    ncol = PEER_HEADS * k
    inv = [0] * ncol
    for c in range(ncol):
        inv[_row_order(c)] = c
    idx_t = jnp.concatenate([experts[inv[j]] for j in range(ncol)], axis=0)
    gate_t = jnp.concatenate(gates, axis=0)
    for blk in range(tr // LANES):
        sl = slice(blk * LANES, (blk + 1) * LANES)
        idx_ref[sl, :] = idx_t[:, sl].T
        gate_ref[sl, :] = gate_t[:, sl].T


def _route(h2, wq_t, sk):
    n = h2.shape[0]
    tr = _TR
    ncol = PEER_HEADS * PEER_TOPK
    return pl.pallas_call(
        _route_kernel,
        grid=(n // tr,),
        in_specs=[
            pl.BlockSpec((tr, D_MODEL), lambda i: (i, 0)),
            pl.BlockSpec(wq_t.shape, lambda i: (0, 0)),
            pl.BlockSpec(sk.shape, lambda i: (0, 0, 0)),
        ],
        out_specs=[pl.BlockSpec((tr, ncol), lambda i: (i, 0))] * 2,
        out_shape=[jax.ShapeDtypeStruct((n, ncol), i32), jax.ShapeDtypeStruct((n, ncol), f32)],
        compiler_params=_cparams(("arbitrary",)),
        name="route",
    )(h2, wq_t, sk)


_TS = 64
_TG = 8
_NROW = PEER_HEADS * PEER_TOPK
_ROW_TILES = 2 * D_MODEL // LANES


def _fold(p, d, sub):
    n = p.shape[0] // 2
    a, b = p[:n], p[n:]
    low = (sub & (2 * d - 1)) < d
    return jnp.where(low, a + pltpu.roll(a, SUBLANES - d, 1), b + pltpu.roll(b, d, 1))


def _expert_kernel(idx_ref, gate_ref, h_ref, x1_ref, mod_ref, gfin_ref, uv_ref, o_ref, buf, sem):
    n_groups = _TS // _TG
    n_dma = _TG * _NROW

    def row_copy(e, slot, j):
        return pltpu.make_async_copy(uv_ref.at[e], buf.at[slot, j], sem.at[slot])

    def issue(grp, slot):
        def body(t, c):
            tok = grp * _TG + t
            for k in range(_NROW):
                row_copy(idx_ref[tok, k], slot, t * _NROW + k).start(priority=k % 2)
            return c
        lax.fori_loop(0, _TG, body, 0)

    def wait(slot):
        def body(j, c):
            row_copy(0, slot, j).wait()
            return c
        lax.fori_loop(0, n_dma, body, 0, unroll=8)

    issue(0, 0)
    sub = lax.broadcasted_iota(i32, (1, SUBLANES, LANES), 1)
    gt_f = mod_ref[0, 5]
    gfin = gfin_ref[...]

    def group(grp, carry):
        slot = grp % 2

        @pl.when(grp + 1 < n_groups)
        def _():
            issue(grp + 1, 1 - slot)

        wait(slot)
        for t in range(_TG):
            tok = grp * _TG + t
            rows = buf[slot, pl.ds(t * _NROW, _NROW)]
            u = rows[:, 0:SUBLANES, :]
            v = rows[:, SUBLANES:2 * SUBLANES, :]
            z = u * h_ref[tok][None]
            z = _fold(z, 4, sub)
            z = _fold(z, 2, sub)
            z = _fold(z, 1, sub)
            act = _gelu(jnp.sum(z, axis=-1, keepdims=True))
            g_tok = gate_ref[tok]
            acc = [None] * 4
            for i in range(16):
                w_i = g_tok[:, i:i + 1] * act[i]
                for s in range(SUBLANES):
                    w = jnp.broadcast_to(w_i[s:s + 1, :], (SUBLANES, LANES))
                    term = w * v[_row_order(s * 16 + i)]
                    a = s % 4
                    acc[a] = term if acc[a] is None else acc[a] + term
            y = (acc[0] + acc[1]) + (acc[2] + acc[3])
            x2 = x1_ref[tok] + gt_f * y
            ms = jnp.sum(jnp.sum(x2 * x2, axis=1, keepdims=True), axis=0, keepdims=True) * (1.0 / D_MODEL)
            o_ref[tok] = x2 * lax.rsqrt(ms + EPS) * gfin
        return carry

    lax.fori_loop(0, n_groups, group, 0)


def _experts(idx, gate3, h3, x13, mod4, gfin, uv, seq):
    n = idx.shape[0]
    steps_per_seq = seq // _TS
    tok3 = lambda i: (i, 0, 0)
    return pl.pallas_call(
        _expert_kernel,
        grid=(n // _TS,),
        in_specs=[
            pl.BlockSpec((_TS, _NROW), lambda i: (i, 0), memory_space=pltpu.SMEM),
            pl.BlockSpec((_TS, SUBLANES, 16), tok3),
            pl.BlockSpec((_TS, SUBLANES, LANES), tok3),
            pl.BlockSpec((_TS, SUBLANES, LANES), tok3),
            pl.BlockSpec((1, 6, SUBLANES, LANES), lambda i: (i // steps_per_seq, 0, 0, 0)),
            pl.BlockSpec((SUBLANES, LANES), lambda i: (0, 0)),
            pl.BlockSpec(memory_space=pl.ANY),
        ],
        out_specs=pl.BlockSpec((_TS, SUBLANES, LANES), tok3),
        out_shape=jax.ShapeDtypeStruct((n, SUBLANES, LANES), f32),
        scratch_shapes=[pltpu.VMEM((2, _TG * _NROW, _ROW_TILES, LANES), f32), pltpu.SemaphoreType.DMA((2,))],
        compiler_params=_cparams(("arbitrary",)),
        name="experts",
    )(idx, gate3, h3, x13, mod4, gfin, uv)


def _rope_tables(seq):
    half = HEAD_DIM // 2
    inv = ROPE_THETA ** (-np.arange(half, dtype=np.float32) / half)
    ang = np.arange(seq, dtype=np.float32)[:, None] * inv[None, :].astype(np.float32)
    cos = np.cos(ang).astype(np.float32)
    sin = np.sin(ang).astype(np.float32)
    cos_t = np.tile(np.concatenate([cos, cos], axis=1), (1, NSA_HEADS))
    sin_t = np.tile(np.concatenate([-sin, sin], axis=1), (1, NSA_HEADS))
    return jnp.asarray(cos_t), jnp.asarray(sin_t)


def _selection_map(n_cmp_pad, n_sel):
    r_sel = SEL_BLOCK // CMP_STRIDE
    r_cmp = CMP_BLOCK // CMP_STRIDE
    i = np.arange(n_cmp_pad)[:, None]
    j = np.arange(n_sel)[None, :]
    d = i - r_sel * j
    cnt = np.minimum(d, r_sel - 1) - np.maximum(d - r_cmp + 1, 0) + 1
    cnt = np.clip(cnt, 0, None).astype(np.float32)
    cnt[n_cmp_pad - r_cmp + 1:] = 0.0
    return jnp.asarray(cnt)


def _block_expand(n_sel, seq):
    e = (np.arange(seq)[None, :] // SEL_BLOCK == np.arange(n_sel)[:, None]).astype(np.float32)
    return jnp.asarray(e, dtype=bf16)


def _compress_weights(w1):
    eye = jnp.eye(NSA_KV_GROUPS, dtype=w1.dtype)
    out = []
    for part in range(CMP_BLOCK // CMP_STRIDE):
        w = w1[part * CMP_STRIDE * HEAD_DIM:(part + 1) * CMP_STRIDE * HEAD_DIM].reshape(CMP_STRIDE, HEAD_DIM, CMP_HIDDEN)
        big = jnp.einsum('pdc,gh->pgdhc', w, eye).reshape(CMP_STRIDE * KV_WIDTH, NSA_KV_GROUPS * CMP_HIDDEN)
        out.append(big.astype(bf16))
    return out


def kernel(x, c, w_ada, b_ada, g_mix, g_ffn, w_in, cmp_pos_k, cmp_pos_v, w_ck1, w_ck2, w_cv1, w_cv2, hgrn_lb_logits, hgrn_out_norm, w_branch, w_out, w_peer_q, peer_sub_keys, peer_u, peer_v, g_final):
    bsz, seq, d = x.shape
    n = bsz * seq
    depth = w_ada.shape[0]
    assert depth == 1, "single-layer block only"
    n_sel = seq // SEL_BLOCK
    n_pieces = seq // CMP_STRIDE
    cos_t, sin_t = _rope_tables(seq)
    selmap = _selection_map(n_pieces, n_sel)
    expand = _block_expand(n_sel, seq)
    xcur = x.reshape(n, d)
    for l in range(depth):
        mod = _adaln(c, w_ada[l], b_ada[l].reshape(1, 6 * d))
        mod3 = mod.reshape(bsz, 6, d)
        w_pad = jnp.concatenate(
            [w_in[l][:, :_C_GL + 3 * NSA_HEADS], jnp.zeros((d, GATE_PAD - 3 * NSA_HEADS), w_in.dtype),
             w_in[l][:, _C_GL + 3 * NSA_HEADS:]], axis=1).astype(bf16)
        q, kc, vc, kvsw, gates, hq, hf, hi, hg, mg = _inproj(xcur, mod3, g_mix[l].reshape(1, d), w_pad, cos_t, sin_t, seq)
        wkt, wkb = _compress_weights(w_ck1[l])
        wvt, wvb = _compress_weights(w_cv1[l])
        kc_c, vc_c = _compress(
            kc.reshape(bsz, n_pieces, CMP_STRIDE * KV_WIDTH), vc.reshape(bsz, n_pieces, CMP_STRIDE * KV_WIDTH),
            wkt, wkb, wvt, wvb, cmp_pos_k[l].reshape(1, -1), cmp_pos_v[l].reshape(1, -1),
            w_ck1[l], w_cv1[l], w_ck2[l], w_cv2[l])
        o_nsa = _nsa(q, kvsw.reshape(bsz, seq, 4 * KV_WIDTH), kc_c, vc_c, gates, selmap, expand, seq)
        o_hgrn = _hgrn(hq, hf, hi, hg, hgrn_lb_logits, hgrn_out_norm[l].reshape(1, -1), seq, l)
        x1, h2 = _merge(o_nsa, o_hgrn, mg, xcur, mod3, g_ffn[l].reshape(1, d),
                        w_branch[l, 0].astype(bf16), w_branch[l, 1].astype(bf16), w_out[l].astype(bf16), seq)
        wq_t = w_peer_q[l].T.astype(bf16)
        sk = jnp.transpose(peer_sub_keys[l], (1, 0, 2, 3)).reshape(2 * PEER_HEADS, PEER_NKEYS, PEER_QDIM // 2).astype(bf16)
        idx, gate = _route(h2, wq_t, sk)
        uv = jnp.concatenate([peer_u[l], peer_v[l]], axis=1).reshape(-1, _ROW_TILES, LANES)
        out3 = _experts(idx, gate.reshape(n, SUBLANES, 16), h2.reshape(n, SUBLANES, LANES), x1.reshape(n, SUBLANES, LANES),
                        mod.reshape(bsz, 6, SUBLANES, LANES), g_final.reshape(SUBLANES, LANES), uv, seq)
        xcur = out3.reshape(n, d)
    return xcur.reshape(bsz, seq, d)
```

```python
import functools

import numpy as np
import jax
import jax.numpy as jnp
from jax import lax
from jax.experimental import pallas as pl
from jax.experimental.pallas import tpu as pltpu

f32 = jnp.float32
bf16 = jnp.bfloat16
i32 = jnp.int32
_HIGHEST = lax.Precision.HIGHEST

D_MODEL = 1024
EPS = 1e-6
HEAD_DIM = 64
ROPE_THETA = 10000.0
NSA_HEADS = 8
NSA_KV_GROUPS = 2
NSA_GROUP = NSA_HEADS // NSA_KV_GROUPS
NSA_WIDTH = NSA_HEADS * HEAD_DIM
KV_WIDTH = NSA_KV_GROUPS * HEAD_DIM
CMP_BLOCK = 32
CMP_STRIDE = 16
CMP_HIDDEN = 128
SEL_BLOCK = 64
SEL_TOPK = 16
WINDOW = 512
FORCE_BONUS = 1000.0
HGRN_HEADS = 4
HGRN_EXPAND = 128
HGRN_HEAD_V = 128
HGRN_WIDTH = HGRN_HEADS * HGRN_EXPAND
HGRN_CHUNK = 32
PEER_HEADS = 8
PEER_NKEYS = 128
PEER_QDIM = 256
PEER_TOPK = 16
GATE_PAD = 128

LANES = 128
SUBLANES = 8
VMEM_LIMIT = 56 * 1024 * 1024

_NEG = -1e30


def _cparams(sem):
    return pltpu.CompilerParams(dimension_semantics=sem, vmem_limit_bytes=VMEM_LIMIT)


def _gelu(x):
    return 0.5 * x * (1.0 + jnp.tanh(0.7978845608028654 * (x + 0.044715 * (x * x * x))))


def _dot(a, b, **kw):
    return jnp.dot(a, b, preferred_element_type=f32, **kw)


def _dot_nt(a, b):
    return lax.dot_general(a, b, (((1,), (1,)), ((), ())), preferred_element_type=f32)


def _rms(x, g):
    return x * lax.rsqrt(jnp.mean(x * x, axis=-1, keepdims=True) + EPS) * g


def _adaln_kernel(c_ref, w_ref, b_ref, o_ref):
    c = c_ref[...]
    cs = c * jax.nn.sigmoid(c)
    o_ref[...] = _dot(cs, w_ref[...], precision=_HIGHEST) + b_ref[...]


def _adaln(c, w, b):
    bsz = c.shape[0]
    tn = 512
    return pl.pallas_call(
        _adaln_kernel,
        grid=(6 * D_MODEL // tn,),
        in_specs=[
            pl.BlockSpec((bsz, D_MODEL), lambda j: (0, 0)),
            pl.BlockSpec((D_MODEL, tn), lambda j: (0, j)),
            pl.BlockSpec((1, tn), lambda j: (0, j)),
        ],
        out_specs=pl.BlockSpec((bsz, tn), lambda j: (0, j)),
        out_shape=jax.ShapeDtypeStruct((bsz, 6 * D_MODEL), f32),
        compiler_params=_cparams(("arbitrary",)),
        name="adaln",
    )(c, w, b)


_C_Q = 0
_C_KC = 512
_C_VC = 640
_C_KS = 768
_C_VS = 896
_C_KW = 1024
_C_VW = 1152
_C_GL = 1280
_C_HQ = _C_GL + GATE_PAD
_C_HF = _C_HQ + 512
_C_HI = _C_HF + 512
_C_HG = _C_HI + 512
_C_MG = _C_HG + 512
_C_END = _C_MG + 2 * D_MODEL


def _inproj_kernel(x_ref, mod_ref, g_ref, w_ref, cos_ref, sin_ref,
                   q_ref, kc_ref, vc_ref, kvsw_ref, gate_ref, hq_ref, hf_ref, hi_ref, hg_ref, mg_ref):
    x = x_ref[...]
    sh = mod_ref[0, 0:1, :]
    sc = mod_ref[0, 1:2, :]
    h = (_rms(x, g_ref[...]) * (1.0 + sc) + sh).astype(bf16)

    def mm(c0, c1):
        return _dot(h, w_ref[:, c0:c1])

    cos = cos_ref[...]
    sin = sin_ref[...]

    def rope(a):
        width = a.shape[1]
        first = (lax.broadcasted_iota(i32, a.shape, 1) & (HEAD_DIM - 1)) < (HEAD_DIM // 2)
        partner = jnp.where(first, pltpu.roll(a, width - HEAD_DIM // 2, 1), pltpu.roll(a, HEAD_DIM // 2, 1))
        return a * cos[:, :width] + partner * sin[:, :width]

    q_ref[...] = (rope(mm(_C_Q, _C_KC)) * (HEAD_DIM ** -0.5)).astype(bf16)
    kc_ref[...] = rope(mm(_C_KC, _C_VC)).astype(bf16)
    vc_ref[...] = mm(_C_VC, _C_KS).astype(bf16)
    kvsw_ref[:, 0:128] = rope(mm(_C_KS, _C_VS)).astype(bf16)
    kvsw_ref[:, 128:256] = mm(_C_VS, _C_KW).astype(bf16)
    kvsw_ref[:, 256:384] = rope(mm(_C_KW, _C_VW)).astype(bf16)
    kvsw_ref[:, 384:512] = mm(_C_VW, _C_GL).astype(bf16)
    gate_ref[...] = jax.nn.sigmoid(mm(_C_GL, _C_HQ))
    hq_ref[...] = mm(_C_HQ, _C_HF).astype(bf16)
    hf_ref[...] = mm(_C_HF, _C_HI)
    hi_ref[...] = mm(_C_HI, _C_HG).astype(bf16)
    hg_ref[...] = mm(_C_HG, _C_MG).astype(bf16)
    mg_ref[...] = jax.nn.sigmoid(mm(_C_MG, _C_END)).astype(bf16)


def _inproj(x2, mod3, g_mix, w_pad, cos_t, sin_t, seq):
    n = x2.shape[0]
    tm = 256
    tiles_per_seq = seq // tm
    row = lambda i: (i, 0)
    outs = [
        (NSA_WIDTH, bf16), (KV_WIDTH, bf16), (KV_WIDTH, bf16), (4 * KV_WIDTH, bf16), (GATE_PAD, f32),
        (HGRN_WIDTH, bf16), (HGRN_WIDTH, f32), (HGRN_WIDTH, bf16), (HGRN_WIDTH, bf16), (2 * D_MODEL, bf16),
    ]
    return pl.pallas_call(
        _inproj_kernel,
        grid=(n // tm,),
        in_specs=[
            pl.BlockSpec((tm, D_MODEL), row),
            pl.BlockSpec((1, 6, D_MODEL), lambda i: (i // tiles_per_seq, 0, 0)),
            pl.BlockSpec((1, D_MODEL), lambda i: (0, 0)),
            pl.BlockSpec((D_MODEL, _C_END), lambda i: (0, 0)),
            pl.BlockSpec((tm, NSA_WIDTH), lambda i: (i % tiles_per_seq, 0)),
            pl.BlockSpec((tm, NSA_WIDTH), lambda i: (i % tiles_per_seq, 0)),
        ],
        out_specs=[pl.BlockSpec((tm, w), row) for w, _ in outs],
        out_shape=[jax.ShapeDtypeStruct((n, w), dt) for w, dt in outs],
        compiler_params=_cparams(("arbitrary",)),
        name="inproj",
    )(x2, mod3, g_mix, w_pad, cos_t, sin_t)


def _compress_kernel(kc_ref, vc_ref, wkt_ref, wkb_ref, wvt_ref, wvb_ref, pk_ref, pv_ref,
                     w1k_ref, w1v_ref, w2k_ref, w2v_ref, okc_ref, ovc_ref):
    def one(x_ref, wt_ref, wb_ref, pos_ref, w1_ref, w2_ref, o_ref):
        pieces = x_ref[0]
        top = _dot(pieces, wt_ref[...])
        bot = _dot(pieces, wb_ref[...])
        nrow = bot.shape[0]
        bot = pltpu.roll(bot, nrow - 1, 0)
        cpos = _dot(pos_ref[...], w1_ref[...], precision=_HIGHEST)
        w2 = w2_ref[...].astype(bf16)
        outs = []
        for g in range(NSA_KV_GROUPS):
            sl = slice(g * CMP_HIDDEN, (g + 1) * CMP_HIDDEN)
            hid = _gelu(top[:, sl] + bot[:, sl] + cpos)
            outs.append(_dot(hid.astype(bf16), w2))
        o_ref[0] = jnp.concatenate(outs, axis=1).astype(bf16)

    one(kc_ref, wkt_ref, wkb_ref, pk_ref, w1k_ref, w2k_ref, okc_ref)
    one(vc_ref, wvt_ref, wvb_ref, pv_ref, w1v_ref, w2v_ref, ovc_ref)


def _compress(kc3, vc3, wkt, wkb, wvt, wvb, pk, pv, w1k, w1v, w2k, w2v):
    bsz, npieces, width = kc3.shape
    full = lambda a: pl.BlockSpec(a.shape, lambda b: (0,) * a.ndim)
    per_b = pl.BlockSpec((1, npieces, width), lambda b: (b, 0, 0))
    out_b = pl.BlockSpec((1, npieces, KV_WIDTH), lambda b: (b, 0, 0))
    return pl.pallas_call(
        _compress_kernel,
        grid=(bsz,),
        in_specs=[per_b, per_b] + [full(a) for a in (wkt, wkb, wvt, wvb, pk, pv, w1k, w1v, w2k, w2v)],
        out_specs=[out_b, out_b],
        out_shape=[jax.ShapeDtypeStruct((bsz, npieces, KV_WIDTH), bf16)] * 2,
        compiler_params=_cparams(("arbitrary",)),
        name="compress",
    )(kc3, vc3, wkt, wkb, wvt, wvb, pk, pv, w1k, w1v, w2k, w2v)


_TQ = 128
_NSA_CLASSES = 4


def _softmax_parts(s, mask):
    sm = jnp.where(mask, s, _NEG)
    m = jnp.max(sm, axis=-1, keepdims=True)
    p = jnp.where(mask, jnp.exp(sm - m), 0.0)
    den = jnp.maximum(jnp.sum(p, axis=-1, keepdims=True), 1e-30)
    return p, den


def _nsa_kernel(q_ref, kvsw_ref, kc_ref, vc_ref, gate_ref, selmap_ref, expand_ref, o_ref, *, seq):
    tq = _TQ
    qi = pl.program_id(1)
    n_cls = _NSA_CLASSES
    per_cls = (seq // tq) // n_cls
    for cls in range(n_cls):
        @pl.when((qi >= cls * per_cls) & (qi < (cls + 1) * per_cls))
        def _(cls=cls):
            _nsa_tile(q_ref, kvsw_ref, kc_ref, vc_ref, gate_ref, selmap_ref, expand_ref, o_ref,
                      seq=seq, width=(cls + 1) * (seq // n_cls))


def _nsa_tile(q_ref, kvsw_ref, kc_ref, vc_ref, gate_ref, selmap_ref, expand_ref, o_ref, *, seq, width):
    tq = _TQ
    rows = NSA_GROUP * tq
    n_sel = seq // SEL_BLOCK
    n_cmp_pad = seq // CMP_STRIDE
    t0 = pl.program_id(1) * tq
    q = q_ref[...]
    gates = gate_ref[...]
    tpos = t0 + lax.broadcasted_iota(i32, (tq, 1), 0)
    t4 = t0 + (lax.broadcasted_iota(i32, (rows, 1), 0) & (tq - 1))
    wk = WINDOW + tq
    ws = pl.multiple_of(jnp.maximum(t0 - WINDOW, 0), tq)
    ks_all = kvsw_ref[0, 0:width, 0:128]
    vs_all = kvsw_ref[0, 0:width, 128:256]
    kw_all = kvsw_ref[0, pl.ds(ws, wk), 256:384]
    vw_all = kvsw_ref[0, pl.ds(ws, wk), 384:512]
    kc_all = kc_ref[0]
    vc_all = vc_ref[0]
    cend = lax.broadcasted_iota(i32, (1, n_cmp_pad), 1) * CMP_STRIDE + (CMP_BLOCK - 1)
    blk = lax.broadcasted_iota(i32, (1, n_sel), 1)
    kpos = lax.broadcasted_iota(i32, (1, width), 1)
    kpos_w = ws + lax.broadcasted_iota(i32, (1, wk), 1)
    cur = tpos >> 6
    forced = (blk == 0) | (blk == cur) | (blk == cur - 1)
    causal_blk = blk * SEL_BLOCK <= tpos
    pieces = []
    for g in range(NSA_KV_GROUPS):
        gs = slice(g * HEAD_DIM, (g + 1) * HEAD_DIM)
        qg = jnp.concatenate(
            [q[:, (NSA_GROUP * g + r) * HEAD_DIM:(NSA_GROUP * g + r + 1) * HEAD_DIM] for r in range(NSA_GROUP)], axis=0)
        p_c, den_c = _softmax_parts(_dot_nt(qg, kc_all[:, gs]), cend <= t4)
        o_c = _dot(p_c.astype(bf16), vc_all[:, gs]) / den_c
        pn = p_c / den_c
        pc_sum = pn[0:tq]
        for r in range(1, NSA_GROUP):
            pc_sum = pc_sum + pn[r * tq:(r + 1) * tq]
        imp = _dot(pc_sum, selmap_ref[...], precision=_HIGHEST)
        imp = jnp.where(forced, imp + FORCE_BONUS, imp)
        imp = jnp.where(causal_blk, imp, -1.0)
        rank = jnp.zeros((tq, n_sel), f32)
        for j in range(n_sel):
            col = imp[:, j:j + 1]
            ahead = jnp.where(col > imp, 1.0, jnp.where(col == imp, jnp.where(blk > j, 1.0, 0.0), 0.0))
            rank = rank + ahead
        sel = jnp.where(rank < float(min(SEL_TOPK, n_sel)), 1.0, 0.0).astype(bf16)
        sel4 = jnp.concatenate([sel] * NSA_GROUP, axis=0)
        selx = _dot(sel4, expand_ref[:, 0:width])
        m_s = jnp.where(kpos <= t4, selx, 0.0) > 0.5
        p_s, den_s = _softmax_parts(_dot_nt(qg, ks_all[:, gs]), m_s)
        o_s = _dot(p_s.astype(bf16), vs_all[:, gs]) / den_s
        dist = t4 - kpos_w
        m_w = jnp.where(dist >= 0, jnp.where(dist < WINDOW, 1.0, 0.0), 0.0) > 0.5
        p_w, den_w = _softmax_parts(_dot_nt(qg, kw_all[:, gs]), m_w)
        o_w = _dot(p_w.astype(bf16), vw_all[:, gs]) / den_w

        def gcol(br):
            return jnp.concatenate(
                [gates[:, (NSA_GROUP * g + r) * 3 + br:(NSA_GROUP * g + r) * 3 + br + 1] for r in range(NSA_GROUP)], axis=0)

        o = gcol(0) * o_c + gcol(1) * o_s + gcol(2) * o_w
        pieces += [o[r * tq:(r + 1) * tq] for r in range(NSA_GROUP)]
    o_ref[...] = jnp.concatenate(pieces, axis=1).astype(bf16)


def _nsa(q, kvsw3, kc3, vc3, gates, selmap, expand, seq):
    n = q.shape[0]
    bsz = n // seq
    tq = _TQ
    nq = seq // tq
    return pl.pallas_call(
        functools.partial(_nsa_kernel, seq=seq),
        grid=(bsz, nq),
        in_specs=[
            pl.BlockSpec((tq, NSA_WIDTH), lambda b, i: (b * nq + i, 0)),
            pl.BlockSpec((1, seq, 4 * KV_WIDTH), lambda b, i: (b, 0, 0)),
            pl.BlockSpec((1,) + kc3.shape[1:], lambda b, i: (b, 0, 0)),
            pl.BlockSpec((1,) + vc3.shape[1:], lambda b, i: (b, 0, 0)),
            pl.BlockSpec((tq, GATE_PAD), lambda b, i: (b * nq + i, 0)),
            pl.BlockSpec(selmap.shape, lambda b, i: (0, 0)),
            pl.BlockSpec(expand.shape, lambda b, i: (0, 0)),
        ],
        out_specs=pl.BlockSpec((tq, NSA_WIDTH), lambda b, i: (b * nq + i, 0)),
        out_shape=jax.ShapeDtypeStruct((n, NSA_WIDTH), bf16),
        compiler_params=_cparams(("arbitrary", "arbitrary")),
        name="nsa",
    )(q, kvsw3, kc3, vc3, gates, selmap, expand)


def _hgrn_kernel(hq_ref, hf_ref, hi_ref, hg_ref, lbl_ref, ng_ref, o_ref, st_ref, *, seq, layer):
    c = HGRN_CHUNK
    logits = lbl_ref[...]
    e = jnp.exp(logits - jnp.max(logits, axis=0, keepdims=True))
    sm = e / jnp.sum(e, axis=0, keepdims=True)
    lb = sm[0:1]
    for l in range(1, layer + 1):
        lb = lb + sm[l:l + 1]
    st_ref[...] = jnp.zeros_like(st_ref)
    r_i = lax.broadcasted_iota(i32, (c, c), 0)
    c_i = lax.broadcasted_iota(i32, (c, c), 1)
    causal = r_i >= c_i
    tri = jnp.where(causal, 1.0, 0.0)
    ng = ng_ref[...]

    def body(ci, carry):
        r0 = pl.multiple_of(ci * c, c)
        hq = hq_ref[pl.ds(r0, c), :].astype(f32)
        hf = hf_ref[pl.ds(r0, c), :]
        v = hi_ref[pl.ds(r0, c), :]
        hg = hg_ref[pl.ds(r0, c), :].astype(f32)
        f = lb + (1.0 - lb) * jax.nn.sigmoid(hf)
        k = 1.0 - f
        qv = hq * jax.nn.sigmoid(hq)
        bcum = _dot(tri, jnp.log(f), precision=_HIGHEST)
        bend = bcum[c - 1:c, :]
        q_dec = (qv * jnp.exp(bcum)).astype(bf16)
        k_inv = (k * jnp.exp(-bcum)).astype(bf16)
        k_end = (k * jnp.exp(bend - bcum)).astype(bf16)
        dec_end = jnp.exp(bend)
        outs = []
        for h in range(HGRN_HEADS):
            sl = slice(h * HGRN_EXPAND, (h + 1) * HGRN_EXPAND)
            a = jnp.where(causal, _dot_nt(q_dec[:, sl], k_inv[:, sl]), 0.0)
            st = st_ref[h]
            o = _dot(a.astype(bf16), v[:, sl]) + _dot_nt(q_dec[:, sl], st.astype(bf16))
            upd = lax.dot_general(v[:, sl], k_end[:, sl], (((0,), (0,)), ((), ())), preferred_element_type=f32)
            st_ref[h] = st * dec_end[:, sl] + upd
            y = _rms(o, ng) * (hg[:, sl] * jax.nn.sigmoid(hg[:, sl]))
            outs.append(y)
        o_ref[pl.ds(r0, c), :] = jnp.concatenate(outs, axis=1).astype(bf16)
        return carry

    lax.fori_loop(0, seq // c, body, 0)


def _hgrn(hq, hf, hi, hg, lb_logits, norm_g, seq, layer):
    n = hq.shape[0]
    bsz = n // seq
    per_b = pl.BlockSpec((seq, HGRN_WIDTH), lambda b: (b, 0))
    return pl.pallas_call(
        functools.partial(_hgrn_kernel, seq=seq, layer=layer),
        grid=(bsz,),
        in_specs=[per_b, per_b, per_b, per_b,
                  pl.BlockSpec(lb_logits.shape, lambda b: (0, 0)),
                  pl.BlockSpec(norm_g.shape, lambda b: (0, 0))],
        out_specs=per_b,
        out_shape=jax.ShapeDtypeStruct((n, HGRN_WIDTH), bf16),
        scratch_shapes=[pltpu.VMEM((HGRN_HEADS, HGRN_HEAD_V, HGRN_EXPAND), f32)],
        compiler_params=_cparams(("arbitrary",)),
        name="hgrn",
    )(hq, hf, hi, hg, lb_logits, norm_g)


def _merge_kernel(on_ref, oh_ref, mg_ref, x_ref, mod_ref, g_ref, wb0_ref, wb1_ref, wo_ref, x1_ref, h2_ref):
    a = _dot(on_ref[...], wb0_ref[...])
    b = _dot(oh_ref[...], wb1_ref[...])
    mg = mg_ref[...].astype(f32)
    y = mg[:, :D_MODEL] * a + mg[:, D_MODEL:] * b
    y2 = _dot(y.astype(bf16), wo_ref[...])
    x1 = x_ref[...] + mod_ref[0, 2:3, :] * y2
    x1_ref[...] = x1
    h2_ref[...] = _rms(x1, g_ref[...]) * (1.0 + mod_ref[0, 4:5, :]) + mod_ref[0, 3:4, :]


def _merge(o_nsa, o_hgrn, mg, x2, mod3, g_ffn, wb0, wb1, wo, seq):
    n = x2.shape[0]
    tm = 256
    tiles_per_seq = seq // tm
    row = lambda i: (i, 0)
    full = lambda a: pl.BlockSpec(a.shape, lambda i: (0,) * a.ndim)
    return pl.pallas_call(
        _merge_kernel,
        grid=(n // tm,),
        in_specs=[
            pl.BlockSpec((tm, NSA_WIDTH), row), pl.BlockSpec((tm, HGRN_WIDTH), row),
            pl.BlockSpec((tm, 2 * D_MODEL), row), pl.BlockSpec((tm, D_MODEL), row),
            pl.BlockSpec((1, 6, D_MODEL), lambda i: (i // tiles_per_seq, 0, 0)),
            full(g_ffn), full(wb0), full(wb1), full(wo),
        ],
        out_specs=[pl.BlockSpec((tm, D_MODEL), row)] * 2,
        out_shape=[jax.ShapeDtypeStruct((n, D_MODEL), f32)] * 2,
        compiler_params=_cparams(("arbitrary",)),
        name="merge",
    )(o_nsa, o_hgrn, mg, x2, mod3, g_ffn, wb0, wb1, wo)


_TR = 256


def _topk_rows(s, k, payload=None):
    n = s.shape[0]
    rowid = lax.broadcasted_iota(i32, s.shape, 0).astype(f32)
    vals, picks = [], []
    for _ in range(k):
        m = jnp.max(s, axis=0, keepdims=True)
        i = jnp.min(jnp.where(s == m, rowid, float(n)), axis=0, keepdims=True)
        hit = rowid == i
        vals.append(m)
        picks.append(i if payload is None else jnp.max(jnp.where(hit, payload, -1.0), axis=0, keepdims=True))
        s = jnp.where(hit, -jnp.inf, s)
    return vals, picks


def _pair_candidates(v1, i1, v2, i2):
    k = len(v1)
    s2 = jnp.concatenate(v2, axis=0)
    e2 = jnp.concatenate(i2, axis=0)
    sub = lax.broadcasted_iota(i32, (SUBLANES, s2.shape[1]), 0)
    comb, cand = [], []
    for a in range(k // 2):
        nb = k // (a + 1)
        rows = -(-nb // SUBLANES) * SUBLANES
        c = v1[a] + s2[0:rows]
        if nb < rows:
            c = jnp.where(sub < nb, c, -jnp.inf)
        comb.append(c)
        cand.append(i1[a] * float(PEER_NKEYS) + e2[0:rows])
    comb.append(jnp.concatenate(v1[k // 2:], axis=0) + s2[0:1])
    cand.append(jnp.concatenate(i1[k // 2:], axis=0) * float(PEER_NKEYS) + e2[0:1])
    return jnp.concatenate(comb, axis=0), jnp.concatenate(cand, axis=0)


def _row_order(c):
    s, i = divmod(c, 16)
    return i * SUBLANES + s


def _route_kernel(h_ref, wq_ref, sk_ref, idx_ref, gate_ref):
    h = h_ref[...].astype(bf16)
    tr = h.shape[0]
    k = PEER_TOPK
    half = PEER_QDIM // 2
    experts, gates = [], []
    for hd in range(PEER_HEADS):
        tops = []
        for p in range(2):
            grp = hd * 2 + p
            q_t = _dot_nt(wq_ref[grp * half:(grp + 1) * half, :], h)
            s_t = _dot(sk_ref[grp], q_t.astype(bf16))
            tops.append(_topk_rows(s_t, k))
        (v1, i1), (v2, i2) = tops
        comb, cand = _pair_candidates(v1, i1, v2, i2)
        tv, picked = _topk_rows(comb, k, payload=cand)
        experts += picked
        ex = [jnp.exp(tv[r] - tv[0]) for r in range(k)]
        den = ex[0]
        for r in range(1, k):
            den = den + ex[r]
        gates += [e / den for e in ex]
    ncol = PEER_HEADS * k
    inv = [0] * ncol
    for c in range(ncol):
        inv[_row_order(c)] = c
    idx_t = jnp.concatenate([experts[inv[j]] for j in range(ncol)], axis=0).astype(i32)
    gate_t = jnp.concatenate(gates, axis=0)
    for blk in range(tr // LANES):
        sl = slice(blk * LANES, (blk + 1) * LANES)
        idx_ref[sl, :] = idx_t[:, sl].T
        gate_ref[sl, :] = gate_t[:, sl].T


def _route(h2, wq_t, sk):
    n = h2.shape[0]
    tr = _TR
    ncol = PEER_HEADS * PEER_TOPK
    return pl.pallas_call(
        _route_kernel,
        grid=(n // tr,),
        in_specs=[
            pl.BlockSpec((tr, D_MODEL), lambda i: (i, 0)),
            pl.BlockSpec(wq_t.shape, lambda i: (0, 0)),
            pl.BlockSpec(sk.shape, lambda i: (0, 0, 0)),
        ],
        out_specs=[pl.BlockSpec((tr, ncol), lambda i: (i, 0))] * 2,
        out_shape=[jax.ShapeDtypeStruct((n, ncol), i32), jax.ShapeDtypeStruct((n, ncol), f32)],
        compiler_params=_cparams(("arbitrary",)),
        name="route",
    )(h2, wq_t, sk)


_TS = 64
_TG = 8
_NROW = PEER_HEADS * PEER_TOPK
_ROW_TILES = 2 * D_MODEL // LANES
_TP = 2


def _split_bf16(x):
    hi = x.astype(bf16)
    return hi, (x - hi.astype(f32)).astype(bf16)


def _expert_kernel(idx0_ref, idxn_ref, gate_ref, h_ref, x1_ref, mod_ref, gfin_ref, rsum_ref, rrep_ref, uv_ref, o_ref,
                   buf0, buf1, sem):
    step = pl.program_id(0)
    bufs = (buf0, buf1)
    n_dma = _TG * _NROW

    def row_copy(e, slot, j):
        return pltpu.make_async_copy(uv_ref.at[e], bufs[slot].at[j], sem.at[slot])

    def issue(idx_ref, row0, slot):
        def body(t, c):
            for k in range(_NROW):
                row_copy(idx_ref[row0 + t, k], slot, t * _NROW + k).start(priority=k % 2)
            return c
        lax.fori_loop(0, _TG, body, 0)

    def wait(slot):
        def body(j, c):
            row_copy(0, slot, j).wait()
            return c
        lax.fori_loop(0, n_dma, body, 0, unroll=8)

    @pl.when(step == 0)
    def _():
        issue(idx0_ref, 0, 0)

    gt_f = mod_ref[0, 5]
    gfin = gfin_ref[...]
    width = _TP * LANES
    r_i = lax.broadcasted_iota(i32, (width, width), 0) // LANES
    c_i = lax.broadcasted_iota(i32, (width, width), 1) // LANES
    lane_sum = jnp.where(r_i == c_i, 1.0, 0.0).astype(bf16)

    def evaluate(tok0, slot, t0):
        src = bufs[slot]
        z = []
        for dt in range(_TP):
            u = src[pl.ds((t0 + dt) * _NROW, _NROW), 0:SUBLANES, :]
            z.append((u * h_ref[tok0 + dt][None]).reshape(_NROW * SUBLANES, LANES).astype(bf16))
        part = _dot(rsum_ref[...], jnp.concatenate(z, axis=1))
        p_hi, p_lo = _split_bf16(part)
        a = _dot(p_hi, lane_sum) + _dot(p_lo, lane_sum)
        g = jnp.concatenate(
            [jnp.concatenate([jnp.broadcast_to(gate_ref[tok0 + dt][:, i:i + 1], (SUBLANES, LANES))
                              for i in range(_NROW // SUBLANES)], axis=0) for dt in range(_TP)], axis=1)
        w_hi, w_lo = _split_bf16(g * _gelu(a))
        spread = _dot(rrep_ref[...], w_hi) + _dot(rrep_ref[...], w_lo)
        for dt in range(_TP):
            tok = tok0 + dt
            wt = spread[:, dt * LANES:(dt + 1) * LANES].reshape(_NROW, SUBLANES, LANES)
            v = src[pl.ds((t0 + dt) * _NROW, _NROW), SUBLANES:2 * SUBLANES, :]
            y = jnp.sum(wt * v, axis=0)
            x2 = x1_ref[tok] + gt_f * y
            ms = jnp.sum(jnp.sum(x2 * x2, axis=1, keepdims=True), axis=0, keepdims=True) * (1.0 / D_MODEL)
            o_ref[tok] = x2 * lax.rsqrt(ms + EPS) * gfin

    def do_group(grp, slot):
        wait(slot)
        for t0 in range(0, _TG, _TP):
            evaluate(grp * _TG + t0, slot, t0)
            for t in range(t0, t0 + _TP):
                for k in range(_NROW):
                    row_copy(idxn_ref[grp * _TG + t, k], 1 - slot, t * _NROW + k).start(priority=k % 2)

    def pair(p, c):
        do_group(2 * p, 0)
        do_group(2 * p + 1, 1)
        return c
    lax.fori_loop(0, _TS // (2 * _TG), pair, 0)

    @pl.when(step == pl.num_programs(0) - 1)
    def _():
        wait(0)


def _experts(idx, gate3, h3, x13, mod4, gfin, uv, seq):
    n = idx.shape[0]
    steps_per_seq = seq // _TS
    tok3 = lambda i: (i, 0, 0)
    idx_next = jnp.roll(idx, -_TG, axis=0)
    rsum = jnp.asarray(np.repeat(np.eye(_NROW, dtype=np.float32), SUBLANES, axis=1), dtype=bf16)
    buf = pltpu.VMEM((_TG * _NROW, _ROW_TILES, LANES), f32)
    return pl.pallas_call(
        _expert_kernel,
        grid=(n // _TS,),
        in_specs=[
            pl.BlockSpec((_TG, _NROW), lambda i: (0, 0), memory_space=pltpu.SMEM),
            pl.BlockSpec((_TS, _NROW), lambda i: (i, 0), memory_space=pltpu.SMEM),
            pl.BlockSpec((_TS, SUBLANES, 16), tok3),
            pl.BlockSpec((_TS, SUBLANES, LANES), tok3),
            pl.BlockSpec((_TS, SUBLANES, LANES), tok3),
            pl.BlockSpec((1, 6, SUBLANES, LANES), lambda i: (i // steps_per_seq, 0, 0, 0)),
            pl.BlockSpec((SUBLANES, LANES), lambda i: (0, 0)),
            pl.BlockSpec(rsum.shape, lambda i: (0, 0)),
            pl.BlockSpec(rsum.shape[::-1], lambda i: (0, 0)),
            pl.BlockSpec(memory_space=pl.ANY),
        ],
        out_specs=pl.BlockSpec((_TS, SUBLANES, LANES), tok3),
        out_shape=jax.ShapeDtypeStruct((n, SUBLANES, LANES), f32),
        scratch_shapes=[buf, buf, pltpu.SemaphoreType.DMA((2,))],
        compiler_params=_cparams(("arbitrary",)),
        name="experts",
    )(idx, idx_next, gate3, h3, x13, mod4, gfin, rsum, rsum.T, uv)


def _rope_tables(seq):
    half = HEAD_DIM // 2
    inv = ROPE_THETA ** (-np.arange(half, dtype=np.float32) / half)
    ang = np.arange(seq, dtype=np.float32)[:, None] * inv[None, :].astype(np.float32)
    cos = np.cos(ang).astype(np.float32)
    sin = np.sin(ang).astype(np.float32)
    cos_t = np.tile(np.concatenate([cos, cos], axis=1), (1, NSA_HEADS))
    sin_t = np.tile(np.concatenate([-sin, sin], axis=1), (1, NSA_HEADS))
    return jnp.asarray(cos_t), jnp.asarray(sin_t)


def _selection_map(n_cmp_pad, n_sel):
    r_sel = SEL_BLOCK // CMP_STRIDE
    r_cmp = CMP_BLOCK // CMP_STRIDE
    i = np.arange(n_cmp_pad)[:, None]
    j = np.arange(n_sel)[None, :]
    d = i - r_sel * j
    cnt = np.minimum(d, r_sel - 1) - np.maximum(d - r_cmp + 1, 0) + 1
    cnt = np.clip(cnt, 0, None).astype(np.float32)
    cnt[n_cmp_pad - r_cmp + 1:] = 0.0
    return jnp.asarray(cnt)


def _block_expand(n_sel, seq):
    e = (np.arange(seq)[None, :] // SEL_BLOCK == np.arange(n_sel)[:, None]).astype(np.float32)
    return jnp.asarray(e, dtype=bf16)


def _compress_weights(w1):
    eye = jnp.eye(NSA_KV_GROUPS, dtype=w1.dtype)
    out = []
    for part in range(CMP_BLOCK // CMP_STRIDE):
        w = w1[part * CMP_STRIDE * HEAD_DIM:(part + 1) * CMP_STRIDE * HEAD_DIM].reshape(CMP_STRIDE, HEAD_DIM, CMP_HIDDEN)
        big = jnp.einsum('pdc,gh->pgdhc', w, eye).reshape(CMP_STRIDE * KV_WIDTH, NSA_KV_GROUPS * CMP_HIDDEN)
        out.append(big.astype(bf16))
    return out


def kernel(x, c, w_ada, b_ada, g_mix, g_ffn, w_in, cmp_pos_k, cmp_pos_v, w_ck1, w_ck2, w_cv1, w_cv2, hgrn_lb_logits, hgrn_out_norm, w_branch, w_out, w_peer_q, peer_sub_keys, peer_u, peer_v, g_final):
    bsz, seq, d = x.shape
    n = bsz * seq
    depth = w_ada.shape[0]
    assert depth == 1, "single-layer block only"
    n_sel = seq // SEL_BLOCK
    n_pieces = seq // CMP_STRIDE
    cos_t, sin_t = _rope_tables(seq)
    selmap = _selection_map(n_pieces, n_sel)
    expand = _block_expand(n_sel, seq)
    xcur = x.reshape(n, d)
    for l in range(depth):
        mod = _adaln(c, w_ada[l], b_ada[l].reshape(1, 6 * d))
        mod3 = mod.reshape(bsz, 6, d)
        w_pad = jnp.concatenate(
            [w_in[l][:, :_C_GL + 3 * NSA_HEADS], jnp.zeros((d, GATE_PAD - 3 * NSA_HEADS), w_in.dtype),
             w_in[l][:, _C_GL + 3 * NSA_HEADS:]], axis=1).astype(bf16)
        q, kc, vc, kvsw, gates, hq, hf, hi, hg, mg = _inproj(xcur, mod3, g_mix[l].reshape(1, d), w_pad, cos_t, sin_t, seq)
        wkt, wkb = _compress_weights(w_ck1[l])
        wvt, wvb = _compress_weights(w_cv1[l])
        kc_c, vc_c = _compress(
            kc.reshape(bsz, n_pieces, CMP_STRIDE * KV_WIDTH), vc.reshape(bsz, n_pieces, CMP_STRIDE * KV_WIDTH),
            wkt, wkb, wvt, wvb, cmp_pos_k[l].reshape(1, -1), cmp_pos_v[l].reshape(1, -1),
            w_ck1[l], w_cv1[l], w_ck2[l], w_cv2[l])
        o_nsa = _nsa(q, kvsw.reshape(bsz, seq, 4 * KV_WIDTH), kc_c, vc_c, gates, selmap, expand, seq)
        o_hgrn = _hgrn(hq, hf, hi, hg, hgrn_lb_logits, hgrn_out_norm[l].reshape(1, -1), seq, l)
        x1, h2 = _merge(o_nsa, o_hgrn, mg, xcur, mod3, g_ffn[l].reshape(1, d),
                        w_branch[l, 0].astype(bf16), w_branch[l, 1].astype(bf16), w_out[l].astype(bf16), seq)
        wq_t = w_peer_q[l].T.astype(bf16)
        sk = jnp.transpose(peer_sub_keys[l], (1, 0, 2, 3)).reshape(2 * PEER_HEADS, PEER_NKEYS, PEER_QDIM // 2).astype(bf16)
        idx, gate = _route(h2, wq_t, sk)
        uv = jnp.concatenate([peer_u[l], peer_v[l]], axis=1).reshape(-1, _ROW_TILES, LANES)
        out3 = _experts(idx, gate.reshape(n, SUBLANES, 16), h2.reshape(n, SUBLANES, LANES), x1.reshape(n, SUBLANES, LANES),
                        mod.reshape(bsz, 6, SUBLANES, LANES), g_final.reshape(SUBLANES, LANES), uv, seq)
        xcur = out3.reshape(n, d)
    return xcur.reshape(bsz, seq, d)
```

```python
import functools

import numpy as np
import jax
import jax.numpy as jnp
from jax import lax
from jax.experimental import pallas as pl
from jax.experimental.pallas import tpu as pltpu

f32 = jnp.float32
bf16 = jnp.bfloat16
i32 = jnp.int32
_HIGHEST = lax.Precision.HIGHEST

D_MODEL = 1024
EPS = 1e-6
HEAD_DIM = 64
ROPE_THETA = 10000.0
NSA_HEADS = 8
NSA_KV_GROUPS = 2
NSA_GROUP = NSA_HEADS // NSA_KV_GROUPS
NSA_WIDTH = NSA_HEADS * HEAD_DIM
KV_WIDTH = NSA_KV_GROUPS * HEAD_DIM
CMP_BLOCK = 32
CMP_STRIDE = 16
CMP_HIDDEN = 128
SEL_BLOCK = 64
SEL_TOPK = 16
WINDOW = 512
FORCE_BONUS = 1000.0
HGRN_HEADS = 4
HGRN_EXPAND = 128
HGRN_HEAD_V = 128
HGRN_WIDTH = HGRN_HEADS * HGRN_EXPAND
HGRN_CHUNK = 32
PEER_HEADS = 8
PEER_NKEYS = 128
PEER_QDIM = 256
PEER_TOPK = 16
GATE_PAD = 128

LANES = 128
SUBLANES = 8
VMEM_LIMIT = 56 * 1024 * 1024

_NEG = -1e30


def _cparams(sem):
    return pltpu.CompilerParams(dimension_semantics=sem, vmem_limit_bytes=VMEM_LIMIT)


def _gelu(x):
    return 0.5 * x * (1.0 + jnp.tanh(0.7978845608028654 * (x + 0.044715 * (x * x * x))))


def _dot(a, b, **kw):
    return jnp.dot(a, b, preferred_element_type=f32, **kw)


def _dot_nt(a, b):
    return lax.dot_general(a, b, (((1,), (1,)), ((), ())), preferred_element_type=f32)


def _rms(x, g):
    return x * lax.rsqrt(jnp.mean(x * x, axis=-1, keepdims=True) + EPS) * g


def _adaln_kernel(c_ref, w_ref, b_ref, o_ref):
    c = c_ref[...]
    cs = c * jax.nn.sigmoid(c)
    o_ref[...] = _dot(cs, w_ref[...], precision=_HIGHEST) + b_ref[...]


def _adaln(c, w, b):
    bsz = c.shape[0]
    tn = 512
    return pl.pallas_call(
        _adaln_kernel,
        grid=(6 * D_MODEL // tn,),
        in_specs=[
            pl.BlockSpec((bsz, D_MODEL), lambda j: (0, 0)),
            pl.BlockSpec((D_MODEL, tn), lambda j: (0, j)),
            pl.BlockSpec((1, tn), lambda j: (0, j)),
        ],
        out_specs=pl.BlockSpec((bsz, tn), lambda j: (0, j)),
        out_shape=jax.ShapeDtypeStruct((bsz, 6 * D_MODEL), f32),
        compiler_params=_cparams(("arbitrary",)),
        name="adaln",
    )(c, w, b)


_C_Q = 0
_C_KC = 512
_C_VC = 640
_C_KS = 768
_C_VS = 896
_C_KW = 1024
_C_VW = 1152
_C_GL = 1280
_C_HQ = _C_GL + GATE_PAD
_C_HF = _C_HQ + 512
_C_HI = _C_HF + 512
_C_HG = _C_HI + 512
_C_MG = _C_HG + 512
_C_END = _C_MG + 2 * D_MODEL


def _inproj_kernel(x_ref, mod_ref, g_ref, w_ref, cos_ref, sin_ref,
                   q_ref, kc_ref, vc_ref, kvsw_ref, gate_ref, hq_ref, hf_ref, hi_ref, hg_ref, mg_ref):
    x = x_ref[...]
    sh = mod_ref[0, 0:1, :]
    sc = mod_ref[0, 1:2, :]
    h = (_rms(x, g_ref[...]) * (1.0 + sc) + sh).astype(bf16)

    def mm(c0, c1):
        return _dot(h, w_ref[:, c0:c1])

    cos = cos_ref[...]
    sin = sin_ref[...]

    def rope(a):
        width = a.shape[1]
        first = (lax.broadcasted_iota(i32, a.shape, 1) & (HEAD_DIM - 1)) < (HEAD_DIM // 2)
        partner = jnp.where(first, pltpu.roll(a, width - HEAD_DIM // 2, 1), pltpu.roll(a, HEAD_DIM // 2, 1))
        return a * cos[:, :width] + partner * sin[:, :width]

    q_ref[...] = (rope(mm(_C_Q, _C_KC)) * (HEAD_DIM ** -0.5)).astype(bf16)
    kc_ref[...] = rope(mm(_C_KC, _C_VC)).astype(bf16)
    vc_ref[...] = mm(_C_VC, _C_KS).astype(bf16)
    kvsw_ref[:, 0:128] = rope(mm(_C_KS, _C_VS)).astype(bf16)
    kvsw_ref[:, 128:256] = mm(_C_VS, _C_KW).astype(bf16)
    kvsw_ref[:, 256:384] = rope(mm(_C_KW, _C_VW)).astype(bf16)
    kvsw_ref[:, 384:512] = mm(_C_VW, _C_GL).astype(bf16)
    gate_ref[...] = jax.nn.sigmoid(mm(_C_GL, _C_HQ))
    hq_ref[...] = mm(_C_HQ, _C_HF).astype(bf16)
    hf_ref[...] = mm(_C_HF, _C_HI)
    hi_ref[...] = mm(_C_HI, _C_HG).astype(bf16)
    hg_ref[...] = mm(_C_HG, _C_MG).astype(bf16)
    mg_ref[...] = jax.nn.sigmoid(mm(_C_MG, _C_END)).astype(bf16)


def _inproj(x2, mod3, g_mix, w_pad, cos_t, sin_t, seq):
    n = x2.shape[0]
    tm = 256
    tiles_per_seq = seq // tm
    row = lambda i: (i, 0)
    outs = [
        (NSA_WIDTH, bf16), (KV_WIDTH, bf16), (KV_WIDTH, bf16), (4 * KV_WIDTH, bf16), (GATE_PAD, f32),
        (HGRN_WIDTH, bf16), (HGRN_WIDTH, f32), (HGRN_WIDTH, bf16), (HGRN_WIDTH, bf16), (2 * D_MODEL, bf16),
    ]
    return pl.pallas_call(
        _inproj_kernel,
        grid=(n // tm,),
        in_specs=[
            pl.BlockSpec((tm, D_MODEL), row),
            pl.BlockSpec((1, 6, D_MODEL), lambda i: (i // tiles_per_seq, 0, 0)),
            pl.BlockSpec((1, D_MODEL), lambda i: (0, 0)),
            pl.BlockSpec((D_MODEL, _C_END), lambda i: (0, 0)),
            pl.BlockSpec((tm, NSA_WIDTH), lambda i: (i % tiles_per_seq, 0)),
            pl.BlockSpec((tm, NSA_WIDTH), lambda i: (i % tiles_per_seq, 0)),
        ],
        out_specs=[pl.BlockSpec((tm, w), row) for w, _ in outs],
        out_shape=[jax.ShapeDtypeStruct((n, w), dt) for w, dt in outs],
        compiler_params=_cparams(("arbitrary",)),
        name="inproj",
    )(x2, mod3, g_mix, w_pad, cos_t, sin_t)


def _compress_kernel(kc_ref, vc_ref, wkt_ref, wkb_ref, wvt_ref, wvb_ref, pk_ref, pv_ref,
                     w1k_ref, w1v_ref, w2k_ref, w2v_ref, okc_ref, ovc_ref):
    def one(x_ref, wt_ref, wb_ref, pos_ref, w1_ref, w2_ref, o_ref):
        pieces = x_ref[0]
        top = _dot(pieces, wt_ref[...])
        bot = _dot(pieces, wb_ref[...])
        nrow = bot.shape[0]
        bot = pltpu.roll(bot, nrow - 1, 0)
        cpos = _dot(pos_ref[...], w1_ref[...], precision=_HIGHEST)
        w2 = w2_ref[...].astype(bf16)
        outs = []
        for g in range(NSA_KV_GROUPS):
            sl = slice(g * CMP_HIDDEN, (g + 1) * CMP_HIDDEN)
            hid = _gelu(top[:, sl] + bot[:, sl] + cpos)
            outs.append(_dot(hid.astype(bf16), w2))
        o_ref[0] = jnp.concatenate(outs, axis=1).astype(bf16)

    one(kc_ref, wkt_ref, wkb_ref, pk_ref, w1k_ref, w2k_ref, okc_ref)
    one(vc_ref, wvt_ref, wvb_ref, pv_ref, w1v_ref, w2v_ref, ovc_ref)


def _compress(kc3, vc3, wkt, wkb, wvt, wvb, pk, pv, w1k, w1v, w2k, w2v):
    bsz, npieces, width = kc3.shape
    full = lambda a: pl.BlockSpec(a.shape, lambda b: (0,) * a.ndim)
    per_b = pl.BlockSpec((1, npieces, width), lambda b: (b, 0, 0))
    out_b = pl.BlockSpec((1, npieces, KV_WIDTH), lambda b: (b, 0, 0))
    return pl.pallas_call(
        _compress_kernel,
        grid=(bsz,),
        in_specs=[per_b, per_b] + [full(a) for a in (wkt, wkb, wvt, wvb, pk, pv, w1k, w1v, w2k, w2v)],
        out_specs=[out_b, out_b],
        out_shape=[jax.ShapeDtypeStruct((bsz, npieces, KV_WIDTH), bf16)] * 2,
        compiler_params=_cparams(("arbitrary",)),
        name="compress",
    )(kc3, vc3, wkt, wkb, wvt, wvb, pk, pv, w1k, w1v, w2k, w2v)


_TQ = 128
_NSA_CLASSES = 4


def _softmax_parts(s, mask):
    sm = jnp.where(mask, s, _NEG)
    m = jnp.max(sm, axis=-1, keepdims=True)
    p = jnp.where(mask, jnp.exp(sm - m), 0.0)
    den = jnp.maximum(jnp.sum(p, axis=-1, keepdims=True), 1e-30)
    return p, den


def _nsa_kernel(q_ref, kvsw_ref, kc_ref, vc_ref, gate_ref, selmap_ref, expand_ref, o_ref, *, seq):
    tq = _TQ
    qi = pl.program_id(1)
    n_cls = _NSA_CLASSES
    per_cls = (seq // tq) // n_cls
    for cls in range(n_cls):
        @pl.when((qi >= cls * per_cls) & (qi < (cls + 1) * per_cls))
        def _(cls=cls):
            _nsa_tile(q_ref, kvsw_ref, kc_ref, vc_ref, gate_ref, selmap_ref, expand_ref, o_ref,
                      seq=seq, width=(cls + 1) * (seq // n_cls))


def _nsa_tile(q_ref, kvsw_ref, kc_ref, vc_ref, gate_ref, selmap_ref, expand_ref, o_ref, *, seq, width):
    tq = _TQ
    rows = NSA_GROUP * tq
    n_sel = seq // SEL_BLOCK
    n_cmp_pad = seq // CMP_STRIDE
    t0 = pl.program_id(1) * tq
    q = q_ref[...]
    gates = gate_ref[...]
    tpos = t0 + lax.broadcasted_iota(i32, (tq, 1), 0)
    t4 = t0 + (lax.broadcasted_iota(i32, (rows, 1), 0) & (tq - 1))
    wk = WINDOW + tq
    ws = pl.multiple_of(jnp.maximum(t0 - WINDOW, 0), tq)
    ks_all = kvsw_ref[0, 0:width, 0:128]
    vs_all = kvsw_ref[0, 0:width, 128:256]
    kw_all = kvsw_ref[0, pl.ds(ws, wk), 256:384]
    vw_all = kvsw_ref[0, pl.ds(ws, wk), 384:512]
    kc_all = kc_ref[0]
    vc_all = vc_ref[0]
    cend = lax.broadcasted_iota(i32, (1, n_cmp_pad), 1) * CMP_STRIDE + (CMP_BLOCK - 1)
    blk = lax.broadcasted_iota(i32, (1, n_sel), 1)
    kpos = lax.broadcasted_iota(i32, (1, width), 1)
    kpos_w = ws + lax.broadcasted_iota(i32, (1, wk), 1)
    cur = tpos >> 6
    forced = (blk == 0) | (blk == cur) | (blk == cur - 1)
    causal_blk = blk * SEL_BLOCK <= tpos
    pieces = []
    for g in range(NSA_KV_GROUPS):
        gs = slice(g * HEAD_DIM, (g + 1) * HEAD_DIM)
        qg = jnp.concatenate(
            [q[:, (NSA_GROUP * g + r) * HEAD_DIM:(NSA_GROUP * g + r + 1) * HEAD_DIM] for r in range(NSA_GROUP)], axis=0)
        p_c, den_c = _softmax_parts(_dot_nt(qg, kc_all[:, gs]), cend <= t4)
        o_c = _dot(p_c.astype(bf16), vc_all[:, gs]) / den_c
        pn = p_c / den_c
        pc_sum = pn[0:tq]
        for r in range(1, NSA_GROUP):
            pc_sum = pc_sum + pn[r * tq:(r + 1) * tq]
        imp = _dot(pc_sum, selmap_ref[...], precision=_HIGHEST)
        imp = jnp.where(forced, imp + FORCE_BONUS, imp)
        imp = jnp.where(causal_blk, imp, -1.0)
        rank = jnp.zeros((tq, n_sel), f32)
        for j in range(n_sel):
            col = imp[:, j:j + 1]
            ahead = jnp.where(col > imp, 1.0, jnp.where(col == imp, jnp.where(blk > j, 1.0, 0.0), 0.0))
            rank = rank + ahead
        sel = jnp.where(rank < float(min(SEL_TOPK, n_sel)), 1.0, 0.0).astype(bf16)
        sel4 = jnp.concatenate([sel] * NSA_GROUP, axis=0)
        selx = _dot(sel4, expand_ref[:, 0:width])
        m_s = jnp.where(kpos <= t4, selx, 0.0) > 0.5
        p_s, den_s = _softmax_parts(_dot_nt(qg, ks_all[:, gs]), m_s)
        o_s = _dot(p_s.astype(bf16), vs_all[:, gs]) / den_s
        dist = t4 - kpos_w
        m_w = jnp.where(dist >= 0, jnp.where(dist < WINDOW, 1.0, 0.0), 0.0) > 0.5
        p_w, den_w = _softmax_parts(_dot_nt(qg, kw_all[:, gs]), m_w)
        o_w = _dot(p_w.astype(bf16), vw_all[:, gs]) / den_w

        def gcol(br):
            return jnp.concatenate(
                [gates[:, (NSA_GROUP * g + r) * 3 + br:(NSA_GROUP * g + r) * 3 + br + 1] for r in range(NSA_GROUP)], axis=0)

        o = gcol(0) * o_c + gcol(1) * o_s + gcol(2) * o_w
        pieces += [o[r * tq:(r + 1) * tq] for r in range(NSA_GROUP)]
    o_ref[...] = jnp.concatenate(pieces, axis=1).astype(bf16)


def _nsa(q, kvsw3, kc3, vc3, gates, selmap, expand, seq):
    n = q.shape[0]
    bsz = n // seq
    tq = _TQ
    nq = seq // tq
    return pl.pallas_call(
        functools.partial(_nsa_kernel, seq=seq),
        grid=(bsz, nq),
        in_specs=[
            pl.BlockSpec((tq, NSA_WIDTH), lambda b, i: (b * nq + i, 0)),
            pl.BlockSpec((1, seq, 4 * KV_WIDTH), lambda b, i: (b, 0, 0)),
            pl.BlockSpec((1,) + kc3.shape[1:], lambda b, i: (b, 0, 0)),
            pl.BlockSpec((1,) + vc3.shape[1:], lambda b, i: (b, 0, 0)),
            pl.BlockSpec((tq, GATE_PAD), lambda b, i: (b * nq + i, 0)),
            pl.BlockSpec(selmap.shape, lambda b, i: (0, 0)),
            pl.BlockSpec(expand.shape, lambda b, i: (0, 0)),
        ],
        out_specs=pl.BlockSpec((tq, NSA_WIDTH), lambda b, i: (b * nq + i, 0)),
        out_shape=jax.ShapeDtypeStruct((n, NSA_WIDTH), bf16),
        compiler_params=_cparams(("arbitrary", "arbitrary")),
        name="nsa",
    )(q, kvsw3, kc3, vc3, gates, selmap, expand)


def _hgrn_kernel(hq_ref, hf_ref, hi_ref, hg_ref, lbl_ref, ng_ref, o_ref, st_ref, *, seq, layer):
    c = HGRN_CHUNK
    logits = lbl_ref[...]
    e = jnp.exp(logits - jnp.max(logits, axis=0, keepdims=True))
    sm = e / jnp.sum(e, axis=0, keepdims=True)
    lb = sm[0:1]
    for l in range(1, layer + 1):
        lb = lb + sm[l:l + 1]
    st_ref[...] = jnp.zeros_like(st_ref)
    r_i = lax.broadcasted_iota(i32, (c, c), 0)
    c_i = lax.broadcasted_iota(i32, (c, c), 1)
    causal = r_i >= c_i
    tri = jnp.where(causal, 1.0, 0.0)
    ng = ng_ref[...]

    def body(ci, carry):
        r0 = pl.multiple_of(ci * c, c)
        hq = hq_ref[pl.ds(r0, c), :].astype(f32)
        hf = hf_ref[pl.ds(r0, c), :]
        v = hi_ref[pl.ds(r0, c), :]
        hg = hg_ref[pl.ds(r0, c), :].astype(f32)
        f = lb + (1.0 - lb) * jax.nn.sigmoid(hf)
        k = 1.0 - f
        qv = hq * jax.nn.sigmoid(hq)
        bcum = _dot(tri, jnp.log(f), precision=_HIGHEST)
        bend = bcum[c - 1:c, :]
        q_dec = (qv * jnp.exp(bcum)).astype(bf16)
        k_inv = (k * jnp.exp(-bcum)).astype(bf16)
        k_end = (k * jnp.exp(bend - bcum)).astype(bf16)
        dec_end = jnp.exp(bend)
        outs = []
        for h in range(HGRN_HEADS):
            sl = slice(h * HGRN_EXPAND, (h + 1) * HGRN_EXPAND)
            a = jnp.where(causal, _dot_nt(q_dec[:, sl], k_inv[:, sl]), 0.0)
            st = st_ref[h]
            o = _dot(a.astype(bf16), v[:, sl]) + _dot_nt(q_dec[:, sl], st.astype(bf16))
            upd = lax.dot_general(v[:, sl], k_end[:, sl], (((0,), (0,)), ((), ())), preferred_element_type=f32)
            st_ref[h] = st * dec_end[:, sl] + upd
            y = _rms(o, ng) * (hg[:, sl] * jax.nn.sigmoid(hg[:, sl]))
            outs.append(y)
        o_ref[pl.ds(r0, c), :] = jnp.concatenate(outs, axis=1).astype(bf16)
        return carry

    lax.fori_loop(0, seq // c, body, 0)


def _hgrn(hq, hf, hi, hg, lb_logits, norm_g, seq, layer):
    n = hq.shape[0]
    bsz = n // seq
    per_b = pl.BlockSpec((seq, HGRN_WIDTH), lambda b: (b, 0))
    return pl.pallas_call(
        functools.partial(_hgrn_kernel, seq=seq, layer=layer),
        grid=(bsz,),
        in_specs=[per_b, per_b, per_b, per_b,
                  pl.BlockSpec(lb_logits.shape, lambda b: (0, 0)),
                  pl.BlockSpec(norm_g.shape, lambda b: (0, 0))],
        out_specs=per_b,
        out_shape=jax.ShapeDtypeStruct((n, HGRN_WIDTH), bf16),
        scratch_shapes=[pltpu.VMEM((HGRN_HEADS, HGRN_HEAD_V, HGRN_EXPAND), f32)],
        compiler_params=_cparams(("arbitrary",)),
        name="hgrn",
    )(hq, hf, hi, hg, lb_logits, norm_g)


def _merge_kernel(on_ref, oh_ref, mg_ref, x_ref, mod_ref, g_ref, wb0_ref, wb1_ref, wo_ref, x1_ref, h2_ref):
    a = _dot(on_ref[...], wb0_ref[...])
    b = _dot(oh_ref[...], wb1_ref[...])
    mg = mg_ref[...].astype(f32)
    y = mg[:, :D_MODEL] * a + mg[:, D_MODEL:] * b
    y2 = _dot(y.astype(bf16), wo_ref[...])
    x1 = x_ref[...] + mod_ref[0, 2:3, :] * y2
    x1_ref[...] = x1
    h2_ref[...] = _rms(x1, g_ref[...]) * (1.0 + mod_ref[0, 4:5, :]) + mod_ref[0, 3:4, :]


def _merge(o_nsa, o_hgrn, mg, x2, mod3, g_ffn, wb0, wb1, wo, seq):
    n = x2.shape[0]
    tm = 256
    tiles_per_seq = seq // tm
    row = lambda i: (i, 0)
    full = lambda a: pl.BlockSpec(a.shape, lambda i: (0,) * a.ndim)
    return pl.pallas_call(
        _merge_kernel,
        grid=(n // tm,),
        in_specs=[
            pl.BlockSpec((tm, NSA_WIDTH), row), pl.BlockSpec((tm, HGRN_WIDTH), row),
            pl.BlockSpec((tm, 2 * D_MODEL), row), pl.BlockSpec((tm, D_MODEL), row),
            pl.BlockSpec((1, 6, D_MODEL), lambda i: (i // tiles_per_seq, 0, 0)),
            full(g_ffn), full(wb0), full(wb1), full(wo),
        ],
        out_specs=[pl.BlockSpec((tm, D_MODEL), row)] * 2,
        out_shape=[jax.ShapeDtypeStruct((n, D_MODEL), f32)] * 2,
        compiler_params=_cparams(("arbitrary",)),
        name="merge",
    )(o_nsa, o_hgrn, mg, x2, mod3, g_ffn, wb0, wb1, wo)


_TR = 256


def _topk_rows(s, k, payload=None):
    n = s.shape[0]
    rowid = lax.broadcasted_iota(i32, s.shape, 0).astype(f32)
    vals, picks = [], []
    for _ in range(k):
        m = jnp.max(s, axis=0, keepdims=True)
        i = jnp.min(jnp.where(s == m, rowid, float(n)), axis=0, keepdims=True)
        hit = rowid == i
        vals.append(m)
        picks.append(i if payload is None else jnp.max(jnp.where(hit, payload, -1.0), axis=0, keepdims=True))
        s = jnp.where(hit, -jnp.inf, s)
    return vals, picks


def _pair_candidates(v1, i1, v2, i2):
    k = len(v1)
    s2 = jnp.concatenate(v2, axis=0)
    e2 = jnp.concatenate(i2, axis=0)
    sub = lax.broadcasted_iota(i32, (SUBLANES, s2.shape[1]), 0)
    comb, cand = [], []
    for a in range(k // 2):
        nb = k // (a + 1)
        rows = -(-nb // SUBLANES) * SUBLANES
        c = v1[a] + s2[0:rows]
        if nb < rows:
            c = jnp.where(sub < nb, c, -jnp.inf)
        comb.append(c)
        cand.append(i1[a] * float(PEER_NKEYS) + e2[0:rows])
    comb.append(jnp.concatenate(v1[k // 2:], axis=0) + s2[0:1])
    cand.append(jnp.concatenate(i1[k // 2:], axis=0) * float(PEER_NKEYS) + e2[0:1])
    return jnp.concatenate(comb, axis=0), jnp.concatenate(cand, axis=0)


def _row_order(c):
    s, i = divmod(c, 16)
    return i * SUBLANES + s


def _route_kernel(h_ref, wq_ref, sk_ref, idx_ref, gate_ref):
    h = h_ref[...].astype(bf16)
    tr = h.shape[0]
    k = PEER_TOPK
    half = PEER_QDIM // 2
    experts, gates = [], []
    for hd in range(PEER_HEADS):
        tops = []
        for p in range(2):
            grp = hd * 2 + p
            q_t = _dot_nt(wq_ref[grp * half:(grp + 1) * half, :], h)
            s_t = _dot(sk_ref[grp], q_t.astype(bf16))
            tops.append(_topk_rows(s_t, k))
        (v1, i1), (v2, i2) = tops
        comb, cand = _pair_candidates(v1, i1, v2, i2)
        tv, picked = _topk_rows(comb, k, payload=cand)
        experts += picked
        ex = [jnp.exp(tv[r] - tv[0]) for r in range(k)]
        den = ex[0]
        for r in range(1, k):
            den = den + ex[r]
        gates += [e / den for e in ex]
    ncol = PEER_HEADS * k
    inv = [0] * ncol
    for c in range(ncol):
        inv[_row_order(c)] = c
    idx_t = jnp.concatenate([experts[inv[j]] for j in range(ncol)], axis=0).astype(i32)
    gate_t = jnp.concatenate(gates, axis=0)
    for blk in range(tr // LANES):
        sl = slice(blk * LANES, (blk + 1) * LANES)
        idx_ref[sl, :] = idx_t[:, sl].T
        gate_ref[sl, :] = gate_t[:, sl].T


def _route(h2, wq_t, sk):
    n = h2.shape[0]
    tr = _TR
    ncol = PEER_HEADS * PEER_TOPK
    return pl.pallas_call(
        _route_kernel,
        grid=(n // tr,),
        in_specs=[
            pl.BlockSpec((tr, D_MODEL), lambda i: (i, 0)),
            pl.BlockSpec(wq_t.shape, lambda i: (0, 0)),
            pl.BlockSpec(sk.shape, lambda i: (0, 0, 0)),
        ],
        out_specs=[pl.BlockSpec((tr, ncol), lambda i: (i, 0))] * 2,
        out_shape=[jax.ShapeDtypeStruct((n, ncol), i32), jax.ShapeDtypeStruct((n, ncol), f32)],
        compiler_params=_cparams(("arbitrary",)),
        name="route",
    )(h2, wq_t, sk)


_TS = 64
_TG = 4
_NSLOT = 4
_AHEAD = (_NSLOT - 1) * _TG
_NROW = PEER_HEADS * PEER_TOPK
_ROW_TILES = 2 * D_MODEL // LANES
_TP = 2


def _split_bf16(x):
    hi = x.astype(bf16)
    return hi, (x - hi.astype(f32)).astype(bf16)


def _expert_kernel(idx0_ref, idxn_ref, gate_ref, h_ref, x1_ref, mod_ref, gfin_ref, rsum_ref, rrep_ref, uv_ref, o_ref,
                   *scratch):
    step = pl.program_id(0)
    bufs, sem = scratch[:_NSLOT], scratch[_NSLOT]
    n_dma = _TG * _NROW

    def row_copy(e, slot, j):
        return pltpu.make_async_copy(uv_ref.at[e], bufs[slot].at[j], sem.at[slot])

    def wait(slot):
        def body(j, c):
            row_copy(0, slot, j).wait()
            return c
        lax.fori_loop(0, n_dma, body, 0, unroll=8)

    @pl.when(step == 0)
    def _():
        for slot in range(_NSLOT - 1):
            def body(t, c, slot=slot):
                for k in range(_NROW):
                    row_copy(idx0_ref[slot * _TG + t, k], slot, t * _NROW + k).start(priority=k % 2)
                return c
            lax.fori_loop(0, _TG, body, 0)

    gt_f = mod_ref[0, 5]
    gfin = gfin_ref[...]
    width = _TP * LANES
    r_i = lax.broadcasted_iota(i32, (width, width), 0) // LANES
    c_i = lax.broadcasted_iota(i32, (width, width), 1) // LANES
    lane_sum = jnp.where(r_i == c_i, 1.0, 0.0).astype(bf16)

    def evaluate(tok0, slot, t0):
        src = bufs[slot]
        z, vs = [], []
        for dt in range(_TP):
            uv = src[pl.ds((t0 + dt) * _NROW, _NROW)].astype(f32)
            vs.append(uv[:, SUBLANES:2 * SUBLANES, :])
            u = uv[:, 0:SUBLANES, :]
            z.append((u * h_ref[tok0 + dt][None]).reshape(_NROW * SUBLANES, LANES).astype(bf16))
        part = _dot(rsum_ref[...], jnp.concatenate(z, axis=1))
        p_hi, p_lo = _split_bf16(part)
        a = _dot(p_hi, lane_sum) + _dot(p_lo, lane_sum)
        g = jnp.concatenate(
            [jnp.concatenate([jnp.broadcast_to(gate_ref[tok0 + dt][:, i:i + 1], (SUBLANES, LANES))
                              for i in range(_NROW // SUBLANES)], axis=0) for dt in range(_TP)], axis=1)
        w_hi, w_lo = _split_bf16(g * _gelu(a))
        spread = _dot(rrep_ref[...], w_hi) + _dot(rrep_ref[...], w_lo)
        for dt in range(_TP):
            tok = tok0 + dt
            wt = spread[:, dt * LANES:(dt + 1) * LANES].reshape(_NROW, SUBLANES, LANES)
            y = jnp.sum(wt * vs[dt], axis=0)
            x2 = x1_ref[tok] + gt_f * y
            ms = jnp.sum(jnp.sum(x2 * x2, axis=1, keepdims=True), axis=0, keepdims=True) * (1.0 / D_MODEL)
            o_ref[tok] = x2 * lax.rsqrt(ms + EPS) * gfin

    def do_group(grp, slot):
        wait(slot)
        dst = (slot + _NSLOT - 1) % _NSLOT
        for t0 in range(0, _TG, _TP):
            evaluate(grp * _TG + t0, slot, t0)
            for t in range(t0, t0 + _TP):
                for k in range(_NROW):
                    row_copy(idxn_ref[grp * _TG + t, k], dst, t * _NROW + k).start(priority=k % 2)

    def cycle(p, c):
        for slot in range(_NSLOT):
            do_group(_NSLOT * p + slot, slot)
        return c
    lax.fori_loop(0, _TS // (_NSLOT * _TG), cycle, 0)

    @pl.when(step == pl.num_programs(0) - 1)
    def _():
        for slot in range(_NSLOT - 1):
            wait(slot)


def _experts(idx, gate3, h3, x13, mod4, gfin, uv, seq):
    n = idx.shape[0]
    steps_per_seq = seq // _TS
    tok3 = lambda i: (i, 0, 0)
    idx_next = jnp.roll(idx, -_AHEAD, axis=0)
    rsum = jnp.asarray(np.repeat(np.eye(_NROW, dtype=np.float32), SUBLANES, axis=1), dtype=bf16)
    buf = pltpu.VMEM((_TG * _NROW, _ROW_TILES, LANES), bf16)
    return pl.pallas_call(
        _expert_kernel,
        grid=(n // _TS,),
        in_specs=[
            pl.BlockSpec((_NSLOT * _TG, _NROW), lambda i: (0, 0), memory_space=pltpu.SMEM),
            pl.BlockSpec((_TS, _NROW), lambda i: (i, 0), memory_space=pltpu.SMEM),
            pl.BlockSpec((_TS, SUBLANES, 16), tok3),
            pl.BlockSpec((_TS, SUBLANES, LANES), tok3),
            pl.BlockSpec((_TS, SUBLANES, LANES), tok3),
            pl.BlockSpec((1, 6, SUBLANES, LANES), lambda i: (i // steps_per_seq, 0, 0, 0)),
            pl.BlockSpec((SUBLANES, LANES), lambda i: (0, 0)),
            pl.BlockSpec(rsum.shape, lambda i: (0, 0)),
            pl.BlockSpec(rsum.shape[::-1], lambda i: (0, 0)),
            pl.BlockSpec(memory_space=pl.ANY),
        ],
        out_specs=pl.BlockSpec((_TS, SUBLANES, LANES), tok3),
        out_shape=jax.ShapeDtypeStruct((n, SUBLANES, LANES), f32),
        scratch_shapes=[buf] * _NSLOT + [pltpu.SemaphoreType.DMA((_NSLOT,))],
        compiler_params=_cparams(("arbitrary",)),
        name="experts",
    )(idx, idx_next, gate3, h3, x13, mod4, gfin, rsum, rsum.T, uv)


def _rope_tables(seq):
    half = HEAD_DIM // 2
    inv = ROPE_THETA ** (-np.arange(half, dtype=np.float32) / half)
    ang = np.arange(seq, dtype=np.float32)[:, None] * inv[None, :].astype(np.float32)
    cos = np.cos(ang).astype(np.float32)
    sin = np.sin(ang).astype(np.float32)
    cos_t = np.tile(np.concatenate([cos, cos], axis=1), (1, NSA_HEADS))
    sin_t = np.tile(np.concatenate([-sin, sin], axis=1), (1, NSA_HEADS))
    return jnp.asarray(cos_t), jnp.asarray(sin_t)


def _selection_map(n_cmp_pad, n_sel):
    r_sel = SEL_BLOCK // CMP_STRIDE
    r_cmp = CMP_BLOCK // CMP_STRIDE
    i = np.arange(n_cmp_pad)[:, None]
    j = np.arange(n_sel)[None, :]
    d = i - r_sel * j
    cnt = np.minimum(d, r_sel - 1) - np.maximum(d - r_cmp + 1, 0) + 1
    cnt = np.clip(cnt, 0, None).astype(np.float32)
    cnt[n_cmp_pad - r_cmp + 1:] = 0.0
    return jnp.asarray(cnt)


def _block_expand(n_sel, seq):
    e = (np.arange(seq)[None, :] // SEL_BLOCK == np.arange(n_sel)[:, None]).astype(np.float32)
    return jnp.asarray(e, dtype=bf16)


def _compress_weights(w1):
    eye = jnp.eye(NSA_KV_GROUPS, dtype=w1.dtype)
    out = []
    for part in range(CMP_BLOCK // CMP_STRIDE):
        w = w1[part * CMP_STRIDE * HEAD_DIM:(part + 1) * CMP_STRIDE * HEAD_DIM].reshape(CMP_STRIDE, HEAD_DIM, CMP_HIDDEN)
        big = jnp.einsum('pdc,gh->pgdhc', w, eye).reshape(CMP_STRIDE * KV_WIDTH, NSA_KV_GROUPS * CMP_HIDDEN)
        out.append(big.astype(bf16))
    return out


def kernel(x, c, w_ada, b_ada, g_mix, g_ffn, w_in, cmp_pos_k, cmp_pos_v, w_ck1, w_ck2, w_cv1, w_cv2, hgrn_lb_logits, hgrn_out_norm, w_branch, w_out, w_peer_q, peer_sub_keys, peer_u, peer_v, g_final):
    bsz, seq, d = x.shape
    n = bsz * seq
    depth = w_ada.shape[0]
    assert depth == 1, "single-layer block only"
    n_sel = seq // SEL_BLOCK
    n_pieces = seq // CMP_STRIDE
    cos_t, sin_t = _rope_tables(seq)
    selmap = _selection_map(n_pieces, n_sel)
    expand = _block_expand(n_sel, seq)
    xcur = x.reshape(n, d)
    for l in range(depth):
        mod = _adaln(c, w_ada[l], b_ada[l].reshape(1, 6 * d))
        mod3 = mod.reshape(bsz, 6, d)
        w_pad = jnp.concatenate(
            [w_in[l][:, :_C_GL + 3 * NSA_HEADS], jnp.zeros((d, GATE_PAD - 3 * NSA_HEADS), w_in.dtype),
             w_in[l][:, _C_GL + 3 * NSA_HEADS:]], axis=1).astype(bf16)
        q, kc, vc, kvsw, gates, hq, hf, hi, hg, mg = _inproj(xcur, mod3, g_mix[l].reshape(1, d), w_pad, cos_t, sin_t, seq)
        wkt, wkb = _compress_weights(w_ck1[l])
        wvt, wvb = _compress_weights(w_cv1[l])
        kc_c, vc_c = _compress(
            kc.reshape(bsz, n_pieces, CMP_STRIDE * KV_WIDTH), vc.reshape(bsz, n_pieces, CMP_STRIDE * KV_WIDTH),
            wkt, wkb, wvt, wvb, cmp_pos_k[l].reshape(1, -1), cmp_pos_v[l].reshape(1, -1),
            w_ck1[l], w_cv1[l], w_ck2[l], w_cv2[l])
        o_nsa = _nsa(q, kvsw.reshape(bsz, seq, 4 * KV_WIDTH), kc_c, vc_c, gates, selmap, expand, seq)
        o_hgrn = _hgrn(hq, hf, hi, hg, hgrn_lb_logits, hgrn_out_norm[l].reshape(1, -1), seq, l)
        x1, h2 = _merge(o_nsa, o_hgrn, mg, xcur, mod3, g_ffn[l].reshape(1, d),
                        w_branch[l, 0].astype(bf16), w_branch[l, 1].astype(bf16), w_out[l].astype(bf16), seq)
        wq_t = w_peer_q[l].T.astype(bf16)
        sk = jnp.transpose(peer_sub_keys[l], (1, 0, 2, 3)).reshape(2 * PEER_HEADS, PEER_NKEYS, PEER_QDIM // 2).astype(bf16)
        idx, gate = _route(h2, wq_t, sk)
        uv = jnp.concatenate([peer_u[l], peer_v[l]], axis=1).astype(bf16).reshape(-1, _ROW_TILES, LANES)
        out3 = _experts(idx, gate.reshape(n, SUBLANES, 16), h2.reshape(n, SUBLANES, LANES), x1.reshape(n, SUBLANES, LANES),
                        mod.reshape(bsz, 6, SUBLANES, LANES), g_final.reshape(SUBLANES, LANES), uv, seq)
        xcur = out3.reshape(n, d)
    return xcur.reshape(bsz, seq, d)
```

```python
import functools

import numpy as np
import jax
import jax.numpy as jnp
from jax import lax
from jax.experimental import pallas as pl
from jax.experimental.pallas import tpu as pltpu

f32 = jnp.float32
bf16 = jnp.bfloat16
i32 = jnp.int32
_HIGHEST = lax.Precision.HIGHEST

D_MODEL = 1024
EPS = 1e-6
HEAD_DIM = 64
ROPE_THETA = 10000.0
NSA_HEADS = 8
NSA_KV_GROUPS = 2
NSA_GROUP = NSA_HEADS // NSA_KV_GROUPS
NSA_WIDTH = NSA_HEADS * HEAD_DIM
KV_WIDTH = NSA_KV_GROUPS * HEAD_DIM
CMP_BLOCK = 32
CMP_STRIDE = 16
CMP_HIDDEN = 128
SEL_BLOCK = 64
SEL_TOPK = 16
WINDOW = 512
FORCE_BONUS = 1000.0
HGRN_HEADS = 4
HGRN_EXPAND = 128
HGRN_HEAD_V = 128
HGRN_WIDTH = HGRN_HEADS * HGRN_EXPAND
HGRN_CHUNK = 32
PEER_HEADS = 8
PEER_NKEYS = 128
PEER_QDIM = 256
PEER_TOPK = 16
GATE_PAD = 128

LANES = 128
SUBLANES = 8
VMEM_LIMIT = 56 * 1024 * 1024

_NEG = -1e30


def _cparams(sem):
    return pltpu.CompilerParams(dimension_semantics=sem, vmem_limit_bytes=VMEM_LIMIT)


def _gelu(x):
    return 0.5 * x * (1.0 + jnp.tanh(0.7978845608028654 * (x + 0.044715 * (x * x * x))))


def _dot(a, b, **kw):
    return jnp.dot(a, b, preferred_element_type=f32, **kw)


def _dot_nt(a, b):
    return lax.dot_general(a, b, (((1,), (1,)), ((), ())), preferred_element_type=f32)


def _rms(x, g):
    return x * lax.rsqrt(jnp.mean(x * x, axis=-1, keepdims=True) + EPS) * g


def _adaln_kernel(c_ref, w_ref, b_ref, o_ref):
    c = c_ref[...]
    cs = c * jax.nn.sigmoid(c)
    o_ref[...] = _dot(cs, w_ref[...], precision=_HIGHEST) + b_ref[...]


def _adaln(c, w, b):
    bsz = c.shape[0]
    tn = 512
    return pl.pallas_call(
        _adaln_kernel,
        grid=(6 * D_MODEL // tn,),
        in_specs=[
            pl.BlockSpec((bsz, D_MODEL), lambda j: (0, 0)),
            pl.BlockSpec((D_MODEL, tn), lambda j: (0, j)),
            pl.BlockSpec((1, tn), lambda j: (0, j)),
        ],
        out_specs=pl.BlockSpec((bsz, tn), lambda j: (0, j)),
        out_shape=jax.ShapeDtypeStruct((bsz, 6 * D_MODEL), f32),
        compiler_params=_cparams(("arbitrary",)),
        name="adaln",
    )(c, w, b)


_C_Q = 0
_C_KC = 512
_C_VC = 640
_C_KS = 768
_C_VS = 896
_C_KW = 1024
_C_VW = 1152
_C_GL = 1280
_C_HQ = _C_GL + GATE_PAD
_C_HF = _C_HQ + 512
_C_HI = _C_HF + 512
_C_HG = _C_HI + 512
_C_MG = _C_HG + 512
_C_END = _C_MG + 2 * D_MODEL


def _inproj_kernel(x_ref, mod_ref, g_ref, w_ref, cos_ref, sin_ref,
                   q_ref, kc_ref, vc_ref, kvsw_ref, gate_ref, hq_ref, hf_ref, hi_ref, hg_ref, mg_ref):
    x = x_ref[...]
    sh = mod_ref[0, 0:1, :]
    sc = mod_ref[0, 1:2, :]
    h = (_rms(x, g_ref[...]) * (1.0 + sc) + sh).astype(bf16)

    def mm(c0, c1):
        return _dot(h, w_ref[:, c0:c1])

    cos = cos_ref[...]
    sin = sin_ref[...]

    def rope(a):
        width = a.shape[1]
        first = (lax.broadcasted_iota(i32, a.shape, 1) & (HEAD_DIM - 1)) < (HEAD_DIM // 2)
        partner = jnp.where(first, pltpu.roll(a, width - HEAD_DIM // 2, 1), pltpu.roll(a, HEAD_DIM // 2, 1))
        return a * cos[:, :width] + partner * sin[:, :width]

    q_ref[...] = (rope(mm(_C_Q, _C_KC)) * (HEAD_DIM ** -0.5)).astype(bf16)
    kc_ref[...] = rope(mm(_C_KC, _C_VC)).astype(bf16)
    vc_ref[...] = mm(_C_VC, _C_KS).astype(bf16)
    kvsw_ref[:, 0:128] = rope(mm(_C_KS, _C_VS)).astype(bf16)
    kvsw_ref[:, 128:256] = mm(_C_VS, _C_KW).astype(bf16)
    kvsw_ref[:, 256:384] = rope(mm(_C_KW, _C_VW)).astype(bf16)
    kvsw_ref[:, 384:512] = mm(_C_VW, _C_GL).astype(bf16)
    gate_ref[...] = jax.nn.sigmoid(mm(_C_GL, _C_HQ))
    hq_ref[...] = mm(_C_HQ, _C_HF).astype(bf16)
    hf_ref[...] = mm(_C_HF, _C_HI)
    hi_ref[...] = mm(_C_HI, _C_HG).astype(bf16)
    hg_ref[...] = mm(_C_HG, _C_MG).astype(bf16)
    mg_ref[...] = jax.nn.sigmoid(mm(_C_MG, _C_END)).astype(bf16)


def _inproj(x2, mod3, g_mix, w_pad, cos_t, sin_t, seq):
    n = x2.shape[0]
    tm = 256
    tiles_per_seq = seq // tm
    row = lambda i: (i, 0)
    outs = [
        (NSA_WIDTH, bf16), (KV_WIDTH, bf16), (KV_WIDTH, bf16), (4 * KV_WIDTH, bf16), (GATE_PAD, f32),
        (HGRN_WIDTH, bf16), (HGRN_WIDTH, f32), (HGRN_WIDTH, bf16), (HGRN_WIDTH, bf16), (2 * D_MODEL, bf16),
    ]
    return pl.pallas_call(
        _inproj_kernel,
        grid=(n // tm,),
        in_specs=[
            pl.BlockSpec((tm, D_MODEL), row),
            pl.BlockSpec((1, 6, D_MODEL), lambda i: (i // tiles_per_seq, 0, 0)),
            pl.BlockSpec((1, D_MODEL), lambda i: (0, 0)),
            pl.BlockSpec((D_MODEL, _C_END), lambda i: (0, 0)),
            pl.BlockSpec((tm, NSA_WIDTH), lambda i: (i % tiles_per_seq, 0)),
            pl.BlockSpec((tm, NSA_WIDTH), lambda i: (i % tiles_per_seq, 0)),
        ],
        out_specs=[pl.BlockSpec((tm, w), row) for w, _ in outs],
        out_shape=[jax.ShapeDtypeStruct((n, w), dt) for w, dt in outs],
        compiler_params=_cparams(("arbitrary",)),
        name="inproj",
    )(x2, mod3, g_mix, w_pad, cos_t, sin_t)


def _compress_kernel(kc_ref, vc_ref, wkt_ref, wkb_ref, wvt_ref, wvb_ref, pk_ref, pv_ref,
                     w1k_ref, w1v_ref, w2k_ref, w2v_ref, okc_ref, ovc_ref):
    def one(x_ref, wt_ref, wb_ref, pos_ref, w1_ref, w2_ref, o_ref):
        pieces = x_ref[0]
        top = _dot(pieces, wt_ref[...])
        bot = _dot(pieces, wb_ref[...])
        nrow = bot.shape[0]
        bot = pltpu.roll(bot, nrow - 1, 0)
        cpos = _dot(pos_ref[...], w1_ref[...], precision=_HIGHEST)
        w2 = w2_ref[...].astype(bf16)
        outs = []
        for g in range(NSA_KV_GROUPS):
            sl = slice(g * CMP_HIDDEN, (g + 1) * CMP_HIDDEN)
            hid = _gelu(top[:, sl] + bot[:, sl] + cpos)
            outs.append(_dot(hid.astype(bf16), w2))
        o_ref[0] = jnp.concatenate(outs, axis=1).astype(bf16)

    one(kc_ref, wkt_ref, wkb_ref, pk_ref, w1k_ref, w2k_ref, okc_ref)
    one(vc_ref, wvt_ref, wvb_ref, pv_ref, w1v_ref, w2v_ref, ovc_ref)


def _compress(kc3, vc3, wkt, wkb, wvt, wvb, pk, pv, w1k, w1v, w2k, w2v):
    bsz, npieces, width = kc3.shape
    full = lambda a: pl.BlockSpec(a.shape, lambda b: (0,) * a.ndim)
    per_b = pl.BlockSpec((1, npieces, width), lambda b: (b, 0, 0))
    out_b = pl.BlockSpec((1, npieces, KV_WIDTH), lambda b: (b, 0, 0))
    return pl.pallas_call(
        _compress_kernel,
        grid=(bsz,),
        in_specs=[per_b, per_b] + [full(a) for a in (wkt, wkb, wvt, wvb, pk, pv, w1k, w1v, w2k, w2v)],
        out_specs=[out_b, out_b],
        out_shape=[jax.ShapeDtypeStruct((bsz, npieces, KV_WIDTH), bf16)] * 2,
        compiler_params=_cparams(("arbitrary",)),
        name="compress",
    )(kc3, vc3, wkt, wkb, wvt, wvb, pk, pv, w1k, w1v, w2k, w2v)


_TQ = 128
_NSA_CLASSES = 4


def _softmax_parts(s, mask):
    sm = jnp.where(mask, s, _NEG)
    m = jnp.max(sm, axis=-1, keepdims=True)
    p = jnp.where(mask, jnp.exp(sm - m), 0.0)
    den = jnp.maximum(jnp.sum(p, axis=-1, keepdims=True), 1e-30)
    return p, den


def _nsa_kernel(q_ref, kvsw_ref, kc_ref, vc_ref, gate_ref, selmap_ref, expand_ref, o_ref, *, seq):
    tq = _TQ
    qi = pl.program_id(1)
    n_cls = _NSA_CLASSES
    per_cls = (seq // tq) // n_cls
    for cls in range(n_cls):
        @pl.when((qi >= cls * per_cls) & (qi < (cls + 1) * per_cls))
        def _(cls=cls):
            _nsa_tile(q_ref, kvsw_ref, kc_ref, vc_ref, gate_ref, selmap_ref, expand_ref, o_ref,
                      seq=seq, width=(cls + 1) * (seq // n_cls))


def _nsa_tile(q_ref, kvsw_ref, kc_ref, vc_ref, gate_ref, selmap_ref, expand_ref, o_ref, *, seq, width):
    tq = _TQ
    rows = NSA_GROUP * tq
    n_sel = seq // SEL_BLOCK
    n_cmp_pad = seq // CMP_STRIDE
    t0 = pl.program_id(1) * tq
    q = q_ref[...]
    gates = gate_ref[...]
    tpos = t0 + lax.broadcasted_iota(i32, (tq, 1), 0)
    t4 = t0 + (lax.broadcasted_iota(i32, (rows, 1), 0) & (tq - 1))
    wk = WINDOW + tq
    ws = pl.multiple_of(jnp.maximum(t0 - WINDOW, 0), tq)
    ks_all = kvsw_ref[0, 0:width, 0:128]
    vs_all = kvsw_ref[0, 0:width, 128:256]
    kw_all = kvsw_ref[0, pl.ds(ws, wk), 256:384]
    vw_all = kvsw_ref[0, pl.ds(ws, wk), 384:512]
    kc_all = kc_ref[0]
    vc_all = vc_ref[0]
    cend = lax.broadcasted_iota(i32, (1, n_cmp_pad), 1) * CMP_STRIDE + (CMP_BLOCK - 1)
    blk = lax.broadcasted_iota(i32, (1, n_sel), 1)
    kpos = lax.broadcasted_iota(i32, (1, width), 1)
    kpos_w = ws + lax.broadcasted_iota(i32, (1, wk), 1)
    cur = tpos >> 6
    forced = (blk == 0) | (blk == cur) | (blk == cur - 1)
    causal_blk = blk * SEL_BLOCK <= tpos
    pieces = []
    for g in range(NSA_KV_GROUPS):
        gs = slice(g * HEAD_DIM, (g + 1) * HEAD_DIM)
        qg = jnp.concatenate(
            [q[:, (NSA_GROUP * g + r) * HEAD_DIM:(NSA_GROUP * g + r + 1) * HEAD_DIM] for r in range(NSA_GROUP)], axis=0)
        p_c, den_c = _softmax_parts(_dot_nt(qg, kc_all[:, gs]), cend <= t4)
        o_c = _dot(p_c.astype(bf16), vc_all[:, gs]) / den_c
        pn = p_c / den_c
        pc_sum = pn[0:tq]
        for r in range(1, NSA_GROUP):
            pc_sum = pc_sum + pn[r * tq:(r + 1) * tq]
        imp = _dot(pc_sum, selmap_ref[...], precision=_HIGHEST)
        imp = jnp.where(forced, imp + FORCE_BONUS, imp)
        imp = jnp.where(causal_blk, imp, -1.0)
        rank = jnp.zeros((tq, n_sel), f32)
        for j in range(n_sel):
            col = imp[:, j:j + 1]
            ahead = jnp.where(col > imp, 1.0, jnp.where(col == imp, jnp.where(blk > j, 1.0, 0.0), 0.0))
            rank = rank + ahead
        sel = jnp.where(rank < float(min(SEL_TOPK, n_sel)), 1.0, 0.0).astype(bf16)
        sel4 = jnp.concatenate([sel] * NSA_GROUP, axis=0)
        selx = _dot(sel4, expand_ref[:, 0:width])
        m_s = jnp.where(kpos <= t4, selx, 0.0) > 0.5
        p_s, den_s = _softmax_parts(_dot_nt(qg, ks_all[:, gs]), m_s)
        o_s = _dot(p_s.astype(bf16), vs_all[:, gs]) / den_s
        dist = t4 - kpos_w
        m_w = jnp.where(dist >= 0, jnp.where(dist < WINDOW, 1.0, 0.0), 0.0) > 0.5
        p_w, den_w = _softmax_parts(_dot_nt(qg, kw_all[:, gs]), m_w)
        o_w = _dot(p_w.astype(bf16), vw_all[:, gs]) / den_w

        def gcol(br):
            return jnp.concatenate(
                [gates[:, (NSA_GROUP * g + r) * 3 + br:(NSA_GROUP * g + r) * 3 + br + 1] for r in range(NSA_GROUP)], axis=0)

        o = gcol(0) * o_c + gcol(1) * o_s + gcol(2) * o_w
        pieces += [o[r * tq:(r + 1) * tq] for r in range(NSA_GROUP)]
    o_ref[...] = jnp.concatenate(pieces, axis=1).astype(bf16)


def _nsa(q, kvsw3, kc3, vc3, gates, selmap, expand, seq):
    n = q.shape[0]
    bsz = n // seq
    tq = _TQ
    nq = seq // tq
    return pl.pallas_call(
        functools.partial(_nsa_kernel, seq=seq),
        grid=(bsz, nq),
        in_specs=[
            pl.BlockSpec((tq, NSA_WIDTH), lambda b, i: (b * nq + i, 0)),
            pl.BlockSpec((1, seq, 4 * KV_WIDTH), lambda b, i: (b, 0, 0)),
            pl.BlockSpec((1,) + kc3.shape[1:], lambda b, i: (b, 0, 0)),
            pl.BlockSpec((1,) + vc3.shape[1:], lambda b, i: (b, 0, 0)),
            pl.BlockSpec((tq, GATE_PAD), lambda b, i: (b * nq + i, 0)),
            pl.BlockSpec(selmap.shape, lambda b, i: (0, 0)),
            pl.BlockSpec(expand.shape, lambda b, i: (0, 0)),
        ],
        out_specs=pl.BlockSpec((tq, NSA_WIDTH), lambda b, i: (b * nq + i, 0)),
        out_shape=jax.ShapeDtypeStruct((n, NSA_WIDTH), bf16),
        compiler_params=_cparams(("arbitrary", "arbitrary")),
        name="nsa",
    )(q, kvsw3, kc3, vc3, gates, selmap, expand)


def _hgrn_kernel(hq_ref, hf_ref, hi_ref, hg_ref, lbl_ref, ng_ref, o_ref, st_ref, *, seq, layer):
    c = HGRN_CHUNK
    logits = lbl_ref[...]
    e = jnp.exp(logits - jnp.max(logits, axis=0, keepdims=True))
    sm = e / jnp.sum(e, axis=0, keepdims=True)
    lb = sm[0:1]
    for l in range(1, layer + 1):
        lb = lb + sm[l:l + 1]
    st_ref[...] = jnp.zeros_like(st_ref)
    r_i = lax.broadcasted_iota(i32, (c, c), 0)
    c_i = lax.broadcasted_iota(i32, (c, c), 1)
    causal = r_i >= c_i
    tri = jnp.where(causal, 1.0, 0.0)
    ng = ng_ref[...]

    def body(ci, carry):
        r0 = pl.multiple_of(ci * c, c)
        hq = hq_ref[pl.ds(r0, c), :].astype(f32)
        hf = hf_ref[pl.ds(r0, c), :]
        v = hi_ref[pl.ds(r0, c), :]
        hg = hg_ref[pl.ds(r0, c), :].astype(f32)
        f = lb + (1.0 - lb) * jax.nn.sigmoid(hf)
        k = 1.0 - f
        qv = hq * jax.nn.sigmoid(hq)
        bcum = _dot(tri, jnp.log(f), precision=_HIGHEST)
        bend = bcum[c - 1:c, :]
        q_dec = (qv * jnp.exp(bcum)).astype(bf16)
        k_inv = (k * jnp.exp(-bcum)).astype(bf16)
        k_end = (k * jnp.exp(bend - bcum)).astype(bf16)
        dec_end = jnp.exp(bend)
        outs = []
        for h in range(HGRN_HEADS):
            sl = slice(h * HGRN_EXPAND, (h + 1) * HGRN_EXPAND)
            a = jnp.where(causal, _dot_nt(q_dec[:, sl], k_inv[:, sl]), 0.0)
            st = st_ref[h]
            o = _dot(a.astype(bf16), v[:, sl]) + _dot_nt(q_dec[:, sl], st.astype(bf16))
            upd = lax.dot_general(v[:, sl], k_end[:, sl], (((0,), (0,)), ((), ())), preferred_element_type=f32)
            st_ref[h] = st * dec_end[:, sl] + upd
            y = _rms(o, ng) * (hg[:, sl] * jax.nn.sigmoid(hg[:, sl]))
            outs.append(y)
        o_ref[pl.ds(r0, c), :] = jnp.concatenate(outs, axis=1).astype(bf16)
        return carry

    lax.fori_loop(0, seq // c, body, 0)


def _hgrn(hq, hf, hi, hg, lb_logits, norm_g, seq, layer):
    n = hq.shape[0]
    bsz = n // seq
    per_b = pl.BlockSpec((seq, HGRN_WIDTH), lambda b: (b, 0))
    return pl.pallas_call(
        functools.partial(_hgrn_kernel, seq=seq, layer=layer),
        grid=(bsz,),
        in_specs=[per_b, per_b, per_b, per_b,
                  pl.BlockSpec(lb_logits.shape, lambda b: (0, 0)),
                  pl.BlockSpec(norm_g.shape, lambda b: (0, 0))],
        out_specs=per_b,
        out_shape=jax.ShapeDtypeStruct((n, HGRN_WIDTH), bf16),
        scratch_shapes=[pltpu.VMEM((HGRN_HEADS, HGRN_HEAD_V, HGRN_EXPAND), f32)],
        compiler_params=_cparams(("arbitrary",)),
        name="hgrn",
    )(hq, hf, hi, hg, lb_logits, norm_g)


def _merge_kernel(on_ref, oh_ref, mg_ref, x_ref, mod_ref, g_ref, wb0_ref, wb1_ref, wo_ref, x1_ref, h2_ref, hb_ref):
    a = _dot(on_ref[...], wb0_ref[...])
    b = _dot(oh_ref[...], wb1_ref[...])
    mg = mg_ref[...].astype(f32)
    y = mg[:, :D_MODEL] * a + mg[:, D_MODEL:] * b
    y2 = _dot(y.astype(bf16), wo_ref[...])
    x1 = x_ref[...] + mod_ref[0, 2:3, :] * y2
    h2 = _rms(x1, g_ref[...]) * (1.0 + mod_ref[0, 4:5, :]) + mod_ref[0, 3:4, :]
    rows = x1.shape[0]
    x1_ref[...] = x1.reshape(rows, SUBLANES, LANES)
    h2_ref[...] = h2.reshape(rows, SUBLANES, LANES)
    hb_ref[...] = h2.astype(bf16)


def _merge(o_nsa, o_hgrn, mg, x2, mod3, g_ffn, wb0, wb1, wo, seq):
    n = x2.shape[0]
    tm = 256
    tiles_per_seq = seq // tm
    row = lambda i: (i, 0)
    full = lambda a: pl.BlockSpec(a.shape, lambda i: (0,) * a.ndim)
    return pl.pallas_call(
        _merge_kernel,
        grid=(n // tm,),
        in_specs=[
            pl.BlockSpec((tm, NSA_WIDTH), row), pl.BlockSpec((tm, HGRN_WIDTH), row),
            pl.BlockSpec((tm, 2 * D_MODEL), row), pl.BlockSpec((tm, D_MODEL), row),
            pl.BlockSpec((1, 6, D_MODEL), lambda i: (i // tiles_per_seq, 0, 0)),
            full(g_ffn), full(wb0), full(wb1), full(wo),
        ],
        out_specs=[pl.BlockSpec((tm, SUBLANES, LANES), lambda i: (i, 0, 0))] * 2 + [pl.BlockSpec((tm, D_MODEL), row)],
        out_shape=[jax.ShapeDtypeStruct((n, SUBLANES, LANES), f32)] * 2 + [jax.ShapeDtypeStruct((n, D_MODEL), bf16)],
        compiler_params=_cparams(("arbitrary",)),
        name="merge",
    )(o_nsa, o_hgrn, mg, x2, mod3, g_ffn, wb0, wb1, wo)


_TR = 256


def _topk_rows(s, k, payload=None):
    n = s.shape[0]
    rowid = lax.broadcasted_iota(i32, s.shape, 0).astype(f32)
    vals, picks = [], []
    for _ in range(k):
        m = jnp.max(s, axis=0, keepdims=True)
        i = jnp.min(jnp.where(s == m, rowid, float(n)), axis=0, keepdims=True)
        hit = rowid == i
        vals.append(m)
        picks.append(i if payload is None else jnp.max(jnp.where(hit, payload, -1.0), axis=0, keepdims=True))
        s = jnp.where(hit, -jnp.inf, s)
    return vals, picks


def _pair_candidates(v1, i1, v2, i2):
    k = len(v1)
    s2 = jnp.concatenate(v2, axis=0)
    e2 = jnp.concatenate(i2, axis=0)
    sub = lax.broadcasted_iota(i32, (SUBLANES, s2.shape[1]), 0)
    comb, cand = [], []
    for a in range(k // 2):
        nb = k // (a + 1)
        rows = -(-nb // SUBLANES) * SUBLANES
        c = v1[a] + s2[0:rows]
        if nb < rows:
            c = jnp.where(sub < nb, c, -jnp.inf)
        comb.append(c)
        cand.append(i1[a] * float(PEER_NKEYS) + e2[0:rows])
    comb.append(jnp.concatenate(v1[k // 2:], axis=0) + s2[0:1])
    cand.append(jnp.concatenate(i1[k // 2:], axis=0) * float(PEER_NKEYS) + e2[0:1])
    return jnp.concatenate(comb, axis=0), jnp.concatenate(cand, axis=0)


def _route_kernel(h_ref, wq_ref, sk_ref, idx_ref, gate_ref):
    h = h_ref[...]
    tr = h.shape[0]
    k = PEER_TOPK
    half = PEER_QDIM // 2
    experts, gates = [], []
    for hd in range(PEER_HEADS):
        tops = []
        for p in range(2):
            grp = hd * 2 + p
            q_t = _dot_nt(wq_ref[grp * half:(grp + 1) * half, :], h)
            s_t = _dot(sk_ref[grp], q_t.astype(bf16))
            tops.append(_topk_rows(s_t, k))
        (v1, i1), (v2, i2) = tops
        comb, cand = _pair_candidates(v1, i1, v2, i2)
        tv, picked = _topk_rows(comb, k, payload=cand)
        experts += picked
        ex = [jnp.exp(tv[r] - tv[0]) for r in range(k)]
        den = ex[0]
        for r in range(1, k):
            den = den + ex[r]
        gates += [e / den for e in ex]
    idx_t = jnp.concatenate(experts, axis=0).astype(i32)
    gate_t = jnp.concatenate(gates, axis=0)
    for blk in range(tr // LANES):
        sl = slice(blk * LANES, (blk + 1) * LANES)
        idx_ref[sl, :] = idx_t[:, sl].T
        gate_ref[sl, :] = gate_t[:, sl].T


def _route(h2, wq_t, sk):
    n = h2.shape[0]
    tr = _TR
    ncol = PEER_HEADS * PEER_TOPK
    return pl.pallas_call(
        _route_kernel,
        grid=(n // tr,),
        in_specs=[
            pl.BlockSpec((tr, D_MODEL), lambda i: (i, 0)),
            pl.BlockSpec(wq_t.shape, lambda i: (0, 0)),
            pl.BlockSpec(sk.shape, lambda i: (0, 0, 0)),
        ],
        out_specs=[pl.BlockSpec((tr, ncol), lambda i: (i, 0))] * 2,
        out_shape=[jax.ShapeDtypeStruct((n, ncol), i32), jax.ShapeDtypeStruct((n, ncol), f32)],
        compiler_params=_cparams(("arbitrary",)),
        name="route",
    )(h2, wq_t, sk)


_TS = 64
_TG = 4
_NSLOT = 4
_AHEAD = (_NSLOT - 1) * _TG
_NROW = PEER_HEADS * PEER_TOPK
_ROW_TILES = 2 * D_MODEL // LANES
_TP = 2


def _split_bf16(x):
    hi = x.astype(bf16)
    return hi, (x - hi.astype(f32)).astype(bf16)


def _expert_kernel(idx0_ref, idxn_ref, gate_ref, h_ref, x1_ref, mod_ref, gfin_ref, rsum_ref, rrep_ref, uv_ref, o_ref,
                   *scratch):
    step = pl.program_id(0)
    bufs, out_tiles, sem = scratch[:_NSLOT], scratch[_NSLOT], scratch[_NSLOT + 1]
    n_dma = _TG * _NROW

    def row_copy(e, slot, j):
        return pltpu.make_async_copy(uv_ref.at[e], bufs[slot].at[j], sem.at[slot])

    def wait(slot):
        pltpu.make_async_copy(uv_ref.at[pl.ds(0, n_dma)], bufs[slot], sem.at[slot]).wait()

    @pl.when(step == 0)
    def _():
        for slot in range(_NSLOT - 1):
            def body(t, c, slot=slot):
                for k in range(_NROW):
                    row_copy(idx0_ref[slot * _TG + t, k], slot, t * _NROW + k).start(priority=k % 2)
                return c
            lax.fori_loop(0, _TG, body, 0)

    gt_f = mod_ref[0, 5]
    gfin = gfin_ref[...]
    width = _TP * LANES
    r_i = lax.broadcasted_iota(i32, (width, width), 0) // LANES
    c_i = lax.broadcasted_iota(i32, (width, width), 1) // LANES
    lane_sum = jnp.where(r_i == c_i, 1.0, 0.0).astype(bf16)
    eye = jnp.where(lax.broadcasted_iota(i32, (_NROW, LANES), 0) == lax.broadcasted_iota(i32, (_NROW, LANES), 1),
                    1.0, 0.0)

    def evaluate(tok0, slot, t0):
        src = bufs[slot]
        z, vs = [], []
        for dt in range(_TP):
            uv = src[pl.ds((t0 + dt) * _NROW, _NROW)].astype(f32)
            vs.append(uv[:, SUBLANES:2 * SUBLANES, :])
            u = uv[:, 0:SUBLANES, :]
            z.append((u * h_ref[tok0 + dt][None]).reshape(_NROW * SUBLANES, LANES).astype(bf16))
        part = _dot(rsum_ref[...], jnp.concatenate(z, axis=1))
        p_hi, p_lo = _split_bf16(part)
        a = _dot(p_hi, lane_sum) + _dot(p_lo, lane_sum)
        g_diag = jnp.concatenate([eye * gate_ref[pl.ds(tok0 + dt, 1), :] for dt in range(_TP)], axis=1)
        g = _dot(g_diag.astype(bf16), lane_sum)
        w = (g * _gelu(a)).astype(bf16)
        spread = _dot(rrep_ref[...], w)
        for dt in range(_TP):
            tok = tok0 + dt
            wt = spread[:, dt * LANES:(dt + 1) * LANES].reshape(_NROW, SUBLANES, LANES)
            y = jnp.sum(wt * vs[dt], axis=0)
            x2 = x1_ref[tok] + gt_f * y
            ms = jnp.sum(jnp.sum(x2 * x2, axis=1, keepdims=True), axis=0, keepdims=True) * (1.0 / D_MODEL)
            out_tiles[tok] = x2 * lax.rsqrt(ms + EPS) * gfin

    def do_group(grp, slot):
        wait(slot)
        dst = (slot + _NSLOT - 1) % _NSLOT
        for t0 in range(0, _TG, _TP):
            evaluate(grp * _TG + t0, slot, t0)
            for t in range(t0, t0 + _TP):
                for k in range(_NROW):
                    row_copy(idxn_ref[grp * _TG + t, k], dst, t * _NROW + k).start(priority=k % 2)

    def cycle(p, c):
        for slot in range(_NSLOT):
            do_group(_NSLOT * p + slot, slot)
        return c
    lax.fori_loop(0, _TS // (_NSLOT * _TG), cycle, 0)
    o_ref[...] = out_tiles[...].reshape(_TS, D_MODEL)

    @pl.when(step == pl.num_programs(0) - 1)
    def _():
        for slot in range(_NSLOT - 1):
            wait(slot)


def _experts(idx, gate, h3, x13, mod4, gfin, uv, seq):
    n = idx.shape[0]
    steps_per_seq = seq // _TS
    tok3 = lambda i: (i, 0, 0)
    idx_next = jnp.roll(idx, -_AHEAD, axis=0)
    rsum = jnp.asarray(np.repeat(np.eye(_NROW, dtype=np.float32), SUBLANES, axis=1), dtype=bf16)
    buf = pltpu.VMEM((_TG * _NROW, _ROW_TILES, LANES), bf16)
    return pl.pallas_call(
        _expert_kernel,
        grid=(n // _TS,),
        in_specs=[
            pl.BlockSpec((_NSLOT * _TG, _NROW), lambda i: (0, 0), memory_space=pltpu.SMEM),
            pl.BlockSpec((_TS, _NROW), lambda i: (i, 0), memory_space=pltpu.SMEM),
            pl.BlockSpec((_TS, _NROW), lambda i: (i, 0)),
            pl.BlockSpec((_TS, SUBLANES, LANES), tok3),
            pl.BlockSpec((_TS, SUBLANES, LANES), tok3),
            pl.BlockSpec((1, 6, SUBLANES, LANES), lambda i: (i // steps_per_seq, 0, 0, 0)),
            pl.BlockSpec((SUBLANES, LANES), lambda i: (0, 0)),
            pl.BlockSpec(rsum.shape, lambda i: (0, 0)),
            pl.BlockSpec(rsum.shape[::-1], lambda i: (0, 0)),
            pl.BlockSpec(memory_space=pl.ANY),
        ],
        out_specs=pl.BlockSpec((_TS, D_MODEL), lambda i: (i, 0)),
        out_shape=jax.ShapeDtypeStruct((n, D_MODEL), f32),
        scratch_shapes=[buf] * _NSLOT + [pltpu.VMEM((_TS, SUBLANES, LANES), f32), pltpu.SemaphoreType.DMA((_NSLOT,))],
        compiler_params=_cparams(("arbitrary",)),
        name="experts",
    )(idx, idx_next, gate, h3, x13, mod4, gfin, rsum, rsum.T, uv)


def _rope_tables(seq):
    half = HEAD_DIM // 2
    inv = ROPE_THETA ** (-np.arange(half, dtype=np.float32) / half)
    ang = np.arange(seq, dtype=np.float32)[:, None] * inv[None, :].astype(np.float32)
    cos = np.cos(ang).astype(np.float32)
    sin = np.sin(ang).astype(np.float32)
    cos_t = np.tile(np.concatenate([cos, cos], axis=1), (1, NSA_HEADS))
    sin_t = np.tile(np.concatenate([-sin, sin], axis=1), (1, NSA_HEADS))
    return jnp.asarray(cos_t), jnp.asarray(sin_t)


def _selection_map(n_cmp_pad, n_sel):
    r_sel = SEL_BLOCK // CMP_STRIDE
    r_cmp = CMP_BLOCK // CMP_STRIDE
    i = np.arange(n_cmp_pad)[:, None]
    j = np.arange(n_sel)[None, :]
    d = i - r_sel * j
    cnt = np.minimum(d, r_sel - 1) - np.maximum(d - r_cmp + 1, 0) + 1
    cnt = np.clip(cnt, 0, None).astype(np.float32)
    cnt[n_cmp_pad - r_cmp + 1:] = 0.0
    return jnp.asarray(cnt)


def _block_expand(n_sel, seq):
    e = (np.arange(seq)[None, :] // SEL_BLOCK == np.arange(n_sel)[:, None]).astype(np.float32)
    return jnp.asarray(e, dtype=bf16)


def _compress_weights(w1):
    eye = jnp.eye(NSA_KV_GROUPS, dtype=w1.dtype)
    out = []
    for part in range(CMP_BLOCK // CMP_STRIDE):
        w = w1[part * CMP_STRIDE * HEAD_DIM:(part + 1) * CMP_STRIDE * HEAD_DIM].reshape(CMP_STRIDE, HEAD_DIM, CMP_HIDDEN)
        big = jnp.einsum('pdc,gh->pgdhc', w, eye).reshape(CMP_STRIDE * KV_WIDTH, NSA_KV_GROUPS * CMP_HIDDEN)
        out.append(big.astype(bf16))
    return out


def kernel(x, c, w_ada, b_ada, g_mix, g_ffn, w_in, cmp_pos_k, cmp_pos_v, w_ck1, w_ck2, w_cv1, w_cv2, hgrn_lb_logits, hgrn_out_norm, w_branch, w_out, w_peer_q, peer_sub_keys, peer_u, peer_v, g_final):
    bsz, seq, d = x.shape
    n = bsz * seq
    depth = w_ada.shape[0]
    assert depth == 1, "single-layer block only"
    n_sel = seq // SEL_BLOCK
    n_pieces = seq // CMP_STRIDE
    cos_t, sin_t = _rope_tables(seq)
    selmap = _selection_map(n_pieces, n_sel)
    expand = _block_expand(n_sel, seq)
    xcur = x.reshape(n, d)
    for l in range(depth):
        mod = _adaln(c, w_ada[l], b_ada[l].reshape(1, 6 * d))
        mod3 = mod.reshape(bsz, 6, d)
        w_pad = jnp.concatenate(
            [w_in[l][:, :_C_GL + 3 * NSA_HEADS], jnp.zeros((d, GATE_PAD - 3 * NSA_HEADS), w_in.dtype),
             w_in[l][:, _C_GL + 3 * NSA_HEADS:]], axis=1).astype(bf16)
        q, kc, vc, kvsw, gates, hq, hf, hi, hg, mg = _inproj(xcur, mod3, g_mix[l].reshape(1, d), w_pad, cos_t, sin_t, seq)
        wkt, wkb = _compress_weights(w_ck1[l])
        wvt, wvb = _compress_weights(w_cv1[l])
        kc_c, vc_c = _compress(
            kc.reshape(bsz, n_pieces, CMP_STRIDE * KV_WIDTH), vc.reshape(bsz, n_pieces, CMP_STRIDE * KV_WIDTH),
            wkt, wkb, wvt, wvb, cmp_pos_k[l].reshape(1, -1), cmp_pos_v[l].reshape(1, -1),
            w_ck1[l], w_cv1[l], w_ck2[l], w_cv2[l])
        o_nsa = _nsa(q, kvsw.reshape(bsz, seq, 4 * KV_WIDTH), kc_c, vc_c, gates, selmap, expand, seq)
        o_hgrn = _hgrn(hq, hf, hi, hg, hgrn_lb_logits, hgrn_out_norm[l].reshape(1, -1), seq, l)
        x1, h2, h2_b = _merge(o_nsa, o_hgrn, mg, xcur, mod3, g_ffn[l].reshape(1, d),
                              w_branch[l, 0].astype(bf16), w_branch[l, 1].astype(bf16), w_out[l].astype(bf16), seq)
        wq_t = w_peer_q[l].T.astype(bf16)
        sk = jnp.transpose(peer_sub_keys[l], (1, 0, 2, 3)).reshape(2 * PEER_HEADS, PEER_NKEYS, PEER_QDIM // 2).astype(bf16)
        idx, gate = _route(h2_b, wq_t, sk)
        uv = jnp.concatenate([peer_u[l], peer_v[l]], axis=1).astype(bf16).reshape(-1, _ROW_TILES, LANES)
        xcur = _experts(idx, gate, h2, x1, mod.reshape(bsz, 6, SUBLANES, LANES), g_final.reshape(SUBLANES, LANES), uv, seq)
    return xcur.reshape(bsz, seq, d)
```

```python
import functools

import numpy as np
import jax
import jax.numpy as jnp
from jax import lax
from jax.experimental import pallas as pl
from jax.experimental.pallas import tpu as pltpu

f32 = jnp.float32
bf16 = jnp.bfloat16
i32 = jnp.int32
_HIGHEST = lax.Precision.HIGHEST

D_MODEL = 1024
EPS = 1e-6
HEAD_DIM = 64
ROPE_THETA = 10000.0
NSA_HEADS = 8
NSA_KV_GROUPS = 2
NSA_GROUP = NSA_HEADS // NSA_KV_GROUPS
NSA_WIDTH = NSA_HEADS * HEAD_DIM
KV_WIDTH = NSA_KV_GROUPS * HEAD_DIM
CMP_BLOCK = 32
CMP_STRIDE = 16
CMP_HIDDEN = 128
SEL_BLOCK = 64
SEL_TOPK = 16
WINDOW = 512
FORCE_BONUS = 1000.0
HGRN_HEADS = 4
HGRN_EXPAND = 128
HGRN_HEAD_V = 128
HGRN_WIDTH = HGRN_HEADS * HGRN_EXPAND
HGRN_CHUNK = 32
PEER_HEADS = 8
PEER_NKEYS = 128
PEER_QDIM = 256
PEER_TOPK = 16
GATE_PAD = 128

LANES = 128
SUBLANES = 8
VMEM_LIMIT = 56 * 1024 * 1024

_NEG = -1e30


def _cparams(sem):
    return pltpu.CompilerParams(dimension_semantics=sem, vmem_limit_bytes=VMEM_LIMIT)


def _gelu(x):
    return 0.5 * x * (1.0 + jnp.tanh(0.7978845608028654 * (x + 0.044715 * (x * x * x))))


def _dot(a, b, **kw):
    return jnp.dot(a, b, preferred_element_type=f32, **kw)


def _dot_nt(a, b):
    return lax.dot_general(a, b, (((1,), (1,)), ((), ())), preferred_element_type=f32)


def _rms(x, g):
    return x * lax.rsqrt(jnp.mean(x * x, axis=-1, keepdims=True) + EPS) * g


def _adaln_kernel(c_ref, w_ref, b_ref, o_ref):
    c = c_ref[...]
    cs = c * jax.nn.sigmoid(c)
    o_ref[...] = _dot(cs, w_ref[...], precision=_HIGHEST) + b_ref[...]


def _adaln(c, w, b):
    bsz = c.shape[0]
    tn = 512
    return pl.pallas_call(
        _adaln_kernel,
        grid=(6 * D_MODEL // tn,),
        in_specs=[
            pl.BlockSpec((bsz, D_MODEL), lambda j: (0, 0)),
            pl.BlockSpec((D_MODEL, tn), lambda j: (0, j)),
            pl.BlockSpec((1, tn), lambda j: (0, j)),
        ],
        out_specs=pl.BlockSpec((bsz, tn), lambda j: (0, j)),
        out_shape=jax.ShapeDtypeStruct((bsz, 6 * D_MODEL), f32),
        compiler_params=_cparams(("arbitrary",)),
        name="adaln",
    )(c, w, b)


_C_Q = 0
_C_KC = 512
_C_VC = 640
_C_KS = 768
_C_VS = 896
_C_KW = 1024
_C_VW = 1152
_C_GL = 1280
_C_HQ = _C_GL + GATE_PAD
_C_HF = _C_HQ + 512
_C_HI = _C_HF + 512
_C_HG = _C_HI + 512
_C_MG = _C_HG + 512
_C_END = _C_MG + 2 * D_MODEL


def _inproj_kernel(x_ref, mod_ref, g_ref, w_ref, cos_ref, sin_ref,
                   q_ref, kc_ref, vc_ref, kvsw_ref, gate_ref, hq_ref, hf_ref, hi_ref, hg_ref, mg_ref):
    x = x_ref[...]
    sh = mod_ref[0, 0:1, :]
    sc = mod_ref[0, 1:2, :]
    h = (_rms(x, g_ref[...]) * (1.0 + sc) + sh).astype(bf16)

    def mm(c0, c1):
        return _dot(h, w_ref[:, c0:c1])

    cos = cos_ref[...]
    sin = sin_ref[...]

    def rope(a):
        width = a.shape[1]
        first = (lax.broadcasted_iota(i32, a.shape, 1) & (HEAD_DIM - 1)) < (HEAD_DIM // 2)
        partner = jnp.where(first, pltpu.roll(a, width - HEAD_DIM // 2, 1), pltpu.roll(a, HEAD_DIM // 2, 1))
        return a * cos[:, :width] + partner * sin[:, :width]

    q_ref[...] = (rope(mm(_C_Q, _C_KC)) * (HEAD_DIM ** -0.5)).astype(bf16)
    kc_ref[...] = rope(mm(_C_KC, _C_VC)).astype(bf16)
    vc_ref[...] = mm(_C_VC, _C_KS).astype(bf16)
    kvsw_ref[:, 0:128] = rope(mm(_C_KS, _C_VS)).astype(bf16)
    kvsw_ref[:, 128:256] = mm(_C_VS, _C_KW).astype(bf16)
    kvsw_ref[:, 256:384] = rope(mm(_C_KW, _C_VW)).astype(bf16)
    kvsw_ref[:, 384:512] = mm(_C_VW, _C_GL).astype(bf16)
    gate_ref[...] = jax.nn.sigmoid(mm(_C_GL, _C_HQ))
    hq_ref[...] = mm(_C_HQ, _C_HF).astype(bf16)
    hf_ref[...] = mm(_C_HF, _C_HI)
    hi_ref[...] = mm(_C_HI, _C_HG).astype(bf16)
    hg_ref[...] = mm(_C_HG, _C_MG).astype(bf16)
    mg_ref[...] = jax.nn.sigmoid(mm(_C_MG, _C_END)).astype(bf16)


def _inproj(x2, mod3, g_mix, w_pad, cos_t, sin_t, seq):
    n = x2.shape[0]
    tm = 256
    tiles_per_seq = seq // tm
    row = lambda i: (i, 0)
    outs = [
        (NSA_WIDTH, bf16), (KV_WIDTH, bf16), (KV_WIDTH, bf16), (4 * KV_WIDTH, bf16), (GATE_PAD, f32),
        (HGRN_WIDTH, bf16), (HGRN_WIDTH, f32), (HGRN_WIDTH, bf16), (HGRN_WIDTH, bf16), (2 * D_MODEL, bf16),
    ]
    return pl.pallas_call(
        _inproj_kernel,
        grid=(n // tm,),
        in_specs=[
            pl.BlockSpec((tm, D_MODEL), row),
            pl.BlockSpec((1, 6, D_MODEL), lambda i: (i // tiles_per_seq, 0, 0)),
            pl.BlockSpec((1, D_MODEL), lambda i: (0, 0)),
            pl.BlockSpec((D_MODEL, _C_END), lambda i: (0, 0)),
            pl.BlockSpec((tm, NSA_WIDTH), lambda i: (i % tiles_per_seq, 0)),
            pl.BlockSpec((tm, NSA_WIDTH), lambda i: (i % tiles_per_seq, 0)),
        ],
        out_specs=[pl.BlockSpec((tm, w), row) for w, _ in outs],
        out_shape=[jax.ShapeDtypeStruct((n, w), dt) for w, dt in outs],
        compiler_params=_cparams(("arbitrary",)),
        name="inproj",
    )(x2, mod3, g_mix, w_pad, cos_t, sin_t)


def _compress_kernel(kc_ref, vc_ref, wkt_ref, wkb_ref, wvt_ref, wvb_ref, pk_ref, pv_ref,
                     w1k_ref, w1v_ref, w2k_ref, w2v_ref, okc_ref, ovc_ref):
    def one(x_ref, wt_ref, wb_ref, pos_ref, w1_ref, w2_ref, o_ref):
        pieces = x_ref[0]
        top = _dot(pieces, wt_ref[...])
        bot = _dot(pieces, wb_ref[...])
        nrow = bot.shape[0]
        bot = pltpu.roll(bot, nrow - 1, 0)
        cpos = _dot(pos_ref[...], w1_ref[...], precision=_HIGHEST)
        w2 = w2_ref[...].astype(bf16)
        outs = []
        for g in range(NSA_KV_GROUPS):
            sl = slice(g * CMP_HIDDEN, (g + 1) * CMP_HIDDEN)
            hid = _gelu(top[:, sl] + bot[:, sl] + cpos)
            outs.append(_dot(hid.astype(bf16), w2))
        o_ref[0] = jnp.concatenate(outs, axis=1).astype(bf16)

    one(kc_ref, wkt_ref, wkb_ref, pk_ref, w1k_ref, w2k_ref, okc_ref)
    one(vc_ref, wvt_ref, wvb_ref, pv_ref, w1v_ref, w2v_ref, ovc_ref)


def _compress(kc3, vc3, wkt, wkb, wvt, wvb, pk, pv, w1k, w1v, w2k, w2v):
    bsz, npieces, width = kc3.shape
    full = lambda a: pl.BlockSpec(a.shape, lambda b: (0,) * a.ndim)
    per_b = pl.BlockSpec((1, npieces, width), lambda b: (b, 0, 0))
    out_b = pl.BlockSpec((1, npieces, KV_WIDTH), lambda b: (b, 0, 0))
    return pl.pallas_call(
        _compress_kernel,
        grid=(bsz,),
        in_specs=[per_b, per_b] + [full(a) for a in (wkt, wkb, wvt, wvb, pk, pv, w1k, w1v, w2k, w2v)],
        out_specs=[out_b, out_b],
        out_shape=[jax.ShapeDtypeStruct((bsz, npieces, KV_WIDTH), bf16)] * 2,
        compiler_params=_cparams(("arbitrary",)),
        name="compress",
    )(kc3, vc3, wkt, wkb, wvt, wvb, pk, pv, w1k, w1v, w2k, w2v)


_TQ = 128
_NSA_CLASSES = 4


def _softmax_parts(s, mask):
    sm = jnp.where(mask, s, _NEG)
    m = jnp.max(sm, axis=-1, keepdims=True)
    p = jnp.where(mask, jnp.exp(sm - m), 0.0)
    den = jnp.maximum(jnp.sum(p, axis=-1, keepdims=True), 1e-30)
    return p, den


def _softmax_plain(s):
    p = jnp.exp(s - jnp.max(s, axis=-1, keepdims=True))
    return p, jnp.sum(p, axis=-1, keepdims=True)


def _nsa_kernel(q_ref, kvsw_ref, kc_ref, vc_ref, gate_ref, selmap_ref, expand_ref, o_ref, *, seq):
    tq = _TQ
    qi = pl.program_id(1)
    n_cls = _NSA_CLASSES
    per_cls = (seq // tq) // n_cls
    for cls in range(n_cls):
        @pl.when((qi >= cls * per_cls) & (qi < (cls + 1) * per_cls))
        def _(cls=cls):
            _nsa_tile(q_ref, kvsw_ref, kc_ref, vc_ref, gate_ref, selmap_ref, expand_ref, o_ref,
                      seq=seq, width=(cls + 1) * (seq // n_cls))


def _nsa_tile(q_ref, kvsw_ref, kc_ref, vc_ref, gate_ref, selmap_ref, expand_ref, o_ref, *, seq, width):
    tq = _TQ
    rows = NSA_GROUP * tq
    n_sel = seq // SEL_BLOCK
    n_cmp_pad = seq // CMP_STRIDE
    t0 = pl.program_id(1) * tq
    q = q_ref[...]
    gates = gate_ref[...]
    tlane = t0 + lax.broadcasted_iota(i32, (1, tq), 1)
    t4 = t0 + (lax.broadcasted_iota(i32, (rows, 1), 0) & (tq - 1))
    wk = WINDOW + tq
    ws = pl.multiple_of(jnp.maximum(t0 - WINDOW, 0), tq)
    ks_all = kvsw_ref[0, 0:width, 0:128]
    vs_all = kvsw_ref[0, 0:width, 128:256]
    kw_all = kvsw_ref[0, pl.ds(ws, wk), 256:384]
    vw_all = kvsw_ref[0, pl.ds(ws, wk), 384:512]
    kc_all = kc_ref[0]
    vc_all = vc_ref[0]
    cend = lax.broadcasted_iota(i32, (1, n_cmp_pad), 1) * CMP_STRIDE + (CMP_BLOCK - 1)
    blk = lax.broadcasted_iota(i32, (n_sel, 1), 0)
    kpos = lax.broadcasted_iota(i32, (1, width), 1)
    kpos_w = ws + lax.broadcasted_iota(i32, (1, wk), 1)
    cur = tlane >> 6
    forced = (blk == 0) | (blk == cur) | (blk == cur - 1)
    causal_blk = blk * SEL_BLOCK <= tlane
    pieces = []
    for g in range(NSA_KV_GROUPS):
        gs = slice(g * HEAD_DIM, (g + 1) * HEAD_DIM)
        qg = jnp.concatenate(
            [q[:, (NSA_GROUP * g + r) * HEAD_DIM:(NSA_GROUP * g + r + 1) * HEAD_DIM] for r in range(NSA_GROUP)], axis=0)
        p_c, den_c = _softmax_parts(_dot_nt(qg, kc_all[:, gs]), cend <= t4)
        o_c = _dot(p_c.astype(bf16), vc_all[:, gs]) / den_c
        pn = p_c / den_c
        pc_sum = pn[0:tq]
        for r in range(1, NSA_GROUP):
            pc_sum = pc_sum + pn[r * tq:(r + 1) * tq]
        imp = lax.dot_general(selmap_ref[...], pc_sum, (((1,), (1,)), ((), ())),
                              preferred_element_type=f32, precision=_HIGHEST)
        imp = jnp.where(forced, imp + FORCE_BONUS, imp)
        imp = jnp.where(causal_blk, imp, -1.0)
        rank = jnp.zeros((n_sel, tq), f32)
        for j in range(n_sel):
            row = imp[j:j + 1, :]
            ahead = jnp.where(row > imp, 1.0, jnp.where(row == imp, jnp.where(blk > j, 1.0, 0.0), 0.0))
            rank = rank + ahead
        bias = jnp.where(rank < float(min(SEL_TOPK, n_sel)), jnp.where(causal_blk, 0.0, _NEG), _NEG).astype(bf16)
        bias_q = lax.dot_general(bias, expand_ref[:, 0:width], (((0,), (0,)), ((), ())),
                                 preferred_element_type=f32)
        bias4 = jnp.concatenate([bias_q] * NSA_GROUP, axis=0)
        p_s, den_s = _softmax_plain(jnp.where(kpos <= t4, _dot_nt(qg, ks_all[:, gs]) + bias4, _NEG))
        o_s = _dot(p_s.astype(bf16), vs_all[:, gs]) / den_s
        in_window = (t4 - kpos_w).astype(jnp.uint32) < WINDOW
        p_w, den_w = _softmax_plain(jnp.where(in_window, _dot_nt(qg, kw_all[:, gs]), _NEG))
        o_w = _dot(p_w.astype(bf16), vw_all[:, gs]) / den_w

        def gcol(br):
            return jnp.concatenate(
                [gates[:, (NSA_GROUP * g + r) * 3 + br:(NSA_GROUP * g + r) * 3 + br + 1] for r in range(NSA_GROUP)], axis=0)

        o = gcol(0) * o_c + gcol(1) * o_s + gcol(2) * o_w
        pieces += [o[r * tq:(r + 1) * tq] for r in range(NSA_GROUP)]
    o_ref[...] = jnp.concatenate(pieces, axis=1).astype(bf16)


def _nsa(q, kvsw3, kc3, vc3, gates, selmap, expand, seq):
    n = q.shape[0]
    bsz = n // seq
    tq = _TQ
    nq = seq // tq
    return pl.pallas_call(
        functools.partial(_nsa_kernel, seq=seq),
        grid=(bsz, nq),
        in_specs=[
            pl.BlockSpec((tq, NSA_WIDTH), lambda b, i: (b * nq + i, 0)),
            pl.BlockSpec((1, seq, 4 * KV_WIDTH), lambda b, i: (b, 0, 0)),
            pl.BlockSpec((1,) + kc3.shape[1:], lambda b, i: (b, 0, 0)),
            pl.BlockSpec((1,) + vc3.shape[1:], lambda b, i: (b, 0, 0)),
            pl.BlockSpec((tq, GATE_PAD), lambda b, i: (b * nq + i, 0)),
            pl.BlockSpec(selmap.shape, lambda b, i: (0, 0)),
            pl.BlockSpec(expand.shape, lambda b, i: (0, 0)),
        ],
        out_specs=pl.BlockSpec((tq, NSA_WIDTH), lambda b, i: (b * nq + i, 0)),
        out_shape=jax.ShapeDtypeStruct((n, NSA_WIDTH), bf16),
        compiler_params=_cparams(("arbitrary", "arbitrary")),
        name="nsa",
    )(q, kvsw3, kc3, vc3, gates, selmap, expand)


def _hgrn_kernel(hq_ref, hf_ref, hi_ref, hg_ref, lbl_ref, ng_ref, o_ref, st_ref, *, seq, layer):
    c = HGRN_CHUNK
    logits = lbl_ref[...]
    e = jnp.exp(logits - jnp.max(logits, axis=0, keepdims=True))
    sm = e / jnp.sum(e, axis=0, keepdims=True)
    lb = sm[0:1]
    for l in range(1, layer + 1):
        lb = lb + sm[l:l + 1]
    st_ref[...] = jnp.zeros_like(st_ref)
    r_i = lax.broadcasted_iota(i32, (c, c), 0)
    c_i = lax.broadcasted_iota(i32, (c, c), 1)
    causal = r_i >= c_i
    tri = jnp.where(causal, 1.0, 0.0)
    ng = ng_ref[...]

    def body(ci, carry):
        r0 = pl.multiple_of(ci * c, c)
        hq = hq_ref[pl.ds(r0, c), :].astype(f32)
        hf = hf_ref[pl.ds(r0, c), :]
        v = hi_ref[pl.ds(r0, c), :]
        hg = hg_ref[pl.ds(r0, c), :].astype(f32)
        f = lb + (1.0 - lb) * jax.nn.sigmoid(hf)
        k = 1.0 - f
        qv = hq * jax.nn.sigmoid(hq)
        bcum = _dot(tri, jnp.log(f), precision=_HIGHEST)
        bend = bcum[c - 1:c, :]
        q_dec = (qv * jnp.exp(bcum)).astype(bf16)
        k_inv = (k * jnp.exp(-bcum)).astype(bf16)
        k_end = (k * jnp.exp(bend - bcum)).astype(bf16)
        dec_end = jnp.exp(bend)
        outs = []
        for h in range(HGRN_HEADS):
            sl = slice(h * HGRN_EXPAND, (h + 1) * HGRN_EXPAND)
            a = jnp.where(causal, _dot_nt(q_dec[:, sl], k_inv[:, sl]), 0.0)
            st = st_ref[h]
            o = _dot(a.astype(bf16), v[:, sl]) + _dot_nt(q_dec[:, sl], st.astype(bf16))
            upd = lax.dot_general(v[:, sl], k_end[:, sl], (((0,), (0,)), ((), ())), preferred_element_type=f32)
            st_ref[h] = st * dec_end[:, sl] + upd
            y = _rms(o, ng) * (hg[:, sl] * jax.nn.sigmoid(hg[:, sl]))
            outs.append(y)
        o_ref[pl.ds(r0, c), :] = jnp.concatenate(outs, axis=1).astype(bf16)
        return carry

    lax.fori_loop(0, seq // c, body, 0, unroll=4)


def _hgrn(hq, hf, hi, hg, lb_logits, norm_g, seq, layer):
    n = hq.shape[0]
    bsz = n // seq
    per_b = pl.BlockSpec((seq, HGRN_WIDTH), lambda b: (b, 0))
    return pl.pallas_call(
        functools.partial(_hgrn_kernel, seq=seq, layer=layer),
        grid=(bsz,),
        in_specs=[per_b, per_b, per_b, per_b,
                  pl.BlockSpec(lb_logits.shape, lambda b: (0, 0)),
                  pl.BlockSpec(norm_g.shape, lambda b: (0, 0))],
        out_specs=per_b,
        out_shape=jax.ShapeDtypeStruct((n, HGRN_WIDTH), bf16),
        scratch_shapes=[pltpu.VMEM((HGRN_HEADS, HGRN_HEAD_V, HGRN_EXPAND), f32)],
        compiler_params=_cparams(("arbitrary",)),
        name="hgrn",
    )(hq, hf, hi, hg, lb_logits, norm_g)


def _merge_kernel(on_ref, oh_ref, mg_ref, x_ref, mod_ref, g_ref, wb0_ref, wb1_ref, wo_ref, x1_ref, h2_ref, hb_ref):
    a = _dot(on_ref[...], wb0_ref[...])
    b = _dot(oh_ref[...], wb1_ref[...])
    mg = mg_ref[...].astype(f32)
    y = mg[:, :D_MODEL] * a + mg[:, D_MODEL:] * b
    y2 = _dot(y.astype(bf16), wo_ref[...])
    x1 = x_ref[...] + mod_ref[0, 2:3, :] * y2
    h2 = _rms(x1, g_ref[...]) * (1.0 + mod_ref[0, 4:5, :]) + mod_ref[0, 3:4, :]
    rows = x1.shape[0]
    x1_ref[...] = x1.reshape(rows, SUBLANES, LANES)
    h2_ref[...] = h2.reshape(rows, SUBLANES, LANES)
    hb_ref[...] = h2.astype(bf16)


def _merge(o_nsa, o_hgrn, mg, x2, mod3, g_ffn, wb0, wb1, wo, seq):
    n = x2.shape[0]
    tm = 256
    tiles_per_seq = seq // tm
    row = lambda i: (i, 0)
    full = lambda a: pl.BlockSpec(a.shape, lambda i: (0,) * a.ndim)
    return pl.pallas_call(
        _merge_kernel,
        grid=(n // tm,),
        in_specs=[
            pl.BlockSpec((tm, NSA_WIDTH), row), pl.BlockSpec((tm, HGRN_WIDTH), row),
            pl.BlockSpec((tm, 2 * D_MODEL), row), pl.BlockSpec((tm, D_MODEL), row),
            pl.BlockSpec((1, 6, D_MODEL), lambda i: (i // tiles_per_seq, 0, 0)),
            full(g_ffn), full(wb0), full(wb1), full(wo),
        ],
        out_specs=[pl.BlockSpec((tm, SUBLANES, LANES), lambda i: (i, 0, 0))] * 2 + [pl.BlockSpec((tm, D_MODEL), row)],
        out_shape=[jax.ShapeDtypeStruct((n, SUBLANES, LANES), f32)] * 2 + [jax.ShapeDtypeStruct((n, D_MODEL), bf16)],
        compiler_params=_cparams(("arbitrary",)),
        name="merge",
    )(o_nsa, o_hgrn, mg, x2, mod3, g_ffn, wb0, wb1, wo)


_TR = 256


def _topk_rows(s, k, payload=None):
    n = s.shape[0]
    rowid = lax.broadcasted_iota(i32, s.shape, 0).astype(f32)
    vals, picks = [], []
    for _ in range(k):
        m = jnp.max(s, axis=0, keepdims=True)
        i = jnp.min(jnp.where(s == m, rowid, float(n)), axis=0, keepdims=True)
        hit = rowid == i
        vals.append(m)
        picks.append(i if payload is None else jnp.max(jnp.where(hit, payload, -1.0), axis=0, keepdims=True))
        s = jnp.where(hit, -jnp.inf, s)
    return vals, picks


def _pair_candidates(v1, i1, v2, i2):
    k = len(v1)
    s2 = jnp.concatenate(v2, axis=0)
    e2 = jnp.concatenate(i2, axis=0)
    sub = lax.broadcasted_iota(i32, (SUBLANES, s2.shape[1]), 0)
    comb, cand = [], []
    for a in range(k // 2):
        nb = k // (a + 1)
        rows = -(-nb // SUBLANES) * SUBLANES
        c = v1[a] + s2[0:rows]
        if nb < rows:
            c = jnp.where(sub < nb, c, -jnp.inf)
        comb.append(c)
        cand.append(i1[a] * float(PEER_NKEYS) + e2[0:rows])
    comb.append(jnp.concatenate(v1[k // 2:], axis=0) + s2[0:1])
    cand.append(jnp.concatenate(i1[k // 2:], axis=0) * float(PEER_NKEYS) + e2[0:1])
    return jnp.concatenate(comb, axis=0), jnp.concatenate(cand, axis=0)


def _route_kernel(h_ref, wq_ref, sk_ref, idx_ref, gate_ref):
    h = h_ref[...]
    tr = h.shape[0]
    k = PEER_TOPK
    half = PEER_QDIM // 2
    experts, gates = [], []
    for hd in range(PEER_HEADS):
        tops = []
        for p in range(2):
            grp = hd * 2 + p
            q_t = _dot_nt(wq_ref[grp * half:(grp + 1) * half, :], h)
            s_t = _dot(sk_ref[grp], q_t.astype(bf16))
            tops.append(_topk_rows(s_t, k))
        (v1, i1), (v2, i2) = tops
        comb, cand = _pair_candidates(v1, i1, v2, i2)
        tv, picked = _topk_rows(comb, k, payload=cand)
        experts += picked
        ex = [jnp.exp(tv[r] - tv[0]) for r in range(k)]
        den = ex[0]
        for r in range(1, k):
            den = den + ex[r]
        gates += [e / den for e in ex]
    idx_t = jnp.concatenate(experts, axis=0).astype(i32)
    gate_t = jnp.concatenate(gates, axis=0)
    for blk in range(tr // LANES):
        sl = slice(blk * LANES, (blk + 1) * LANES)
        idx_ref[sl, :] = idx_t[:, sl].T
        gate_ref[sl, :] = gate_t[:, sl].T


def _route(h2, wq_t, sk):
    n = h2.shape[0]
    tr = _TR
    ncol = PEER_HEADS * PEER_TOPK
    return pl.pallas_call(
        _route_kernel,
        grid=(n // tr,),
        in_specs=[
            pl.BlockSpec((tr, D_MODEL), lambda i: (i, 0)),
            pl.BlockSpec(wq_t.shape, lambda i: (0, 0)),
            pl.BlockSpec(sk.shape, lambda i: (0, 0, 0)),
        ],
        out_specs=[pl.BlockSpec((tr, ncol), lambda i: (i, 0))] * 2,
        out_shape=[jax.ShapeDtypeStruct((n, ncol), i32), jax.ShapeDtypeStruct((n, ncol), f32)],
        compiler_params=_cparams(("arbitrary",)),
        name="route",
    )(h2, wq_t, sk)


_TS = 64
_TG = 4
_NSLOT = 4
_AHEAD = (_NSLOT - 1) * _TG
_NROW = PEER_HEADS * PEER_TOPK
_ROW_TILES = 2 * D_MODEL // LANES
_TP = 2


def _split_bf16(x):
    hi = x.astype(bf16)
    return hi, (x - hi.astype(f32)).astype(bf16)


def _expert_kernel(idx0_ref, idxn_ref, gate_ref, h_ref, x1_ref, mod_ref, gfin_ref, rsum_ref, rrep_ref, uv_ref, o_ref,
                   *scratch):
    step = pl.program_id(0)
    bufs, out_tiles, sem = scratch[:_NSLOT], scratch[_NSLOT], scratch[_NSLOT + 1]
    n_dma = _TG * _NROW

    def row_copy(e, slot, j):
        return pltpu.make_async_copy(uv_ref.at[e], bufs[slot].at[j], sem.at[slot])

    def wait(slot):
        pltpu.make_async_copy(uv_ref.at[pl.ds(0, n_dma)], bufs[slot], sem.at[slot]).wait()

    @pl.when(step == 0)
    def _():
        for slot in range(_NSLOT - 1):
            def body(t, c, slot=slot):
                for k in range(_NROW):
                    row_copy(idx0_ref[slot * _TG + t, k], slot, t * _NROW + k).start(priority=k % 2)
                return c
            lax.fori_loop(0, _TG, body, 0)

    gt_f = mod_ref[0, 5]
    gfin = gfin_ref[...]
    width = _TP * LANES
    r_i = lax.broadcasted_iota(i32, (width, width), 0) // LANES
    c_i = lax.broadcasted_iota(i32, (width, width), 1) // LANES
    lane_sum = jnp.where(r_i == c_i, 1.0, 0.0).astype(bf16)
    eye = jnp.where(lax.broadcasted_iota(i32, (_NROW, LANES), 0) == lax.broadcasted_iota(i32, (_NROW, LANES), 1),
                    1.0, 0.0)

    def evaluate(tok0, slot, t0):
        src = bufs[slot]
        z, vs = [], []
        for dt in range(_TP):
            uv = src[pl.ds((t0 + dt) * _NROW, _NROW)].astype(f32)
            vs.append(uv[:, SUBLANES:2 * SUBLANES, :])
            u = uv[:, 0:SUBLANES, :]
            z.append((u * h_ref[tok0 + dt][None]).reshape(_NROW * SUBLANES, LANES).astype(bf16))
        part = _dot(rsum_ref[...], jnp.concatenate(z, axis=1))
        p_hi, p_lo = _split_bf16(part)
        a = _dot(p_hi, lane_sum) + _dot(p_lo, lane_sum)
        g_diag = jnp.concatenate([eye * gate_ref[pl.ds(tok0 + dt, 1), :] for dt in range(_TP)], axis=1)
        g = _dot(g_diag.astype(bf16), lane_sum)
        w = (g * _gelu(a)).astype(bf16)
        spread = _dot(rrep_ref[...], w)
        for dt in range(_TP):
            tok = tok0 + dt
            wt = spread[:, dt * LANES:(dt + 1) * LANES].reshape(_NROW, SUBLANES, LANES)
            y = jnp.sum(wt * vs[dt], axis=0)
            x2 = x1_ref[tok] + gt_f * y
            ms = jnp.sum(jnp.sum(x2 * x2, axis=1, keepdims=True), axis=0, keepdims=True) * (1.0 / D_MODEL)
            out_tiles[tok] = x2 * lax.rsqrt(ms + EPS) * gfin

    def do_group(grp, slot):
        wait(slot)
        dst = (slot + _NSLOT - 1) % _NSLOT
        for t0 in range(0, _TG, _TP):
            evaluate(grp * _TG + t0, slot, t0)
            for t in range(t0, t0 + _TP):
                for k in range(_NROW):
                    row_copy(idxn_ref[grp * _TG + t, k], dst, t * _NROW + k).start(priority=k % 2)

    def cycle(p, c):
        for slot in range(_NSLOT):
            do_group(_NSLOT * p + slot, slot)
        return c
    lax.fori_loop(0, _TS // (_NSLOT * _TG), cycle, 0)
    o_ref[...] = out_tiles[...].reshape(_TS, D_MODEL)

    @pl.when(step == pl.num_programs(0) - 1)
    def _():
        for slot in range(_NSLOT - 1):
            wait(slot)


def _experts(idx, gate, h3, x13, mod4, gfin, uv, seq):
    n = idx.shape[0]
    steps_per_seq = seq // _TS
    tok3 = lambda i: (i, 0, 0)
    idx_next = jnp.roll(idx, -_AHEAD, axis=0)
    rsum = jnp.asarray(np.repeat(np.eye(_NROW, dtype=np.float32), SUBLANES, axis=1), dtype=bf16)
    buf = pltpu.VMEM((_TG * _NROW, _ROW_TILES, LANES), bf16)
    return pl.pallas_call(
        _expert_kernel,
        grid=(n // _TS,),
        in_specs=[
            pl.BlockSpec((_NSLOT * _TG, _NROW), lambda i: (0, 0), memory_space=pltpu.SMEM),
            pl.BlockSpec((_TS, _NROW), lambda i: (i, 0), memory_space=pltpu.SMEM),
            pl.BlockSpec((_TS, _NROW), lambda i: (i, 0)),
            pl.BlockSpec((_TS, SUBLANES, LANES), tok3),
            pl.BlockSpec((_TS, SUBLANES, LANES), tok3),
            pl.BlockSpec((1, 6, SUBLANES, LANES), lambda i: (i // steps_per_seq, 0, 0, 0)),
            pl.BlockSpec((SUBLANES, LANES), lambda i: (0, 0)),
            pl.BlockSpec(rsum.shape, lambda i: (0, 0)),
            pl.BlockSpec(rsum.shape[::-1], lambda i: (0, 0)),
            pl.BlockSpec(memory_space=pl.ANY),
        ],
        out_specs=pl.BlockSpec((_TS, D_MODEL), lambda i: (i, 0)),
        out_shape=jax.ShapeDtypeStruct((n, D_MODEL), f32),
        scratch_shapes=[buf] * _NSLOT + [pltpu.VMEM((_TS, SUBLANES, LANES), f32), pltpu.SemaphoreType.DMA((_NSLOT,))],
        compiler_params=_cparams(("arbitrary",)),
        name="experts",
    )(idx, idx_next, gate, h3, x13, mod4, gfin, rsum, rsum.T, uv)


def _rope_tables(seq):
    half = HEAD_DIM // 2
    inv = ROPE_THETA ** (-np.arange(half, dtype=np.float32) / half)
    ang = np.arange(seq, dtype=np.float32)[:, None] * inv[None, :].astype(np.float32)
    cos = np.cos(ang).astype(np.float32)
    sin = np.sin(ang).astype(np.float32)
    cos_t = np.tile(np.concatenate([cos, cos], axis=1), (1, NSA_HEADS))
    sin_t = np.tile(np.concatenate([-sin, sin], axis=1), (1, NSA_HEADS))
    return jnp.asarray(cos_t), jnp.asarray(sin_t)


def _selection_map(n_cmp_pad, n_sel):
    r_sel = SEL_BLOCK // CMP_STRIDE
    r_cmp = CMP_BLOCK // CMP_STRIDE
    i = np.arange(n_cmp_pad)[:, None]
    j = np.arange(n_sel)[None, :]
    d = i - r_sel * j
    cnt = np.minimum(d, r_sel - 1) - np.maximum(d - r_cmp + 1, 0) + 1
    cnt = np.clip(cnt, 0, None).astype(np.float32)
    cnt[n_cmp_pad - r_cmp + 1:] = 0.0
    return jnp.asarray(cnt.T)


def _block_expand(n_sel, seq):
    e = (np.arange(seq)[None, :] // SEL_BLOCK == np.arange(n_sel)[:, None]).astype(np.float32)
    return jnp.asarray(e, dtype=bf16)


def _compress_weights(w1):
    eye = jnp.eye(NSA_KV_GROUPS, dtype=w1.dtype)
    out = []
    for part in range(CMP_BLOCK // CMP_STRIDE):
        w = w1[part * CMP_STRIDE * HEAD_DIM:(part + 1) * CMP_STRIDE * HEAD_DIM].reshape(CMP_STRIDE, HEAD_DIM, CMP_HIDDEN)
        big = jnp.einsum('pdc,gh->pgdhc', w, eye).reshape(CMP_STRIDE * KV_WIDTH, NSA_KV_GROUPS * CMP_HIDDEN)
        out.append(big.astype(bf16))
    return out


def kernel(x, c, w_ada, b_ada, g_mix, g_ffn, w_in, cmp_pos_k, cmp_pos_v, w_ck1, w_ck2, w_cv1, w_cv2, hgrn_lb_logits, hgrn_out_norm, w_branch, w_out, w_peer_q, peer_sub_keys, peer_u, peer_v, g_final):
    bsz, seq, d = x.shape
    n = bsz * seq
    depth = w_ada.shape[0]
    assert depth == 1, "single-layer block only"
    n_sel = seq // SEL_BLOCK
    n_pieces = seq // CMP_STRIDE
    cos_t, sin_t = _rope_tables(seq)
    selmap = _selection_map(n_pieces, n_sel)
    expand = _block_expand(n_sel, seq)
    xcur = x.reshape(n, d)
    for l in range(depth):
        mod = _adaln(c, w_ada[l], b_ada[l].reshape(1, 6 * d))
        mod3 = mod.reshape(bsz, 6, d)
        w_pad = jnp.concatenate(
            [w_in[l][:, :_C_GL + 3 * NSA_HEADS], jnp.zeros((d, GATE_PAD - 3 * NSA_HEADS), w_in.dtype),
             w_in[l][:, _C_GL + 3 * NSA_HEADS:]], axis=1).astype(bf16)
        q, kc, vc, kvsw, gates, hq, hf, hi, hg, mg = _inproj(xcur, mod3, g_mix[l].reshape(1, d), w_pad, cos_t, sin_t, seq)
        wkt, wkb = _compress_weights(w_ck1[l])
        wvt, wvb = _compress_weights(w_cv1[l])
        kc_c, vc_c = _compress(
            kc.reshape(bsz, n_pieces, CMP_STRIDE * KV_WIDTH), vc.reshape(bsz, n_pieces, CMP_STRIDE * KV_WIDTH),
            wkt, wkb, wvt, wvb, cmp_pos_k[l].reshape(1, -1), cmp_pos_v[l].reshape(1, -1),
            w_ck1[l], w_cv1[l], w_ck2[l], w_cv2[l])
        o_nsa = _nsa(q, kvsw.reshape(bsz, seq, 4 * KV_WIDTH), kc_c, vc_c, gates, selmap, expand, seq)
        o_hgrn = _hgrn(hq, hf, hi, hg, hgrn_lb_logits, hgrn_out_norm[l].reshape(1, -1), seq, l)
        x1, h2, h2_b = _merge(o_nsa, o_hgrn, mg, xcur, mod3, g_ffn[l].reshape(1, d),
                              w_branch[l, 0].astype(bf16), w_branch[l, 1].astype(bf16), w_out[l].astype(bf16), seq)
        wq_t = w_peer_q[l].T.astype(bf16)
        sk = jnp.transpose(peer_sub_keys[l], (1, 0, 2, 3)).reshape(2 * PEER_HEADS, PEER_NKEYS, PEER_QDIM // 2).astype(bf16)
        idx, gate = _route(h2_b, wq_t, sk)
        uv = jnp.concatenate([peer_u[l], peer_v[l]], axis=1).astype(bf16).reshape(-1, _ROW_TILES, LANES)
        xcur = _experts(idx, gate, h2, x1, mod.reshape(bsz, 6, SUBLANES, LANES), g_final.reshape(SUBLANES, LANES), uv, seq)
    return xcur.reshape(bsz, seq, d)
```

```python
import functools

import numpy as np
import jax
import jax.numpy as jnp
from jax import lax
from jax.experimental import pallas as pl
from jax.experimental.pallas import tpu as pltpu

f32 = jnp.float32
bf16 = jnp.bfloat16
i32 = jnp.int32
_HIGHEST = lax.Precision.HIGHEST

D_MODEL = 1024
EPS = 1e-6
HEAD_DIM = 64
ROPE_THETA = 10000.0
NSA_HEADS = 8
NSA_KV_GROUPS = 2
NSA_GROUP = NSA_HEADS // NSA_KV_GROUPS
NSA_WIDTH = NSA_HEADS * HEAD_DIM
KV_WIDTH = NSA_KV_GROUPS * HEAD_DIM
CMP_BLOCK = 32
CMP_STRIDE = 16
CMP_HIDDEN = 128
SEL_BLOCK = 64
SEL_TOPK = 16
WINDOW = 512
FORCE_BONUS = 1000.0
HGRN_HEADS = 4
HGRN_EXPAND = 128
HGRN_HEAD_V = 128
HGRN_WIDTH = HGRN_HEADS * HGRN_EXPAND
HGRN_CHUNK = 32
PEER_HEADS = 8
PEER_NKEYS = 128
PEER_QDIM = 256
PEER_TOPK = 16
GATE_PAD = 128

LANES = 128
SUBLANES = 8
VMEM_LIMIT = 56 * 1024 * 1024

_NEG = -1e30


def _cparams(sem):
    return pltpu.CompilerParams(dimension_semantics=sem, vmem_limit_bytes=VMEM_LIMIT)


def _gelu(x):
    return 0.5 * x * (1.0 + jnp.tanh(0.7978845608028654 * (x + 0.044715 * (x * x * x))))


def _dot(a, b, **kw):
    return jnp.dot(a, b, preferred_element_type=f32, **kw)


def _dot_nt(a, b):
    return lax.dot_general(a, b, (((1,), (1,)), ((), ())), preferred_element_type=f32)


def _rms(x, g):
    return x * lax.rsqrt(jnp.mean(x * x, axis=-1, keepdims=True) + EPS) * g


def _adaln_kernel(c_ref, w_ref, b_ref, o_ref):
    c = c_ref[...]
    cs = c * jax.nn.sigmoid(c)
    o_ref[...] = _dot(cs, w_ref[...], precision=_HIGHEST) + b_ref[...]


def _adaln(c, w, b):
    bsz = c.shape[0]
    tn = 512
    return pl.pallas_call(
        _adaln_kernel,
        grid=(6 * D_MODEL // tn,),
        in_specs=[
            pl.BlockSpec((bsz, D_MODEL), lambda j: (0, 0)),
            pl.BlockSpec((D_MODEL, tn), lambda j: (0, j)),
            pl.BlockSpec((1, tn), lambda j: (0, j)),
        ],
        out_specs=pl.BlockSpec((bsz, tn), lambda j: (0, j)),
        out_shape=jax.ShapeDtypeStruct((bsz, 6 * D_MODEL), f32),
        compiler_params=_cparams(("arbitrary",)),
        name="adaln",
    )(c, w, b)


_C_Q = 0
_C_KC = 512
_C_VC = 640
_C_KS = 768
_C_VS = 896
_C_KW = 1024
_C_VW = 1152
_C_GL = 1280
_C_HQ = _C_GL + GATE_PAD
_C_HF = _C_HQ + 512
_C_HI = _C_HF + 512
_C_HG = _C_HI + 512
_C_MG = _C_HG + 512
_C_END = _C_MG + 2 * D_MODEL


def _inproj_kernel(x_ref, mod_ref, g_ref, w_ref, cos_ref, sin_ref,
                   q_ref, kc_ref, vc_ref, kvsw_ref, gate_ref, hq_ref, hf_ref, hi_ref, hg_ref, mg_ref):
    x = x_ref[...]
    sh = mod_ref[0, 0:1, :]
    sc = mod_ref[0, 1:2, :]
    h = (_rms(x, g_ref[...]) * (1.0 + sc) + sh).astype(bf16)

    def mm(c0, c1):
        return _dot(h, w_ref[:, c0:c1])

    cos = cos_ref[...]
    sin = sin_ref[...]

    def rope(a):
        width = a.shape[1]
        first = (lax.broadcasted_iota(i32, a.shape, 1) & (HEAD_DIM - 1)) < (HEAD_DIM // 2)
        partner = jnp.where(first, pltpu.roll(a, width - HEAD_DIM // 2, 1), pltpu.roll(a, HEAD_DIM // 2, 1))
        return a * cos[:, :width] + partner * sin[:, :width]

    q_ref[...] = (rope(mm(_C_Q, _C_KC)) * (HEAD_DIM ** -0.5)).astype(bf16)
    kc_ref[...] = rope(mm(_C_KC, _C_VC)).astype(bf16)
    vc_ref[...] = mm(_C_VC, _C_KS).astype(bf16)
    kvsw_ref[:, 0:128] = rope(mm(_C_KS, _C_VS)).astype(bf16)
    kvsw_ref[:, 128:256] = mm(_C_VS, _C_KW).astype(bf16)
    kvsw_ref[:, 256:384] = rope(mm(_C_KW, _C_VW)).astype(bf16)
    kvsw_ref[:, 384:512] = mm(_C_VW, _C_GL).astype(bf16)
    gate_ref[...] = jax.nn.sigmoid(mm(_C_GL, _C_HQ))
    hq_ref[...] = mm(_C_HQ, _C_HF).astype(bf16)
    hf_ref[...] = mm(_C_HF, _C_HI)
    hi_ref[...] = mm(_C_HI, _C_HG).astype(bf16)
    hg_ref[...] = mm(_C_HG, _C_MG).astype(bf16)
    mg_ref[...] = jax.nn.sigmoid(mm(_C_MG, _C_END)).astype(bf16)


def _inproj(x2, mod3, g_mix, w_pad, cos_t, sin_t, seq):
    n = x2.shape[0]
    tm = 256
    tiles_per_seq = seq // tm
    row = lambda i: (i, 0)
    outs = [
        (NSA_WIDTH, bf16), (KV_WIDTH, bf16), (KV_WIDTH, bf16), (4 * KV_WIDTH, bf16), (GATE_PAD, f32),
        (HGRN_WIDTH, bf16), (HGRN_WIDTH, f32), (HGRN_WIDTH, bf16), (HGRN_WIDTH, bf16), (2 * D_MODEL, bf16),
    ]
    return pl.pallas_call(
        _inproj_kernel,
        grid=(n // tm,),
        in_specs=[
            pl.BlockSpec((tm, D_MODEL), row),
            pl.BlockSpec((1, 6, D_MODEL), lambda i: (i // tiles_per_seq, 0, 0)),
            pl.BlockSpec((1, D_MODEL), lambda i: (0, 0)),
            pl.BlockSpec((D_MODEL, _C_END), lambda i: (0, 0)),
            pl.BlockSpec((tm, NSA_WIDTH), lambda i: (i % tiles_per_seq, 0)),
            pl.BlockSpec((tm, NSA_WIDTH), lambda i: (i % tiles_per_seq, 0)),
        ],
        out_specs=[pl.BlockSpec((tm, w), row) for w, _ in outs],
        out_shape=[jax.ShapeDtypeStruct((n, w), dt) for w, dt in outs],
        compiler_params=_cparams(("arbitrary",)),
        name="inproj",
    )(x2, mod3, g_mix, w_pad, cos_t, sin_t)


def _compress_kernel(kc_ref, vc_ref, wkt_ref, wkb_ref, wvt_ref, wvb_ref, pk_ref, pv_ref,
                     w1k_ref, w1v_ref, w2k_ref, w2v_ref, okc_ref, ovc_ref):
    def one(x_ref, wt_ref, wb_ref, pos_ref, w1_ref, w2_ref, o_ref):
        pieces = x_ref[0]
        top = _dot(pieces, wt_ref[...])
        bot = _dot(pieces, wb_ref[...])
        nrow = bot.shape[0]
        bot = pltpu.roll(bot, nrow - 1, 0)
        cpos = _dot(pos_ref[...], w1_ref[...], precision=_HIGHEST)
        w2 = w2_ref[...].astype(bf16)
        outs = []
        for g in range(NSA_KV_GROUPS):
            sl = slice(g * CMP_HIDDEN, (g + 1) * CMP_HIDDEN)
            hid = _gelu(top[:, sl] + bot[:, sl] + cpos)
            outs.append(_dot(hid.astype(bf16), w2))
        o_ref[0] = jnp.concatenate(outs, axis=1).astype(bf16)

    one(kc_ref, wkt_ref, wkb_ref, pk_ref, w1k_ref, w2k_ref, okc_ref)
    one(vc_ref, wvt_ref, wvb_ref, pv_ref, w1v_ref, w2v_ref, ovc_ref)


def _compress(kc3, vc3, wkt, wkb, wvt, wvb, pk, pv, w1k, w1v, w2k, w2v):
    bsz, npieces, width = kc3.shape
    full = lambda a: pl.BlockSpec(a.shape, lambda b: (0,) * a.ndim)
    per_b = pl.BlockSpec((1, npieces, width), lambda b: (b, 0, 0))
    out_b = pl.BlockSpec((1, npieces, KV_WIDTH), lambda b: (b, 0, 0))
    return pl.pallas_call(
        _compress_kernel,
        grid=(bsz,),
        in_specs=[per_b, per_b] + [full(a) for a in (wkt, wkb, wvt, wvb, pk, pv, w1k, w1v, w2k, w2v)],
        out_specs=[out_b, out_b],
        out_shape=[jax.ShapeDtypeStruct((bsz, npieces, KV_WIDTH), bf16)] * 2,
        compiler_params=_cparams(("arbitrary",)),
        name="compress",
    )(kc3, vc3, wkt, wkb, wvt, wvb, pk, pv, w1k, w1v, w2k, w2v)


_TQ = 128
_NSA_CLASSES = 4


def _softmax_parts(s, mask):
    sm = jnp.where(mask, s, _NEG)
    m = jnp.max(sm, axis=-1, keepdims=True)
    p = jnp.where(mask, jnp.exp(sm - m), 0.0)
    den = jnp.maximum(jnp.sum(p, axis=-1, keepdims=True), 1e-30)
    return p, den


def _softmax_plain(s):
    p = jnp.exp(s - jnp.max(s, axis=-1, keepdims=True))
    return p, jnp.sum(p, axis=-1, keepdims=True)


def _nsa_kernel(q_ref, kvsw_ref, kc_ref, vc_ref, gate_ref, selmap_ref, expand_ref, o_ref, *, seq):
    tq = _TQ
    qi = pl.program_id(1)
    n_cls = _NSA_CLASSES
    per_cls = (seq // tq) // n_cls
    for cls in range(n_cls):
        @pl.when((qi >= cls * per_cls) & (qi < (cls + 1) * per_cls))
        def _(cls=cls):
            _nsa_tile(q_ref, kvsw_ref, kc_ref, vc_ref, gate_ref, selmap_ref, expand_ref, o_ref,
                      seq=seq, width=(cls + 1) * (seq // n_cls))


def _nsa_tile(q_ref, kvsw_ref, kc_ref, vc_ref, gate_ref, selmap_ref, expand_ref, o_ref, *, seq, width):
    tq = _TQ
    rows = NSA_GROUP * tq
    n_sel = seq // SEL_BLOCK
    n_cmp_pad = seq // CMP_STRIDE
    t0 = pl.program_id(1) * tq
    q = q_ref[...]
    gates = gate_ref[...]
    tlane = t0 + lax.broadcasted_iota(i32, (1, tq), 1)
    t4 = t0 + (lax.broadcasted_iota(i32, (rows, 1), 0) & (tq - 1))
    wk = WINDOW + tq
    ws = pl.multiple_of(jnp.maximum(t0 - WINDOW, 0), tq)
    ks_all = kvsw_ref[0, 0:width, 0:128]
    vs_all = kvsw_ref[0, 0:width, 128:256]
    kw_all = kvsw_ref[0, pl.ds(ws, wk), 256:384]
    vw_all = kvsw_ref[0, pl.ds(ws, wk), 384:512]
    kc_all = kc_ref[0]
    vc_all = vc_ref[0]
    cend = lax.broadcasted_iota(i32, (1, n_cmp_pad), 1) * CMP_STRIDE + (CMP_BLOCK - 1)
    blk = lax.broadcasted_iota(i32, (n_sel, 1), 0)
    kpos = lax.broadcasted_iota(i32, (1, width), 1)
    kpos_w = ws + lax.broadcasted_iota(i32, (1, wk), 1)
    cur = tlane >> 6
    forced = (blk == 0) | (blk == cur) | (blk == cur - 1)
    causal_blk = blk * SEL_BLOCK <= tlane
    pieces = []
    for g in range(NSA_KV_GROUPS):
        gs = slice(g * HEAD_DIM, (g + 1) * HEAD_DIM)
        qg = jnp.concatenate(
            [q[:, (NSA_GROUP * g + r) * HEAD_DIM:(NSA_GROUP * g + r + 1) * HEAD_DIM] for r in range(NSA_GROUP)], axis=0)
        p_c, den_c = _softmax_parts(_dot_nt(qg, kc_all[:, gs]), cend <= t4)
        o_c = _dot(p_c.astype(bf16), vc_all[:, gs]) / den_c
        pn = p_c / den_c
        pc_sum = pn[0:tq]
        for r in range(1, NSA_GROUP):
            pc_sum = pc_sum + pn[r * tq:(r + 1) * tq]
        imp = lax.dot_general(selmap_ref[...], pc_sum, (((1,), (1,)), ((), ())),
                              preferred_element_type=f32, precision=_HIGHEST)
        imp = jnp.where(forced, imp + FORCE_BONUS, imp)
        imp = jnp.where(causal_blk, imp, -1.0)
        rank = jnp.zeros((n_sel, tq), f32)
        for j in range(n_sel):
            row = imp[j:j + 1, :]
            ahead = jnp.where(row > imp, 1.0, jnp.where(row == imp, jnp.where(blk > j, 1.0, 0.0), 0.0))
            rank = rank + ahead
        bias = jnp.where(rank < float(min(SEL_TOPK, n_sel)), jnp.where(causal_blk, 0.0, _NEG), _NEG).astype(bf16)
        bias_q = lax.dot_general(bias, expand_ref[:, 0:width], (((0,), (0,)), ((), ())),
                                 preferred_element_type=f32)
        bias4 = jnp.concatenate([bias_q] * NSA_GROUP, axis=0)
        p_s, den_s = _softmax_plain(jnp.where(kpos <= t4, _dot_nt(qg, ks_all[:, gs]) + bias4, _NEG))
        o_s = _dot(p_s.astype(bf16), vs_all[:, gs]) / den_s
        in_window = (t4 - kpos_w).astype(jnp.uint32) < WINDOW
        p_w, den_w = _softmax_plain(jnp.where(in_window, _dot_nt(qg, kw_all[:, gs]), _NEG))
        o_w = _dot(p_w.astype(bf16), vw_all[:, gs]) / den_w

        def gcol(br):
            return jnp.concatenate(
                [gates[:, (NSA_GROUP * g + r) * 3 + br:(NSA_GROUP * g + r) * 3 + br + 1] for r in range(NSA_GROUP)], axis=0)

        o = gcol(0) * o_c + gcol(1) * o_s + gcol(2) * o_w
        pieces += [o[r * tq:(r + 1) * tq] for r in range(NSA_GROUP)]
    o_ref[...] = jnp.concatenate(pieces, axis=1).astype(bf16)


def _nsa(q, kvsw3, kc3, vc3, gates, selmap, expand, seq):
    n = q.shape[0]
    bsz = n // seq
    tq = _TQ
    nq = seq // tq
    return pl.pallas_call(
        functools.partial(_nsa_kernel, seq=seq),
        grid=(bsz, nq),
        in_specs=[
            pl.BlockSpec((tq, NSA_WIDTH), lambda b, i: (b * nq + i, 0)),
            pl.BlockSpec((1, seq, 4 * KV_WIDTH), lambda b, i: (b, 0, 0)),
            pl.BlockSpec((1,) + kc3.shape[1:], lambda b, i: (b, 0, 0)),
            pl.BlockSpec((1,) + vc3.shape[1:], lambda b, i: (b, 0, 0)),
            pl.BlockSpec((tq, GATE_PAD), lambda b, i: (b * nq + i, 0)),
            pl.BlockSpec(selmap.shape, lambda b, i: (0, 0)),
            pl.BlockSpec(expand.shape, lambda b, i: (0, 0)),
        ],
        out_specs=pl.BlockSpec((tq, NSA_WIDTH), lambda b, i: (b * nq + i, 0)),
        out_shape=jax.ShapeDtypeStruct((n, NSA_WIDTH), bf16),
        compiler_params=_cparams(("arbitrary", "arbitrary")),
        name="nsa",
    )(q, kvsw3, kc3, vc3, gates, selmap, expand)


def _hgrn_kernel(hq_ref, hf_ref, hi_ref, hg_ref, lbl_ref, ng_ref, o_ref, st_ref, *, seq, layer):
    c = HGRN_CHUNK
    logits = lbl_ref[...]
    e = jnp.exp(logits - jnp.max(logits, axis=0, keepdims=True))
    sm = e / jnp.sum(e, axis=0, keepdims=True)
    lb = sm[0:1]
    for l in range(1, layer + 1):
        lb = lb + sm[l:l + 1]
    st_ref[...] = jnp.zeros_like(st_ref)
    r_i = lax.broadcasted_iota(i32, (c, c), 0)
    c_i = lax.broadcasted_iota(i32, (c, c), 1)
    causal = r_i >= c_i
    tri = jnp.where(causal, 1.0, 0.0)
    ng = ng_ref[...]

    def body(ci, carry):
        r0 = pl.multiple_of(ci * c, c)
        hq = hq_ref[pl.ds(r0, c), :].astype(f32)
        hf = hf_ref[pl.ds(r0, c), :]
        v = hi_ref[pl.ds(r0, c), :]
        hg = hg_ref[pl.ds(r0, c), :].astype(f32)
        f = lb + (1.0 - lb) * jax.nn.sigmoid(hf)
        k = 1.0 - f
        qv = hq * jax.nn.sigmoid(hq)
        bcum = _dot(tri, jnp.log(f), precision=_HIGHEST)
        bend = bcum[c - 1:c, :]
        q_dec = (qv * jnp.exp(bcum)).astype(bf16)
        k_inv = (k * jnp.exp(-bcum)).astype(bf16)
        k_end = (k * jnp.exp(bend - bcum)).astype(bf16)
        dec_end = jnp.exp(bend)
        outs = []
        for h in range(HGRN_HEADS):
            sl = slice(h * HGRN_EXPAND, (h + 1) * HGRN_EXPAND)
            a = jnp.where(causal, _dot_nt(q_dec[:, sl], k_inv[:, sl]), 0.0)
            st = st_ref[h]
            o = _dot(a.astype(bf16), v[:, sl]) + _dot_nt(q_dec[:, sl], st.astype(bf16))
            upd = lax.dot_general(v[:, sl], k_end[:, sl], (((0,), (0,)), ((), ())), preferred_element_type=f32)
            st_ref[h] = st * dec_end[:, sl] + upd
            y = _rms(o, ng) * (hg[:, sl] * jax.nn.sigmoid(hg[:, sl]))
            outs.append(y)
        o_ref[pl.ds(r0, c), :] = jnp.concatenate(outs, axis=1).astype(bf16)
        return carry

    lax.fori_loop(0, seq // c, body, 0, unroll=4)


def _hgrn(hq, hf, hi, hg, lb_logits, norm_g, seq, layer):
    n = hq.shape[0]
    bsz = n // seq
    per_b = pl.BlockSpec((seq, HGRN_WIDTH), lambda b: (b, 0))
    return pl.pallas_call(
        functools.partial(_hgrn_kernel, seq=seq, layer=layer),
        grid=(bsz,),
        in_specs=[per_b, per_b, per_b, per_b,
                  pl.BlockSpec(lb_logits.shape, lambda b: (0, 0)),
                  pl.BlockSpec(norm_g.shape, lambda b: (0, 0))],
        out_specs=per_b,
        out_shape=jax.ShapeDtypeStruct((n, HGRN_WIDTH), bf16),
        scratch_shapes=[pltpu.VMEM((HGRN_HEADS, HGRN_HEAD_V, HGRN_EXPAND), f32)],
        compiler_params=_cparams(("arbitrary",)),
        name="hgrn",
    )(hq, hf, hi, hg, lb_logits, norm_g)


def _merge_kernel(on_ref, oh_ref, mg_ref, x_ref, mod_ref, g_ref, wb0_ref, wb1_ref, wo_ref, x1_ref, h2_ref, hb_ref):
    a = _dot(on_ref[...], wb0_ref[...])
    b = _dot(oh_ref[...], wb1_ref[...])
    mg = mg_ref[...].astype(f32)
    y = mg[:, :D_MODEL] * a + mg[:, D_MODEL:] * b
    y2 = _dot(y.astype(bf16), wo_ref[...])
    x1 = x_ref[...] + mod_ref[0, 2:3, :] * y2
    h2 = _rms(x1, g_ref[...]) * (1.0 + mod_ref[0, 4:5, :]) + mod_ref[0, 3:4, :]
    rows = x1.shape[0]
    x1_ref[...] = x1.reshape(rows, SUBLANES, LANES)
    h2_ref[...] = h2.reshape(rows, SUBLANES, LANES)
    hb_ref[...] = h2.astype(bf16)


def _merge(o_nsa, o_hgrn, mg, x2, mod3, g_ffn, wb0, wb1, wo, seq):
    n = x2.shape[0]
    tm = 256
    tiles_per_seq = seq // tm
    row = lambda i: (i, 0)
    full = lambda a: pl.BlockSpec(a.shape, lambda i: (0,) * a.ndim)
    return pl.pallas_call(
        _merge_kernel,
        grid=(n // tm,),
        in_specs=[
            pl.BlockSpec((tm, NSA_WIDTH), row), pl.BlockSpec((tm, HGRN_WIDTH), row),
            pl.BlockSpec((tm, 2 * D_MODEL), row), pl.BlockSpec((tm, D_MODEL), row),
            pl.BlockSpec((1, 6, D_MODEL), lambda i: (i // tiles_per_seq, 0, 0)),
            full(g_ffn), full(wb0), full(wb1), full(wo),
        ],
        out_specs=[pl.BlockSpec((tm, SUBLANES, LANES), lambda i: (i, 0, 0))] * 2 + [pl.BlockSpec((tm, D_MODEL), row)],
        out_shape=[jax.ShapeDtypeStruct((n, SUBLANES, LANES), f32)] * 2 + [jax.ShapeDtypeStruct((n, D_MODEL), bf16)],
        compiler_params=_cparams(("arbitrary",)),
        name="merge",
    )(o_nsa, o_hgrn, mg, x2, mod3, g_ffn, wb0, wb1, wo)


_TR = 256


def _topk_rows(s, k, payload=None):
    n = s.shape[0]
    rowid = lax.broadcasted_iota(i32, s.shape, 0).astype(f32)
    vals, picks = [], []
    for _ in range(k):
        m = jnp.max(s, axis=0, keepdims=True)
        i = jnp.min(jnp.where(s == m, rowid, float(n)), axis=0, keepdims=True)
        hit = rowid == i
        vals.append(m)
        picks.append(i if payload is None else jnp.max(jnp.where(hit, payload, -1.0), axis=0, keepdims=True))
        s = jnp.where(hit, -jnp.inf, s)
    return vals, picks


def _pair_candidates(v1, i1, v2, i2):
    k = len(v1)
    s2 = jnp.concatenate(v2, axis=0)
    e2 = jnp.concatenate(i2, axis=0)
    sub = lax.broadcasted_iota(i32, (SUBLANES, s2.shape[1]), 0)
    comb, cand = [], []
    for a in range(k // 2):
        nb = k // (a + 1)
        rows = -(-nb // SUBLANES) * SUBLANES
        c = v1[a] + s2[0:rows]
        if nb < rows:
            c = jnp.where(sub < nb, c, -jnp.inf)
        comb.append(c)
        cand.append(i1[a] * float(PEER_NKEYS) + e2[0:rows])
    comb.append(jnp.concatenate(v1[k // 2:], axis=0) + s2[0:1])
    cand.append(jnp.concatenate(i1[k // 2:], axis=0) * float(PEER_NKEYS) + e2[0:1])
    return jnp.concatenate(comb, axis=0), jnp.concatenate(cand, axis=0)


def _route_kernel(h_ref, wq_ref, sk_ref, idx_ref, gate_ref):
    h = h_ref[...]
    tr = h.shape[0]
    k = PEER_TOPK
    half = PEER_QDIM // 2
    experts, gates = [], []
    for hd in range(PEER_HEADS):
        tops = []
        for p in range(2):
            grp = hd * 2 + p
            q_t = _dot_nt(wq_ref[grp * half:(grp + 1) * half, :], h)
            s_t = _dot(sk_ref[grp], q_t.astype(bf16))
            tops.append(_topk_rows(s_t, k))
        (v1, i1), (v2, i2) = tops
        comb, cand = _pair_candidates(v1, i1, v2, i2)
        tv, picked = _topk_rows(comb, k, payload=cand)
        experts += picked
        ex = [jnp.exp(tv[r] - tv[0]) for r in range(k)]
        den = ex[0]
        for r in range(1, k):
            den = den + ex[r]
        gates += [e / den for e in ex]
    idx_t = jnp.concatenate(experts, axis=0).astype(i32)
    gate_t = jnp.concatenate(gates, axis=0)
    for blk in range(tr // LANES):
        sl = slice(blk * LANES, (blk + 1) * LANES)
        idx_ref[sl, :] = idx_t[:, sl].T
        gate_ref[sl, :] = gate_t[:, sl].T


def _route(h2, wq_t, sk):
    n = h2.shape[0]
    tr = _TR
    ncol = PEER_HEADS * PEER_TOPK
    return pl.pallas_call(
        _route_kernel,
        grid=(n // tr,),
        in_specs=[
            pl.BlockSpec((tr, D_MODEL), lambda i: (i, 0)),
            pl.BlockSpec(wq_t.shape, lambda i: (0, 0)),
            pl.BlockSpec(sk.shape, lambda i: (0, 0, 0)),
        ],
        out_specs=[pl.BlockSpec((tr, ncol), lambda i: (i, 0))] * 2,
        out_shape=[jax.ShapeDtypeStruct((n, ncol), i32), jax.ShapeDtypeStruct((n, ncol), f32)],
        compiler_params=_cparams(("arbitrary",)),
        name="route",
    )(h2, wq_t, sk)


_TS = 64
_TG = 4
_NSLOT = 4
_AHEAD = (_NSLOT - 1) * _TG
_NROW = PEER_HEADS * PEER_TOPK
_ROW_TILES = 2 * D_MODEL // LANES
_TP = 2


def _split_bf16(x):
    hi = x.astype(bf16)
    return hi, (x - hi.astype(f32)).astype(bf16)


def _expert_kernel(idx0_ref, idxn_ref, gate_ref, h_ref, x1_ref, mod_ref, gfin_ref, rsum_ref, rrep_ref, uv_ref, o_ref,
                   *scratch):
    step = pl.program_id(0)
    bufs, out_tiles, sem = scratch[:_NSLOT], scratch[_NSLOT], scratch[_NSLOT + 1]
    n_dma = _TG * _NROW

    def row_copy(e, slot, j):
        return pltpu.make_async_copy(uv_ref.at[e], bufs[slot].at[j], sem.at[slot])

    def wait(slot):
        pltpu.make_async_copy(uv_ref.at[pl.ds(0, n_dma)], bufs[slot], sem.at[slot]).wait()

    @pl.when(step == 0)
    def _():
        for slot in range(_NSLOT - 1):
            def body(t, c, slot=slot):
                for k in range(_NROW):
                    row_copy(idx0_ref[slot * _TG + t, k], slot, t * _NROW + k).start(priority=k % 2)
                return c
            lax.fori_loop(0, _TG, body, 0)

    gt_f = mod_ref[0, 5]
    gfin = gfin_ref[...]
    width = _TP * LANES
    r_i = lax.broadcasted_iota(i32, (width, width), 0) // LANES
    c_i = lax.broadcasted_iota(i32, (width, width), 1) // LANES
    lane_sum = jnp.where(r_i == c_i, 1.0, 0.0).astype(bf16)
    eye = jnp.where(lax.broadcasted_iota(i32, (_NROW, LANES), 0) == lax.broadcasted_iota(i32, (_NROW, LANES), 1),
                    1.0, 0.0)

    def evaluate(tok0, slot, t0):
        src = bufs[slot]
        z, vs = [], []
        for dt in range(_TP):
            uv = src[pl.ds((t0 + dt) * _NROW, _NROW)].astype(f32)
            vs.append(uv[:, SUBLANES:2 * SUBLANES, :])
            u = uv[:, 0:SUBLANES, :]
            z.append((u * h_ref[tok0 + dt][None]).reshape(_NROW * SUBLANES, LANES).astype(bf16))
        part = _dot(rsum_ref[...], jnp.concatenate(z, axis=1))
        p_hi, p_lo = _split_bf16(part)
        a = _dot(p_hi, lane_sum) + _dot(p_lo, lane_sum)
        g_diag = jnp.concatenate([eye * gate_ref[pl.ds(tok0 + dt, 1), :] for dt in range(_TP)], axis=1)
        g = _dot(g_diag.astype(bf16), lane_sum)
        w = (g * _gelu(a)).astype(bf16)
        spread = _dot(rrep_ref[...], w)
        for dt in range(_TP):
            tok = tok0 + dt
            wt = spread[:, dt * LANES:(dt + 1) * LANES].reshape(_NROW, SUBLANES, LANES)
            y = jnp.sum(wt * vs[dt], axis=0)
            x2 = x1_ref[tok] + gt_f * y
            ms = jnp.sum(jnp.sum(x2 * x2, axis=1, keepdims=True), axis=0, keepdims=True) * (1.0 / D_MODEL)
            out_tiles[tok] = x2 * lax.rsqrt(ms + EPS) * gfin

    def do_group(grp, slot):
        wait(slot)
        dst = (slot + _NSLOT - 1) % _NSLOT
        for t0 in range(0, _TG, _TP):
            evaluate(grp * _TG + t0, slot, t0)
            for t in range(t0, t0 + _TP):
                for k in range(_NROW):
                    row_copy(idxn_ref[grp * _TG + t, k], dst, t * _NROW + k).start(priority=k % 2)

    for grp in range(_TS // _TG):
        do_group(grp, grp % _NSLOT)
    o_ref[...] = out_tiles[...].reshape(_TS, D_MODEL)

    @pl.when(step == pl.num_programs(0) - 1)
    def _():
        for slot in range(_NSLOT - 1):
            wait(slot)


def _experts(idx, gate, h3, x13, mod4, gfin, uv, seq):
    n = idx.shape[0]
    steps_per_seq = seq // _TS
    tok3 = lambda i: (i, 0, 0)
    idx_next = jnp.roll(idx, -_AHEAD, axis=0)
    rsum = jnp.asarray(np.repeat(np.eye(_NROW, dtype=np.float32), SUBLANES, axis=1), dtype=bf16)
    buf = pltpu.VMEM((_TG * _NROW, _ROW_TILES, LANES), bf16)
    return pl.pallas_call(
        _expert_kernel,
        grid=(n // _TS,),
        in_specs=[
            pl.BlockSpec((_NSLOT * _TG, _NROW), lambda i: (0, 0), memory_space=pltpu.SMEM),
            pl.BlockSpec((_TS, _NROW), lambda i: (i, 0), memory_space=pltpu.SMEM),
            pl.BlockSpec((_TS, _NROW), lambda i: (i, 0)),
            pl.BlockSpec((_TS, SUBLANES, LANES), tok3),
            pl.BlockSpec((_TS, SUBLANES, LANES), tok3),
            pl.BlockSpec((1, 6, SUBLANES, LANES), lambda i: (i // steps_per_seq, 0, 0, 0)),
            pl.BlockSpec((SUBLANES, LANES), lambda i: (0, 0)),
            pl.BlockSpec(rsum.shape, lambda i: (0, 0)),
            pl.BlockSpec(rsum.shape[::-1], lambda i: (0, 0)),
            pl.BlockSpec(memory_space=pl.ANY),
        ],
        out_specs=pl.BlockSpec((_TS, D_MODEL), lambda i: (i, 0)),
        out_shape=jax.ShapeDtypeStruct((n, D_MODEL), f32),
        scratch_shapes=[buf] * _NSLOT + [pltpu.VMEM((_TS, SUBLANES, LANES), f32), pltpu.SemaphoreType.DMA((_NSLOT,))],
        compiler_params=_cparams(("arbitrary",)),
        name="experts",
    )(idx, idx_next, gate, h3, x13, mod4, gfin, rsum, rsum.T, uv)


def _rope_tables(seq):
    half = HEAD_DIM // 2
    inv = ROPE_THETA ** (-np.arange(half, dtype=np.float32) / half)
    ang = np.arange(seq, dtype=np.float32)[:, None] * inv[None, :].astype(np.float32)
    cos = np.cos(ang).astype(np.float32)
    sin = np.sin(ang).astype(np.float32)
    cos_t = np.tile(np.concatenate([cos, cos], axis=1), (1, NSA_HEADS))
    sin_t = np.tile(np.concatenate([-sin, sin], axis=1), (1, NSA_HEADS))
    return jnp.asarray(cos_t), jnp.asarray(sin_t)


def _selection_map(n_cmp_pad, n_sel):
    r_sel = SEL_BLOCK // CMP_STRIDE
    r_cmp = CMP_BLOCK // CMP_STRIDE
    i = np.arange(n_cmp_pad)[:, None]
    j = np.arange(n_sel)[None, :]
    d = i - r_sel * j
    cnt = np.minimum(d, r_sel - 1) - np.maximum(d - r_cmp + 1, 0) + 1
    cnt = np.clip(cnt, 0, None).astype(np.float32)
    cnt[n_cmp_pad - r_cmp + 1:] = 0.0
    return jnp.asarray(cnt.T)


def _block_expand(n_sel, seq):
    e = (np.arange(seq)[None, :] // SEL_BLOCK == np.arange(n_sel)[:, None]).astype(np.float32)
    return jnp.asarray(e, dtype=bf16)


def _compress_weights(w1):
    eye = jnp.eye(NSA_KV_GROUPS, dtype=w1.dtype)
    out = []
    for part in range(CMP_BLOCK // CMP_STRIDE):
        w = w1[part * CMP_STRIDE * HEAD_DIM:(part + 1) * CMP_STRIDE * HEAD_DIM].reshape(CMP_STRIDE, HEAD_DIM, CMP_HIDDEN)
        big = jnp.einsum('pdc,gh->pgdhc', w, eye).reshape(CMP_STRIDE * KV_WIDTH, NSA_KV_GROUPS * CMP_HIDDEN)
        out.append(big.astype(bf16))
    return out


def kernel(x, c, w_ada, b_ada, g_mix, g_ffn, w_in, cmp_pos_k, cmp_pos_v, w_ck1, w_ck2, w_cv1, w_cv2, hgrn_lb_logits, hgrn_out_norm, w_branch, w_out, w_peer_q, peer_sub_keys, peer_u, peer_v, g_final):
    bsz, seq, d = x.shape
    n = bsz * seq
    depth = w_ada.shape[0]
    assert depth == 1, "single-layer block only"
    n_sel = seq // SEL_BLOCK
    n_pieces = seq // CMP_STRIDE
    cos_t, sin_t = _rope_tables(seq)
    selmap = _selection_map(n_pieces, n_sel)
    expand = _block_expand(n_sel, seq)
    xcur = x.reshape(n, d)
    for l in range(depth):
        mod = _adaln(c, w_ada[l], b_ada[l].reshape(1, 6 * d))
        mod3 = mod.reshape(bsz, 6, d)
        w_pad = jnp.concatenate(
            [w_in[l][:, :_C_GL + 3 * NSA_HEADS], jnp.zeros((d, GATE_PAD - 3 * NSA_HEADS), w_in.dtype),
             w_in[l][:, _C_GL + 3 * NSA_HEADS:]], axis=1).astype(bf16)
        q, kc, vc, kvsw, gates, hq, hf, hi, hg, mg = _inproj(xcur, mod3, g_mix[l].reshape(1, d), w_pad, cos_t, sin_t, seq)
        wkt, wkb = _compress_weights(w_ck1[l])
        wvt, wvb = _compress_weights(w_cv1[l])
        kc_c, vc_c = _compress(
            kc.reshape(bsz, n_pieces, CMP_STRIDE * KV_WIDTH), vc.reshape(bsz, n_pieces, CMP_STRIDE * KV_WIDTH),
            wkt, wkb, wvt, wvb, cmp_pos_k[l].reshape(1, -1), cmp_pos_v[l].reshape(1, -1),
            w_ck1[l], w_cv1[l], w_ck2[l], w_cv2[l])
        o_nsa = _nsa(q, kvsw.reshape(bsz, seq, 4 * KV_WIDTH), kc_c, vc_c, gates, selmap, expand, seq)
        o_hgrn = _hgrn(hq, hf, hi, hg, hgrn_lb_logits, hgrn_out_norm[l].reshape(1, -1), seq, l)
        x1, h2, h2_b = _merge(o_nsa, o_hgrn, mg, xcur, mod3, g_ffn[l].reshape(1, d),
                              w_branch[l, 0].astype(bf16), w_branch[l, 1].astype(bf16), w_out[l].astype(bf16), seq)
        wq_t = w_peer_q[l].T.astype(bf16)
        sk = jnp.transpose(peer_sub_keys[l], (1, 0, 2, 3)).reshape(2 * PEER_HEADS, PEER_NKEYS, PEER_QDIM // 2).astype(bf16)
        idx, gate = _route(h2_b, wq_t, sk)
        uv = jnp.concatenate([peer_u[l], peer_v[l]], axis=1).astype(bf16).reshape(-1, _ROW_TILES, LANES)
        xcur = _experts(idx, gate, h2, x1, mod.reshape(bsz, 6, SUBLANES, LANES), g_final.reshape(SUBLANES, LANES), uv, seq)
    return xcur.reshape(bsz, seq, d)
```

```python
import functools

import numpy as np
import jax
import jax.numpy as jnp
from jax import lax
from jax.experimental import pallas as pl
from jax.experimental.pallas import tpu as pltpu

f32 = jnp.float32
bf16 = jnp.bfloat16
i32 = jnp.int32
_HIGHEST = lax.Precision.HIGHEST

D_MODEL = 1024
EPS = 1e-6
HEAD_DIM = 64
ROPE_THETA = 10000.0
NSA_HEADS = 8
NSA_KV_GROUPS = 2
NSA_GROUP = NSA_HEADS // NSA_KV_GROUPS
NSA_WIDTH = NSA_HEADS * HEAD_DIM
KV_WIDTH = NSA_KV_GROUPS * HEAD_DIM
CMP_BLOCK = 32
CMP_STRIDE = 16
CMP_HIDDEN = 128
SEL_BLOCK = 64
SEL_TOPK = 16
WINDOW = 512
FORCE_BONUS = 1000.0
HGRN_HEADS = 4
HGRN_EXPAND = 128
HGRN_HEAD_V = 128
HGRN_WIDTH = HGRN_HEADS * HGRN_EXPAND
HGRN_CHUNK = 32
PEER_HEADS = 8
PEER_NKEYS = 128
PEER_QDIM = 256
PEER_TOPK = 16
GATE_PAD = 128

LANES = 128
SUBLANES = 8
VMEM_LIMIT = 56 * 1024 * 1024

_NEG = -1e30


def _cparams(sem):
    return pltpu.CompilerParams(dimension_semantics=sem, vmem_limit_bytes=VMEM_LIMIT)


def _gelu(x):
    return 0.5 * x * (1.0 + jnp.tanh(0.7978845608028654 * (x + 0.044715 * (x * x * x))))


def _dot(a, b, **kw):
    return jnp.dot(a, b, preferred_element_type=f32, **kw)


def _dot_nt(a, b):
    return lax.dot_general(a, b, (((1,), (1,)), ((), ())), preferred_element_type=f32)


def _rms(x, g):
    return x * lax.rsqrt(jnp.mean(x * x, axis=-1, keepdims=True) + EPS) * g


def _adaln_kernel(c_ref, w_ref, b_ref, o_ref):
    c = c_ref[...]
    cs = c * jax.nn.sigmoid(c)
    o_ref[...] = _dot(cs, w_ref[...], precision=_HIGHEST) + b_ref[...]


def _adaln(c, w, b):
    bsz = c.shape[0]
    tn = 512
    return pl.pallas_call(
        _adaln_kernel,
        grid=(6 * D_MODEL // tn,),
        in_specs=[
            pl.BlockSpec((bsz, D_MODEL), lambda j: (0, 0)),
            pl.BlockSpec((D_MODEL, tn), lambda j: (0, j)),
            pl.BlockSpec((1, tn), lambda j: (0, j)),
        ],
        out_specs=pl.BlockSpec((bsz, tn), lambda j: (0, j)),
        out_shape=jax.ShapeDtypeStruct((bsz, 6 * D_MODEL), f32),
        compiler_params=_cparams(("arbitrary",)),
        name="adaln",
    )(c, w, b)


_C_Q = 0
_C_KC = 512
_C_VC = 640
_C_KS = 768
_C_VS = 896
_C_KW = 1024
_C_VW = 1152
_C_GL = 1280
_C_HQ = _C_GL + GATE_PAD
_C_HF = _C_HQ + 512
_C_HI = _C_HF + 512
_C_HG = _C_HI + 512
_C_MG = _C_HG + 512
_C_END = _C_MG + 2 * D_MODEL


def _inproj_kernel(x_ref, mod_ref, g_ref, w_ref, cos_ref, sin_ref,
                   q_ref, kc_ref, vc_ref, kvsw_ref, gate_ref, hq_ref, hf_ref, hi_ref, hg_ref, mg_ref):
    x = x_ref[...]
    sh = mod_ref[0, 0:1, :]
    sc = mod_ref[0, 1:2, :]
    h = (_rms(x, g_ref[...]) * (1.0 + sc) + sh).astype(bf16)

    def mm(c0, c1):
        return _dot(h, w_ref[:, c0:c1])

    cos = cos_ref[...]
    sin = sin_ref[...]

    def rope(a):
        width = a.shape[1]
        first = (lax.broadcasted_iota(i32, a.shape, 1) & (HEAD_DIM - 1)) < (HEAD_DIM // 2)
        partner = jnp.where(first, pltpu.roll(a, width - HEAD_DIM // 2, 1), pltpu.roll(a, HEAD_DIM // 2, 1))
        return a * cos[:, :width] + partner * sin[:, :width]

    q_ref[...] = (rope(mm(_C_Q, _C_KC)) * (HEAD_DIM ** -0.5)).astype(bf16)
    kc_ref[...] = rope(mm(_C_KC, _C_VC)).astype(bf16)
    vc_ref[...] = mm(_C_VC, _C_KS).astype(bf16)
    kvsw_ref[:, 0:128] = rope(mm(_C_KS, _C_VS)).astype(bf16)
    kvsw_ref[:, 128:256] = mm(_C_VS, _C_KW).astype(bf16)
    kvsw_ref[:, 256:384] = rope(mm(_C_KW, _C_VW)).astype(bf16)
    kvsw_ref[:, 384:512] = mm(_C_VW, _C_GL).astype(bf16)
    gate_ref[...] = jax.nn.sigmoid(mm(_C_GL, _C_HQ))
    hq_ref[...] = mm(_C_HQ, _C_HF).astype(bf16)
    hf_ref[...] = mm(_C_HF, _C_HI)
    hi_ref[...] = mm(_C_HI, _C_HG).astype(bf16)
    hg_ref[...] = mm(_C_HG, _C_MG).astype(bf16)
    mg_ref[...] = jax.nn.sigmoid(mm(_C_MG, _C_END)).astype(bf16)


def _inproj(x2, mod3, g_mix, w_pad, cos_t, sin_t, seq):
    n = x2.shape[0]
    tm = 256
    tiles_per_seq = seq // tm
    row = lambda i: (i, 0)
    outs = [
        (NSA_WIDTH, bf16), (KV_WIDTH, bf16), (KV_WIDTH, bf16), (4 * KV_WIDTH, bf16), (GATE_PAD, f32),
        (HGRN_WIDTH, bf16), (HGRN_WIDTH, f32), (HGRN_WIDTH, bf16), (HGRN_WIDTH, bf16), (2 * D_MODEL, bf16),
    ]
    return pl.pallas_call(
        _inproj_kernel,
        grid=(n // tm,),
        in_specs=[
            pl.BlockSpec((tm, D_MODEL), row),
            pl.BlockSpec((1, 6, D_MODEL), lambda i: (i // tiles_per_seq, 0, 0)),
            pl.BlockSpec((1, D_MODEL), lambda i: (0, 0)),
            pl.BlockSpec((D_MODEL, _C_END), lambda i: (0, 0)),
            pl.BlockSpec((tm, NSA_WIDTH), lambda i: (i % tiles_per_seq, 0)),
            pl.BlockSpec((tm, NSA_WIDTH), lambda i: (i % tiles_per_seq, 0)),
        ],
        out_specs=[pl.BlockSpec((tm, w), row) for w, _ in outs],
        out_shape=[jax.ShapeDtypeStruct((n, w), dt) for w, dt in outs],
        compiler_params=_cparams(("arbitrary",)),
        name="inproj",
    )(x2, mod3, g_mix, w_pad, cos_t, sin_t)


def _compress_kernel(kc_ref, vc_ref, wkt_ref, wkb_ref, wvt_ref, wvb_ref, pk_ref, pv_ref,
                     w1k_ref, w1v_ref, w2k_ref, w2v_ref, okc_ref, ovc_ref):
    def one(x_ref, wt_ref, wb_ref, pos_ref, w1_ref, w2_ref, o_ref):
        pieces = x_ref[0]
        top = _dot(pieces, wt_ref[...])
        bot = _dot(pieces, wb_ref[...])
        nrow = bot.shape[0]
        bot = pltpu.roll(bot, nrow - 1, 0)
        cpos = _dot(pos_ref[...], w1_ref[...], precision=_HIGHEST)
        w2 = w2_ref[...].astype(bf16)
        outs = []
        for g in range(NSA_KV_GROUPS):
            sl = slice(g * CMP_HIDDEN, (g + 1) * CMP_HIDDEN)
            hid = _gelu(top[:, sl] + bot[:, sl] + cpos)
            outs.append(_dot(hid.astype(bf16), w2))
        o_ref[0] = jnp.concatenate(outs, axis=1).astype(bf16)

    one(kc_ref, wkt_ref, wkb_ref, pk_ref, w1k_ref, w2k_ref, okc_ref)
    one(vc_ref, wvt_ref, wvb_ref, pv_ref, w1v_ref, w2v_ref, ovc_ref)


def _compress(kc3, vc3, wkt, wkb, wvt, wvb, pk, pv, w1k, w1v, w2k, w2v):
    bsz, npieces, width = kc3.shape
    full = lambda a: pl.BlockSpec(a.shape, lambda b: (0,) * a.ndim)
    per_b = pl.BlockSpec((1, npieces, width), lambda b: (b, 0, 0))
    out_b = pl.BlockSpec((1, npieces, KV_WIDTH), lambda b: (b, 0, 0))
    return pl.pallas_call(
        _compress_kernel,
        grid=(bsz,),
        in_specs=[per_b, per_b] + [full(a) for a in (wkt, wkb, wvt, wvb, pk, pv, w1k, w1v, w2k, w2v)],
        out_specs=[out_b, out_b],
        out_shape=[jax.ShapeDtypeStruct((bsz, npieces, KV_WIDTH), bf16)] * 2,
        compiler_params=_cparams(("arbitrary",)),
        name="compress",
    )(kc3, vc3, wkt, wkb, wvt, wvb, pk, pv, w1k, w1v, w2k, w2v)


_TQ = 128
_NSA_CLASSES = 4


def _softmax_parts(s, mask):
    sm = jnp.where(mask, s, _NEG)
    m = jnp.max(sm, axis=-1, keepdims=True)
    p = jnp.where(mask, jnp.exp(sm - m), 0.0)
    den = jnp.maximum(jnp.sum(p, axis=-1, keepdims=True), 1e-30)
    return p, den


def _softmax_plain(s):
    p = jnp.exp(s - jnp.max(s, axis=-1, keepdims=True))
    return p, jnp.sum(p, axis=-1, keepdims=True)


def _nsa_kernel(q_ref, kvsw_ref, kc_ref, vc_ref, gate_ref, selmap_ref, expand_ref, o_ref, *, seq):
    tq = _TQ
    qi = pl.program_id(1)
    n_cls = _NSA_CLASSES
    per_cls = (seq // tq) // n_cls
    for cls in range(n_cls):
        @pl.when((qi >= cls * per_cls) & (qi < (cls + 1) * per_cls))
        def _(cls=cls):
            _nsa_tile(q_ref, kvsw_ref, kc_ref, vc_ref, gate_ref, selmap_ref, expand_ref, o_ref,
                      seq=seq, width=(cls + 1) * (seq // n_cls))


def _nsa_tile(q_ref, kvsw_ref, kc_ref, vc_ref, gate_ref, selmap_ref, expand_ref, o_ref, *, seq, width):
    tq = _TQ
    rows = NSA_GROUP * tq
    n_sel = seq // SEL_BLOCK
    n_cmp_pad = seq // CMP_STRIDE
    t0 = pl.program_id(1) * tq
    q = q_ref[...]
    gates = gate_ref[...]
    tlane = t0 + lax.broadcasted_iota(i32, (1, tq), 1)
    t4 = t0 + (lax.broadcasted_iota(i32, (rows, 1), 0) & (tq - 1))
    wk = WINDOW + tq
    ws = pl.multiple_of(jnp.maximum(t0 - WINDOW, 0), tq)
    ks_all = kvsw_ref[0, 0:width, 0:128]
    vs_all = kvsw_ref[0, 0:width, 128:256]
    kw_all = kvsw_ref[0, pl.ds(ws, wk), 256:384]
    vw_all = kvsw_ref[0, pl.ds(ws, wk), 384:512]
    kc_all = kc_ref[0]
    vc_all = vc_ref[0]
    cend = lax.broadcasted_iota(i32, (1, n_cmp_pad), 1) * CMP_STRIDE + (CMP_BLOCK - 1)
    blk = lax.broadcasted_iota(i32, (n_sel, 1), 0)
    kpos = lax.broadcasted_iota(i32, (1, width), 1)
    kpos_w = ws + lax.broadcasted_iota(i32, (1, wk), 1)
    cur = tlane >> 6
    forced = (blk == 0) | (blk == cur) | (blk == cur - 1)
    causal_blk = blk * SEL_BLOCK <= tlane
    pieces = []
    for g in range(NSA_KV_GROUPS):
        gs = slice(g * HEAD_DIM, (g + 1) * HEAD_DIM)
        qg = jnp.concatenate(
            [q[:, (NSA_GROUP * g + r) * HEAD_DIM:(NSA_GROUP * g + r + 1) * HEAD_DIM] for r in range(NSA_GROUP)], axis=0)
        p_c, den_c = _softmax_parts(_dot_nt(qg, kc_all[:, gs]), cend <= t4)
        o_c = _dot(p_c.astype(bf16), vc_all[:, gs]) / den_c
        pn = p_c / den_c
        pc_sum = pn[0:tq]
        for r in range(1, NSA_GROUP):
            pc_sum = pc_sum + pn[r * tq:(r + 1) * tq]
        imp = lax.dot_general(selmap_ref[...], pc_sum, (((1,), (1,)), ((), ())),
                              preferred_element_type=f32, precision=_HIGHEST)
        imp = jnp.where(forced, imp + FORCE_BONUS, imp)
        imp = jnp.where(causal_blk, imp, -1.0)
        rank = jnp.zeros((n_sel, tq), f32)
        for j in range(n_sel):
            row = imp[j:j + 1, :]
            ahead = jnp.where(row > imp, 1.0, jnp.where(row == imp, jnp.where(blk > j, 1.0, 0.0), 0.0))
            rank = rank + ahead
        bias = jnp.where(rank < float(min(SEL_TOPK, n_sel)), jnp.where(causal_blk, 0.0, _NEG), _NEG).astype(bf16)
        bias_q = lax.dot_general(bias, expand_ref[:, 0:width], (((0,), (0,)), ((), ())),
                                 preferred_element_type=f32)
        bias4 = jnp.concatenate([bias_q] * NSA_GROUP, axis=0)
        p_s, den_s = _softmax_plain(jnp.where(kpos <= t4, _dot_nt(qg, ks_all[:, gs]) + bias4, _NEG))
        o_s = _dot(p_s.astype(bf16), vs_all[:, gs]) / den_s
        in_window = (t4 - kpos_w).astype(jnp.uint32) < WINDOW
        p_w, den_w = _softmax_plain(jnp.where(in_window, _dot_nt(qg, kw_all[:, gs]), _NEG))
        o_w = _dot(p_w.astype(bf16), vw_all[:, gs]) / den_w

        def gcol(br):
            return jnp.concatenate(
                [gates[:, (NSA_GROUP * g + r) * 3 + br:(NSA_GROUP * g + r) * 3 + br + 1] for r in range(NSA_GROUP)], axis=0)

        o = gcol(0) * o_c + gcol(1) * o_s + gcol(2) * o_w
        pieces += [o[r * tq:(r + 1) * tq] for r in range(NSA_GROUP)]
    o_ref[...] = jnp.concatenate(pieces, axis=1).astype(bf16)


def _nsa(q, kvsw3, kc3, vc3, gates, selmap, expand, seq):
    n = q.shape[0]
    bsz = n // seq
    tq = _TQ
    nq = seq // tq
    return pl.pallas_call(
        functools.partial(_nsa_kernel, seq=seq),
        grid=(bsz, nq),
        in_specs=[
            pl.BlockSpec((tq, NSA_WIDTH), lambda b, i: (b * nq + i, 0)),
            pl.BlockSpec((1, seq, 4 * KV_WIDTH), lambda b, i: (b, 0, 0)),
            pl.BlockSpec((1,) + kc3.shape[1:], lambda b, i: (b, 0, 0)),
            pl.BlockSpec((1,) + vc3.shape[1:], lambda b, i: (b, 0, 0)),
            pl.BlockSpec((tq, GATE_PAD), lambda b, i: (b * nq + i, 0)),
            pl.BlockSpec(selmap.shape, lambda b, i: (0, 0)),
            pl.BlockSpec(expand.shape, lambda b, i: (0, 0)),
        ],
        out_specs=pl.BlockSpec((tq, NSA_WIDTH), lambda b, i: (b * nq + i, 0)),
        out_shape=jax.ShapeDtypeStruct((n, NSA_WIDTH), bf16),
        compiler_params=_cparams(("arbitrary", "arbitrary")),
        name="nsa",
    )(q, kvsw3, kc3, vc3, gates, selmap, expand)


def _hgrn_kernel(hq_ref, hf_ref, hi_ref, hg_ref, lbl_ref, ng_ref, o_ref, st_ref, *, seq, layer):
    c = HGRN_CHUNK
    logits = lbl_ref[...]
    e = jnp.exp(logits - jnp.max(logits, axis=0, keepdims=True))
    sm = e / jnp.sum(e, axis=0, keepdims=True)
    lb = sm[0:1]
    for l in range(1, layer + 1):
        lb = lb + sm[l:l + 1]
    st_ref[...] = jnp.zeros_like(st_ref)
    r_i = lax.broadcasted_iota(i32, (c, c), 0)
    c_i = lax.broadcasted_iota(i32, (c, c), 1)
    causal = r_i >= c_i
    tri = jnp.where(causal, 1.0, 0.0)
    ng = ng_ref[...]

    def body(ci, carry):
        r0 = pl.multiple_of(ci * c, c)
        hq = hq_ref[pl.ds(r0, c), :].astype(f32)
        hf = hf_ref[pl.ds(r0, c), :]
        v = hi_ref[pl.ds(r0, c), :]
        hg = hg_ref[pl.ds(r0, c), :].astype(f32)
        f = lb + (1.0 - lb) * jax.nn.sigmoid(hf)
        k = 1.0 - f
        qv = hq * jax.nn.sigmoid(hq)
        bcum = _dot(tri, jnp.log(f), precision=_HIGHEST)
        bend = bcum[c - 1:c, :]
        q_dec = (qv * jnp.exp(bcum)).astype(bf16)
        k_inv = (k * jnp.exp(-bcum)).astype(bf16)
        k_end = (k * jnp.exp(bend - bcum)).astype(bf16)
        dec_end = jnp.exp(bend)
        outs = []
        for h in range(HGRN_HEADS):
            sl = slice(h * HGRN_EXPAND, (h + 1) * HGRN_EXPAND)
            a = jnp.where(causal, _dot_nt(q_dec[:, sl], k_inv[:, sl]), 0.0)
            st = st_ref[h]
            o = _dot(a.astype(bf16), v[:, sl]) + _dot_nt(q_dec[:, sl], st.astype(bf16))
            upd = lax.dot_general(v[:, sl], k_end[:, sl], (((0,), (0,)), ((), ())), preferred_element_type=f32)
            st_ref[h] = st * dec_end[:, sl] + upd
            y = _rms(o, ng) * (hg[:, sl] * jax.nn.sigmoid(hg[:, sl]))
            outs.append(y)
        o_ref[pl.ds(r0, c), :] = jnp.concatenate(outs, axis=1).astype(bf16)
        return carry

    lax.fori_loop(0, seq // c, body, 0, unroll=4)


def _hgrn(hq, hf, hi, hg, lb_logits, norm_g, seq, layer):
    n = hq.shape[0]
    bsz = n // seq
    per_b = pl.BlockSpec((seq, HGRN_WIDTH), lambda b: (b, 0))
    return pl.pallas_call(
        functools.partial(_hgrn_kernel, seq=seq, layer=layer),
        grid=(bsz,),
        in_specs=[per_b, per_b, per_b, per_b,
                  pl.BlockSpec(lb_logits.shape, lambda b: (0, 0)),
                  pl.BlockSpec(norm_g.shape, lambda b: (0, 0))],
        out_specs=per_b,
        out_shape=jax.ShapeDtypeStruct((n, HGRN_WIDTH), bf16),
        scratch_shapes=[pltpu.VMEM((HGRN_HEADS, HGRN_HEAD_V, HGRN_EXPAND), f32)],
        compiler_params=_cparams(("arbitrary",)),
        name="hgrn",
    )(hq, hf, hi, hg, lb_logits, norm_g)


def _merge_kernel(on_ref, oh_ref, mg_ref, x_ref, mod_ref, g_ref, wb0_ref, wb1_ref, wo_ref, x1_ref, h2_ref, hb_ref):
    a = _dot(on_ref[...], wb0_ref[...])
    b = _dot(oh_ref[...], wb1_ref[...])
    mg = mg_ref[...].astype(f32)
    y = mg[:, :D_MODEL] * a + mg[:, D_MODEL:] * b
    y2 = _dot(y.astype(bf16), wo_ref[...])
    x1 = x_ref[...] + mod_ref[0, 2:3, :] * y2
    h2 = _rms(x1, g_ref[...]) * (1.0 + mod_ref[0, 4:5, :]) + mod_ref[0, 3:4, :]
    rows = x1.shape[0]
    x1_ref[...] = x1.reshape(rows, SUBLANES, LANES)
    h2_ref[...] = h2.reshape(rows, SUBLANES, LANES)
    hb_ref[...] = h2.astype(bf16)


def _merge(o_nsa, o_hgrn, mg, x2, mod3, g_ffn, wb0, wb1, wo, seq):
    n = x2.shape[0]
    tm = 256
    tiles_per_seq = seq // tm
    row = lambda i: (i, 0)
    full = lambda a: pl.BlockSpec(a.shape, lambda i: (0,) * a.ndim)
    return pl.pallas_call(
        _merge_kernel,
        grid=(n // tm,),
        in_specs=[
            pl.BlockSpec((tm, NSA_WIDTH), row), pl.BlockSpec((tm, HGRN_WIDTH), row),
            pl.BlockSpec((tm, 2 * D_MODEL), row), pl.BlockSpec((tm, D_MODEL), row),
            pl.BlockSpec((1, 6, D_MODEL), lambda i: (i // tiles_per_seq, 0, 0)),
            full(g_ffn), full(wb0), full(wb1), full(wo),
        ],
        out_specs=[pl.BlockSpec((tm, SUBLANES, LANES), lambda i: (i, 0, 0))] * 2 + [pl.BlockSpec((tm, D_MODEL), row)],
        out_shape=[jax.ShapeDtypeStruct((n, SUBLANES, LANES), f32)] * 2 + [jax.ShapeDtypeStruct((n, D_MODEL), bf16)],
        compiler_params=_cparams(("arbitrary",)),
        name="merge",
    )(o_nsa, o_hgrn, mg, x2, mod3, g_ffn, wb0, wb1, wo)


_TR = 256


def _topk_rows(s, k, payload=None):
    n = s.shape[0]
    rowid = lax.broadcasted_iota(i32, s.shape, 0).astype(f32)
    vals, picks = [], []
    for _ in range(k):
        m = jnp.max(s, axis=0, keepdims=True)
        i = jnp.min(jnp.where(s == m, rowid, float(n)), axis=0, keepdims=True)
        hit = rowid == i
        vals.append(m)
        picks.append(i if payload is None else jnp.max(jnp.where(hit, payload, -1.0), axis=0, keepdims=True))
        s = jnp.where(hit, -jnp.inf, s)
    return vals, picks


def _pair_candidates(v1, i1, v2, i2):
    k = len(v1)
    s2 = jnp.concatenate(v2, axis=0)
    e2 = jnp.concatenate(i2, axis=0)
    sub = lax.broadcasted_iota(i32, (SUBLANES, s2.shape[1]), 0)
    comb, cand = [], []
    for a in range(k // 2):
        nb = k // (a + 1)
        rows = -(-nb // SUBLANES) * SUBLANES
        c = v1[a] + s2[0:rows]
        if nb < rows:
            c = jnp.where(sub < nb, c, -jnp.inf)
        comb.append(c)
        cand.append(i1[a] * float(PEER_NKEYS) + e2[0:rows])
    comb.append(jnp.concatenate(v1[k // 2:], axis=0) + s2[0:1])
    cand.append(jnp.concatenate(i1[k // 2:], axis=0) * float(PEER_NKEYS) + e2[0:1])
    return jnp.concatenate(comb, axis=0), jnp.concatenate(cand, axis=0)


def _head_routing(tops1, tops2):
    k = PEER_TOPK
    comb, cand = _pair_candidates(tops1[0], tops1[1], tops2[0], tops2[1])
    tv, picked = _topk_rows(comb, k, payload=cand)
    ex = [jnp.exp(tv[r] - tv[0]) for r in range(k)]
    den = ex[0]
    for r in range(1, k):
        den = den + ex[r]
    return picked, [e / den for e in ex]


def _scores_kernel(h_ref, wq_ref, sk_ref, o_ref):
    h = h_ref[...]
    half = PEER_QDIM // 2
    for grp in range(2 * PEER_HEADS):
        q_t = _dot_nt(wq_ref[grp * half:(grp + 1) * half, :], h)
        o_ref[grp] = _dot(sk_ref[grp], q_t.astype(bf16))


def _scores(h2, wq_t, sk):
    n = h2.shape[0]
    tr = _TR
    return pl.pallas_call(
        _scores_kernel,
        grid=(n // tr,),
        in_specs=[
            pl.BlockSpec((tr, D_MODEL), lambda i: (i, 0)),
            pl.BlockSpec(wq_t.shape, lambda i: (0, 0)),
            pl.BlockSpec(sk.shape, lambda i: (0, 0, 0)),
        ],
        out_specs=pl.BlockSpec((2 * PEER_HEADS, PEER_NKEYS, tr), lambda i: (0, 0, i)),
        out_shape=jax.ShapeDtypeStruct((2 * PEER_HEADS, PEER_NKEYS, n), f32),
        compiler_params=_cparams(("arbitrary",)),
        name="scores",
    )(h2, wq_t, sk)


def _route_kernel(s_ref, idx_ref, gate_ref):
    k = PEER_TOPK
    experts, gates = [], []
    for hd in range(PEER_HEADS):
        picked, g = _head_routing(_topk_rows(s_ref[2 * hd], k), _topk_rows(s_ref[2 * hd + 1], k))
        experts += picked
        gates += g
    idx_t = jnp.concatenate(experts, axis=0).astype(i32)
    gate_t = jnp.concatenate(gates, axis=0)
    for blk in range(idx_t.shape[1] // LANES):
        sl = slice(blk * LANES, (blk + 1) * LANES)
        idx_ref[sl, :] = idx_t[:, sl].T
        gate_ref[sl, :] = gate_t[:, sl].T


def _route(scores, n_tok):
    ncol = PEER_HEADS * PEER_TOPK
    return pl.pallas_call(
        _route_kernel,
        grid=(1,),
        in_specs=[pl.BlockSpec((2 * PEER_HEADS, PEER_NKEYS, n_tok), lambda i: (0, 0, 0))],
        out_specs=[pl.BlockSpec((n_tok, ncol), lambda i: (0, 0))] * 2,
        out_shape=[jax.ShapeDtypeStruct((n_tok, ncol), i32), jax.ShapeDtypeStruct((n_tok, ncol), f32)],
        compiler_params=_cparams(("arbitrary",)),
        name="route",
    )(scores)


_TS = 128
_HALF = 64
_TG = 4
_NSLOT = 4
_AHEAD = (_NSLOT - 1) * _TG
_NROW = PEER_HEADS * PEER_TOPK
_ROW_TILES = 2 * D_MODEL // LANES
_TP = 2
_RING = 3
_RING_ROWS = (_RING + 2) * _TS


def _split_bf16(x):
    hi = x.astype(bf16)
    return hi, (x - hi.astype(f32)).astype(bf16)


def _expert_kernel(sc_ref, h_ref, x1_ref, mod_ref, gfin_ref, rsum_ref, rrep_ref, idx01_ref, gate01_ref, uv_ref, o_ref,
                   *scratch):
    step = pl.program_id(0)
    last = pl.num_programs(0) - 1
    bufs = scratch[:_NSLOT]
    out_tiles, idx_stage, gate_stage, idx_rows, gate_ring, idx_ring, sem, aux = scratch[_NSLOT:]
    n_dma = _TG * _NROW
    k = PEER_TOPK
    cur = lax.rem(step, _RING)
    nxt = lax.rem(step + 2, _RING)
    mirror = jnp.where(nxt == 0, _RING * _TS, (_RING + 1) * _TS)

    def row_copy(e, slot, j):
        return pltpu.make_async_copy(uv_ref.at[e], bufs[slot].at[j], sem.at[slot])

    def wait(slot):
        pltpu.make_async_copy(uv_ref.at[pl.ds(0, n_dma)], bufs[slot], sem.at[slot]).wait()

    def ring_copies():
        return (pltpu.make_async_copy(idx_rows, idx_ring.at[pl.ds(nxt * _TS, _TS)], aux.at[0]),
                pltpu.make_async_copy(idx_rows, idx_ring.at[pl.ds(mirror, _TS)], aux.at[1]))

    @pl.when(step == 0)
    def _():
        prime = (pltpu.make_async_copy(idx01_ref, idx_ring.at[pl.ds(0, 2 * _TS)], aux.at[0]),
                 pltpu.make_async_copy(idx01_ref.at[pl.ds(0, _TS)], idx_ring.at[pl.ds(_RING * _TS, _TS)], aux.at[1]),
                 pltpu.make_async_copy(gate01_ref, gate_ring.at[pl.ds(0, 2 * _TS)], aux.at[2]))
        for c in prime:
            c.start()
        for c in prime:
            c.wait()
        for slot in range(_NSLOT - 1):
            def body(t, c, slot=slot):
                for r in range(_NROW):
                    row_copy(idx_ring[slot * _TG + t, r], slot, t * _NROW + r).start(priority=r % 2)
                return c
            lax.fori_loop(0, _TG, body, 0)

    @pl.when(step > 0)
    def _():
        for c in ring_copies():
            c.wait()

    gt_f = mod_ref[0, 5]
    gfin = gfin_ref[...]
    width = _TP * LANES
    r_i = lax.broadcasted_iota(i32, (width, width), 0) // LANES
    c_i = lax.broadcasted_iota(i32, (width, width), 1) // LANES
    lane_sum = jnp.where(r_i == c_i, 1.0, 0.0).astype(bf16)
    eye = jnp.where(lax.broadcasted_iota(i32, (_NROW, LANES), 0) == lax.broadcasted_iota(i32, (_NROW, LANES), 1),
                    1.0, 0.0)

    def evaluate(tok0, slot, t0):
        src = bufs[slot]
        z, vs = [], []
        for dt in range(_TP):
            uv = src[pl.ds((t0 + dt) * _NROW, _NROW)].astype(f32)
            vs.append(uv[:, SUBLANES:2 * SUBLANES, :])
            u = uv[:, 0:SUBLANES, :]
            z.append((u * h_ref[tok0 + dt][None]).reshape(_NROW * SUBLANES, LANES).astype(bf16))
        part = _dot(rsum_ref[...], jnp.concatenate(z, axis=1))
        p_hi, p_lo = _split_bf16(part)
        a = _dot(p_hi, lane_sum) + _dot(p_lo, lane_sum)
        g_diag = jnp.concatenate([eye * gate_ring[pl.ds(cur * _TS + tok0 + dt, 1), :] for dt in range(_TP)], axis=1)
        g = _dot(g_diag.astype(bf16), lane_sum)
        w = (g * _gelu(a)).astype(bf16)
        spread = _dot(rrep_ref[...], w)
        for dt in range(_TP):
            tok = tok0 + dt
            wt = spread[:, dt * LANES:(dt + 1) * LANES].reshape(_NROW, SUBLANES, LANES)
            y = jnp.sum(wt * vs[dt], axis=0)
            x2 = x1_ref[tok] + gt_f * y
            ms = jnp.sum(jnp.sum(x2 * x2, axis=1, keepdims=True), axis=0, keepdims=True) * (1.0 / D_MODEL)
            out_tiles[tok] = x2 * lax.rsqrt(ms + EPS) * gfin

    def half_step(half, carry):
        tok_h = half * _HALF
        ring_h = cur * _TS + tok_h
        head0 = half * (PEER_HEADS // 2)
        experts, gates, tops = [], [], {}
        pieces = []
        for hh in range(PEER_HEADS // 2):
            for p in range(2):
                def sub_keys(hh=hh, p=p):
                    tops[hh, p] = _topk_rows(sc_ref[(head0 + hh) * 2 + p], k)
                pieces.append(sub_keys)

            def pairs(hh=hh):
                picked, g = _head_routing(tops[hh, 0], tops[hh, 1])
                experts.extend(picked)
                gates.extend(g)
            pieces.append(pairs)

        def stage():
            rows = pl.ds(pl.multiple_of(head0 * k, _HALF), _HALF)
            idx_stage[rows, :] = jnp.concatenate(experts, axis=0).astype(i32)
            gate_stage[rows, :] = jnp.concatenate(gates, axis=0)
        pieces.append(stage)

        for grp in range(_HALF // _TG):
            slot = grp % _NSLOT
            wait(slot)
            if grp < len(pieces):
                pieces[grp]()
            dst = (slot + _NSLOT - 1) % _NSLOT
            for t0 in range(0, _TG, _TP):
                evaluate(tok_h + grp * _TG + t0, slot, t0)
                for t in range(t0, t0 + _TP):
                    for r in range(_NROW):
                        row_copy(idx_ring[ring_h + grp * _TG + t + _AHEAD, r], dst, t * _NROW + r).start(priority=r % 2)
        return carry

    lax.fori_loop(0, _TS // _HALF, half_step, 0)
    o_ref[...] = out_tiles[...].reshape(_TS, D_MODEL)
    idx_rows[...] = idx_stage[...].T
    gate_ring[pl.ds(pl.multiple_of(nxt * _TS, _TS), _TS), :] = gate_stage[...].T
    for c in ring_copies():
        c.start()

    @pl.when(step == last)
    def _():
        for c in ring_copies():
            c.wait()
        for slot in range(_NSLOT - 1):
            wait(slot)


def _experts(scores, idx01, gate01, h3, x13, mod4, gfin, uv, seq):
    n = h3.shape[0]
    n_steps = n // _TS
    steps_per_seq = seq // _TS
    tok3 = lambda i: (i, 0, 0)
    rsum = jnp.asarray(np.repeat(np.eye(_NROW, dtype=np.float32), SUBLANES, axis=1), dtype=bf16)
    buf = pltpu.VMEM((_TG * _NROW, _ROW_TILES, LANES), bf16)
    return pl.pallas_call(
        _expert_kernel,
        grid=(n_steps,),
        in_specs=[
            pl.BlockSpec((2 * PEER_HEADS, PEER_NKEYS, _TS), lambda i: (0, 0, jnp.minimum(i + 2, n_steps - 1))),
            pl.BlockSpec((_TS, SUBLANES, LANES), tok3),
            pl.BlockSpec((_TS, SUBLANES, LANES), tok3),
            pl.BlockSpec((1, 6, SUBLANES, LANES), lambda i: (i // steps_per_seq, 0, 0, 0)),
            pl.BlockSpec((SUBLANES, LANES), lambda i: (0, 0)),
            pl.BlockSpec(rsum.shape, lambda i: (0, 0)),
            pl.BlockSpec(rsum.shape[::-1], lambda i: (0, 0)),
            pl.BlockSpec(memory_space=pl.ANY),
            pl.BlockSpec(memory_space=pl.ANY),
            pl.BlockSpec(memory_space=pl.ANY),
        ],
        out_specs=pl.BlockSpec((_TS, D_MODEL), lambda i: (i, 0)),
        out_shape=jax.ShapeDtypeStruct((n, D_MODEL), f32),
        scratch_shapes=[buf] * _NSLOT + [
            pltpu.VMEM((_TS, SUBLANES, LANES), f32),
            pltpu.VMEM((_NROW, _TS), i32),
            pltpu.VMEM((_NROW, _TS), f32),
            pltpu.VMEM((_TS, _NROW), i32),
            pltpu.VMEM((_RING * _TS, _NROW), f32),
            pltpu.SMEM((_RING_ROWS, _NROW), i32),
            pltpu.SemaphoreType.DMA((_NSLOT,)),
            pltpu.SemaphoreType.DMA((3,)),
        ],
        compiler_params=_cparams(("arbitrary",)),
        name="experts",
    )(scores, h3, x13, mod4, gfin, rsum, rsum.T, idx01, gate01, uv)


def _rope_tables(seq):
    half = HEAD_DIM // 2
    inv = ROPE_THETA ** (-np.arange(half, dtype=np.float32) / half)
    ang = np.arange(seq, dtype=np.float32)[:, None] * inv[None, :].astype(np.float32)
    cos = np.cos(ang).astype(np.float32)
    sin = np.sin(ang).astype(np.float32)
    cos_t = np.tile(np.concatenate([cos, cos], axis=1), (1, NSA_HEADS))
    sin_t = np.tile(np.concatenate([-sin, sin], axis=1), (1, NSA_HEADS))
    return jnp.asarray(cos_t), jnp.asarray(sin_t)


def _selection_map(n_cmp_pad, n_sel):
    r_sel = SEL_BLOCK // CMP_STRIDE
    r_cmp = CMP_BLOCK // CMP_STRIDE
    i = np.arange(n_cmp_pad)[:, None]
    j = np.arange(n_sel)[None, :]
    d = i - r_sel * j
    cnt = np.minimum(d, r_sel - 1) - np.maximum(d - r_cmp + 1, 0) + 1
    cnt = np.clip(cnt, 0, None).astype(np.float32)
    cnt[n_cmp_pad - r_cmp + 1:] = 0.0
    return jnp.asarray(cnt.T)


def _block_expand(n_sel, seq):
    e = (np.arange(seq)[None, :] // SEL_BLOCK == np.arange(n_sel)[:, None]).astype(np.float32)
    return jnp.asarray(e, dtype=bf16)


def _compress_weights(w1):
    eye = jnp.eye(NSA_KV_GROUPS, dtype=w1.dtype)
    out = []
    for part in range(CMP_BLOCK // CMP_STRIDE):
        w = w1[part * CMP_STRIDE * HEAD_DIM:(part + 1) * CMP_STRIDE * HEAD_DIM].reshape(CMP_STRIDE, HEAD_DIM, CMP_HIDDEN)
        big = jnp.einsum('pdc,gh->pgdhc', w, eye).reshape(CMP_STRIDE * KV_WIDTH, NSA_KV_GROUPS * CMP_HIDDEN)
        out.append(big.astype(bf16))
    return out


def kernel(x, c, w_ada, b_ada, g_mix, g_ffn, w_in, cmp_pos_k, cmp_pos_v, w_ck1, w_ck2, w_cv1, w_cv2, hgrn_lb_logits, hgrn_out_norm, w_branch, w_out, w_peer_q, peer_sub_keys, peer_u, peer_v, g_final):
    bsz, seq, d = x.shape
    n = bsz * seq
    depth = w_ada.shape[0]
    assert depth == 1, "single-layer block only"
    n_sel = seq // SEL_BLOCK
    n_pieces = seq // CMP_STRIDE
    cos_t, sin_t = _rope_tables(seq)
    selmap = _selection_map(n_pieces, n_sel)
    expand = _block_expand(n_sel, seq)
    xcur = x.reshape(n, d)
    for l in range(depth):
        mod = _adaln(c, w_ada[l], b_ada[l].reshape(1, 6 * d))
        mod3 = mod.reshape(bsz, 6, d)
        w_pad = jnp.concatenate(
            [w_in[l][:, :_C_GL + 3 * NSA_HEADS], jnp.zeros((d, GATE_PAD - 3 * NSA_HEADS), w_in.dtype),
             w_in[l][:, _C_GL + 3 * NSA_HEADS:]], axis=1).astype(bf16)
        q, kc, vc, kvsw, gates, hq, hf, hi, hg, mg = _inproj(xcur, mod3, g_mix[l].reshape(1, d), w_pad, cos_t, sin_t, seq)
        wkt, wkb = _compress_weights(w_ck1[l])
        wvt, wvb = _compress_weights(w_cv1[l])
        kc_c, vc_c = _compress(
            kc.reshape(bsz, n_pieces, CMP_STRIDE * KV_WIDTH), vc.reshape(bsz, n_pieces, CMP_STRIDE * KV_WIDTH),
            wkt, wkb, wvt, wvb, cmp_pos_k[l].reshape(1, -1), cmp_pos_v[l].reshape(1, -1),
            w_ck1[l], w_cv1[l], w_ck2[l], w_cv2[l])
        o_nsa = _nsa(q, kvsw.reshape(bsz, seq, 4 * KV_WIDTH), kc_c, vc_c, gates, selmap, expand, seq)
        o_hgrn = _hgrn(hq, hf, hi, hg, hgrn_lb_logits, hgrn_out_norm[l].reshape(1, -1), seq, l)
        x1, h2, h2_b = _merge(o_nsa, o_hgrn, mg, xcur, mod3, g_ffn[l].reshape(1, d),
                              w_branch[l, 0].astype(bf16), w_branch[l, 1].astype(bf16), w_out[l].astype(bf16), seq)
        wq_t = w_peer_q[l].T.astype(bf16)
        sk = jnp.transpose(peer_sub_keys[l], (1, 0, 2, 3)).reshape(2 * PEER_HEADS, PEER_NKEYS, PEER_QDIM // 2).astype(bf16)
        scores = _scores(h2_b, wq_t, sk)
        idx01, gate01 = _route(scores, 2 * _TS)
        uv = jnp.concatenate([peer_u[l], peer_v[l]], axis=1).astype(bf16).reshape(-1, _ROW_TILES, LANES)
        xcur = _experts(scores, idx01, gate01, h2, x1, mod.reshape(bsz, 6, SUBLANES, LANES),
                        g_final.reshape(SUBLANES, LANES), uv, seq)
    return xcur.reshape(bsz, seq, d)
```

```python
import functools

import numpy as np
import jax
import jax.numpy as jnp
from jax import lax
from jax.experimental import pallas as pl
from jax.experimental.pallas import tpu as pltpu

f32 = jnp.float32
bf16 = jnp.bfloat16
i32 = jnp.int32
_HIGHEST = lax.Precision.HIGHEST

D_MODEL = 1024
EPS = 1e-6
HEAD_DIM = 64
ROPE_THETA = 10000.0
NSA_HEADS = 8
NSA_KV_GROUPS = 2
NSA_GROUP = NSA_HEADS // NSA_KV_GROUPS
NSA_WIDTH = NSA_HEADS * HEAD_DIM
KV_WIDTH = NSA_KV_GROUPS * HEAD_DIM
CMP_BLOCK = 32
CMP_STRIDE = 16
CMP_HIDDEN = 128
SEL_BLOCK = 64
SEL_TOPK = 16
WINDOW = 512
FORCE_BONUS = 1000.0
HGRN_HEADS = 4
HGRN_EXPAND = 128
HGRN_HEAD_V = 128
HGRN_WIDTH = HGRN_HEADS * HGRN_EXPAND
HGRN_CHUNK = 32
PEER_HEADS = 8
PEER_NKEYS = 128
PEER_QDIM = 256
PEER_TOPK = 16
GATE_PAD = 128

LANES = 128
SUBLANES = 8
VMEM_LIMIT = 56 * 1024 * 1024

_NEG = -1e30


def _cparams(sem):
    return pltpu.CompilerParams(dimension_semantics=sem, vmem_limit_bytes=VMEM_LIMIT)


def _gelu(x):
    return 0.5 * x * (1.0 + jnp.tanh(0.7978845608028654 * (x + 0.044715 * (x * x * x))))


def _dot(a, b, **kw):
    return jnp.dot(a, b, preferred_element_type=f32, **kw)


def _dot_nt(a, b):
    return lax.dot_general(a, b, (((1,), (1,)), ((), ())), preferred_element_type=f32)


def _rms(x, g):
    return x * lax.rsqrt(jnp.mean(x * x, axis=-1, keepdims=True) + EPS) * g


def _adaln_kernel(c_ref, w_ref, b_ref, o_ref):
    c = c_ref[...]
    cs = c * jax.nn.sigmoid(c)
    o_ref[...] = _dot(cs, w_ref[...], precision=_HIGHEST) + b_ref[...]


def _adaln(c, w, b):
    bsz = c.shape[0]
    tn = 512
    return pl.pallas_call(
        _adaln_kernel,
        grid=(6 * D_MODEL // tn,),
        in_specs=[
            pl.BlockSpec((bsz, D_MODEL), lambda j: (0, 0)),
            pl.BlockSpec((D_MODEL, tn), lambda j: (0, j)),
            pl.BlockSpec((1, tn), lambda j: (0, j)),
        ],
        out_specs=pl.BlockSpec((bsz, tn), lambda j: (0, j)),
        out_shape=jax.ShapeDtypeStruct((bsz, 6 * D_MODEL), f32),
        compiler_params=_cparams(("arbitrary",)),
        name="adaln",
    )(c, w, b)


_C_Q = 0
_C_KC = 512
_C_VC = 640
_C_KS = 768
_C_VS = 896
_C_KW = 1024
_C_VW = 1152
_C_GL = 1280
_C_HQ = _C_GL + GATE_PAD
_C_HF = _C_HQ + 512
_C_HI = _C_HF + 512
_C_HG = _C_HI + 512
_C_MG = _C_HG + 512
_C_END = _C_MG + 2 * D_MODEL


def _inproj_kernel(x_ref, mod_ref, g_ref, w_ref, cos_ref, sin_ref,
                   q_ref, kc_ref, vc_ref, kvsw_ref, gate_ref, hq_ref, hf_ref, hi_ref, hg_ref, mg_ref):
    x = x_ref[...]
    sh = mod_ref[0, 0:1, :]
    sc = mod_ref[0, 1:2, :]
    h = (_rms(x, g_ref[...]) * (1.0 + sc) + sh).astype(bf16)

    def mm(c0, c1):
        return _dot(h, w_ref[:, c0:c1])

    cos = cos_ref[...]
    sin = sin_ref[...]

    def rope(a):
        width = a.shape[1]
        first = (lax.broadcasted_iota(i32, a.shape, 1) & (HEAD_DIM - 1)) < (HEAD_DIM // 2)
        partner = jnp.where(first, pltpu.roll(a, width - HEAD_DIM // 2, 1), pltpu.roll(a, HEAD_DIM // 2, 1))
        return a * cos[:, :width] + partner * sin[:, :width]

    q_ref[...] = (rope(mm(_C_Q, _C_KC)) * (HEAD_DIM ** -0.5)).astype(bf16)
    kc_ref[...] = rope(mm(_C_KC, _C_VC)).astype(bf16)
    vc_ref[...] = mm(_C_VC, _C_KS).astype(bf16)
    kvsw_ref[:, 0:128] = rope(mm(_C_KS, _C_VS)).astype(bf16)
    kvsw_ref[:, 128:256] = mm(_C_VS, _C_KW).astype(bf16)
    kvsw_ref[:, 256:384] = rope(mm(_C_KW, _C_VW)).astype(bf16)
    kvsw_ref[:, 384:512] = mm(_C_VW, _C_GL).astype(bf16)
    gate_ref[...] = jax.nn.sigmoid(mm(_C_GL, _C_HQ))
    hq_ref[...] = mm(_C_HQ, _C_HF).astype(bf16)
    hf_ref[...] = mm(_C_HF, _C_HI)
    hi_ref[...] = mm(_C_HI, _C_HG).astype(bf16)
    hg_ref[...] = mm(_C_HG, _C_MG).astype(bf16)
    mg_ref[...] = jax.nn.sigmoid(mm(_C_MG, _C_END)).astype(bf16)


def _inproj(x2, mod3, g_mix, w_pad, cos_t, sin_t, seq):
    n = x2.shape[0]
    tm = 256
    tiles_per_seq = seq // tm
    row = lambda i: (i, 0)
    outs = [
        (NSA_WIDTH, bf16), (KV_WIDTH, bf16), (KV_WIDTH, bf16), (4 * KV_WIDTH, bf16), (GATE_PAD, f32),
        (HGRN_WIDTH, bf16), (HGRN_WIDTH, f32), (HGRN_WIDTH, bf16), (HGRN_WIDTH, bf16), (2 * D_MODEL, bf16),
    ]
    return pl.pallas_call(
        _inproj_kernel,
        grid=(n // tm,),
        in_specs=[
            pl.BlockSpec((tm, D_MODEL), row),
            pl.BlockSpec((1, 6, D_MODEL), lambda i: (i // tiles_per_seq, 0, 0)),
            pl.BlockSpec((1, D_MODEL), lambda i: (0, 0)),
            pl.BlockSpec((D_MODEL, _C_END), lambda i: (0, 0)),
            pl.BlockSpec((tm, NSA_WIDTH), lambda i: (i % tiles_per_seq, 0)),
            pl.BlockSpec((tm, NSA_WIDTH), lambda i: (i % tiles_per_seq, 0)),
        ],
        out_specs=[pl.BlockSpec((tm, w), row) for w, _ in outs],
        out_shape=[jax.ShapeDtypeStruct((n, w), dt) for w, dt in outs],
        compiler_params=_cparams(("arbitrary",)),
        name="inproj",
    )(x2, mod3, g_mix, w_pad, cos_t, sin_t)


def _compress_kernel(kc_ref, vc_ref, wkt_ref, wkb_ref, wvt_ref, wvb_ref, pk_ref, pv_ref,
                     w1k_ref, w1v_ref, w2k_ref, w2v_ref, okc_ref, ovc_ref):
    def one(x_ref, wt_ref, wb_ref, pos_ref, w1_ref, w2_ref, o_ref):
        pieces = x_ref[0]
        top = _dot(pieces, wt_ref[...])
        bot = _dot(pieces, wb_ref[...])
        nrow = bot.shape[0]
        bot = pltpu.roll(bot, nrow - 1, 0)
        cpos = _dot(pos_ref[...], w1_ref[...], precision=_HIGHEST)
        w2 = w2_ref[...].astype(bf16)
        outs = []
        for g in range(NSA_KV_GROUPS):
            sl = slice(g * CMP_HIDDEN, (g + 1) * CMP_HIDDEN)
            hid = _gelu(top[:, sl] + bot[:, sl] + cpos)
            outs.append(_dot(hid.astype(bf16), w2))
        o_ref[0] = jnp.concatenate(outs, axis=1).astype(bf16)

    one(kc_ref, wkt_ref, wkb_ref, pk_ref, w1k_ref, w2k_ref, okc_ref)
    one(vc_ref, wvt_ref, wvb_ref, pv_ref, w1v_ref, w2v_ref, ovc_ref)


def _compress(kc3, vc3, wkt, wkb, wvt, wvb, pk, pv, w1k, w1v, w2k, w2v):
    bsz, npieces, width = kc3.shape
    full = lambda a: pl.BlockSpec(a.shape, lambda b: (0,) * a.ndim)
    per_b = pl.BlockSpec((1, npieces, width), lambda b: (b, 0, 0))
    out_b = pl.BlockSpec((1, npieces, KV_WIDTH), lambda b: (b, 0, 0))
    return pl.pallas_call(
        _compress_kernel,
        grid=(bsz,),
        in_specs=[per_b, per_b] + [full(a) for a in (wkt, wkb, wvt, wvb, pk, pv, w1k, w1v, w2k, w2v)],
        out_specs=[out_b, out_b],
        out_shape=[jax.ShapeDtypeStruct((bsz, npieces, KV_WIDTH), bf16)] * 2,
        compiler_params=_cparams(("arbitrary",)),
        name="compress",
    )(kc3, vc3, wkt, wkb, wvt, wvb, pk, pv, w1k, w1v, w2k, w2v)


_TQ = 128
_NSA_CLASSES = 4


def _softmax_parts(s, mask):
    sm = jnp.where(mask, s, _NEG)
    m = jnp.max(sm, axis=-1, keepdims=True)
    p = jnp.where(mask, jnp.exp(sm - m), 0.0)
    den = jnp.maximum(jnp.sum(p, axis=-1, keepdims=True), 1e-30)
    return p, den


def _softmax_plain(s):
    p = jnp.exp(s - jnp.max(s, axis=-1, keepdims=True))
    return p, jnp.sum(p, axis=-1, keepdims=True)


def _nsa_kernel(q_ref, kvsw_ref, kc_ref, vc_ref, gate_ref, selmap_ref, expand_ref, o_ref, *, seq):
    tq = _TQ
    qi = pl.program_id(1)
    n_cls = _NSA_CLASSES
    per_cls = (seq // tq) // n_cls
    for cls in range(n_cls):
        @pl.when((qi >= cls * per_cls) & (qi < (cls + 1) * per_cls))
        def _(cls=cls):
            _nsa_tile(q_ref, kvsw_ref, kc_ref, vc_ref, gate_ref, selmap_ref, expand_ref, o_ref,
                      seq=seq, width=(cls + 1) * (seq // n_cls))


def _nsa_tile(q_ref, kvsw_ref, kc_ref, vc_ref, gate_ref, selmap_ref, expand_ref, o_ref, *, seq, width):
    tq = _TQ
    rows = NSA_GROUP * tq
    n_sel = seq // SEL_BLOCK
    n_cmp_pad = seq // CMP_STRIDE
    t0 = pl.program_id(1) * tq
    q = q_ref[...]
    gates = gate_ref[...]
    tlane = t0 + lax.broadcasted_iota(i32, (1, tq), 1)
    t4 = t0 + (lax.broadcasted_iota(i32, (rows, 1), 0) & (tq - 1))
    wk = WINDOW + tq
    ws = pl.multiple_of(jnp.maximum(t0 - WINDOW, 0), tq)
    ks_all = kvsw_ref[0, 0:width, 0:128]
    vs_all = kvsw_ref[0, 0:width, 128:256]
    kw_all = kvsw_ref[0, pl.ds(ws, wk), 256:384]
    vw_all = kvsw_ref[0, pl.ds(ws, wk), 384:512]
    kc_all = kc_ref[0]
    vc_all = vc_ref[0]
    cend = lax.broadcasted_iota(i32, (1, n_cmp_pad), 1) * CMP_STRIDE + (CMP_BLOCK - 1)
    blk = lax.broadcasted_iota(i32, (n_sel, 1), 0)
    kpos = lax.broadcasted_iota(i32, (1, width), 1)
    kpos_w = ws + lax.broadcasted_iota(i32, (1, wk), 1)
    cur = tlane >> 6
    forced = (blk == 0) | (blk == cur) | (blk == cur - 1)
    causal_blk = blk * SEL_BLOCK <= tlane
    pieces = []
    for g in range(NSA_KV_GROUPS):
        gs = slice(g * HEAD_DIM, (g + 1) * HEAD_DIM)
        qg = jnp.concatenate(
            [q[:, (NSA_GROUP * g + r) * HEAD_DIM:(NSA_GROUP * g + r + 1) * HEAD_DIM] for r in range(NSA_GROUP)], axis=0)
        p_c, den_c = _softmax_parts(_dot_nt(qg, kc_all[:, gs]), cend <= t4)
        o_c = _dot(p_c.astype(bf16), vc_all[:, gs]) / den_c
        pn = p_c / den_c
        pc_sum = pn[0:tq]
        for r in range(1, NSA_GROUP):
            pc_sum = pc_sum + pn[r * tq:(r + 1) * tq]
        imp = lax.dot_general(selmap_ref[...], pc_sum, (((1,), (1,)), ((), ())),
                              preferred_element_type=f32, precision=_HIGHEST)
        imp = jnp.where(forced, imp + FORCE_BONUS, imp)
        imp = jnp.where(causal_blk, imp, -1.0)
        rank = jnp.zeros((n_sel, tq), f32)
        for j in range(n_sel):
            row = imp[j:j + 1, :]
            ahead = jnp.where(row > imp, 1.0, jnp.where(row == imp, jnp.where(blk > j, 1.0, 0.0), 0.0))
            rank = rank + ahead
        bias = jnp.where(rank < float(min(SEL_TOPK, n_sel)), jnp.where(causal_blk, 0.0, _NEG), _NEG).astype(bf16)
        bias_q = lax.dot_general(bias, expand_ref[:, 0:width], (((0,), (0,)), ((), ())),
                                 preferred_element_type=f32)
        bias4 = jnp.concatenate([bias_q] * NSA_GROUP, axis=0)
        p_s, den_s = _softmax_plain(jnp.where(kpos <= t4, _dot_nt(qg, ks_all[:, gs]) + bias4, _NEG))
        o_s = _dot(p_s.astype(bf16), vs_all[:, gs]) / den_s
        in_window = (t4 - kpos_w).astype(jnp.uint32) < WINDOW
        p_w, den_w = _softmax_plain(jnp.where(in_window, _dot_nt(qg, kw_all[:, gs]), _NEG))
        o_w = _dot(p_w.astype(bf16), vw_all[:, gs]) / den_w

        def gcol(br):
            return jnp.concatenate(
                [gates[:, (NSA_GROUP * g + r) * 3 + br:(NSA_GROUP * g + r) * 3 + br + 1] for r in range(NSA_GROUP)], axis=0)

        o = gcol(0) * o_c + gcol(1) * o_s + gcol(2) * o_w
        pieces += [o[r * tq:(r + 1) * tq] for r in range(NSA_GROUP)]
    o_ref[...] = jnp.concatenate(pieces, axis=1).astype(bf16)


def _nsa(q, kvsw3, kc3, vc3, gates, selmap, expand, seq):
    n = q.shape[0]
    bsz = n // seq
    tq = _TQ
    nq = seq // tq
    return pl.pallas_call(
        functools.partial(_nsa_kernel, seq=seq),
        grid=(bsz, nq),
        in_specs=[
            pl.BlockSpec((tq, NSA_WIDTH), lambda b, i: (b * nq + i, 0)),
            pl.BlockSpec((1, seq, 4 * KV_WIDTH), lambda b, i: (b, 0, 0)),
            pl.BlockSpec((1,) + kc3.shape[1:], lambda b, i: (b, 0, 0)),
            pl.BlockSpec((1,) + vc3.shape[1:], lambda b, i: (b, 0, 0)),
            pl.BlockSpec((tq, GATE_PAD), lambda b, i: (b * nq + i, 0)),
            pl.BlockSpec(selmap.shape, lambda b, i: (0, 0)),
            pl.BlockSpec(expand.shape, lambda b, i: (0, 0)),
        ],
        out_specs=pl.BlockSpec((tq, NSA_WIDTH), lambda b, i: (b * nq + i, 0)),
        out_shape=jax.ShapeDtypeStruct((n, NSA_WIDTH), bf16),
        compiler_params=_cparams(("arbitrary", "arbitrary")),
        name="nsa",
    )(q, kvsw3, kc3, vc3, gates, selmap, expand)


def _hgrn_kernel(hq_ref, hf_ref, hi_ref, hg_ref, lbl_ref, ng_ref, o_ref, st_ref, *, seq, layer):
    c = HGRN_CHUNK
    logits = lbl_ref[...]
    e = jnp.exp(logits - jnp.max(logits, axis=0, keepdims=True))
    sm = e / jnp.sum(e, axis=0, keepdims=True)
    lb = sm[0:1]
    for l in range(1, layer + 1):
        lb = lb + sm[l:l + 1]
    st_ref[...] = jnp.zeros_like(st_ref)
    r_i = lax.broadcasted_iota(i32, (c, c), 0)
    c_i = lax.broadcasted_iota(i32, (c, c), 1)
    causal = r_i >= c_i
    tri = jnp.where(causal, 1.0, 0.0)
    ng = ng_ref[...]

    def body(ci, carry):
        r0 = pl.multiple_of(ci * c, c)
        hq = hq_ref[pl.ds(r0, c), :].astype(f32)
        hf = hf_ref[pl.ds(r0, c), :]
        v = hi_ref[pl.ds(r0, c), :]
        hg = hg_ref[pl.ds(r0, c), :].astype(f32)
        f = lb + (1.0 - lb) * jax.nn.sigmoid(hf)
        k = 1.0 - f
        qv = hq * jax.nn.sigmoid(hq)
        bcum = _dot(tri, jnp.log(f), precision=_HIGHEST)
        bend = bcum[c - 1:c, :]
        q_dec = (qv * jnp.exp(bcum)).astype(bf16)
        k_inv = (k * jnp.exp(-bcum)).astype(bf16)
        k_end = (k * jnp.exp(bend - bcum)).astype(bf16)
        dec_end = jnp.exp(bend)
        outs = []
        for h in range(HGRN_HEADS):
            sl = slice(h * HGRN_EXPAND, (h + 1) * HGRN_EXPAND)
            a = jnp.where(causal, _dot_nt(q_dec[:, sl], k_inv[:, sl]), 0.0)
            st = st_ref[h]
            o = _dot(a.astype(bf16), v[:, sl]) + _dot_nt(q_dec[:, sl], st.astype(bf16))
            upd = lax.dot_general(v[:, sl], k_end[:, sl], (((0,), (0,)), ((), ())), preferred_element_type=f32)
            st_ref[h] = st * dec_end[:, sl] + upd
            y = _rms(o, ng) * (hg[:, sl] * jax.nn.sigmoid(hg[:, sl]))
            outs.append(y)
        o_ref[pl.ds(r0, c), :] = jnp.concatenate(outs, axis=1).astype(bf16)
        return carry

    lax.fori_loop(0, seq // c, body, 0, unroll=4)


def _hgrn(hq, hf, hi, hg, lb_logits, norm_g, seq, layer):
    n = hq.shape[0]
    bsz = n // seq
    per_b = pl.BlockSpec((seq, HGRN_WIDTH), lambda b: (b, 0))
    return pl.pallas_call(
        functools.partial(_hgrn_kernel, seq=seq, layer=layer),
        grid=(bsz,),
        in_specs=[per_b, per_b, per_b, per_b,
                  pl.BlockSpec(lb_logits.shape, lambda b: (0, 0)),
                  pl.BlockSpec(norm_g.shape, lambda b: (0, 0))],
        out_specs=per_b,
        out_shape=jax.ShapeDtypeStruct((n, HGRN_WIDTH), bf16),
        scratch_shapes=[pltpu.VMEM((HGRN_HEADS, HGRN_HEAD_V, HGRN_EXPAND), f32)],
        compiler_params=_cparams(("arbitrary",)),
        name="hgrn",
    )(hq, hf, hi, hg, lb_logits, norm_g)


def _merge_kernel(on_ref, oh_ref, mg_ref, x_ref, mod_ref, g_ref, wb0_ref, wb1_ref, wo_ref, x1_ref, h2_ref, hb_ref):
    a = _dot(on_ref[...], wb0_ref[...])
    b = _dot(oh_ref[...], wb1_ref[...])
    mg = mg_ref[...].astype(f32)
    y = mg[:, :D_MODEL] * a + mg[:, D_MODEL:] * b
    y2 = _dot(y.astype(bf16), wo_ref[...])
    x1 = x_ref[...] + mod_ref[0, 2:3, :] * y2
    h2 = _rms(x1, g_ref[...]) * (1.0 + mod_ref[0, 4:5, :]) + mod_ref[0, 3:4, :]
    rows = x1.shape[0]
    x1_ref[...] = x1.reshape(rows, SUBLANES, LANES)
    h2_ref[...] = h2.reshape(rows, SUBLANES, LANES)
    hb_ref[...] = h2.astype(bf16)


def _merge(o_nsa, o_hgrn, mg, x2, mod3, g_ffn, wb0, wb1, wo, seq):
    n = x2.shape[0]
    tm = 256
    tiles_per_seq = seq // tm
    row = lambda i: (i, 0)
    full = lambda a: pl.BlockSpec(a.shape, lambda i: (0,) * a.ndim)
    return pl.pallas_call(
        _merge_kernel,
        grid=(n // tm,),
        in_specs=[
            pl.BlockSpec((tm, NSA_WIDTH), row), pl.BlockSpec((tm, HGRN_WIDTH), row),
            pl.BlockSpec((tm, 2 * D_MODEL), row), pl.BlockSpec((tm, D_MODEL), row),
            pl.BlockSpec((1, 6, D_MODEL), lambda i: (i // tiles_per_seq, 0, 0)),
            full(g_ffn), full(wb0), full(wb1), full(wo),
        ],
        out_specs=[pl.BlockSpec((tm, SUBLANES, LANES), lambda i: (i, 0, 0))] * 2 + [pl.BlockSpec((tm, D_MODEL), row)],
        out_shape=[jax.ShapeDtypeStruct((n, SUBLANES, LANES), f32)] * 2 + [jax.ShapeDtypeStruct((n, D_MODEL), bf16)],
        compiler_params=_cparams(("arbitrary",)),
        name="merge",
    )(o_nsa, o_hgrn, mg, x2, mod3, g_ffn, wb0, wb1, wo)


_TR = 256


def _topk_rows(s, k, payload=None):
    n = s.shape[0]
    rowid = lax.broadcasted_iota(i32, s.shape, 0).astype(f32)
    vals, picks = [], []
    for _ in range(k):
        m = jnp.max(s, axis=0, keepdims=True)
        i = jnp.min(jnp.where(s == m, rowid, float(n)), axis=0, keepdims=True)
        hit = rowid == i
        vals.append(m)
        picks.append(i if payload is None else jnp.max(jnp.where(hit, payload, -1.0), axis=0, keepdims=True))
        s = jnp.where(hit, -jnp.inf, s)
    return vals, picks


def _pair_candidates(v1, i1, v2, i2):
    k = len(v1)
    s2 = jnp.concatenate(v2, axis=0)
    e2 = jnp.concatenate(i2, axis=0)
    sub = lax.broadcasted_iota(i32, (SUBLANES, s2.shape[1]), 0)
    comb, cand = [], []
    for a in range(k // 2):
        nb = k // (a + 1)
        rows = -(-nb // SUBLANES) * SUBLANES
        c = v1[a] + s2[0:rows]
        if nb < rows:
            c = jnp.where(sub < nb, c, -jnp.inf)
        comb.append(c)
        cand.append(i1[a] * float(PEER_NKEYS) + e2[0:rows])
    comb.append(jnp.concatenate(v1[k // 2:], axis=0) + s2[0:1])
    cand.append(jnp.concatenate(i1[k // 2:], axis=0) * float(PEER_NKEYS) + e2[0:1])
    return jnp.concatenate(comb, axis=0), jnp.concatenate(cand, axis=0)


def _head_routing(tops1, tops2):
    k = PEER_TOPK
    comb, cand = _pair_candidates(tops1[0], tops1[1], tops2[0], tops2[1])
    tv, picked = _topk_rows(comb, k, payload=cand)
    ex = [jnp.exp(tv[r] - tv[0]) for r in range(k)]
    den = ex[0]
    for r in range(1, k):
        den = den + ex[r]
    return picked, [e / den for e in ex]


def _scores_kernel(h_ref, wq_ref, sk_ref, o_ref):
    q = _dot(h_ref[...], wq_ref[...]).astype(bf16)
    half = PEER_QDIM // 2
    for grp in range(2 * PEER_HEADS):
        o_ref[grp] = _dot_nt(sk_ref[grp], q[:, grp * half:(grp + 1) * half])


def _scores(h2, wq, sk):
    n = h2.shape[0]
    tr = _TR
    return pl.pallas_call(
        _scores_kernel,
        grid=(n // tr,),
        in_specs=[
            pl.BlockSpec((tr, D_MODEL), lambda i: (i, 0)),
            pl.BlockSpec(wq.shape, lambda i: (0, 0)),
            pl.BlockSpec(sk.shape, lambda i: (0, 0, 0)),
        ],
        out_specs=pl.BlockSpec((2 * PEER_HEADS, PEER_NKEYS, tr), lambda i: (0, 0, i)),
        out_shape=jax.ShapeDtypeStruct((2 * PEER_HEADS, PEER_NKEYS, n), f32),
        compiler_params=_cparams(("arbitrary",)),
        name="scores",
    )(h2, wq, sk)


def _route_kernel(s_ref, idx_ref, gate_ref):
    k = PEER_TOPK
    experts, gates = [], []
    for hd in range(PEER_HEADS):
        picked, g = _head_routing(_topk_rows(s_ref[2 * hd], k), _topk_rows(s_ref[2 * hd + 1], k))
        experts += picked
        gates += g
    idx_t = jnp.concatenate(experts, axis=0).astype(i32)
    gate_t = jnp.concatenate(gates, axis=0)
    for blk in range(idx_t.shape[1] // LANES):
        sl = slice(blk * LANES, (blk + 1) * LANES)
        idx_ref[sl, :] = idx_t[:, sl].T
        gate_ref[sl, :] = gate_t[:, sl].T


def _route(scores, n_tok):
    ncol = PEER_HEADS * PEER_TOPK
    return pl.pallas_call(
        _route_kernel,
        grid=(1,),
        in_specs=[pl.BlockSpec((2 * PEER_HEADS, PEER_NKEYS, n_tok), lambda i: (0, 0, 0))],
        out_specs=[pl.BlockSpec((n_tok, ncol), lambda i: (0, 0))] * 2,
        out_shape=[jax.ShapeDtypeStruct((n_tok, ncol), i32), jax.ShapeDtypeStruct((n_tok, ncol), f32)],
        compiler_params=_cparams(("arbitrary",)),
        name="route",
    )(scores)


_TS = 128
_HALF = 64
_TG = 4
_NSLOT = 4
_AHEAD = (_NSLOT - 1) * _TG
_NROW = PEER_HEADS * PEER_TOPK
_ROW_TILES = 2 * D_MODEL // LANES
_TP = 2
_RING = 3
_RING_ROWS = (_RING + 2) * _TS


def _split_bf16(x):
    hi = x.astype(bf16)
    return hi, (x - hi.astype(f32)).astype(bf16)


def _expert_kernel(sc_ref, h_ref, x1_ref, mod_ref, gfin_ref, rsum_ref, rrep_ref, idx01_ref, gate01_ref, uv_ref, o_ref,
                   *scratch):
    step = pl.program_id(0)
    last = pl.num_programs(0) - 1
    bufs = scratch[:_NSLOT]
    out_tiles, idx_stage, gate_stage, idx_rows, gate_ring, idx_ring, sem, aux = scratch[_NSLOT:]
    n_dma = _TG * _NROW
    k = PEER_TOPK
    cur = lax.rem(step, _RING)
    nxt = lax.rem(step + 2, _RING)
    mirror = jnp.where(nxt == 0, _RING * _TS, (_RING + 1) * _TS)

    def row_copy(e, slot, j):
        return pltpu.make_async_copy(uv_ref.at[e], bufs[slot].at[j], sem.at[slot])

    def wait(slot):
        pltpu.make_async_copy(uv_ref.at[pl.ds(0, n_dma)], bufs[slot], sem.at[slot]).wait()

    def ring_copies():
        return (pltpu.make_async_copy(idx_rows, idx_ring.at[pl.ds(nxt * _TS, _TS)], aux.at[0]),
                pltpu.make_async_copy(idx_rows, idx_ring.at[pl.ds(mirror, _TS)], aux.at[1]))

    @pl.when(step == 0)
    def _():
        prime = (pltpu.make_async_copy(idx01_ref, idx_ring.at[pl.ds(0, 2 * _TS)], aux.at[0]),
                 pltpu.make_async_copy(idx01_ref.at[pl.ds(0, _TS)], idx_ring.at[pl.ds(_RING * _TS, _TS)], aux.at[1]),
                 pltpu.make_async_copy(gate01_ref, gate_ring.at[pl.ds(0, 2 * _TS)], aux.at[2]))
        for c in prime:
            c.start()
        for c in prime:
            c.wait()
        for slot in range(_NSLOT - 1):
            def body(t, c, slot=slot):
                for r in range(_NROW):
                    row_copy(idx_ring[slot * _TG + t, r], slot, t * _NROW + r).start(priority=r % 2)
                return c
            lax.fori_loop(0, _TG, body, 0)

    @pl.when(step > 0)
    def _():
        for c in ring_copies():
            c.wait()

    gt_f = mod_ref[0, 5]
    gfin = gfin_ref[...]
    width = _TP * LANES
    r_i = lax.broadcasted_iota(i32, (width, width), 0) // LANES
    c_i = lax.broadcasted_iota(i32, (width, width), 1) // LANES
    lane_sum = jnp.where(r_i == c_i, 1.0, 0.0).astype(bf16)
    eye = jnp.where(lax.broadcasted_iota(i32, (_NROW, LANES), 0) == lax.broadcasted_iota(i32, (_NROW, LANES), 1),
                    1.0, 0.0)

    def evaluate(tok0, slot, t0):
        src = bufs[slot]
        z, vs = [], []
        for dt in range(_TP):
            uv = src[pl.ds((t0 + dt) * _NROW, _NROW)].astype(f32)
            vs.append(uv[:, SUBLANES:2 * SUBLANES, :])
            u = uv[:, 0:SUBLANES, :]
            z.append((u * h_ref[tok0 + dt][None]).reshape(_NROW * SUBLANES, LANES).astype(bf16))
        part = _dot(rsum_ref[...], jnp.concatenate(z, axis=1))
        p_hi, p_lo = _split_bf16(part)
        a = _dot(p_hi, lane_sum) + _dot(p_lo, lane_sum)
        g_diag = jnp.concatenate([eye * gate_ring[pl.ds(cur * _TS + tok0 + dt, 1), :] for dt in range(_TP)], axis=1)
        g = _dot(g_diag.astype(bf16), lane_sum)
        w = (g * _gelu(a)).astype(bf16)
        spread = _dot(rrep_ref[...], w)
        for dt in range(_TP):
            tok = tok0 + dt
            wt = spread[:, dt * LANES:(dt + 1) * LANES].reshape(_NROW, SUBLANES, LANES)
            y = jnp.sum(wt * vs[dt], axis=0)
            x2 = x1_ref[tok] + gt_f * y
            ms = jnp.sum(jnp.sum(x2 * x2, axis=1, keepdims=True), axis=0, keepdims=True) * (1.0 / D_MODEL)
            out_tiles[tok] = x2 * lax.rsqrt(ms + EPS) * gfin

    def half_step(half, carry):
        tok_h = half * _HALF
        ring_h = cur * _TS + tok_h
        head0 = half * (PEER_HEADS // 2)
        experts, gates, tops = [], [], {}
        pieces = []
        for hh in range(PEER_HEADS // 2):
            for p in range(2):
                def sub_keys(hh=hh, p=p):
                    tops[hh, p] = _topk_rows(sc_ref[(head0 + hh) * 2 + p], k)
                pieces.append(sub_keys)

            def pairs(hh=hh):
                picked, g = _head_routing(tops[hh, 0], tops[hh, 1])
                experts.extend(picked)
                gates.extend(g)
            pieces.append(pairs)

        def stage():
            rows = pl.ds(pl.multiple_of(head0 * k, _HALF), _HALF)
            idx_stage[rows, :] = jnp.concatenate(experts, axis=0).astype(i32)
            gate_stage[rows, :] = jnp.concatenate(gates, axis=0)
        pieces.append(stage)

        for grp in range(_HALF // _TG):
            slot = grp % _NSLOT
            wait(slot)
            if grp < len(pieces):
                pieces[grp]()
            dst = (slot + _NSLOT - 1) % _NSLOT
            for t0 in range(0, _TG, _TP):
                evaluate(tok_h + grp * _TG + t0, slot, t0)
                for t in range(t0, t0 + _TP):
                    for r in range(_NROW):
                        row_copy(idx_ring[ring_h + grp * _TG + t + _AHEAD, r], dst, t * _NROW + r).start(priority=r % 2)
        return carry

    lax.fori_loop(0, _TS // _HALF, half_step, 0)
    o_ref[...] = out_tiles[...].reshape(_TS, D_MODEL)
    idx_rows[...] = idx_stage[...].T
    gate_ring[pl.ds(pl.multiple_of(nxt * _TS, _TS), _TS), :] = gate_stage[...].T
    for c in ring_copies():
        c.start()

    @pl.when(step == last)
    def _():
        for c in ring_copies():
            c.wait()
        for slot in range(_NSLOT - 1):
            wait(slot)


def _experts(scores, idx01, gate01, h3, x13, mod4, gfin, uv, seq):
    n = h3.shape[0]
    n_steps = n // _TS
    steps_per_seq = seq // _TS
    tok3 = lambda i: (i, 0, 0)
    rsum = jnp.asarray(np.repeat(np.eye(_NROW, dtype=np.float32), SUBLANES, axis=1), dtype=bf16)
    buf = pltpu.VMEM((_TG * _NROW, _ROW_TILES, LANES), bf16)
    return pl.pallas_call(
        _expert_kernel,
        grid=(n_steps,),
        in_specs=[
            pl.BlockSpec((2 * PEER_HEADS, PEER_NKEYS, _TS), lambda i: (0, 0, jnp.minimum(i + 2, n_steps - 1))),
            pl.BlockSpec((_TS, SUBLANES, LANES), tok3),
            pl.BlockSpec((_TS, SUBLANES, LANES), tok3),
            pl.BlockSpec((1, 6, SUBLANES, LANES), lambda i: (i // steps_per_seq, 0, 0, 0)),
            pl.BlockSpec((SUBLANES, LANES), lambda i: (0, 0)),
            pl.BlockSpec(rsum.shape, lambda i: (0, 0)),
            pl.BlockSpec(rsum.shape[::-1], lambda i: (0, 0)),
            pl.BlockSpec(memory_space=pl.ANY),
            pl.BlockSpec(memory_space=pl.ANY),
            pl.BlockSpec(memory_space=pl.ANY),
        ],
        out_specs=pl.BlockSpec((_TS, D_MODEL), lambda i: (i, 0)),
        out_shape=jax.ShapeDtypeStruct((n, D_MODEL), f32),
        scratch_shapes=[buf] * _NSLOT + [
            pltpu.VMEM((_TS, SUBLANES, LANES), f32),
            pltpu.VMEM((_NROW, _TS), i32),
            pltpu.VMEM((_NROW, _TS), f32),
            pltpu.VMEM((_TS, _NROW), i32),
            pltpu.VMEM((_RING * _TS, _NROW), f32),
            pltpu.SMEM((_RING_ROWS, _NROW), i32),
            pltpu.SemaphoreType.DMA((_NSLOT,)),
            pltpu.SemaphoreType.DMA((3,)),
        ],
        compiler_params=_cparams(("arbitrary",)),
        name="experts",
    )(scores, h3, x13, mod4, gfin, rsum, rsum.T, idx01, gate01, uv)


def _rope_tables(seq):
    half = HEAD_DIM // 2
    inv = ROPE_THETA ** (-np.arange(half, dtype=np.float32) / half)
    ang = np.arange(seq, dtype=np.float32)[:, None] * inv[None, :].astype(np.float32)
    cos = np.cos(ang).astype(np.float32)
    sin = np.sin(ang).astype(np.float32)
    cos_t = np.tile(np.concatenate([cos, cos], axis=1), (1, NSA_HEADS))
    sin_t = np.tile(np.concatenate([-sin, sin], axis=1), (1, NSA_HEADS))
    return jnp.asarray(cos_t), jnp.asarray(sin_t)


def _selection_map(n_cmp_pad, n_sel):
    r_sel = SEL_BLOCK // CMP_STRIDE
    r_cmp = CMP_BLOCK // CMP_STRIDE
    i = np.arange(n_cmp_pad)[:, None]
    j = np.arange(n_sel)[None, :]
    d = i - r_sel * j
    cnt = np.minimum(d, r_sel - 1) - np.maximum(d - r_cmp + 1, 0) + 1
    cnt = np.clip(cnt, 0, None).astype(np.float32)
    cnt[n_cmp_pad - r_cmp + 1:] = 0.0
    return jnp.asarray(cnt.T)


def _block_expand(n_sel, seq):
    e = (np.arange(seq)[None, :] // SEL_BLOCK == np.arange(n_sel)[:, None]).astype(np.float32)
    return jnp.asarray(e, dtype=bf16)


def _compress_weights(w1):
    eye = jnp.eye(NSA_KV_GROUPS, dtype=w1.dtype)
    out = []
    for part in range(CMP_BLOCK // CMP_STRIDE):
        w = w1[part * CMP_STRIDE * HEAD_DIM:(part + 1) * CMP_STRIDE * HEAD_DIM].reshape(CMP_STRIDE, HEAD_DIM, CMP_HIDDEN)
        big = jnp.einsum('pdc,gh->pgdhc', w, eye).reshape(CMP_STRIDE * KV_WIDTH, NSA_KV_GROUPS * CMP_HIDDEN)
        out.append(big.astype(bf16))
    return out


def kernel(x, c, w_ada, b_ada, g_mix, g_ffn, w_in, cmp_pos_k, cmp_pos_v, w_ck1, w_ck2, w_cv1, w_cv2, hgrn_lb_logits, hgrn_out_norm, w_branch, w_out, w_peer_q, peer_sub_keys, peer_u, peer_v, g_final):
    bsz, seq, d = x.shape
    n = bsz * seq
    depth = w_ada.shape[0]
    assert depth == 1, "single-layer block only"
    n_sel = seq // SEL_BLOCK
    n_pieces = seq // CMP_STRIDE
    cos_t, sin_t = _rope_tables(seq)
    selmap = _selection_map(n_pieces, n_sel)
    expand = _block_expand(n_sel, seq)
    xcur = x.reshape(n, d)
    for l in range(depth):
        mod = _adaln(c, w_ada[l], b_ada[l].reshape(1, 6 * d))
        mod3 = mod.reshape(bsz, 6, d)
        w_pad = jnp.concatenate(
            [w_in[l][:, :_C_GL + 3 * NSA_HEADS], jnp.zeros((d, GATE_PAD - 3 * NSA_HEADS), w_in.dtype),
             w_in[l][:, _C_GL + 3 * NSA_HEADS:]], axis=1).astype(bf16)
        q, kc, vc, kvsw, gates, hq, hf, hi, hg, mg = _inproj(xcur, mod3, g_mix[l].reshape(1, d), w_pad, cos_t, sin_t, seq)
        wkt, wkb = _compress_weights(w_ck1[l])
        wvt, wvb = _compress_weights(w_cv1[l])
        kc_c, vc_c = _compress(
            kc.reshape(bsz, n_pieces, CMP_STRIDE * KV_WIDTH), vc.reshape(bsz, n_pieces, CMP_STRIDE * KV_WIDTH),
            wkt, wkb, wvt, wvb, cmp_pos_k[l].reshape(1, -1), cmp_pos_v[l].reshape(1, -1),
            w_ck1[l], w_cv1[l], w_ck2[l], w_cv2[l])
        o_nsa = _nsa(q, kvsw.reshape(bsz, seq, 4 * KV_WIDTH), kc_c, vc_c, gates, selmap, expand, seq)
        o_hgrn = _hgrn(hq, hf, hi, hg, hgrn_lb_logits, hgrn_out_norm[l].reshape(1, -1), seq, l)
        x1, h2, h2_b = _merge(o_nsa, o_hgrn, mg, xcur, mod3, g_ffn[l].reshape(1, d),
                              w_branch[l, 0].astype(bf16), w_branch[l, 1].astype(bf16), w_out[l].astype(bf16), seq)
        wq = w_peer_q[l].astype(bf16)
        sk = jnp.transpose(peer_sub_keys[l], (1, 0, 2, 3)).reshape(2 * PEER_HEADS, PEER_NKEYS, PEER_QDIM // 2).astype(bf16)
        scores = _scores(h2_b, wq, sk)
        idx01, gate01 = _route(scores, 2 * _TS)
        uv = jnp.concatenate([peer_u[l], peer_v[l]], axis=1).astype(bf16).reshape(-1, _ROW_TILES, LANES)
        xcur = _experts(scores, idx01, gate01, h2, x1, mod.reshape(bsz, 6, SUBLANES, LANES),
                        g_final.reshape(SUBLANES, LANES), uv, seq)
    return xcur.reshape(bsz, seq, d)
```

```python
import functools

import numpy as np
import jax
import jax.numpy as jnp
from jax import lax
from jax.experimental import pallas as pl
from jax.experimental.pallas import tpu as pltpu

f32 = jnp.float32
bf16 = jnp.bfloat16
i32 = jnp.int32
_HIGHEST = lax.Precision.HIGHEST

D_MODEL = 1024
EPS = 1e-6
HEAD_DIM = 64
ROPE_THETA = 10000.0
NSA_HEADS = 8
NSA_KV_GROUPS = 2
NSA_GROUP = NSA_HEADS // NSA_KV_GROUPS
NSA_WIDTH = NSA_HEADS * HEAD_DIM
KV_WIDTH = NSA_KV_GROUPS * HEAD_DIM
CMP_BLOCK = 32
CMP_STRIDE = 16
CMP_HIDDEN = 128
SEL_BLOCK = 64
SEL_TOPK = 16
WINDOW = 512
FORCE_BONUS = 1000.0
HGRN_HEADS = 4
HGRN_EXPAND = 128
HGRN_HEAD_V = 128
HGRN_WIDTH = HGRN_HEADS * HGRN_EXPAND
HGRN_CHUNK = 32
PEER_HEADS = 8
PEER_NKEYS = 128
PEER_QDIM = 256
PEER_TOPK = 16
GATE_PAD = 128

LANES = 128
SUBLANES = 8
VMEM_LIMIT = 56 * 1024 * 1024

_NEG = -1e30


def _cparams(sem):
    return pltpu.CompilerParams(dimension_semantics=sem, vmem_limit_bytes=VMEM_LIMIT)


def _gelu(x):
    return 0.5 * x * (1.0 + jnp.tanh(0.7978845608028654 * (x + 0.044715 * (x * x * x))))


def _dot(a, b, **kw):
    return jnp.dot(a, b, preferred_element_type=f32, **kw)


def _dot_nt(a, b):
    return lax.dot_general(a, b, (((1,), (1,)), ((), ())), preferred_element_type=f32)


def _rms(x, g):
    return x * lax.rsqrt(jnp.mean(x * x, axis=-1, keepdims=True) + EPS) * g


def _adaln_kernel(c_ref, w_ref, b_ref, o_ref):
    c = c_ref[...]
    cs = c * jax.nn.sigmoid(c)
    o_ref[...] = _dot(cs, w_ref[...], precision=_HIGHEST) + b_ref[...]


def _adaln(c, w, b):
    bsz = c.shape[0]
    tn = 512
    return pl.pallas_call(
        _adaln_kernel,
        grid=(6 * D_MODEL // tn,),
        in_specs=[
            pl.BlockSpec((bsz, D_MODEL), lambda j: (0, 0)),
            pl.BlockSpec((D_MODEL, tn), lambda j: (0, j)),
            pl.BlockSpec((1, tn), lambda j: (0, j)),
        ],
        out_specs=pl.BlockSpec((bsz, tn), lambda j: (0, j)),
        out_shape=jax.ShapeDtypeStruct((bsz, 6 * D_MODEL), f32),
        compiler_params=_cparams(("arbitrary",)),
        name="adaln",
    )(c, w, b)


_C_Q = 0
_C_KC = 512
_C_VC = 640
_C_KS = 768
_C_VS = 896
_C_KW = 1024
_C_VW = 1152
_C_GL = 1280
_C_HQ = _C_GL + GATE_PAD
_C_HF = _C_HQ + 512
_C_HI = _C_HF + 512
_C_HG = _C_HI + 512
_C_MG = _C_HG + 512
_C_END = _C_MG + 2 * D_MODEL


def _inproj_kernel(x_ref, mod_ref, g_ref, w_ref, cos_ref, sin_ref,
                   q_ref, kc_ref, vc_ref, kvsw_ref, gate_ref, hq_ref, hf_ref, hi_ref, hg_ref, mg_ref):
    x = x_ref[...]
    sh = mod_ref[0, 0:1, :]
    sc = mod_ref[0, 1:2, :]
    h = (_rms(x, g_ref[...]) * (1.0 + sc) + sh).astype(bf16)

    def mm(c0, c1):
        return _dot(h, w_ref[:, c0:c1])

    cos = cos_ref[...]
    sin = sin_ref[...]

    def rope(a):
        width = a.shape[1]
        first = (lax.broadcasted_iota(i32, a.shape, 1) & (HEAD_DIM - 1)) < (HEAD_DIM // 2)
        partner = jnp.where(first, pltpu.roll(a, width - HEAD_DIM // 2, 1), pltpu.roll(a, HEAD_DIM // 2, 1))
        return a * cos[:, :width] + partner * sin[:, :width]

    q_ref[...] = (rope(mm(_C_Q, _C_KC)) * (HEAD_DIM ** -0.5)).astype(bf16)
    kc_ref[...] = rope(mm(_C_KC, _C_VC)).astype(bf16)
    vc_ref[...] = mm(_C_VC, _C_KS).astype(bf16)
    kvsw_ref[:, 0:128] = rope(mm(_C_KS, _C_VS)).astype(bf16)
    kvsw_ref[:, 128:256] = mm(_C_VS, _C_KW).astype(bf16)
    kvsw_ref[:, 256:384] = rope(mm(_C_KW, _C_VW)).astype(bf16)
    kvsw_ref[:, 384:512] = mm(_C_VW, _C_GL).astype(bf16)
    gate_ref[...] = jax.nn.sigmoid(mm(_C_GL, _C_HQ))
    hq_ref[...] = mm(_C_HQ, _C_HF).astype(bf16)
    hf_ref[...] = mm(_C_HF, _C_HI)
    hi_ref[...] = mm(_C_HI, _C_HG).astype(bf16)
    hg_ref[...] = mm(_C_HG, _C_MG).astype(bf16)
    mg_ref[...] = jax.nn.sigmoid(mm(_C_MG, _C_END)).astype(bf16)


def _inproj(x2, mod3, g_mix, w_pad, cos_t, sin_t, seq):
    n = x2.shape[0]
    tm = 256
    tiles_per_seq = seq // tm
    row = lambda i: (i, 0)
    outs = [
        (NSA_WIDTH, bf16), (KV_WIDTH, bf16), (KV_WIDTH, bf16), (4 * KV_WIDTH, bf16), (GATE_PAD, f32),
        (HGRN_WIDTH, bf16), (HGRN_WIDTH, f32), (HGRN_WIDTH, bf16), (HGRN_WIDTH, bf16), (2 * D_MODEL, bf16),
    ]
    return pl.pallas_call(
        _inproj_kernel,
        grid=(n // tm,),
        in_specs=[
            pl.BlockSpec((tm, D_MODEL), row),
            pl.BlockSpec((1, 6, D_MODEL), lambda i: (i // tiles_per_seq, 0, 0)),
            pl.BlockSpec((1, D_MODEL), lambda i: (0, 0)),
            pl.BlockSpec((D_MODEL, _C_END), lambda i: (0, 0)),
            pl.BlockSpec((tm, NSA_WIDTH), lambda i: (i % tiles_per_seq, 0)),
            pl.BlockSpec((tm, NSA_WIDTH), lambda i: (i % tiles_per_seq, 0)),
        ],
        out_specs=[pl.BlockSpec((tm, w), row) for w, _ in outs],
        out_shape=[jax.ShapeDtypeStruct((n, w), dt) for w, dt in outs],
        compiler_params=_cparams(("arbitrary",)),
        name="inproj",
    )(x2, mod3, g_mix, w_pad, cos_t, sin_t)


def _compress_kernel(kc_ref, vc_ref, wkt_ref, wkb_ref, wvt_ref, wvb_ref, pk_ref, pv_ref,
                     w1k_ref, w1v_ref, w2k_ref, w2v_ref, okc_ref, ovc_ref):
    def one(x_ref, wt_ref, wb_ref, pos_ref, w1_ref, w2_ref, o_ref):
        pieces = x_ref[0]
        top = _dot(pieces, wt_ref[...])
        bot = _dot(pieces, wb_ref[...])
        nrow = bot.shape[0]
        bot = pltpu.roll(bot, nrow - 1, 0)
        cpos = _dot(pos_ref[...], w1_ref[...], precision=_HIGHEST)
        w2 = w2_ref[...].astype(bf16)
        outs = []
        for g in range(NSA_KV_GROUPS):
            sl = slice(g * CMP_HIDDEN, (g + 1) * CMP_HIDDEN)
            hid = _gelu(top[:, sl] + bot[:, sl] + cpos)
            outs.append(_dot(hid.astype(bf16), w2))
        o_ref[0] = jnp.concatenate(outs, axis=1).astype(bf16)

    one(kc_ref, wkt_ref, wkb_ref, pk_ref, w1k_ref, w2k_ref, okc_ref)
    one(vc_ref, wvt_ref, wvb_ref, pv_ref, w1v_ref, w2v_ref, ovc_ref)


def _compress(kc3, vc3, wkt, wkb, wvt, wvb, pk, pv, w1k, w1v, w2k, w2v):
    bsz, npieces, width = kc3.shape
    full = lambda a: pl.BlockSpec(a.shape, lambda b: (0,) * a.ndim)
    per_b = pl.BlockSpec((1, npieces, width), lambda b: (b, 0, 0))
    out_b = pl.BlockSpec((1, npieces, KV_WIDTH), lambda b: (b, 0, 0))
    return pl.pallas_call(
        _compress_kernel,
        grid=(bsz,),
        in_specs=[per_b, per_b] + [full(a) for a in (wkt, wkb, wvt, wvb, pk, pv, w1k, w1v, w2k, w2v)],
        out_specs=[out_b, out_b],
        out_shape=[jax.ShapeDtypeStruct((bsz, npieces, KV_WIDTH), bf16)] * 2,
        compiler_params=_cparams(("arbitrary",)),
        name="compress",
    )(kc3, vc3, wkt, wkb, wvt, wvb, pk, pv, w1k, w1v, w2k, w2v)


_TQ = 128
_NSA_CLASSES = 4


def _softmax_parts(s, mask):
    sm = jnp.where(mask, s, _NEG)
    m = jnp.max(sm, axis=-1, keepdims=True)
    p = jnp.where(mask, jnp.exp(sm - m), 0.0)
    den = jnp.maximum(jnp.sum(p, axis=-1, keepdims=True), 1e-30)
    return p, den


def _softmax_plain(s):
    p = jnp.exp(s - jnp.max(s, axis=-1, keepdims=True))
    return p, jnp.sum(p, axis=-1, keepdims=True)


def _nsa_kernel(q_ref, kvsw_ref, kc_ref, vc_ref, gate_ref, selmap_ref, expand_ref, o_ref, *, seq):
    tq = _TQ
    qi = pl.program_id(1)
    n_cls = _NSA_CLASSES
    per_cls = (seq // tq) // n_cls
    for cls in range(n_cls):
        @pl.when((qi >= cls * per_cls) & (qi < (cls + 1) * per_cls))
        def _(cls=cls):
            _nsa_tile(q_ref, kvsw_ref, kc_ref, vc_ref, gate_ref, selmap_ref, expand_ref, o_ref,
                      seq=seq, width=(cls + 1) * (seq // n_cls))


def _nsa_tile(q_ref, kvsw_ref, kc_ref, vc_ref, gate_ref, selmap_ref, expand_ref, o_ref, *, seq, width):
    tq = _TQ
    rows = NSA_GROUP * tq
    n_sel = seq // SEL_BLOCK
    n_cmp_pad = seq // CMP_STRIDE
    t0 = pl.program_id(1) * tq
    q = q_ref[...]
    gates = gate_ref[...]
    tlane = t0 + lax.broadcasted_iota(i32, (1, tq), 1)
    t4 = t0 + (lax.broadcasted_iota(i32, (rows, 1), 0) & (tq - 1))
    wk = WINDOW + tq
    ws = pl.multiple_of(jnp.maximum(t0 - WINDOW, 0), tq)
    ks_all = kvsw_ref[0, 0:width, 0:128]
    vs_all = kvsw_ref[0, 0:width, 128:256]
    kw_all = kvsw_ref[0, pl.ds(ws, wk), 256:384]
    vw_all = kvsw_ref[0, pl.ds(ws, wk), 384:512]
    kc_all = kc_ref[0]
    vc_all = vc_ref[0]
    cend = lax.broadcasted_iota(i32, (1, n_cmp_pad), 1) * CMP_STRIDE + (CMP_BLOCK - 1)
    blk = lax.broadcasted_iota(i32, (n_sel, 1), 0)
    kpos = lax.broadcasted_iota(i32, (1, width), 1)
    kpos_w = ws + lax.broadcasted_iota(i32, (1, wk), 1)
    cur = tlane >> 6
    forced = (blk == 0) | (blk == cur) | (blk == cur - 1)
    causal_blk = blk * SEL_BLOCK <= tlane
    pieces = []
    for g in range(NSA_KV_GROUPS):
        gs = slice(g * HEAD_DIM, (g + 1) * HEAD_DIM)
        qg = jnp.concatenate(
            [q[:, (NSA_GROUP * g + r) * HEAD_DIM:(NSA_GROUP * g + r + 1) * HEAD_DIM] for r in range(NSA_GROUP)], axis=0)
        p_c, den_c = _softmax_parts(_dot_nt(qg, kc_all[:, gs]), cend <= t4)
        o_c = _dot(p_c.astype(bf16), vc_all[:, gs]) / den_c
        pn = p_c / den_c
        pc_sum = pn[0:tq]
        for r in range(1, NSA_GROUP):
            pc_sum = pc_sum + pn[r * tq:(r + 1) * tq]
        imp = lax.dot_general(selmap_ref[...], pc_sum, (((1,), (1,)), ((), ())),
                              preferred_element_type=f32, precision=_HIGHEST)
        imp = jnp.where(forced, imp + FORCE_BONUS, imp)
        imp = jnp.where(causal_blk, imp, -1.0)
        rank = jnp.zeros((n_sel, tq), f32)
        for j in range(n_sel):
            row = imp[j:j + 1, :]
            ahead = jnp.where(row > imp, 1.0, jnp.where(row == imp, jnp.where(blk > j, 1.0, 0.0), 0.0))
            rank = rank + ahead
        bias = jnp.where(rank < float(min(SEL_TOPK, n_sel)), jnp.where(causal_blk, 0.0, _NEG), _NEG).astype(bf16)
        bias_q = lax.dot_general(bias, expand_ref[:, 0:width], (((0,), (0,)), ((), ())),
                                 preferred_element_type=f32)
        bias4 = jnp.concatenate([bias_q] * NSA_GROUP, axis=0)
        p_s, den_s = _softmax_plain(jnp.where(kpos <= t4, _dot_nt(qg, ks_all[:, gs]) + bias4, _NEG))
        o_s = _dot(p_s.astype(bf16), vs_all[:, gs]) / den_s
        in_window = (t4 - kpos_w).astype(jnp.uint32) < WINDOW
        p_w, den_w = _softmax_plain(jnp.where(in_window, _dot_nt(qg, kw_all[:, gs]), _NEG))
        o_w = _dot(p_w.astype(bf16), vw_all[:, gs]) / den_w

        def gcol(br):
            return jnp.concatenate(
                [gates[:, (NSA_GROUP * g + r) * 3 + br:(NSA_GROUP * g + r) * 3 + br + 1] for r in range(NSA_GROUP)], axis=0)

        o = gcol(0) * o_c + gcol(1) * o_s + gcol(2) * o_w
        pieces += [o[r * tq:(r + 1) * tq] for r in range(NSA_GROUP)]
    o_ref[...] = jnp.concatenate(pieces, axis=1).astype(bf16)


def _nsa(q, kvsw3, kc3, vc3, gates, selmap, expand, seq):
    n = q.shape[0]
    bsz = n // seq
    tq = _TQ
    nq = seq // tq
    return pl.pallas_call(
        functools.partial(_nsa_kernel, seq=seq),
        grid=(bsz, nq),
        in_specs=[
            pl.BlockSpec((tq, NSA_WIDTH), lambda b, i: (b * nq + i, 0)),
            pl.BlockSpec((1, seq, 4 * KV_WIDTH), lambda b, i: (b, 0, 0)),
            pl.BlockSpec((1,) + kc3.shape[1:], lambda b, i: (b, 0, 0)),
            pl.BlockSpec((1,) + vc3.shape[1:], lambda b, i: (b, 0, 0)),
            pl.BlockSpec((tq, GATE_PAD), lambda b, i: (b * nq + i, 0)),
            pl.BlockSpec(selmap.shape, lambda b, i: (0, 0)),
            pl.BlockSpec(expand.shape, lambda b, i: (0, 0)),
        ],
        out_specs=pl.BlockSpec((tq, NSA_WIDTH), lambda b, i: (b * nq + i, 0)),
        out_shape=jax.ShapeDtypeStruct((n, NSA_WIDTH), bf16),
        compiler_params=_cparams(("arbitrary", "arbitrary")),
        name="nsa",
    )(q, kvsw3, kc3, vc3, gates, selmap, expand)


def _hgrn_kernel(hq_ref, hf_ref, hi_ref, hg_ref, lbl_ref, ng_ref, o_ref, st_ref, *, seq, layer):
    c = HGRN_CHUNK
    logits = lbl_ref[...]
    e = jnp.exp(logits - jnp.max(logits, axis=0, keepdims=True))
    sm = e / jnp.sum(e, axis=0, keepdims=True)
    lb = sm[0:1]
    for l in range(1, layer + 1):
        lb = lb + sm[l:l + 1]
    st_ref[...] = jnp.zeros_like(st_ref)
    r_i = lax.broadcasted_iota(i32, (c, c), 0)
    c_i = lax.broadcasted_iota(i32, (c, c), 1)
    causal = r_i >= c_i
    tri = jnp.where(causal, 1.0, 0.0)
    ng = ng_ref[...]

    def body(ci, carry):
        r0 = pl.multiple_of(ci * c, c)
        hq = hq_ref[pl.ds(r0, c), :].astype(f32)
        hf = hf_ref[pl.ds(r0, c), :]
        v = hi_ref[pl.ds(r0, c), :]
        hg = hg_ref[pl.ds(r0, c), :].astype(f32)
        f = lb + (1.0 - lb) * jax.nn.sigmoid(hf)
        k = 1.0 - f
        qv = hq * jax.nn.sigmoid(hq)
        bcum = _dot(tri, jnp.log(f), precision=_HIGHEST)
        bend = bcum[c - 1:c, :]
        q_dec = (qv * jnp.exp(bcum)).astype(bf16)
        k_inv = (k * jnp.exp(-bcum)).astype(bf16)
        k_end = (k * jnp.exp(bend - bcum)).astype(bf16)
        dec_end = jnp.exp(bend)
        outs = []
        for h in range(HGRN_HEADS):
            sl = slice(h * HGRN_EXPAND, (h + 1) * HGRN_EXPAND)
            a = jnp.where(causal, _dot_nt(q_dec[:, sl], k_inv[:, sl]), 0.0)
            st = st_ref[h]
            o = _dot(a.astype(bf16), v[:, sl]) + _dot_nt(q_dec[:, sl], st.astype(bf16))
            upd = lax.dot_general(v[:, sl], k_end[:, sl], (((0,), (0,)), ((), ())), preferred_element_type=f32)
            st_ref[h] = st * dec_end[:, sl] + upd
            y = _rms(o, ng) * (hg[:, sl] * jax.nn.sigmoid(hg[:, sl]))
            outs.append(y)
        o_ref[pl.ds(r0, c), :] = jnp.concatenate(outs, axis=1).astype(bf16)
        return carry

    lax.fori_loop(0, seq // c, body, 0, unroll=4)


def _hgrn(hq, hf, hi, hg, lb_logits, norm_g, seq, layer):
    n = hq.shape[0]
    bsz = n // seq
    per_b = pl.BlockSpec((seq, HGRN_WIDTH), lambda b: (b, 0))
    return pl.pallas_call(
        functools.partial(_hgrn_kernel, seq=seq, layer=layer),
        grid=(bsz,),
        in_specs=[per_b, per_b, per_b, per_b,
                  pl.BlockSpec(lb_logits.shape, lambda b: (0, 0)),
                  pl.BlockSpec(norm_g.shape, lambda b: (0, 0))],
        out_specs=per_b,
        out_shape=jax.ShapeDtypeStruct((n, HGRN_WIDTH), bf16),
        scratch_shapes=[pltpu.VMEM((HGRN_HEADS, HGRN_HEAD_V, HGRN_EXPAND), f32)],
        compiler_params=_cparams(("arbitrary",)),
        name="hgrn",
    )(hq, hf, hi, hg, lb_logits, norm_g)


def _merge_kernel(on_ref, oh_ref, mg_ref, x_ref, mod_ref, g_ref, wb0_ref, wb1_ref, wo_ref, x1_ref, h2_ref, hb_ref):
    a = _dot(on_ref[...], wb0_ref[...])
    b = _dot(oh_ref[...], wb1_ref[...])
    mg = mg_ref[...].astype(f32)
    y = mg[:, :D_MODEL] * a + mg[:, D_MODEL:] * b
    y2 = _dot(y.astype(bf16), wo_ref[...])
    x1 = x_ref[...] + mod_ref[0, 2:3, :] * y2
    h2 = _rms(x1, g_ref[...]) * (1.0 + mod_ref[0, 4:5, :]) + mod_ref[0, 3:4, :]
    rows = x1.shape[0]
    x1_ref[...] = x1.reshape(rows, SUBLANES, LANES)
    h2_ref[...] = h2.reshape(rows, SUBLANES, LANES)
    hb_ref[...] = h2.astype(bf16)


def _merge(o_nsa, o_hgrn, mg, x2, mod3, g_ffn, wb0, wb1, wo, seq):
    n = x2.shape[0]
    tm = 256
    tiles_per_seq = seq // tm
    row = lambda i: (i, 0)
    full = lambda a: pl.BlockSpec(a.shape, lambda i: (0,) * a.ndim)
    return pl.pallas_call(
        _merge_kernel,
        grid=(n // tm,),
        in_specs=[
            pl.BlockSpec((tm, NSA_WIDTH), row), pl.BlockSpec((tm, HGRN_WIDTH), row),
            pl.BlockSpec((tm, 2 * D_MODEL), row), pl.BlockSpec((tm, D_MODEL), row),
            pl.BlockSpec((1, 6, D_MODEL), lambda i: (i // tiles_per_seq, 0, 0)),
            full(g_ffn), full(wb0), full(wb1), full(wo),
        ],
        out_specs=[pl.BlockSpec((tm, SUBLANES, LANES), lambda i: (i, 0, 0))] * 2 + [pl.BlockSpec((tm, D_MODEL), row)],
        out_shape=[jax.ShapeDtypeStruct((n, SUBLANES, LANES), f32)] * 2 + [jax.ShapeDtypeStruct((n, D_MODEL), bf16)],
        compiler_params=_cparams(("arbitrary",)),
        name="merge",
    )(o_nsa, o_hgrn, mg, x2, mod3, g_ffn, wb0, wb1, wo)


_TR = 256


def _topk_rows(s, k, payload=None):
    n = s.shape[0]
    rowid = lax.broadcasted_iota(i32, s.shape, 0).astype(f32)
    vals, picks = [], []
    for _ in range(k):
        m = jnp.max(s, axis=0, keepdims=True)
        i = jnp.min(jnp.where(s == m, rowid, float(n)), axis=0, keepdims=True)
        hit = rowid == i
        vals.append(m)
        picks.append(i if payload is None else jnp.max(jnp.where(hit, payload, -1.0), axis=0, keepdims=True))
        s = jnp.where(hit, -jnp.inf, s)
    return vals, picks


def _pair_candidates(v1, i1, v2, i2):
    k = len(v1)
    s2 = jnp.concatenate(v2, axis=0)
    e2 = jnp.concatenate(i2, axis=0)
    sub = lax.broadcasted_iota(i32, (SUBLANES, s2.shape[1]), 0)
    comb, cand = [], []
    for a in range(k // 2):
        nb = k // (a + 1)
        rows = -(-nb // SUBLANES) * SUBLANES
        c = v1[a] + s2[0:rows]
        if nb < rows:
            c = jnp.where(sub < nb, c, -jnp.inf)
        comb.append(c)
        cand.append(i1[a] * float(PEER_NKEYS) + e2[0:rows])
    comb.append(jnp.concatenate(v1[k // 2:], axis=0) + s2[0:1])
    cand.append(jnp.concatenate(i1[k // 2:], axis=0) * float(PEER_NKEYS) + e2[0:1])
    return jnp.concatenate(comb, axis=0), jnp.concatenate(cand, axis=0)


def _head_routing(tops1, tops2):
    k = PEER_TOPK
    comb, cand = _pair_candidates(tops1[0], tops1[1], tops2[0], tops2[1])
    tv, picked = _topk_rows(comb, k, payload=cand)
    ex = [jnp.exp(tv[r] - tv[0]) for r in range(k)]
    den = ex[0]
    for r in range(1, k):
        den = den + ex[r]
    return picked, [e / den for e in ex]


def _scores_kernel(h_ref, wq_ref, sk_ref, o_ref):
    q = _dot(h_ref[...], wq_ref[...]).astype(bf16)
    half = PEER_QDIM // 2
    for grp in range(2 * PEER_HEADS):
        o_ref[grp] = _dot_nt(sk_ref[grp], q[:, grp * half:(grp + 1) * half])


def _scores(h2, wq, sk):
    n = h2.shape[0]
    tr = _TR
    return pl.pallas_call(
        _scores_kernel,
        grid=(n // tr,),
        in_specs=[
            pl.BlockSpec((tr, D_MODEL), lambda i: (i, 0)),
            pl.BlockSpec(wq.shape, lambda i: (0, 0)),
            pl.BlockSpec(sk.shape, lambda i: (0, 0, 0)),
        ],
        out_specs=pl.BlockSpec((2 * PEER_HEADS, PEER_NKEYS, tr), lambda i: (0, 0, i)),
        out_shape=jax.ShapeDtypeStruct((2 * PEER_HEADS, PEER_NKEYS, n), f32),
        compiler_params=_cparams(("arbitrary",)),
        name="scores",
    )(h2, wq, sk)


def _route_kernel(s_ref, idx_ref, gate_ref):
    k = PEER_TOPK
    experts, gates = [], []
    for hd in range(PEER_HEADS):
        picked, g = _head_routing(_topk_rows(s_ref[2 * hd], k), _topk_rows(s_ref[2 * hd + 1], k))
        experts += picked
        gates += g
    idx_t = jnp.concatenate(experts, axis=0).astype(i32)
    gate_t = jnp.concatenate(gates, axis=0)
    for blk in range(idx_t.shape[1] // LANES):
        sl = slice(blk * LANES, (blk + 1) * LANES)
        idx_ref[sl, :] = idx_t[:, sl].T
        gate_ref[sl, :] = gate_t[:, sl].T


def _route(scores, n_tok):
    ncol = PEER_HEADS * PEER_TOPK
    return pl.pallas_call(
        _route_kernel,
        grid=(1,),
        in_specs=[pl.BlockSpec((2 * PEER_HEADS, PEER_NKEYS, n_tok), lambda i: (0, 0, 0))],
        out_specs=[pl.BlockSpec((n_tok, ncol), lambda i: (0, 0))] * 2,
        out_shape=[jax.ShapeDtypeStruct((n_tok, ncol), i32), jax.ShapeDtypeStruct((n_tok, ncol), f32)],
        compiler_params=_cparams(("arbitrary",)),
        name="route",
    )(scores)


_TS = 128
_HALF = 64
_TG = 4
_NSLOT = 4
_AHEAD = (_NSLOT - 1) * _TG
_NROW = PEER_HEADS * PEER_TOPK
_ROW_TILES = 2 * D_MODEL // LANES
_TP = 2
_RING = 3
_RING_ROWS = (_RING + 2) * _TS


def _split_bf16(x):
    hi = x.astype(bf16)
    return hi, (x - hi.astype(f32)).astype(bf16)


def _expert_kernel(sc_ref, h_ref, x1_ref, mod_ref, gfin_ref, rsum_ref, rrep_ref, idx01_ref, gate01_ref, uv_ref, o_ref,
                   *scratch):
    step = pl.program_id(0)
    last = pl.num_programs(0) - 1
    bufs = scratch[:_NSLOT]
    out_tiles, idx_stage, gate_stage, idx_rows, gate_ring, idx_ring, sem, aux = scratch[_NSLOT:]
    n_dma = _TG * _NROW
    k = PEER_TOPK
    cur = lax.rem(step, _RING)
    nxt = lax.rem(step + 2, _RING)
    mirror = jnp.where(nxt == 0, _RING * _TS, (_RING + 1) * _TS)

    def row_copy(e, slot, j):
        return pltpu.make_async_copy(uv_ref.at[e], bufs[slot].at[j], sem.at[slot])

    def wait(slot):
        pltpu.make_async_copy(uv_ref.at[pl.ds(0, n_dma)], bufs[slot], sem.at[slot]).wait()

    def ring_copies():
        return (pltpu.make_async_copy(idx_rows, idx_ring.at[pl.ds(nxt * _TS, _TS)], aux.at[0]),
                pltpu.make_async_copy(idx_rows, idx_ring.at[pl.ds(mirror, _TS)], aux.at[1]))

    @pl.when(step == 0)
    def _():
        prime = (pltpu.make_async_copy(idx01_ref, idx_ring.at[pl.ds(0, 2 * _TS)], aux.at[0]),
                 pltpu.make_async_copy(idx01_ref.at[pl.ds(0, _TS)], idx_ring.at[pl.ds(_RING * _TS, _TS)], aux.at[1]),
                 pltpu.make_async_copy(gate01_ref, gate_ring.at[pl.ds(0, 2 * _TS)], aux.at[2]))
        for c in prime:
            c.start()
        for c in prime:
            c.wait()
        for slot in range(_NSLOT - 1):
            def body(t, c, slot=slot):
                for r in range(_NROW):
                    row_copy(idx_ring[slot * _TG + t, r], slot, t * _NROW + r).start(priority=r % 2)
                return c
            lax.fori_loop(0, _TG, body, 0)

    gt_f = mod_ref[0, 5]
    gfin = gfin_ref[...]
    width = _TP * LANES
    r_i = lax.broadcasted_iota(i32, (width, width), 0) // LANES
    c_i = lax.broadcasted_iota(i32, (width, width), 1) // LANES
    lane_sum = jnp.where(r_i == c_i, 1.0, 0.0).astype(bf16)
    eye = jnp.where(lax.broadcasted_iota(i32, (_NROW, LANES), 0) == lax.broadcasted_iota(i32, (_NROW, LANES), 1),
                    1.0, 0.0)

    def evaluate(tok0, slot, t0):
        src = bufs[slot]
        z, vs = [], []
        for dt in range(_TP):
            uv = src[pl.ds((t0 + dt) * _NROW, _NROW)].astype(f32)
            vs.append(uv[:, SUBLANES:2 * SUBLANES, :])
            u = uv[:, 0:SUBLANES, :]
            z.append((u * h_ref[tok0 + dt][None]).reshape(_NROW * SUBLANES, LANES).astype(bf16))
        part = _dot(rsum_ref[...], jnp.concatenate(z, axis=1))
        p_hi, p_lo = _split_bf16(part)
        a = _dot(p_hi, lane_sum) + _dot(p_lo, lane_sum)
        g_diag = jnp.concatenate([eye * gate_ring[pl.ds(cur * _TS + tok0 + dt, 1), :] for dt in range(_TP)], axis=1)
        g = _dot(g_diag.astype(bf16), lane_sum)
        w = (g * _gelu(a)).astype(bf16)
        spread = _dot(rrep_ref[...], w)
        for dt in range(_TP):
            tok = tok0 + dt
            wt = spread[:, dt * LANES:(dt + 1) * LANES].reshape(_NROW, SUBLANES, LANES)
            y = jnp.sum(wt * vs[dt], axis=0)
            x2 = x1_ref[tok] + gt_f * y
            ms = jnp.sum(jnp.sum(x2 * x2, axis=1, keepdims=True), axis=0, keepdims=True) * (1.0 / D_MODEL)
            out_tiles[tok] = x2 * lax.rsqrt(ms + EPS) * gfin

    def half_step(half, carry):
        @pl.when((half == _TS // _HALF - 1) & (step > 0))
        def _():
            for c in ring_copies():
                c.wait()

        tok_h = half * _HALF
        ring_h = cur * _TS + tok_h
        head0 = half * (PEER_HEADS // 2)
        experts, gates, tops = [], [], {}
        pieces = []
        for hh in range(PEER_HEADS // 2):
            for p in range(2):
                def sub_keys(hh=hh, p=p):
                    tops[hh, p] = _topk_rows(sc_ref[(head0 + hh) * 2 + p], k)
                pieces.append(sub_keys)

            def pairs(hh=hh):
                picked, g = _head_routing(tops[hh, 0], tops[hh, 1])
                experts.extend(picked)
                gates.extend(g)
            pieces.append(pairs)

        def stage():
            rows = pl.ds(pl.multiple_of(head0 * k, _HALF), _HALF)
            idx_stage[rows, :] = jnp.concatenate(experts, axis=0).astype(i32)
            gate_stage[rows, :] = jnp.concatenate(gates, axis=0)
        pieces.append(stage)

        for grp in range(_HALF // _TG):
            slot = grp % _NSLOT
            wait(slot)
            if grp < len(pieces):
                pieces[grp]()
            dst = (slot + _NSLOT - 1) % _NSLOT
            for t0 in range(0, _TG, _TP):
                evaluate(tok_h + grp * _TG + t0, slot, t0)
                for t in range(t0, t0 + _TP):
                    for r in range(_NROW):
                        row_copy(idx_ring[ring_h + grp * _TG + t + _AHEAD, r], dst, t * _NROW + r).start(priority=r % 2)
        return carry

    lax.fori_loop(0, _TS // _HALF, half_step, 0)
    o_ref[...] = out_tiles[...].reshape(_TS, D_MODEL)
    idx_rows[...] = idx_stage[...].T
    gate_ring[pl.ds(pl.multiple_of(nxt * _TS, _TS), _TS), :] = gate_stage[...].T
    for c in ring_copies():
        c.start()

    @pl.when(step == last)
    def _():
        for c in ring_copies():
            c.wait()
        for slot in range(_NSLOT - 1):
            wait(slot)


def _experts(scores, idx01, gate01, h3, x13, mod4, gfin, uv, seq):
    n = h3.shape[0]
    n_steps = n // _TS
    steps_per_seq = seq // _TS
    tok3 = lambda i: (i, 0, 0)
    rsum = jnp.asarray(np.repeat(np.eye(_NROW, dtype=np.float32), SUBLANES, axis=1), dtype=bf16)
    buf = pltpu.VMEM((_TG * _NROW, _ROW_TILES, LANES), bf16)
    return pl.pallas_call(
        _expert_kernel,
        grid=(n_steps,),
        in_specs=[
            pl.BlockSpec((2 * PEER_HEADS, PEER_NKEYS, _TS), lambda i: (0, 0, jnp.minimum(i + 2, n_steps - 1))),
            pl.BlockSpec((_TS, SUBLANES, LANES), tok3),
            pl.BlockSpec((_TS, SUBLANES, LANES), tok3),
            pl.BlockSpec((1, 6, SUBLANES, LANES), lambda i: (i // steps_per_seq, 0, 0, 0)),
            pl.BlockSpec((SUBLANES, LANES), lambda i: (0, 0)),
            pl.BlockSpec(rsum.shape, lambda i: (0, 0)),
            pl.BlockSpec(rsum.shape[::-1], lambda i: (0, 0)),
            pl.BlockSpec(memory_space=pl.ANY),
            pl.BlockSpec(memory_space=pl.ANY),
            pl.BlockSpec(memory_space=pl.ANY),
        ],
        out_specs=pl.BlockSpec((_TS, D_MODEL), lambda i: (i, 0)),
        out_shape=jax.ShapeDtypeStruct((n, D_MODEL), f32),
        scratch_shapes=[buf] * _NSLOT + [
            pltpu.VMEM((_TS, SUBLANES, LANES), f32),
            pltpu.VMEM((_NROW, _TS), i32),
            pltpu.VMEM((_NROW, _TS), f32),
            pltpu.VMEM((_TS, _NROW), i32),
            pltpu.VMEM((_RING * _TS, _NROW), f32),
            pltpu.SMEM((_RING_ROWS, _NROW), i32),
            pltpu.SemaphoreType.DMA((_NSLOT,)),
            pltpu.SemaphoreType.DMA((3,)),
        ],
        compiler_params=_cparams(("arbitrary",)),
        name="experts",
    )(scores, h3, x13, mod4, gfin, rsum, rsum.T, idx01, gate01, uv)


def _rope_tables(seq):
    half = HEAD_DIM // 2
    inv = ROPE_THETA ** (-np.arange(half, dtype=np.float32) / half)
    ang = np.arange(seq, dtype=np.float32)[:, None] * inv[None, :].astype(np.float32)
    cos = np.cos(ang).astype(np.float32)
    sin = np.sin(ang).astype(np.float32)
    cos_t = np.tile(np.concatenate([cos, cos], axis=1), (1, NSA_HEADS))
    sin_t = np.tile(np.concatenate([-sin, sin], axis=1), (1, NSA_HEADS))
    return jnp.asarray(cos_t), jnp.asarray(sin_t)


def _selection_map(n_cmp_pad, n_sel):
    r_sel = SEL_BLOCK // CMP_STRIDE
    r_cmp = CMP_BLOCK // CMP_STRIDE
    i = np.arange(n_cmp_pad)[:, None]
    j = np.arange(n_sel)[None, :]
    d = i - r_sel * j
    cnt = np.minimum(d, r_sel - 1) - np.maximum(d - r_cmp + 1, 0) + 1
    cnt = np.clip(cnt, 0, None).astype(np.float32)
    cnt[n_cmp_pad - r_cmp + 1:] = 0.0
    return jnp.asarray(cnt.T)


def _block_expand(n_sel, seq):
    e = (np.arange(seq)[None, :] // SEL_BLOCK == np.arange(n_sel)[:, None]).astype(np.float32)
    return jnp.asarray(e, dtype=bf16)


def _compress_weights(w1):
    eye = jnp.eye(NSA_KV_GROUPS, dtype=w1.dtype)
    out = []
    for part in range(CMP_BLOCK // CMP_STRIDE):
        w = w1[part * CMP_STRIDE * HEAD_DIM:(part + 1) * CMP_STRIDE * HEAD_DIM].reshape(CMP_STRIDE, HEAD_DIM, CMP_HIDDEN)
        big = jnp.einsum('pdc,gh->pgdhc', w, eye).reshape(CMP_STRIDE * KV_WIDTH, NSA_KV_GROUPS * CMP_HIDDEN)
        out.append(big.astype(bf16))
    return out


def kernel(x, c, w_ada, b_ada, g_mix, g_ffn, w_in, cmp_pos_k, cmp_pos_v, w_ck1, w_ck2, w_cv1, w_cv2, hgrn_lb_logits, hgrn_out_norm, w_branch, w_out, w_peer_q, peer_sub_keys, peer_u, peer_v, g_final):
    bsz, seq, d = x.shape
    n = bsz * seq
    depth = w_ada.shape[0]
    assert depth == 1, "single-layer block only"
    n_sel = seq // SEL_BLOCK
    n_pieces = seq // CMP_STRIDE
    cos_t, sin_t = _rope_tables(seq)
    selmap = _selection_map(n_pieces, n_sel)
    expand = _block_expand(n_sel, seq)
    xcur = x.reshape(n, d)
    for l in range(depth):
        mod = _adaln(c, w_ada[l], b_ada[l].reshape(1, 6 * d))
        mod3 = mod.reshape(bsz, 6, d)
        w_pad = jnp.concatenate(
            [w_in[l][:, :_C_GL + 3 * NSA_HEADS], jnp.zeros((d, GATE_PAD - 3 * NSA_HEADS), w_in.dtype),
             w_in[l][:, _C_GL + 3 * NSA_HEADS:]], axis=1).astype(bf16)
        q, kc, vc, kvsw, gates, hq, hf, hi, hg, mg = _inproj(xcur, mod3, g_mix[l].reshape(1, d), w_pad, cos_t, sin_t, seq)
        wkt, wkb = _compress_weights(w_ck1[l])
        wvt, wvb = _compress_weights(w_cv1[l])
        kc_c, vc_c = _compress(
            kc.reshape(bsz, n_pieces, CMP_STRIDE * KV_WIDTH), vc.reshape(bsz, n_pieces, CMP_STRIDE * KV_WIDTH),
            wkt, wkb, wvt, wvb, cmp_pos_k[l].reshape(1, -1), cmp_pos_v[l].reshape(1, -1),
            w_ck1[l], w_cv1[l], w_ck2[l], w_cv2[l])
        o_nsa = _nsa(q, kvsw.reshape(bsz, seq, 4 * KV_WIDTH), kc_c, vc_c, gates, selmap, expand, seq)
        o_hgrn = _hgrn(hq, hf, hi, hg, hgrn_lb_logits, hgrn_out_norm[l].reshape(1, -1), seq, l)
        x1, h2, h2_b = _merge(o_nsa, o_hgrn, mg, xcur, mod3, g_ffn[l].reshape(1, d),
                              w_branch[l, 0].astype(bf16), w_branch[l, 1].astype(bf16), w_out[l].astype(bf16), seq)
        wq = w_peer_q[l].astype(bf16)
        sk = jnp.transpose(peer_sub_keys[l], (1, 0, 2, 3)).reshape(2 * PEER_HEADS, PEER_NKEYS, PEER_QDIM // 2).astype(bf16)
        scores = _scores(h2_b, wq, sk)
        idx01, gate01 = _route(scores, 2 * _TS)
        uv = jnp.concatenate([peer_u[l], peer_v[l]], axis=1).astype(bf16).reshape(-1, _ROW_TILES, LANES)
        xcur = _experts(scores, idx01, gate01, h2, x1, mod.reshape(bsz, 6, SUBLANES, LANES),
                        g_final.reshape(SUBLANES, LANES), uv, seq)
    return xcur.reshape(bsz, seq, d)
```

```python
import functools

import numpy as np
import jax
import jax.numpy as jnp
from jax import lax
from jax.experimental import pallas as pl
from jax.experimental.pallas import tpu as pltpu

f32 = jnp.float32
bf16 = jnp.bfloat16
i32 = jnp.int32
_HIGHEST = lax.Precision.HIGHEST

D_MODEL = 1024
EPS = 1e-6
HEAD_DIM = 64
ROPE_THETA = 10000.0
NSA_HEADS = 8
NSA_KV_GROUPS = 2
NSA_GROUP = NSA_HEADS // NSA_KV_GROUPS
NSA_WIDTH = NSA_HEADS * HEAD_DIM
KV_WIDTH = NSA_KV_GROUPS * HEAD_DIM
CMP_BLOCK = 32
CMP_STRIDE = 16
CMP_HIDDEN = 128
SEL_BLOCK = 64
SEL_TOPK = 16
WINDOW = 512
FORCE_BONUS = 1000.0
HGRN_HEADS = 4
HGRN_EXPAND = 128
HGRN_HEAD_V = 128
HGRN_WIDTH = HGRN_HEADS * HGRN_EXPAND
HGRN_CHUNK = 32
PEER_HEADS = 8
PEER_NKEYS = 128
PEER_QDIM = 256
PEER_TOPK = 16
GATE_PAD = 128

LANES = 128
SUBLANES = 8
VMEM_LIMIT = 56 * 1024 * 1024

_NEG = -1e30


def _cparams(sem):
    return pltpu.CompilerParams(dimension_semantics=sem, vmem_limit_bytes=VMEM_LIMIT)


def _gelu(x):
    return 0.5 * x * (1.0 + jnp.tanh(0.7978845608028654 * (x + 0.044715 * (x * x * x))))


def _dot(a, b, **kw):
    return jnp.dot(a, b, preferred_element_type=f32, **kw)


def _dot_nt(a, b):
    return lax.dot_general(a, b, (((1,), (1,)), ((), ())), preferred_element_type=f32)


def _rms(x, g):
    return x * lax.rsqrt(jnp.mean(x * x, axis=-1, keepdims=True) + EPS) * g


def _adaln_kernel(c_ref, w_ref, b_ref, o_ref):
    c = c_ref[...]
    cs = c * jax.nn.sigmoid(c)
    o_ref[...] = _dot(cs, w_ref[...], precision=_HIGHEST) + b_ref[...]


def _adaln(c, w, b):
    bsz = c.shape[0]
    tn = 512
    return pl.pallas_call(
        _adaln_kernel,
        grid=(6 * D_MODEL // tn,),
        in_specs=[
            pl.BlockSpec((bsz, D_MODEL), lambda j: (0, 0)),
            pl.BlockSpec((D_MODEL, tn), lambda j: (0, j)),
            pl.BlockSpec((1, tn), lambda j: (0, j)),
        ],
        out_specs=pl.BlockSpec((bsz, tn), lambda j: (0, j)),
        out_shape=jax.ShapeDtypeStruct((bsz, 6 * D_MODEL), f32),
        compiler_params=_cparams(("arbitrary",)),
        name="adaln",
    )(c, w, b)


_C_Q = 0
_C_KC = 512
_C_VC = 640
_C_KS = 768
_C_VS = 896
_C_KW = 1024
_C_VW = 1152
_C_GL = 1280
_C_HQ = _C_GL + GATE_PAD
_C_HF = _C_HQ + 512
_C_HI = _C_HF + 512
_C_HG = _C_HI + 512
_C_MG = _C_HG + 512
_C_END = _C_MG + 2 * D_MODEL


def _inproj_kernel(x_ref, mod_ref, g_ref, w_ref, cos_ref, sin_ref,
                   q_ref, kc_ref, vc_ref, kvsw_ref, gate_ref, hq_ref, hf_ref, hi_ref, hg_ref, mg_ref):
    x = x_ref[...]
    sh = mod_ref[0, 0:1, :]
    sc = mod_ref[0, 1:2, :]
    h = (_rms(x, g_ref[...]) * (1.0 + sc) + sh).astype(bf16)

    def mm(c0, c1):
        return _dot(h, w_ref[:, c0:c1])

    cos = cos_ref[...]
    sin = sin_ref[...]

    def rope(a):
        width = a.shape[1]
        first = (lax.broadcasted_iota(i32, a.shape, 1) & (HEAD_DIM - 1)) < (HEAD_DIM // 2)
        partner = jnp.where(first, pltpu.roll(a, width - HEAD_DIM // 2, 1), pltpu.roll(a, HEAD_DIM // 2, 1))
        return a * cos[:, :width] + partner * sin[:, :width]

    q_ref[...] = (rope(mm(_C_Q, _C_KC)) * (HEAD_DIM ** -0.5)).astype(bf16)
    kc_ref[...] = rope(mm(_C_KC, _C_VC)).astype(bf16)
    vc_ref[...] = mm(_C_VC, _C_KS).astype(bf16)
    kvsw_ref[:, 0:128] = rope(mm(_C_KS, _C_VS)).astype(bf16)
    kvsw_ref[:, 128:256] = mm(_C_VS, _C_KW).astype(bf16)
    kvsw_ref[:, 256:384] = rope(mm(_C_KW, _C_VW)).astype(bf16)
    kvsw_ref[:, 384:512] = mm(_C_VW, _C_GL).astype(bf16)
    gate_ref[...] = jax.nn.sigmoid(mm(_C_GL, _C_HQ))
    hq_ref[...] = mm(_C_HQ, _C_HF).astype(bf16)
    hf_ref[...] = mm(_C_HF, _C_HI)
    hi_ref[...] = mm(_C_HI, _C_HG).astype(bf16)
    hg_ref[...] = mm(_C_HG, _C_MG).astype(bf16)
    mg_ref[...] = jax.nn.sigmoid(mm(_C_MG, _C_END)).astype(bf16)


def _inproj(x2, mod3, g_mix, w_pad, cos_t, sin_t, seq):
    n = x2.shape[0]
    tm = 256
    tiles_per_seq = seq // tm
    row = lambda i: (i, 0)
    outs = [
        (NSA_WIDTH, bf16), (KV_WIDTH, bf16), (KV_WIDTH, bf16), (4 * KV_WIDTH, bf16), (GATE_PAD, f32),
        (HGRN_WIDTH, bf16), (HGRN_WIDTH, f32), (HGRN_WIDTH, bf16), (HGRN_WIDTH, bf16), (2 * D_MODEL, bf16),
    ]
    return pl.pallas_call(
        _inproj_kernel,
        grid=(n // tm,),
        in_specs=[
            pl.BlockSpec((tm, D_MODEL), row),
            pl.BlockSpec((1, 6, D_MODEL), lambda i: (i // tiles_per_seq, 0, 0)),
            pl.BlockSpec((1, D_MODEL), lambda i: (0, 0)),
            pl.BlockSpec((D_MODEL, _C_END), lambda i: (0, 0)),
            pl.BlockSpec((tm, NSA_WIDTH), lambda i: (i % tiles_per_seq, 0)),
            pl.BlockSpec((tm, NSA_WIDTH), lambda i: (i % tiles_per_seq, 0)),
        ],
        out_specs=[pl.BlockSpec((tm, w), row) for w, _ in outs],
        out_shape=[jax.ShapeDtypeStruct((n, w), dt) for w, dt in outs],
        compiler_params=_cparams(("arbitrary",)),
        name="inproj",
    )(x2, mod3, g_mix, w_pad, cos_t, sin_t)


def _compress_kernel(kc_ref, vc_ref, wkt_ref, wkb_ref, wvt_ref, wvb_ref, pk_ref, pv_ref,
                     w1k_ref, w1v_ref, w2k_ref, w2v_ref, okc_ref, ovc_ref):
    def one(x_ref, wt_ref, wb_ref, pos_ref, w1_ref, w2_ref, o_ref):
        pieces = x_ref[0]
        top = _dot(pieces, wt_ref[...])
        bot = _dot(pieces, wb_ref[...])
        nrow = bot.shape[0]
        bot = pltpu.roll(bot, nrow - 1, 0)
        cpos = _dot(pos_ref[...], w1_ref[...], precision=_HIGHEST)
        w2 = w2_ref[...].astype(bf16)
        outs = []
        for g in range(NSA_KV_GROUPS):
            sl = slice(g * CMP_HIDDEN, (g + 1) * CMP_HIDDEN)
            hid = _gelu(top[:, sl] + bot[:, sl] + cpos)
            outs.append(_dot(hid.astype(bf16), w2))
        o_ref[0] = jnp.concatenate(outs, axis=1).astype(bf16)

    one(kc_ref, wkt_ref, wkb_ref, pk_ref, w1k_ref, w2k_ref, okc_ref)
    one(vc_ref, wvt_ref, wvb_ref, pv_ref, w1v_ref, w2v_ref, ovc_ref)


def _compress(kc3, vc3, wkt, wkb, wvt, wvb, pk, pv, w1k, w1v, w2k, w2v):
    bsz, npieces, width = kc3.shape
    full = lambda a: pl.BlockSpec(a.shape, lambda b: (0,) * a.ndim)
    per_b = pl.BlockSpec((1, npieces, width), lambda b: (b, 0, 0))
    out_b = pl.BlockSpec((1, npieces, KV_WIDTH), lambda b: (b, 0, 0))
    return pl.pallas_call(
        _compress_kernel,
        grid=(bsz,),
        in_specs=[per_b, per_b] + [full(a) for a in (wkt, wkb, wvt, wvb, pk, pv, w1k, w1v, w2k, w2v)],
        out_specs=[out_b, out_b],
        out_shape=[jax.ShapeDtypeStruct((bsz, npieces, KV_WIDTH), bf16)] * 2,
        compiler_params=_cparams(("arbitrary",)),
        name="compress",
    )(kc3, vc3, wkt, wkb, wvt, wvb, pk, pv, w1k, w1v, w2k, w2v)


_TQ = 256
_NSA_CLASSES = 4


def _softmax_parts(s, mask):
    sm = jnp.where(mask, s, _NEG)
    m = jnp.max(sm, axis=-1, keepdims=True)
    p = jnp.where(mask, jnp.exp(sm - m), 0.0)
    den = jnp.maximum(jnp.sum(p, axis=-1, keepdims=True), 1e-30)
    return p, den


def _softmax_plain(s):
    p = jnp.exp(s - jnp.max(s, axis=-1, keepdims=True))
    return p, jnp.sum(p, axis=-1, keepdims=True)


def _nsa_kernel(q_ref, kvsw_ref, kc_ref, vc_ref, gate_ref, selmap_ref, expand_ref, o_ref, *, seq):
    tq = _TQ
    qi = pl.program_id(1)
    n_cls = _NSA_CLASSES
    per_cls = (seq // tq) // n_cls
    for cls in range(n_cls):
        @pl.when((qi >= cls * per_cls) & (qi < (cls + 1) * per_cls))
        def _(cls=cls):
            _nsa_tile(q_ref, kvsw_ref, kc_ref, vc_ref, gate_ref, selmap_ref, expand_ref, o_ref,
                      seq=seq, width=(cls + 1) * (seq // n_cls))


def _nsa_tile(q_ref, kvsw_ref, kc_ref, vc_ref, gate_ref, selmap_ref, expand_ref, o_ref, *, seq, width):
    tq = _TQ
    rows = NSA_GROUP * tq
    n_sel = seq // SEL_BLOCK
    n_cmp_pad = seq // CMP_STRIDE
    t0 = pl.program_id(1) * tq
    q = q_ref[...]
    gates = gate_ref[...]
    tlane = t0 + lax.broadcasted_iota(i32, (1, tq), 1)
    t4 = t0 + (lax.broadcasted_iota(i32, (rows, 1), 0) & (tq - 1))
    wk = WINDOW + tq
    ws = pl.multiple_of(jnp.maximum(t0 - WINDOW, 0), tq)
    ks_all = kvsw_ref[0, 0:width, 0:128]
    vs_all = kvsw_ref[0, 0:width, 128:256]
    kw_all = kvsw_ref[0, pl.ds(ws, wk), 256:384]
    vw_all = kvsw_ref[0, pl.ds(ws, wk), 384:512]
    kc_all = kc_ref[0]
    vc_all = vc_ref[0]
    cend = lax.broadcasted_iota(i32, (1, n_cmp_pad), 1) * CMP_STRIDE + (CMP_BLOCK - 1)
    blk = lax.broadcasted_iota(i32, (n_sel, 1), 0)
    kpos = lax.broadcasted_iota(i32, (1, width), 1)
    kpos_w = ws + lax.broadcasted_iota(i32, (1, wk), 1)
    cur = tlane >> 6
    forced = (blk == 0) | (blk == cur) | (blk == cur - 1)
    causal_blk = blk * SEL_BLOCK <= tlane
    pieces = []
    for g in range(NSA_KV_GROUPS):
        gs = slice(g * HEAD_DIM, (g + 1) * HEAD_DIM)
        qg = jnp.concatenate(
            [q[:, (NSA_GROUP * g + r) * HEAD_DIM:(NSA_GROUP * g + r + 1) * HEAD_DIM] for r in range(NSA_GROUP)], axis=0)
        p_c, den_c = _softmax_parts(_dot_nt(qg, kc_all[:, gs]), cend <= t4)
        o_c = _dot(p_c.astype(bf16), vc_all[:, gs]) / den_c
        pn = p_c / den_c
        pc_sum = pn[0:tq]
        for r in range(1, NSA_GROUP):
            pc_sum = pc_sum + pn[r * tq:(r + 1) * tq]
        imp = lax.dot_general(selmap_ref[...], pc_sum, (((1,), (1,)), ((), ())),
                              preferred_element_type=f32, precision=_HIGHEST)
        imp = jnp.where(forced, imp + FORCE_BONUS, imp)
        imp = jnp.where(causal_blk, imp, -1.0)
        rank = jnp.zeros((n_sel, tq), f32)
        for j in range(n_sel):
            row = imp[j:j + 1, :]
            ahead = jnp.where(row > imp, 1.0, jnp.where(row == imp, jnp.where(blk > j, 1.0, 0.0), 0.0))
            rank = rank + ahead
        bias = jnp.where(rank < float(min(SEL_TOPK, n_sel)), jnp.where(causal_blk, 0.0, _NEG), _NEG).astype(bf16)
        bias_q = lax.dot_general(bias, expand_ref[:, 0:width], (((0,), (0,)), ((), ())),
                                 preferred_element_type=f32)
        bias4 = jnp.concatenate([bias_q] * NSA_GROUP, axis=0)
        p_s, den_s = _softmax_plain(jnp.where(kpos <= t4, _dot_nt(qg, ks_all[:, gs]) + bias4, _NEG))
        o_s = _dot(p_s.astype(bf16), vs_all[:, gs]) / den_s
        in_window = (t4 - kpos_w).astype(jnp.uint32) < WINDOW
        p_w, den_w = _softmax_plain(jnp.where(in_window, _dot_nt(qg, kw_all[:, gs]), _NEG))
        o_w = _dot(p_w.astype(bf16), vw_all[:, gs]) / den_w

        def gcol(br):
            return jnp.concatenate(
                [gates[:, (NSA_GROUP * g + r) * 3 + br:(NSA_GROUP * g + r) * 3 + br + 1] for r in range(NSA_GROUP)], axis=0)

        o = gcol(0) * o_c + gcol(1) * o_s + gcol(2) * o_w
        pieces += [o[r * tq:(r + 1) * tq] for r in range(NSA_GROUP)]
    o_ref[...] = jnp.concatenate(pieces, axis=1).astype(bf16)


def _nsa(q, kvsw3, kc3, vc3, gates, selmap, expand, seq):
    n = q.shape[0]
    bsz = n // seq
    tq = _TQ
    nq = seq // tq
    return pl.pallas_call(
        functools.partial(_nsa_kernel, seq=seq),
        grid=(bsz, nq),
        in_specs=[
            pl.BlockSpec((tq, NSA_WIDTH), lambda b, i: (b * nq + i, 0)),
            pl.BlockSpec((1, seq, 4 * KV_WIDTH), lambda b, i: (b, 0, 0)),
            pl.BlockSpec((1,) + kc3.shape[1:], lambda b, i: (b, 0, 0)),
            pl.BlockSpec((1,) + vc3.shape[1:], lambda b, i: (b, 0, 0)),
            pl.BlockSpec((tq, GATE_PAD), lambda b, i: (b * nq + i, 0)),
            pl.BlockSpec(selmap.shape, lambda b, i: (0, 0)),
            pl.BlockSpec(expand.shape, lambda b, i: (0, 0)),
        ],
        out_specs=pl.BlockSpec((tq, NSA_WIDTH), lambda b, i: (b * nq + i, 0)),
        out_shape=jax.ShapeDtypeStruct((n, NSA_WIDTH), bf16),
        compiler_params=_cparams(("arbitrary", "arbitrary")),
        name="nsa",
    )(q, kvsw3, kc3, vc3, gates, selmap, expand)


def _hgrn_kernel(hq_ref, hf_ref, hi_ref, hg_ref, lbl_ref, ng_ref, o_ref, st_ref, *, seq, layer):
    c = HGRN_CHUNK
    logits = lbl_ref[...]
    e = jnp.exp(logits - jnp.max(logits, axis=0, keepdims=True))
    sm = e / jnp.sum(e, axis=0, keepdims=True)
    lb = sm[0:1]
    for l in range(1, layer + 1):
        lb = lb + sm[l:l + 1]
    st_ref[...] = jnp.zeros_like(st_ref)
    r_i = lax.broadcasted_iota(i32, (c, c), 0)
    c_i = lax.broadcasted_iota(i32, (c, c), 1)
    causal = r_i >= c_i
    tri = jnp.where(causal, 1.0, 0.0)
    ng = ng_ref[...]

    def body(ci, carry):
        r0 = pl.multiple_of(ci * c, c)
        hq = hq_ref[pl.ds(r0, c), :].astype(f32)
        hf = hf_ref[pl.ds(r0, c), :]
        v = hi_ref[pl.ds(r0, c), :]
        hg = hg_ref[pl.ds(r0, c), :].astype(f32)
        f = lb + (1.0 - lb) * jax.nn.sigmoid(hf)
        k = 1.0 - f
        qv = hq * jax.nn.sigmoid(hq)
        bcum = _dot(tri, jnp.log(f), precision=_HIGHEST)
        bend = bcum[c - 1:c, :]
        q_dec = (qv * jnp.exp(bcum)).astype(bf16)
        k_inv = (k * jnp.exp(-bcum)).astype(bf16)
        k_end = (k * jnp.exp(bend - bcum)).astype(bf16)
        dec_end = jnp.exp(bend)
        outs = []
        for h in range(HGRN_HEADS):
            sl = slice(h * HGRN_EXPAND, (h + 1) * HGRN_EXPAND)
            a = jnp.where(causal, _dot_nt(q_dec[:, sl], k_inv[:, sl]), 0.0)
            st = st_ref[h]
            o = _dot(a.astype(bf16), v[:, sl]) + _dot_nt(q_dec[:, sl], st.astype(bf16))
            upd = lax.dot_general(v[:, sl], k_end[:, sl], (((0,), (0,)), ((), ())), preferred_element_type=f32)
            st_ref[h] = st * dec_end[:, sl] + upd
            y = _rms(o, ng) * (hg[:, sl] * jax.nn.sigmoid(hg[:, sl]))
            outs.append(y)
        o_ref[pl.ds(r0, c), :] = jnp.concatenate(outs, axis=1).astype(bf16)
        return carry

    lax.fori_loop(0, seq // c, body, 0, unroll=4)


def _hgrn(hq, hf, hi, hg, lb_logits, norm_g, seq, layer):
    n = hq.shape[0]
    bsz = n // seq
    per_b = pl.BlockSpec((seq, HGRN_WIDTH), lambda b: (b, 0))
    return pl.pallas_call(
        functools.partial(_hgrn_kernel, seq=seq, layer=layer),
        grid=(bsz,),
        in_specs=[per_b, per_b, per_b, per_b,
                  pl.BlockSpec(lb_logits.shape, lambda b: (0, 0)),
                  pl.BlockSpec(norm_g.shape, lambda b: (0, 0))],
        out_specs=per_b,
        out_shape=jax.ShapeDtypeStruct((n, HGRN_WIDTH), bf16),
        scratch_shapes=[pltpu.VMEM((HGRN_HEADS, HGRN_HEAD_V, HGRN_EXPAND), f32)],
        compiler_params=_cparams(("arbitrary",)),
        name="hgrn",
    )(hq, hf, hi, hg, lb_logits, norm_g)


def _merge_kernel(on_ref, oh_ref, mg_ref, x_ref, mod_ref, g_ref, wb0_ref, wb1_ref, wo_ref, x1_ref, h2_ref, hb_ref):
    a = _dot(on_ref[...], wb0_ref[...])
    b = _dot(oh_ref[...], wb1_ref[...])
    mg = mg_ref[...].astype(f32)
    y = mg[:, :D_MODEL] * a + mg[:, D_MODEL:] * b
    y2 = _dot(y.astype(bf16), wo_ref[...])
    x1 = x_ref[...] + mod_ref[0, 2:3, :] * y2
    h2 = _rms(x1, g_ref[...]) * (1.0 + mod_ref[0, 4:5, :]) + mod_ref[0, 3:4, :]
    rows = x1.shape[0]
    x1_ref[...] = x1.reshape(rows, SUBLANES, LANES)
    h2_ref[...] = h2.reshape(rows, SUBLANES, LANES)
    hb_ref[...] = h2.astype(bf16)


def _merge(o_nsa, o_hgrn, mg, x2, mod3, g_ffn, wb0, wb1, wo, seq):
    n = x2.shape[0]
    tm = 256
    tiles_per_seq = seq // tm
    row = lambda i: (i, 0)
    full = lambda a: pl.BlockSpec(a.shape, lambda i: (0,) * a.ndim)
    return pl.pallas_call(
        _merge_kernel,
        grid=(n // tm,),
        in_specs=[
            pl.BlockSpec((tm, NSA_WIDTH), row), pl.BlockSpec((tm, HGRN_WIDTH), row),
            pl.BlockSpec((tm, 2 * D_MODEL), row), pl.BlockSpec((tm, D_MODEL), row),
            pl.BlockSpec((1, 6, D_MODEL), lambda i: (i // tiles_per_seq, 0, 0)),
            full(g_ffn), full(wb0), full(wb1), full(wo),
        ],
        out_specs=[pl.BlockSpec((tm, SUBLANES, LANES), lambda i: (i, 0, 0))] * 2 + [pl.BlockSpec((tm, D_MODEL), row)],
        out_shape=[jax.ShapeDtypeStruct((n, SUBLANES, LANES), f32)] * 2 + [jax.ShapeDtypeStruct((n, D_MODEL), bf16)],
        compiler_params=_cparams(("arbitrary",)),
        name="merge",
    )(o_nsa, o_hgrn, mg, x2, mod3, g_ffn, wb0, wb1, wo)


_TR = 256


def _topk_rows(s, k, payload=None):
    n = s.shape[0]
    rowid = lax.broadcasted_iota(i32, s.shape, 0).astype(f32)
    vals, picks = [], []
    for _ in range(k):
        m = jnp.max(s, axis=0, keepdims=True)
        i = jnp.min(jnp.where(s == m, rowid, float(n)), axis=0, keepdims=True)
        hit = rowid == i
        vals.append(m)
        picks.append(i if payload is None else jnp.max(jnp.where(hit, payload, -1.0), axis=0, keepdims=True))
        s = jnp.where(hit, -jnp.inf, s)
    return vals, picks


def _pair_candidates(v1, i1, v2, i2):
    k = len(v1)
    s2 = jnp.concatenate(v2, axis=0)
    e2 = jnp.concatenate(i2, axis=0)
    sub = lax.broadcasted_iota(i32, (SUBLANES, s2.shape[1]), 0)
    comb, cand = [], []
    for a in range(k // 2):
        nb = k // (a + 1)
        rows = -(-nb // SUBLANES) * SUBLANES
        c = v1[a] + s2[0:rows]
        if nb < rows:
            c = jnp.where(sub < nb, c, -jnp.inf)
        comb.append(c)
        cand.append(i1[a] * float(PEER_NKEYS) + e2[0:rows])
    comb.append(jnp.concatenate(v1[k // 2:], axis=0) + s2[0:1])
    cand.append(jnp.concatenate(i1[k // 2:], axis=0) * float(PEER_NKEYS) + e2[0:1])
    return jnp.concatenate(comb, axis=0), jnp.concatenate(cand, axis=0)


def _head_routing(tops1, tops2):
    k = PEER_TOPK
    comb, cand = _pair_candidates(tops1[0], tops1[1], tops2[0], tops2[1])
    tv, picked = _topk_rows(comb, k, payload=cand)
    ex = [jnp.exp(tv[r] - tv[0]) for r in range(k)]
    den = ex[0]
    for r in range(1, k):
        den = den + ex[r]
    return picked, [e / den for e in ex]


def _scores_kernel(h_ref, wq_ref, sk_ref, o_ref):
    q = _dot(h_ref[...], wq_ref[...]).astype(bf16)
    half = PEER_QDIM // 2
    for grp in range(2 * PEER_HEADS):
        o_ref[grp] = _dot_nt(sk_ref[grp], q[:, grp * half:(grp + 1) * half])


def _scores(h2, wq, sk):
    n = h2.shape[0]
    tr = _TR
    return pl.pallas_call(
        _scores_kernel,
        grid=(n // tr,),
        in_specs=[
            pl.BlockSpec((tr, D_MODEL), lambda i: (i, 0)),
            pl.BlockSpec(wq.shape, lambda i: (0, 0)),
            pl.BlockSpec(sk.shape, lambda i: (0, 0, 0)),
        ],
        out_specs=pl.BlockSpec((2 * PEER_HEADS, PEER_NKEYS, tr), lambda i: (0, 0, i)),
        out_shape=jax.ShapeDtypeStruct((2 * PEER_HEADS, PEER_NKEYS, n), f32),
        compiler_params=_cparams(("arbitrary",)),
        name="scores",
    )(h2, wq, sk)


def _route_kernel(s_ref, idx_ref, gate_ref):
    k = PEER_TOPK
    experts, gates = [], []
    for hd in range(PEER_HEADS):
        picked, g = _head_routing(_topk_rows(s_ref[2 * hd], k), _topk_rows(s_ref[2 * hd + 1], k))
        experts += picked
        gates += g
    idx_t = jnp.concatenate(experts, axis=0).astype(i32)
    gate_t = jnp.concatenate(gates, axis=0)
    for blk in range(idx_t.shape[1] // LANES):
        sl = slice(blk * LANES, (blk + 1) * LANES)
        idx_ref[sl, :] = idx_t[:, sl].T
        gate_ref[sl, :] = gate_t[:, sl].T


def _route(scores, n_tok):
    ncol = PEER_HEADS * PEER_TOPK
    return pl.pallas_call(
        _route_kernel,
        grid=(1,),
        in_specs=[pl.BlockSpec((2 * PEER_HEADS, PEER_NKEYS, n_tok), lambda i: (0, 0, 0))],
        out_specs=[pl.BlockSpec((n_tok, ncol), lambda i: (0, 0))] * 2,
        out_shape=[jax.ShapeDtypeStruct((n_tok, ncol), i32), jax.ShapeDtypeStruct((n_tok, ncol), f32)],
        compiler_params=_cparams(("arbitrary",)),
        name="route",
    )(scores)


_TS = 128
_HALF = 64
_TG = 4
_NSLOT = 4
_AHEAD = (_NSLOT - 1) * _TG
_NROW = PEER_HEADS * PEER_TOPK
_ROW_TILES = 2 * D_MODEL // LANES
_TP = 2
_RING = 3
_RING_ROWS = (_RING + 2) * _TS


def _split_bf16(x):
    hi = x.astype(bf16)
    return hi, (x - hi.astype(f32)).astype(bf16)


def _expert_kernel(sc_ref, h_ref, x1_ref, mod_ref, gfin_ref, rsum_ref, rrep_ref, idx01_ref, gate01_ref, uv_ref, o_ref,
                   *scratch):
    step = pl.program_id(0)
    last = pl.num_programs(0) - 1
    bufs = scratch[:_NSLOT]
    out_tiles, idx_stage, gate_stage, idx_rows, gate_ring, idx_ring, sem, aux = scratch[_NSLOT:]
    n_dma = _TG * _NROW
    k = PEER_TOPK
    cur = lax.rem(step, _RING)
    nxt = lax.rem(step + 2, _RING)
    mirror = jnp.where(nxt == 0, _RING * _TS, (_RING + 1) * _TS)

    def row_copy(e, slot, j):
        return pltpu.make_async_copy(uv_ref.at[e], bufs[slot].at[j], sem.at[slot])

    def wait(slot):
        pltpu.make_async_copy(uv_ref.at[pl.ds(0, n_dma)], bufs[slot], sem.at[slot]).wait()

    def ring_copies():
        return (pltpu.make_async_copy(idx_rows, idx_ring.at[pl.ds(nxt * _TS, _TS)], aux.at[0]),
                pltpu.make_async_copy(idx_rows, idx_ring.at[pl.ds(mirror, _TS)], aux.at[1]))

    @pl.when(step == 0)
    def _():
        prime = (pltpu.make_async_copy(idx01_ref, idx_ring.at[pl.ds(0, 2 * _TS)], aux.at[0]),
                 pltpu.make_async_copy(idx01_ref.at[pl.ds(0, _TS)], idx_ring.at[pl.ds(_RING * _TS, _TS)], aux.at[1]),
                 pltpu.make_async_copy(gate01_ref, gate_ring.at[pl.ds(0, 2 * _TS)], aux.at[2]))
        for c in prime:
            c.start()
        for c in prime:
            c.wait()
        for slot in range(_NSLOT - 1):
            def body(t, c, slot=slot):
                for r in range(_NROW):
                    row_copy(idx_ring[slot * _TG + t, r], slot, t * _NROW + r).start(priority=r % 2)
                return c
            lax.fori_loop(0, _TG, body, 0)

    gt_f = mod_ref[0, 5]
    gfin = gfin_ref[...]
    width = _TP * LANES
    r_i = lax.broadcasted_iota(i32, (width, width), 0) // LANES
    c_i = lax.broadcasted_iota(i32, (width, width), 1) // LANES
    lane_sum = jnp.where(r_i == c_i, 1.0, 0.0).astype(bf16)
    eye = jnp.where(lax.broadcasted_iota(i32, (_NROW, LANES), 0) == lax.broadcasted_iota(i32, (_NROW, LANES), 1),
                    1.0, 0.0)

    def evaluate(tok0, slot, t0):
        src = bufs[slot]
        z, vs = [], []
        for dt in range(_TP):
            uv = src[pl.ds((t0 + dt) * _NROW, _NROW)].astype(f32)
            vs.append(uv[:, SUBLANES:2 * SUBLANES, :])
            u = uv[:, 0:SUBLANES, :]
            z.append((u * h_ref[tok0 + dt][None]).reshape(_NROW * SUBLANES, LANES).astype(bf16))
        part = _dot(rsum_ref[...], jnp.concatenate(z, axis=1))
        p_hi, p_lo = _split_bf16(part)
        a = _dot(p_hi, lane_sum) + _dot(p_lo, lane_sum)
        g_diag = jnp.concatenate([eye * gate_ring[pl.ds(cur * _TS + tok0 + dt, 1), :] for dt in range(_TP)], axis=1)
        g = _dot(g_diag.astype(bf16), lane_sum)
        w = (g * _gelu(a)).astype(bf16)
        spread = _dot(rrep_ref[...], w)
        for dt in range(_TP):
            tok = tok0 + dt
            wt = spread[:, dt * LANES:(dt + 1) * LANES].reshape(_NROW, SUBLANES, LANES)
            y = jnp.sum(wt * vs[dt], axis=0)
            x2 = x1_ref[tok] + gt_f * y
            ms = jnp.sum(jnp.sum(x2 * x2, axis=1, keepdims=True), axis=0, keepdims=True) * (1.0 / D_MODEL)
            out_tiles[tok] = x2 * lax.rsqrt(ms + EPS) * gfin

    def half_step(half, carry):
        @pl.when((half == _TS // _HALF - 1) & (step > 0))
        def _():
            for c in ring_copies():
                c.wait()

        tok_h = half * _HALF
        ring_h = cur * _TS + tok_h
        head0 = half * (PEER_HEADS // 2)
        experts, gates, tops = [], [], {}
        pieces = []
        for hh in range(PEER_HEADS // 2):
            for p in range(2):
                def sub_keys(hh=hh, p=p):
                    tops[hh, p] = _topk_rows(sc_ref[(head0 + hh) * 2 + p], k)
                pieces.append(sub_keys)

            def pairs(hh=hh):
                picked, g = _head_routing(tops[hh, 0], tops[hh, 1])
                experts.extend(picked)
                gates.extend(g)
            pieces.append(pairs)

        def stage():
            rows = pl.ds(pl.multiple_of(head0 * k, _HALF), _HALF)
            idx_stage[rows, :] = jnp.concatenate(experts, axis=0).astype(i32)
            gate_stage[rows, :] = jnp.concatenate(gates, axis=0)
        pieces.append(stage)

        for grp in range(_HALF // _TG):
            slot = grp % _NSLOT
            wait(slot)
            if grp < len(pieces):
                pieces[grp]()
            dst = (slot + _NSLOT - 1) % _NSLOT
            for t0 in range(0, _TG, _TP):
                evaluate(tok_h + grp * _TG + t0, slot, t0)
                for t in range(t0, t0 + _TP):
                    for r in range(_NROW):
                        row_copy(idx_ring[ring_h + grp * _TG + t + _AHEAD, r], dst, t * _NROW + r).start(priority=r % 2)
        return carry

    lax.fori_loop(0, _TS // _HALF, half_step, 0)
    o_ref[...] = out_tiles[...].reshape(_TS, D_MODEL)
    idx_rows[...] = idx_stage[...].T
    gate_ring[pl.ds(pl.multiple_of(nxt * _TS, _TS), _TS), :] = gate_stage[...].T
    for c in ring_copies():
        c.start()

    @pl.when(step == last)
    def _():
        for c in ring_copies():
            c.wait()
        for slot in range(_NSLOT - 1):
            wait(slot)


def _experts(scores, idx01, gate01, h3, x13, mod4, gfin, uv, seq):
    n = h3.shape[0]
    n_steps = n // _TS
    steps_per_seq = seq // _TS
    tok3 = lambda i: (i, 0, 0)
    rsum = jnp.asarray(np.repeat(np.eye(_NROW, dtype=np.float32), SUBLANES, axis=1), dtype=bf16)
    buf = pltpu.VMEM((_TG * _NROW, _ROW_TILES, LANES), bf16)
    return pl.pallas_call(
        _expert_kernel,
        grid=(n_steps,),
        in_specs=[
            pl.BlockSpec((2 * PEER_HEADS, PEER_NKEYS, _TS), lambda i: (0, 0, jnp.minimum(i + 2, n_steps - 1))),
            pl.BlockSpec((_TS, SUBLANES, LANES), tok3),
            pl.BlockSpec((_TS, SUBLANES, LANES), tok3),
            pl.BlockSpec((1, 6, SUBLANES, LANES), lambda i: (i // steps_per_seq, 0, 0, 0)),
            pl.BlockSpec((SUBLANES, LANES), lambda i: (0, 0)),
            pl.BlockSpec(rsum.shape, lambda i: (0, 0)),
            pl.BlockSpec(rsum.shape[::-1], lambda i: (0, 0)),
            pl.BlockSpec(memory_space=pl.ANY),
            pl.BlockSpec(memory_space=pl.ANY),
            pl.BlockSpec(memory_space=pl.ANY),
        ],
        out_specs=pl.BlockSpec((_TS, D_MODEL), lambda i: (i, 0)),
        out_shape=jax.ShapeDtypeStruct((n, D_MODEL), f32),
        scratch_shapes=[buf] * _NSLOT + [
            pltpu.VMEM((_TS, SUBLANES, LANES), f32),
            pltpu.VMEM((_NROW, _TS), i32),
            pltpu.VMEM((_NROW, _TS), f32),
            pltpu.VMEM((_TS, _NROW), i32),
            pltpu.VMEM((_RING * _TS, _NROW), f32),
            pltpu.SMEM((_RING_ROWS, _NROW), i32),
            pltpu.SemaphoreType.DMA((_NSLOT,)),
            pltpu.SemaphoreType.DMA((3,)),
        ],
        compiler_params=_cparams(("arbitrary",)),
        name="experts",
    )(scores, h3, x13, mod4, gfin, rsum, rsum.T, idx01, gate01, uv)


def _rope_tables(seq):
    half = HEAD_DIM // 2
    inv = ROPE_THETA ** (-np.arange(half, dtype=np.float32) / half)
    ang = np.arange(seq, dtype=np.float32)[:, None] * inv[None, :].astype(np.float32)
    cos = np.cos(ang).astype(np.float32)
    sin = np.sin(ang).astype(np.float32)
    cos_t = np.tile(np.concatenate([cos, cos], axis=1), (1, NSA_HEADS))
    sin_t = np.tile(np.concatenate([-sin, sin], axis=1), (1, NSA_HEADS))
    return jnp.asarray(cos_t), jnp.asarray(sin_t)


def _selection_map(n_cmp_pad, n_sel):
    r_sel = SEL_BLOCK // CMP_STRIDE
    r_cmp = CMP_BLOCK // CMP_STRIDE
    i = np.arange(n_cmp_pad)[:, None]
    j = np.arange(n_sel)[None, :]
    d = i - r_sel * j
    cnt = np.minimum(d, r_sel - 1) - np.maximum(d - r_cmp + 1, 0) + 1
    cnt = np.clip(cnt, 0, None).astype(np.float32)
    cnt[n_cmp_pad - r_cmp + 1:] = 0.0
    return jnp.asarray(cnt.T)


def _block_expand(n_sel, seq):
    e = (np.arange(seq)[None, :] // SEL_BLOCK == np.arange(n_sel)[:, None]).astype(np.float32)
    return jnp.asarray(e, dtype=bf16)


def _compress_weights(w1):
    eye = jnp.eye(NSA_KV_GROUPS, dtype=w1.dtype)
    out = []
    for part in range(CMP_BLOCK // CMP_STRIDE):
        w = w1[part * CMP_STRIDE * HEAD_DIM:(part + 1) * CMP_STRIDE * HEAD_DIM].reshape(CMP_STRIDE, HEAD_DIM, CMP_HIDDEN)
        big = jnp.einsum('pdc,gh->pgdhc', w, eye).reshape(CMP_STRIDE * KV_WIDTH, NSA_KV_GROUPS * CMP_HIDDEN)
        out.append(big.astype(bf16))
    return out


def kernel(x, c, w_ada, b_ada, g_mix, g_ffn, w_in, cmp_pos_k, cmp_pos_v, w_ck1, w_ck2, w_cv1, w_cv2, hgrn_lb_logits, hgrn_out_norm, w_branch, w_out, w_peer_q, peer_sub_keys, peer_u, peer_v, g_final):
    bsz, seq, d = x.shape
    n = bsz * seq
    depth = w_ada.shape[0]
    assert depth == 1, "single-layer block only"
    n_sel = seq // SEL_BLOCK
    n_pieces = seq // CMP_STRIDE
    cos_t, sin_t = _rope_tables(seq)
    selmap = _selection_map(n_pieces, n_sel)
    expand = _block_expand(n_sel, seq)
    xcur = x.reshape(n, d)
    for l in range(depth):
        mod = _adaln(c, w_ada[l], b_ada[l].reshape(1, 6 * d))
        mod3 = mod.reshape(bsz, 6, d)
        w_pad = jnp.concatenate(
            [w_in[l][:, :_C_GL + 3 * NSA_HEADS], jnp.zeros((d, GATE_PAD - 3 * NSA_HEADS), w_in.dtype),
             w_in[l][:, _C_GL + 3 * NSA_HEADS:]], axis=1).astype(bf16)
        q, kc, vc, kvsw, gates, hq, hf, hi, hg, mg = _inproj(xcur, mod3, g_mix[l].reshape(1, d), w_pad, cos_t, sin_t, seq)
        wkt, wkb = _compress_weights(w_ck1[l])
        wvt, wvb = _compress_weights(w_cv1[l])
        kc_c, vc_c = _compress(
            kc.reshape(bsz, n_pieces, CMP_STRIDE * KV_WIDTH), vc.reshape(bsz, n_pieces, CMP_STRIDE * KV_WIDTH),
            wkt, wkb, wvt, wvb, cmp_pos_k[l].reshape(1, -1), cmp_pos_v[l].reshape(1, -1),
            w_ck1[l], w_cv1[l], w_ck2[l], w_cv2[l])
        o_nsa = _nsa(q, kvsw.reshape(bsz, seq, 4 * KV_WIDTH), kc_c, vc_c, gates, selmap, expand, seq)
        o_hgrn = _hgrn(hq, hf, hi, hg, hgrn_lb_logits, hgrn_out_norm[l].reshape(1, -1), seq, l)
        x1, h2, h2_b = _merge(o_nsa, o_hgrn, mg, xcur, mod3, g_ffn[l].reshape(1, d),
                              w_branch[l, 0].astype(bf16), w_branch[l, 1].astype(bf16), w_out[l].astype(bf16), seq)
        wq = w_peer_q[l].astype(bf16)
        sk = jnp.transpose(peer_sub_keys[l], (1, 0, 2, 3)).reshape(2 * PEER_HEADS, PEER_NKEYS, PEER_QDIM // 2).astype(bf16)
        scores = _scores(h2_b, wq, sk)
        idx01, gate01 = _route(scores, 2 * _TS)
        uv = jnp.concatenate([peer_u[l], peer_v[l]], axis=1).astype(bf16).reshape(-1, _ROW_TILES, LANES)
        xcur = _experts(scores, idx01, gate01, h2, x1, mod.reshape(bsz, 6, SUBLANES, LANES),
                        g_final.reshape(SUBLANES, LANES), uv, seq)
    return xcur.reshape(bsz, seq, d)
```

```python
import functools

import numpy as np
import jax
import jax.numpy as jnp
from jax import lax
from jax.experimental import pallas as pl
from jax.experimental.pallas import tpu as pltpu

f32 = jnp.float32
bf16 = jnp.bfloat16
i32 = jnp.int32
_HIGHEST = lax.Precision.HIGHEST

D_MODEL = 1024
EPS = 1e-6
HEAD_DIM = 64
ROPE_THETA = 10000.0
NSA_HEADS = 8
NSA_KV_GROUPS = 2
NSA_GROUP = NSA_HEADS // NSA_KV_GROUPS
NSA_WIDTH = NSA_HEADS * HEAD_DIM
KV_WIDTH = NSA_KV_GROUPS * HEAD_DIM
CMP_BLOCK = 32
CMP_STRIDE = 16
CMP_HIDDEN = 128
SEL_BLOCK = 64
SEL_TOPK = 16
WINDOW = 512
FORCE_BONUS = 1000.0
HGRN_HEADS = 4
HGRN_EXPAND = 128
HGRN_HEAD_V = 128
HGRN_WIDTH = HGRN_HEADS * HGRN_EXPAND
HGRN_CHUNK = 32
PEER_HEADS = 8
PEER_NKEYS = 128
PEER_QDIM = 256
PEER_TOPK = 16
GATE_PAD = 128

LANES = 128
SUBLANES = 8
VMEM_LIMIT = 56 * 1024 * 1024

_NEG = -1e30


def _cparams(sem):
    return pltpu.CompilerParams(dimension_semantics=sem, vmem_limit_bytes=VMEM_LIMIT)


def _gelu(x):
    return 0.5 * x * (1.0 + jnp.tanh(0.7978845608028654 * (x + 0.044715 * (x * x * x))))


def _dot(a, b, **kw):
    return jnp.dot(a, b, preferred_element_type=f32, **kw)


def _dot_nt(a, b):
    return lax.dot_general(a, b, (((1,), (1,)), ((), ())), preferred_element_type=f32)


def _rms(x, g):
    return x * lax.rsqrt(jnp.mean(x * x, axis=-1, keepdims=True) + EPS) * g


def _adaln_kernel(c_ref, w_ref, b_ref, o_ref):
    c = c_ref[...]
    cs = c * jax.nn.sigmoid(c)
    o_ref[...] = _dot(cs, w_ref[...], precision=_HIGHEST) + b_ref[...]


def _adaln(c, w, b):
    bsz = c.shape[0]
    tn = 512
    return pl.pallas_call(
        _adaln_kernel,
        grid=(6 * D_MODEL // tn,),
        in_specs=[
            pl.BlockSpec((bsz, D_MODEL), lambda j: (0, 0)),
            pl.BlockSpec((D_MODEL, tn), lambda j: (0, j)),
            pl.BlockSpec((1, tn), lambda j: (0, j)),
        ],
        out_specs=pl.BlockSpec((bsz, tn), lambda j: (0, j)),
        out_shape=jax.ShapeDtypeStruct((bsz, 6 * D_MODEL), f32),
        compiler_params=_cparams(("arbitrary",)),
        name="adaln",
    )(c, w, b)


_C_Q = 0
_C_KC = 512
_C_VC = 640
_C_KS = 768
_C_VS = 896
_C_KW = 1024
_C_VW = 1152
_C_GL = 1280
_C_HQ = _C_GL + GATE_PAD
_C_HF = _C_HQ + 512
_C_HI = _C_HF + 512
_C_HG = _C_HI + 512
_C_MG = _C_HG + 512
_C_END = _C_MG + 2 * D_MODEL


def _inproj_kernel(x_ref, mod_ref, g_ref, w_ref, cos_ref, sin_ref,
                   q_ref, kc_ref, vc_ref, kvsw_ref, gate_ref, hq_ref, hf_ref, hi_ref, hg_ref, mg_ref):
    x = x_ref[...]
    sh = mod_ref[0, 0:1, :]
    sc = mod_ref[0, 1:2, :]
    h = (_rms(x, g_ref[...]) * (1.0 + sc) + sh).astype(bf16)

    def mm(c0, c1):
        return _dot(h, w_ref[:, c0:c1])

    cos = cos_ref[...]
    sin = sin_ref[...]

    def rope(a):
        width = a.shape[1]
        first = (lax.broadcasted_iota(i32, a.shape, 1) & (HEAD_DIM - 1)) < (HEAD_DIM // 2)
        partner = jnp.where(first, pltpu.roll(a, width - HEAD_DIM // 2, 1), pltpu.roll(a, HEAD_DIM // 2, 1))
        return a * cos[:, :width] + partner * sin[:, :width]

    q_ref[...] = (rope(mm(_C_Q, _C_KC)) * (HEAD_DIM ** -0.5)).astype(bf16)
    kc_ref[...] = rope(mm(_C_KC, _C_VC)).astype(bf16)
    vc_ref[...] = mm(_C_VC, _C_KS).astype(bf16)
    kvsw_ref[:, 0:128] = rope(mm(_C_KS, _C_VS)).astype(bf16)
    kvsw_ref[:, 128:256] = mm(_C_VS, _C_KW).astype(bf16)
    kvsw_ref[:, 256:384] = rope(mm(_C_KW, _C_VW)).astype(bf16)
    kvsw_ref[:, 384:512] = mm(_C_VW, _C_GL).astype(bf16)
    gate_ref[...] = jax.nn.sigmoid(mm(_C_GL, _C_HQ))
    hq_ref[...] = mm(_C_HQ, _C_HF).astype(bf16)
    hf_ref[...] = mm(_C_HF, _C_HI)
    hi_ref[...] = mm(_C_HI, _C_HG).astype(bf16)
    hg_ref[...] = mm(_C_HG, _C_MG).astype(bf16)
    mg_ref[...] = jax.nn.sigmoid(mm(_C_MG, _C_END)).astype(bf16)


def _inproj(x2, mod3, g_mix, w_pad, cos_t, sin_t, seq):
    n = x2.shape[0]
    tm = 512
    tiles_per_seq = seq // tm
    row = lambda i: (i, 0)
    outs = [
        (NSA_WIDTH, bf16), (KV_WIDTH, bf16), (KV_WIDTH, bf16), (4 * KV_WIDTH, bf16), (GATE_PAD, f32),
        (HGRN_WIDTH, bf16), (HGRN_WIDTH, f32), (HGRN_WIDTH, bf16), (HGRN_WIDTH, bf16), (2 * D_MODEL, bf16),
    ]
    return pl.pallas_call(
        _inproj_kernel,
        grid=(n // tm,),
        in_specs=[
            pl.BlockSpec((tm, D_MODEL), row),
            pl.BlockSpec((1, 6, D_MODEL), lambda i: (i // tiles_per_seq, 0, 0)),
            pl.BlockSpec((1, D_MODEL), lambda i: (0, 0)),
            pl.BlockSpec((D_MODEL, _C_END), lambda i: (0, 0)),
            pl.BlockSpec((tm, NSA_WIDTH), lambda i: (i % tiles_per_seq, 0)),
            pl.BlockSpec((tm, NSA_WIDTH), lambda i: (i % tiles_per_seq, 0)),
        ],
        out_specs=[pl.BlockSpec((tm, w), row) for w, _ in outs],
        out_shape=[jax.ShapeDtypeStruct((n, w), dt) for w, dt in outs],
        compiler_params=_cparams(("arbitrary",)),
        name="inproj",
    )(x2, mod3, g_mix, w_pad, cos_t, sin_t)


def _compress_kernel(kc_ref, vc_ref, wkt_ref, wkb_ref, wvt_ref, wvb_ref, pk_ref, pv_ref,
                     w1k_ref, w1v_ref, w2k_ref, w2v_ref, okc_ref, ovc_ref):
    def one(x_ref, wt_ref, wb_ref, pos_ref, w1_ref, w2_ref, o_ref):
        pieces = x_ref[0]
        top = _dot(pieces, wt_ref[...])
        bot = _dot(pieces, wb_ref[...])
        nrow = bot.shape[0]
        bot = pltpu.roll(bot, nrow - 1, 0)
        cpos = _dot(pos_ref[...], w1_ref[...], precision=_HIGHEST)
        w2 = w2_ref[...].astype(bf16)
        outs = []
        for g in range(NSA_KV_GROUPS):
            sl = slice(g * CMP_HIDDEN, (g + 1) * CMP_HIDDEN)
            hid = _gelu(top[:, sl] + bot[:, sl] + cpos)
            outs.append(_dot(hid.astype(bf16), w2))
        o_ref[0] = jnp.concatenate(outs, axis=1).astype(bf16)

    one(kc_ref, wkt_ref, wkb_ref, pk_ref, w1k_ref, w2k_ref, okc_ref)
    one(vc_ref, wvt_ref, wvb_ref, pv_ref, w1v_ref, w2v_ref, ovc_ref)


def _compress(kc3, vc3, wkt, wkb, wvt, wvb, pk, pv, w1k, w1v, w2k, w2v):
    bsz, npieces, width = kc3.shape
    full = lambda a: pl.BlockSpec(a.shape, lambda b: (0,) * a.ndim)
    per_b = pl.BlockSpec((1, npieces, width), lambda b: (b, 0, 0))
    out_b = pl.BlockSpec((1, npieces, KV_WIDTH), lambda b: (b, 0, 0))
    return pl.pallas_call(
        _compress_kernel,
        grid=(bsz,),
        in_specs=[per_b, per_b] + [full(a) for a in (wkt, wkb, wvt, wvb, pk, pv, w1k, w1v, w2k, w2v)],
        out_specs=[out_b, out_b],
        out_shape=[jax.ShapeDtypeStruct((bsz, npieces, KV_WIDTH), bf16)] * 2,
        compiler_params=_cparams(("arbitrary",)),
        name="compress",
    )(kc3, vc3, wkt, wkb, wvt, wvb, pk, pv, w1k, w1v, w2k, w2v)


_TQ = 128
_NSA_CLASSES = 8


def _softmax_parts(s, mask):
    sm = jnp.where(mask, s, _NEG)
    m = jnp.max(sm, axis=-1, keepdims=True)
    p = jnp.where(mask, jnp.exp(sm - m), 0.0)
    den = jnp.maximum(jnp.sum(p, axis=-1, keepdims=True), 1e-30)
    return p, den


def _softmax_plain(s):
    p = jnp.exp(s - jnp.max(s, axis=-1, keepdims=True))
    return p, jnp.sum(p, axis=-1, keepdims=True)


def _nsa_kernel(q_ref, kvsw_ref, kc_ref, vc_ref, gate_ref, selmap_ref, expand_ref, o_ref, *, seq):
    tq = _TQ
    qi = pl.program_id(1)
    n_cls = _NSA_CLASSES
    per_cls = (seq // tq) // n_cls
    for cls in range(n_cls):
        @pl.when((qi >= cls * per_cls) & (qi < (cls + 1) * per_cls))
        def _(cls=cls):
            _nsa_tile(q_ref, kvsw_ref, kc_ref, vc_ref, gate_ref, selmap_ref, expand_ref, o_ref,
                      seq=seq, width=(cls + 1) * (seq // n_cls))


def _nsa_tile(q_ref, kvsw_ref, kc_ref, vc_ref, gate_ref, selmap_ref, expand_ref, o_ref, *, seq, width):
    tq = _TQ
    rows = NSA_GROUP * tq
    n_sel = seq // SEL_BLOCK
    n_cmp_pad = seq // CMP_STRIDE
    t0 = pl.program_id(1) * tq
    q = q_ref[...]
    gates = gate_ref[...]
    tlane = t0 + lax.broadcasted_iota(i32, (1, tq), 1)
    t4 = t0 + (lax.broadcasted_iota(i32, (rows, 1), 0) & (tq - 1))
    wk = WINDOW + tq
    ws = pl.multiple_of(jnp.maximum(t0 - WINDOW, 0), tq)
    ks_all = kvsw_ref[0, 0:width, 0:128]
    vs_all = kvsw_ref[0, 0:width, 128:256]
    kw_all = kvsw_ref[0, pl.ds(ws, wk), 256:384]
    vw_all = kvsw_ref[0, pl.ds(ws, wk), 384:512]
    kc_all = kc_ref[0]
    vc_all = vc_ref[0]
    cend = lax.broadcasted_iota(i32, (1, n_cmp_pad), 1) * CMP_STRIDE + (CMP_BLOCK - 1)
    blk = lax.broadcasted_iota(i32, (n_sel, 1), 0)
    kpos = lax.broadcasted_iota(i32, (1, width), 1)
    kpos_w = ws + lax.broadcasted_iota(i32, (1, wk), 1)
    cur = tlane >> 6
    forced = (blk == 0) | (blk == cur) | (blk == cur - 1)
    causal_blk = blk * SEL_BLOCK <= tlane
    pieces = []
    for g in range(NSA_KV_GROUPS):
        gs = slice(g * HEAD_DIM, (g + 1) * HEAD_DIM)
        qg = jnp.concatenate(
            [q[:, (NSA_GROUP * g + r) * HEAD_DIM:(NSA_GROUP * g + r + 1) * HEAD_DIM] for r in range(NSA_GROUP)], axis=0)
        p_c, den_c = _softmax_parts(_dot_nt(qg, kc_all[:, gs]), cend <= t4)
        o_c = _dot(p_c.astype(bf16), vc_all[:, gs]) / den_c
        pn = p_c / den_c
        pc_sum = pn[0:tq]
        for r in range(1, NSA_GROUP):
            pc_sum = pc_sum + pn[r * tq:(r + 1) * tq]
        imp = lax.dot_general(selmap_ref[...], pc_sum, (((1,), (1,)), ((), ())),
                              preferred_element_type=f32, precision=_HIGHEST)
        imp = jnp.where(forced, imp + FORCE_BONUS, imp)
        imp = jnp.where(causal_blk, imp, -1.0)
        rank = jnp.zeros((n_sel, tq), f32)
        for j in range(n_sel):
            row = imp[j:j + 1, :]
            ahead = jnp.where(row > imp, 1.0, jnp.where(row == imp, jnp.where(blk > j, 1.0, 0.0), 0.0))
            rank = rank + ahead
        bias = jnp.where(rank < float(min(SEL_TOPK, n_sel)), jnp.where(causal_blk, 0.0, _NEG), _NEG).astype(bf16)
        bias_q = lax.dot_general(bias, expand_ref[:, 0:width], (((0,), (0,)), ((), ())),
                                 preferred_element_type=f32)
        bias4 = jnp.concatenate([bias_q] * NSA_GROUP, axis=0)
        p_s, den_s = _softmax_plain(jnp.where(kpos <= t4, _dot_nt(qg, ks_all[:, gs]) + bias4, _NEG))
        o_s = _dot(p_s.astype(bf16), vs_all[:, gs]) / den_s
        in_window = (t4 - kpos_w).astype(jnp.uint32) < WINDOW
        p_w, den_w = _softmax_plain(jnp.where(in_window, _dot_nt(qg, kw_all[:, gs]), _NEG))
        o_w = _dot(p_w.astype(bf16), vw_all[:, gs]) / den_w

        def gcol(br):
            return jnp.concatenate(
                [gates[:, (NSA_GROUP * g + r) * 3 + br:(NSA_GROUP * g + r) * 3 + br + 1] for r in range(NSA_GROUP)], axis=0)

        o = gcol(0) * o_c + gcol(1) * o_s + gcol(2) * o_w
        pieces += [o[r * tq:(r + 1) * tq] for r in range(NSA_GROUP)]
    o_ref[...] = jnp.concatenate(pieces, axis=1).astype(bf16)


def _nsa(q, kvsw3, kc3, vc3, gates, selmap, expand, seq):
    n = q.shape[0]
    bsz = n // seq
    tq = _TQ
    nq = seq // tq
    return pl.pallas_call(
        functools.partial(_nsa_kernel, seq=seq),
        grid=(bsz, nq),
        in_specs=[
            pl.BlockSpec((tq, NSA_WIDTH), lambda b, i: (b * nq + i, 0)),
            pl.BlockSpec((1, seq, 4 * KV_WIDTH), lambda b, i: (b, 0, 0)),
            pl.BlockSpec((1,) + kc3.shape[1:], lambda b, i: (b, 0, 0)),
            pl.BlockSpec((1,) + vc3.shape[1:], lambda b, i: (b, 0, 0)),
            pl.BlockSpec((tq, GATE_PAD), lambda b, i: (b * nq + i, 0)),
            pl.BlockSpec(selmap.shape, lambda b, i: (0, 0)),
            pl.BlockSpec(expand.shape, lambda b, i: (0, 0)),
        ],
        out_specs=pl.BlockSpec((tq, NSA_WIDTH), lambda b, i: (b * nq + i, 0)),
        out_shape=jax.ShapeDtypeStruct((n, NSA_WIDTH), bf16),
        compiler_params=_cparams(("arbitrary", "arbitrary")),
        name="nsa",
    )(q, kvsw3, kc3, vc3, gates, selmap, expand)


def _hgrn_kernel(hq_ref, hf_ref, hi_ref, hg_ref, lbl_ref, ng_ref, o_ref, st_ref, *, seq, layer):
    c = HGRN_CHUNK
    logits = lbl_ref[...]
    e = jnp.exp(logits - jnp.max(logits, axis=0, keepdims=True))
    sm = e / jnp.sum(e, axis=0, keepdims=True)
    lb = sm[0:1]
    for l in range(1, layer + 1):
        lb = lb + sm[l:l + 1]
    st_ref[...] = jnp.zeros_like(st_ref)
    r_i = lax.broadcasted_iota(i32, (c, c), 0)
    c_i = lax.broadcasted_iota(i32, (c, c), 1)
    causal = r_i >= c_i
    tri = jnp.where(causal, 1.0, 0.0)
    ng = ng_ref[...]

    def body(ci, carry):
        r0 = pl.multiple_of(ci * c, c)
        hq = hq_ref[pl.ds(r0, c), :].astype(f32)
        hf = hf_ref[pl.ds(r0, c), :]
        v = hi_ref[pl.ds(r0, c), :]
        hg = hg_ref[pl.ds(r0, c), :].astype(f32)
        f = lb + (1.0 - lb) * jax.nn.sigmoid(hf)
        k = 1.0 - f
        qv = hq * jax.nn.sigmoid(hq)
        bcum = _dot(tri, jnp.log(f), precision=_HIGHEST)
        bend = bcum[c - 1:c, :]
        q_dec = (qv * jnp.exp(bcum)).astype(bf16)
        k_inv = (k * jnp.exp(-bcum)).astype(bf16)
        k_end = (k * jnp.exp(bend - bcum)).astype(bf16)
        dec_end = jnp.exp(bend)
        outs = []
        for h in range(HGRN_HEADS):
            sl = slice(h * HGRN_EXPAND, (h + 1) * HGRN_EXPAND)
            a = jnp.where(causal, _dot_nt(q_dec[:, sl], k_inv[:, sl]), 0.0)
            st = st_ref[h]
            o = _dot(a.astype(bf16), v[:, sl]) + _dot_nt(q_dec[:, sl], st.astype(bf16))
            upd = lax.dot_general(v[:, sl], k_end[:, sl], (((0,), (0,)), ((), ())), preferred_element_type=f32)
            st_ref[h] = st * dec_end[:, sl] + upd
            y = _rms(o, ng) * (hg[:, sl] * jax.nn.sigmoid(hg[:, sl]))
            outs.append(y)
        o_ref[pl.ds(r0, c), :] = jnp.concatenate(outs, axis=1).astype(bf16)
        return carry

    lax.fori_loop(0, seq // c, body, 0, unroll=8)


def _hgrn(hq, hf, hi, hg, lb_logits, norm_g, seq, layer):
    n = hq.shape[0]
    bsz = n // seq
    per_b = pl.BlockSpec((seq, HGRN_WIDTH), lambda b: (b, 0))
    return pl.pallas_call(
        functools.partial(_hgrn_kernel, seq=seq, layer=layer),
        grid=(bsz,),
        in_specs=[per_b, per_b, per_b, per_b,
                  pl.BlockSpec(lb_logits.shape, lambda b: (0, 0)),
                  pl.BlockSpec(norm_g.shape, lambda b: (0, 0))],
        out_specs=per_b,
        out_shape=jax.ShapeDtypeStruct((n, HGRN_WIDTH), bf16),
        scratch_shapes=[pltpu.VMEM((HGRN_HEADS, HGRN_HEAD_V, HGRN_EXPAND), f32)],
        compiler_params=_cparams(("arbitrary",)),
        name="hgrn",
    )(hq, hf, hi, hg, lb_logits, norm_g)


def _merge_kernel(on_ref, oh_ref, mg_ref, x_ref, mod_ref, g_ref, wb0_ref, wb1_ref, wo_ref, x1_ref, h2_ref, hb_ref):
    a = _dot(on_ref[...], wb0_ref[...])
    b = _dot(oh_ref[...], wb1_ref[...])
    mg = mg_ref[...].astype(f32)
    y = mg[:, :D_MODEL] * a + mg[:, D_MODEL:] * b
    y2 = _dot(y.astype(bf16), wo_ref[...])
    x1 = x_ref[...] + mod_ref[0, 2:3, :] * y2
    h2 = _rms(x1, g_ref[...]) * (1.0 + mod_ref[0, 4:5, :]) + mod_ref[0, 3:4, :]
    rows = x1.shape[0]
    x1_ref[...] = x1.reshape(rows, SUBLANES, LANES)
    h2_ref[...] = h2.reshape(rows, SUBLANES, LANES)
    hb_ref[...] = h2.astype(bf16)


def _merge(o_nsa, o_hgrn, mg, x2, mod3, g_ffn, wb0, wb1, wo, seq):
    n = x2.shape[0]
    tm = 256
    tiles_per_seq = seq // tm
    row = lambda i: (i, 0)
    full = lambda a: pl.BlockSpec(a.shape, lambda i: (0,) * a.ndim)
    return pl.pallas_call(
        _merge_kernel,
        grid=(n // tm,),
        in_specs=[
            pl.BlockSpec((tm, NSA_WIDTH), row), pl.BlockSpec((tm, HGRN_WIDTH), row),
            pl.BlockSpec((tm, 2 * D_MODEL), row), pl.BlockSpec((tm, D_MODEL), row),
            pl.BlockSpec((1, 6, D_MODEL), lambda i: (i // tiles_per_seq, 0, 0)),
            full(g_ffn), full(wb0), full(wb1), full(wo),
        ],
        out_specs=[pl.BlockSpec((tm, SUBLANES, LANES), lambda i: (i, 0, 0))] * 2 + [pl.BlockSpec((tm, D_MODEL), row)],
        out_shape=[jax.ShapeDtypeStruct((n, SUBLANES, LANES), f32)] * 2 + [jax.ShapeDtypeStruct((n, D_MODEL), bf16)],
        compiler_params=_cparams(("arbitrary",)),
        name="merge",
    )(o_nsa, o_hgrn, mg, x2, mod3, g_ffn, wb0, wb1, wo)


_TR = 256


def _topk_rows(s, k, payload=None):
    n = s.shape[0]
    rowid = lax.broadcasted_iota(i32, s.shape, 0).astype(f32)
    vals, picks = [], []
    for _ in range(k):
        m = jnp.max(s, axis=0, keepdims=True)
        i = jnp.min(jnp.where(s == m, rowid, float(n)), axis=0, keepdims=True)
        hit = rowid == i
        vals.append(m)
        picks.append(i if payload is None else jnp.max(jnp.where(hit, payload, -1.0), axis=0, keepdims=True))
        s = jnp.where(hit, -jnp.inf, s)
    return vals, picks


def _pair_candidates(v1, i1, v2, i2):
    k = len(v1)
    s2 = jnp.concatenate(v2, axis=0)
    e2 = jnp.concatenate(i2, axis=0)
    sub = lax.broadcasted_iota(i32, (SUBLANES, s2.shape[1]), 0)
    comb, cand = [], []
    for a in range(k // 2):
        nb = k // (a + 1)
        rows = -(-nb // SUBLANES) * SUBLANES
        c = v1[a] + s2[0:rows]
        if nb < rows:
            c = jnp.where(sub < nb, c, -jnp.inf)
        comb.append(c)
        cand.append(i1[a] * float(PEER_NKEYS) + e2[0:rows])
    comb.append(jnp.concatenate(v1[k // 2:], axis=0) + s2[0:1])
    cand.append(jnp.concatenate(i1[k // 2:], axis=0) * float(PEER_NKEYS) + e2[0:1])
    return jnp.concatenate(comb, axis=0), jnp.concatenate(cand, axis=0)


def _head_routing(tops1, tops2):
    k = PEER_TOPK
    comb, cand = _pair_candidates(tops1[0], tops1[1], tops2[0], tops2[1])
    tv, picked = _topk_rows(comb, k, payload=cand)
    ex = [jnp.exp(tv[r] - tv[0]) for r in range(k)]
    den = ex[0]
    for r in range(1, k):
        den = den + ex[r]
    return picked, [e / den for e in ex]


def _scores_kernel(h_ref, wq_ref, sk_ref, o_ref):
    q = _dot(h_ref[...], wq_ref[...]).astype(bf16)
    half = PEER_QDIM // 2
    for grp in range(2 * PEER_HEADS):
        o_ref[grp] = _dot_nt(sk_ref[grp], q[:, grp * half:(grp + 1) * half])


def _scores(h2, wq, sk):
    n = h2.shape[0]
    tr = _TR
    return pl.pallas_call(
        _scores_kernel,
        grid=(n // tr,),
        in_specs=[
            pl.BlockSpec((tr, D_MODEL), lambda i: (i, 0)),
            pl.BlockSpec(wq.shape, lambda i: (0, 0)),
            pl.BlockSpec(sk.shape, lambda i: (0, 0, 0)),
        ],
        out_specs=pl.BlockSpec((2 * PEER_HEADS, PEER_NKEYS, tr), lambda i: (0, 0, i)),
        out_shape=jax.ShapeDtypeStruct((2 * PEER_HEADS, PEER_NKEYS, n), f32),
        compiler_params=_cparams(("arbitrary",)),
        name="scores",
    )(h2, wq, sk)


def _route_kernel(s_ref, idx_ref, gate_ref):
    k = PEER_TOPK
    experts, gates = [], []
    for hd in range(PEER_HEADS):
        picked, g = _head_routing(_topk_rows(s_ref[2 * hd], k), _topk_rows(s_ref[2 * hd + 1], k))
        experts += picked
        gates += g
    idx_t = jnp.concatenate(experts, axis=0).astype(i32)
    gate_t = jnp.concatenate(gates, axis=0)
    for blk in range(idx_t.shape[1] // LANES):
        sl = slice(blk * LANES, (blk + 1) * LANES)
        idx_ref[sl, :] = idx_t[:, sl].T
        gate_ref[sl, :] = gate_t[:, sl].T


def _route(scores, n_tok):
    ncol = PEER_HEADS * PEER_TOPK
    return pl.pallas_call(
        _route_kernel,
        grid=(1,),
        in_specs=[pl.BlockSpec((2 * PEER_HEADS, PEER_NKEYS, n_tok), lambda i: (0, 0, 0))],
        out_specs=[pl.BlockSpec((n_tok, ncol), lambda i: (0, 0))] * 2,
        out_shape=[jax.ShapeDtypeStruct((n_tok, ncol), i32), jax.ShapeDtypeStruct((n_tok, ncol), f32)],
        compiler_params=_cparams(("arbitrary",)),
        name="route",
    )(scores)


_TS = 128
_HALF = 64
_TG = 4
_NSLOT = 4
_AHEAD = (_NSLOT - 1) * _TG
_NROW = PEER_HEADS * PEER_TOPK
_ROW_TILES = 2 * D_MODEL // LANES
_TP = 2
_RING = 3
_RING_ROWS = (_RING + 2) * _TS


def _split_bf16(x):
    hi = x.astype(bf16)
    return hi, (x - hi.astype(f32)).astype(bf16)


def _expert_kernel(sc_ref, h_ref, x1_ref, mod_ref, gfin_ref, rsum_ref, rrep_ref, idx01_ref, gate01_ref, uv_ref, o_ref,
                   *scratch):
    step = pl.program_id(0)
    last = pl.num_programs(0) - 1
    bufs = scratch[:_NSLOT]
    out_tiles, idx_stage, gate_stage, idx_rows, gate_ring, idx_ring, sem, aux = scratch[_NSLOT:]
    n_dma = _TG * _NROW
    k = PEER_TOPK
    cur = lax.rem(step, _RING)
    nxt = lax.rem(step + 2, _RING)
    mirror = jnp.where(nxt == 0, _RING * _TS, (_RING + 1) * _TS)

    def row_copy(e, slot, j):
        return pltpu.make_async_copy(uv_ref.at[e], bufs[slot].at[j], sem.at[slot])

    def wait(slot):
        pltpu.make_async_copy(uv_ref.at[pl.ds(0, n_dma)], bufs[slot], sem.at[slot]).wait()

    def ring_copies():
        return (pltpu.make_async_copy(idx_rows, idx_ring.at[pl.ds(nxt * _TS, _TS)], aux.at[0]),
                pltpu.make_async_copy(idx_rows, idx_ring.at[pl.ds(mirror, _TS)], aux.at[1]))

    @pl.when(step == 0)
    def _():
        prime = (pltpu.make_async_copy(idx01_ref, idx_ring.at[pl.ds(0, 2 * _TS)], aux.at[0]),
                 pltpu.make_async_copy(idx01_ref.at[pl.ds(0, _TS)], idx_ring.at[pl.ds(_RING * _TS, _TS)], aux.at[1]),
                 pltpu.make_async_copy(gate01_ref, gate_ring.at[pl.ds(0, 2 * _TS)], aux.at[2]))
        for c in prime:
            c.start()
        for c in prime:
            c.wait()
        for slot in range(_NSLOT - 1):
            def body(t, c, slot=slot):
                for r in range(_NROW):
                    row_copy(idx_ring[slot * _TG + t, r], slot, t * _NROW + r).start(priority=r % 2)
                return c
            lax.fori_loop(0, _TG, body, 0)

    gt_f = mod_ref[0, 5]
    gfin = gfin_ref[...]
    width = _TP * LANES
    r_i = lax.broadcasted_iota(i32, (width, width), 0) // LANES
    c_i = lax.broadcasted_iota(i32, (width, width), 1) // LANES
    lane_sum = jnp.where(r_i == c_i, 1.0, 0.0).astype(bf16)
    eye = jnp.where(lax.broadcasted_iota(i32, (_NROW, LANES), 0) == lax.broadcasted_iota(i32, (_NROW, LANES), 1),
                    1.0, 0.0)

    def evaluate(tok0, slot, t0):
        src = bufs[slot]
        z, vs = [], []
        for dt in range(_TP):
            uv = src[pl.ds((t0 + dt) * _NROW, _NROW)].astype(f32)
            vs.append(uv[:, SUBLANES:2 * SUBLANES, :])
            u = uv[:, 0:SUBLANES, :]
            z.append((u * h_ref[tok0 + dt][None]).reshape(_NROW * SUBLANES, LANES).astype(bf16))
        part = _dot(rsum_ref[...], jnp.concatenate(z, axis=1))
        p_hi, p_lo = _split_bf16(part)
        a = _dot(p_hi, lane_sum) + _dot(p_lo, lane_sum)
        g_diag = jnp.concatenate([eye * gate_ring[pl.ds(cur * _TS + tok0 + dt, 1), :] for dt in range(_TP)], axis=1)
        g = _dot(g_diag.astype(bf16), lane_sum)
        w = (g * _gelu(a)).astype(bf16)
        spread = _dot(rrep_ref[...], w)
        for dt in range(_TP):
            tok = tok0 + dt
            wt = spread[:, dt * LANES:(dt + 1) * LANES].reshape(_NROW, SUBLANES, LANES)
            y = jnp.sum(wt * vs[dt], axis=0)
            x2 = x1_ref[tok] + gt_f * y
            ms = jnp.sum(jnp.sum(x2 * x2, axis=1, keepdims=True), axis=0, keepdims=True) * (1.0 / D_MODEL)
            out_tiles[tok] = x2 * lax.rsqrt(ms + EPS) * gfin

    def half_step(half, carry):
        @pl.when((half == _TS // _HALF - 1) & (step > 0))
        def _():
            for c in ring_copies():
                c.wait()

        tok_h = half * _HALF
        ring_h = cur * _TS + tok_h
        head0 = half * (PEER_HEADS // 2)
        experts, gates, tops = [], [], {}
        pieces = []
        for hh in range(PEER_HEADS // 2):
            for p in range(2):
                def sub_keys(hh=hh, p=p):
                    tops[hh, p] = _topk_rows(sc_ref[(head0 + hh) * 2 + p], k)
                pieces.append(sub_keys)

            def pairs(hh=hh):
                picked, g = _head_routing(tops[hh, 0], tops[hh, 1])
                experts.extend(picked)
                gates.extend(g)
            pieces.append(pairs)

        def stage():
            rows = pl.ds(pl.multiple_of(head0 * k, _HALF), _HALF)
            idx_stage[rows, :] = jnp.concatenate(experts, axis=0).astype(i32)
            gate_stage[rows, :] = jnp.concatenate(gates, axis=0)
        pieces.append(stage)

        for grp in range(_HALF // _TG):
            slot = grp % _NSLOT
            wait(slot)
            if grp < len(pieces):
                pieces[grp]()
            dst = (slot + _NSLOT - 1) % _NSLOT
            for t0 in range(0, _TG, _TP):
                evaluate(tok_h + grp * _TG + t0, slot, t0)
                for t in range(t0, t0 + _TP):
                    for r in range(_NROW):
                        row_copy(idx_ring[ring_h + grp * _TG + t + _AHEAD, r], dst, t * _NROW + r).start(priority=r % 2)
        return carry

    lax.fori_loop(0, _TS // _HALF, half_step, 0)
    o_ref[...] = out_tiles[...].reshape(_TS, D_MODEL)
    idx_rows[...] = idx_stage[...].T
    gate_ring[pl.ds(pl.multiple_of(nxt * _TS, _TS), _TS), :] = gate_stage[...].T
    for c in ring_copies():
        c.start()

    @pl.when(step == last)
    def _():
        for c in ring_copies():
            c.wait()
        for slot in range(_NSLOT - 1):
            wait(slot)


def _experts(scores, idx01, gate01, h3, x13, mod4, gfin, uv, seq):
    n = h3.shape[0]
    n_steps = n // _TS
    steps_per_seq = seq // _TS
    tok3 = lambda i: (i, 0, 0)
    rsum = jnp.asarray(np.repeat(np.eye(_NROW, dtype=np.float32), SUBLANES, axis=1), dtype=bf16)
    buf = pltpu.VMEM((_TG * _NROW, _ROW_TILES, LANES), bf16)
    return pl.pallas_call(
        _expert_kernel,
        grid=(n_steps,),
        in_specs=[
            pl.BlockSpec((2 * PEER_HEADS, PEER_NKEYS, _TS), lambda i: (0, 0, jnp.minimum(i + 2, n_steps - 1))),
            pl.BlockSpec((_TS, SUBLANES, LANES), tok3),
            pl.BlockSpec((_TS, SUBLANES, LANES), tok3),
            pl.BlockSpec((1, 6, SUBLANES, LANES), lambda i: (i // steps_per_seq, 0, 0, 0)),
            pl.BlockSpec((SUBLANES, LANES), lambda i: (0, 0)),
            pl.BlockSpec(rsum.shape, lambda i: (0, 0)),
            pl.BlockSpec(rsum.shape[::-1], lambda i: (0, 0)),
            pl.BlockSpec(memory_space=pl.ANY),
            pl.BlockSpec(memory_space=pl.ANY),
            pl.BlockSpec(memory_space=pl.ANY),
        ],
        out_specs=pl.BlockSpec((_TS, D_MODEL), lambda i: (i, 0)),
        out_shape=jax.ShapeDtypeStruct((n, D_MODEL), f32),
        scratch_shapes=[buf] * _NSLOT + [
            pltpu.VMEM((_TS, SUBLANES, LANES), f32),
            pltpu.VMEM((_NROW, _TS), i32),
            pltpu.VMEM((_NROW, _TS), f32),
            pltpu.VMEM((_TS, _NROW), i32),
            pltpu.VMEM((_RING * _TS, _NROW), f32),
            pltpu.SMEM((_RING_ROWS, _NROW), i32),
            pltpu.SemaphoreType.DMA((_NSLOT,)),
            pltpu.SemaphoreType.DMA((3,)),
        ],
        compiler_params=_cparams(("arbitrary",)),
        name="experts",
    )(scores, h3, x13, mod4, gfin, rsum, rsum.T, idx01, gate01, uv)


def _rope_tables(seq):
    half = HEAD_DIM // 2
    inv = ROPE_THETA ** (-np.arange(half, dtype=np.float32) / half)
    ang = np.arange(seq, dtype=np.float32)[:, None] * inv[None, :].astype(np.float32)
    cos = np.cos(ang).astype(np.float32)
    sin = np.sin(ang).astype(np.float32)
    cos_t = np.tile(np.concatenate([cos, cos], axis=1), (1, NSA_HEADS))
    sin_t = np.tile(np.concatenate([-sin, sin], axis=1), (1, NSA_HEADS))
    return jnp.asarray(cos_t), jnp.asarray(sin_t)


def _selection_map(n_cmp_pad, n_sel):
    r_sel = SEL_BLOCK // CMP_STRIDE
    r_cmp = CMP_BLOCK // CMP_STRIDE
    i = np.arange(n_cmp_pad)[:, None]
    j = np.arange(n_sel)[None, :]
    d = i - r_sel * j
    cnt = np.minimum(d, r_sel - 1) - np.maximum(d - r_cmp + 1, 0) + 1
    cnt = np.clip(cnt, 0, None).astype(np.float32)
    cnt[n_cmp_pad - r_cmp + 1:] = 0.0
    return jnp.asarray(cnt.T)


def _block_expand(n_sel, seq):
    e = (np.arange(seq)[None, :] // SEL_BLOCK == np.arange(n_sel)[:, None]).astype(np.float32)
    return jnp.asarray(e, dtype=bf16)


def _compress_weights(w1):
    eye = jnp.eye(NSA_KV_GROUPS, dtype=w1.dtype)
    out = []
    for part in range(CMP_BLOCK // CMP_STRIDE):
        w = w1[part * CMP_STRIDE * HEAD_DIM:(part + 1) * CMP_STRIDE * HEAD_DIM].reshape(CMP_STRIDE, HEAD_DIM, CMP_HIDDEN)
        big = jnp.einsum('pdc,gh->pgdhc', w, eye).reshape(CMP_STRIDE * KV_WIDTH, NSA_KV_GROUPS * CMP_HIDDEN)
        out.append(big.astype(bf16))
    return out


def kernel(x, c, w_ada, b_ada, g_mix, g_ffn, w_in, cmp_pos_k, cmp_pos_v, w_ck1, w_ck2, w_cv1, w_cv2, hgrn_lb_logits, hgrn_out_norm, w_branch, w_out, w_peer_q, peer_sub_keys, peer_u, peer_v, g_final):
    bsz, seq, d = x.shape
    n = bsz * seq
    depth = w_ada.shape[0]
    assert depth == 1, "single-layer block only"
    n_sel = seq // SEL_BLOCK
    n_pieces = seq // CMP_STRIDE
    cos_t, sin_t = _rope_tables(seq)
    selmap = _selection_map(n_pieces, n_sel)
    expand = _block_expand(n_sel, seq)
    xcur = x.reshape(n, d)
    for l in range(depth):
        mod = _adaln(c, w_ada[l], b_ada[l].reshape(1, 6 * d))
        mod3 = mod.reshape(bsz, 6, d)
        w_pad = jnp.concatenate(
            [w_in[l][:, :_C_GL + 3 * NSA_HEADS], jnp.zeros((d, GATE_PAD - 3 * NSA_HEADS), w_in.dtype),
             w_in[l][:, _C_GL + 3 * NSA_HEADS:]], axis=1).astype(bf16)
        q, kc, vc, kvsw, gates, hq, hf, hi, hg, mg = _inproj(xcur, mod3, g_mix[l].reshape(1, d), w_pad, cos_t, sin_t, seq)
        wkt, wkb = _compress_weights(w_ck1[l])
        wvt, wvb = _compress_weights(w_cv1[l])
        kc_c, vc_c = _compress(
            kc.reshape(bsz, n_pieces, CMP_STRIDE * KV_WIDTH), vc.reshape(bsz, n_pieces, CMP_STRIDE * KV_WIDTH),
            wkt, wkb, wvt, wvb, cmp_pos_k[l].reshape(1, -1), cmp_pos_v[l].reshape(1, -1),
            w_ck1[l], w_cv1[l], w_ck2[l], w_cv2[l])
        o_nsa = _nsa(q, kvsw.reshape(bsz, seq, 4 * KV_WIDTH), kc_c, vc_c, gates, selmap, expand, seq)
        o_hgrn = _hgrn(hq, hf, hi, hg, hgrn_lb_logits, hgrn_out_norm[l].reshape(1, -1), seq, l)
        x1, h2, h2_b = _merge(o_nsa, o_hgrn, mg, xcur, mod3, g_ffn[l].reshape(1, d),
                              w_branch[l, 0].astype(bf16), w_branch[l, 1].astype(bf16), w_out[l].astype(bf16), seq)
        wq = w_peer_q[l].astype(bf16)
        sk = jnp.transpose(peer_sub_keys[l], (1, 0, 2, 3)).reshape(2 * PEER_HEADS, PEER_NKEYS, PEER_QDIM // 2).astype(bf16)
        scores = _scores(h2_b, wq, sk)
        idx01, gate01 = _route(scores, 2 * _TS)
        uv = jnp.concatenate([peer_u[l], peer_v[l]], axis=1).astype(bf16).reshape(-1, _ROW_TILES, LANES)
        xcur = _experts(scores, idx01, gate01, h2, x1, mod.reshape(bsz, 6, SUBLANES, LANES),
                        g_final.reshape(SUBLANES, LANES), uv, seq)
    return xcur.reshape(bsz, seq, d)
```

```python
import functools

import numpy as np
import jax
import jax.numpy as jnp
from jax import lax
from jax.experimental import pallas as pl
from jax.experimental.pallas import tpu as pltpu

f32 = jnp.float32
bf16 = jnp.bfloat16
i32 = jnp.int32
_HIGHEST = lax.Precision.HIGHEST

D_MODEL = 1024
EPS = 1e-6
HEAD_DIM = 64
ROPE_THETA = 10000.0
NSA_HEADS = 8
NSA_KV_GROUPS = 2
NSA_GROUP = NSA_HEADS // NSA_KV_GROUPS
NSA_WIDTH = NSA_HEADS * HEAD_DIM
KV_WIDTH = NSA_KV_GROUPS * HEAD_DIM
CMP_BLOCK = 32
CMP_STRIDE = 16
CMP_HIDDEN = 128
SEL_BLOCK = 64
SEL_TOPK = 16
WINDOW = 512
FORCE_BONUS = 1000.0
HGRN_HEADS = 4
HGRN_EXPAND = 128
HGRN_HEAD_V = 128
HGRN_WIDTH = HGRN_HEADS * HGRN_EXPAND
HGRN_CHUNK = 32
PEER_HEADS = 8
PEER_NKEYS = 128
PEER_QDIM = 256
PEER_TOPK = 16
GATE_PAD = 128

LANES = 128
SUBLANES = 8
VMEM_LIMIT = 56 * 1024 * 1024

_NEG = -1e30


def _cparams(sem):
    return pltpu.CompilerParams(dimension_semantics=sem, vmem_limit_bytes=VMEM_LIMIT)


def _gelu(x):
    return 0.5 * x * (1.0 + jnp.tanh(0.7978845608028654 * (x + 0.044715 * (x * x * x))))


def _dot(a, b, **kw):
    return jnp.dot(a, b, preferred_element_type=f32, **kw)


def _dot_nt(a, b):
    return lax.dot_general(a, b, (((1,), (1,)), ((), ())), preferred_element_type=f32)


def _rms(x, g):
    return x * lax.rsqrt(jnp.mean(x * x, axis=-1, keepdims=True) + EPS) * g


def _adaln_kernel(c_ref, w_ref, b_ref, o_ref):
    c = c_ref[...]
    cs = c * jax.nn.sigmoid(c)
    o_ref[...] = _dot(cs, w_ref[...], precision=_HIGHEST) + b_ref[...]


def _adaln(c, w, b):
    bsz = c.shape[0]
    tn = 512
    return pl.pallas_call(
        _adaln_kernel,
        grid=(6 * D_MODEL // tn,),
        in_specs=[
            pl.BlockSpec((bsz, D_MODEL), lambda j: (0, 0)),
            pl.BlockSpec((D_MODEL, tn), lambda j: (0, j)),
            pl.BlockSpec((1, tn), lambda j: (0, j)),
        ],
        out_specs=pl.BlockSpec((bsz, tn), lambda j: (0, j)),
        out_shape=jax.ShapeDtypeStruct((bsz, 6 * D_MODEL), f32),
        compiler_params=_cparams(("arbitrary",)),
        name="adaln",
    )(c, w, b)


_C_Q = 0
_C_KC = 512
_C_VC = 640
_C_KS = 768
_C_VS = 896
_C_KW = 1024
_C_VW = 1152
_C_GL = 1280
_C_HQ = _C_GL + GATE_PAD
_C_HF = _C_HQ + 512
_C_HI = _C_HF + 512
_C_HG = _C_HI + 512
_C_MG = _C_HG + 512
_C_END = _C_MG + 2 * D_MODEL


def _inproj_kernel(x_ref, mod_ref, g_ref, w_ref, cos_ref, sin_ref,
                   q_ref, kc_ref, vc_ref, kvsw_ref, gate_ref, hq_ref, hf_ref, hi_ref, hg_ref, mg_ref):
    x = x_ref[...]
    sh = mod_ref[0, 0:1, :]
    sc = mod_ref[0, 1:2, :]
    h = (_rms(x, g_ref[...]) * (1.0 + sc) + sh).astype(bf16)

    def mm(c0, c1):
        return _dot(h, w_ref[:, c0:c1])

    cos = cos_ref[...]
    sin = sin_ref[...]

    def rope(a):
        width = a.shape[1]
        first = (lax.broadcasted_iota(i32, a.shape, 1) & (HEAD_DIM - 1)) < (HEAD_DIM // 2)
        partner = jnp.where(first, pltpu.roll(a, width - HEAD_DIM // 2, 1), pltpu.roll(a, HEAD_DIM // 2, 1))
        return a * cos[:, :width] + partner * sin[:, :width]

    q_ref[...] = (rope(mm(_C_Q, _C_KC)) * (HEAD_DIM ** -0.5)).astype(bf16)
    kc_ref[...] = rope(mm(_C_KC, _C_VC)).astype(bf16)
    vc_ref[...] = mm(_C_VC, _C_KS).astype(bf16)
    kvsw_ref[:, 0:128] = rope(mm(_C_KS, _C_VS)).astype(bf16)
    kvsw_ref[:, 128:256] = mm(_C_VS, _C_KW).astype(bf16)
    kvsw_ref[:, 256:384] = rope(mm(_C_KW, _C_VW)).astype(bf16)
    kvsw_ref[:, 384:512] = mm(_C_VW, _C_GL).astype(bf16)
    gate_ref[...] = jax.nn.sigmoid(mm(_C_GL, _C_HQ))
    hq_ref[...] = mm(_C_HQ, _C_HF).astype(bf16)
    hf_ref[...] = mm(_C_HF, _C_HI)
    hi_ref[...] = mm(_C_HI, _C_HG).astype(bf16)
    hg_ref[...] = mm(_C_HG, _C_MG).astype(bf16)
    mg_ref[...] = jax.nn.sigmoid(mm(_C_MG, _C_END)).astype(bf16)


def _inproj(x2, mod3, g_mix, w_pad, cos_t, sin_t, seq):
    n = x2.shape[0]
    tm = 256
    tiles_per_seq = seq // tm
    row = lambda i: (i, 0)
    outs = [
        (NSA_WIDTH, bf16), (KV_WIDTH, bf16), (KV_WIDTH, bf16), (4 * KV_WIDTH, bf16), (GATE_PAD, f32),
        (HGRN_WIDTH, bf16), (HGRN_WIDTH, f32), (HGRN_WIDTH, bf16), (HGRN_WIDTH, bf16), (2 * D_MODEL, bf16),
    ]
    return pl.pallas_call(
        _inproj_kernel,
        grid=(n // tm,),
        in_specs=[
            pl.BlockSpec((tm, D_MODEL), row),
            pl.BlockSpec((1, 6, D_MODEL), lambda i: (i // tiles_per_seq, 0, 0)),
            pl.BlockSpec((1, D_MODEL), lambda i: (0, 0)),
            pl.BlockSpec((D_MODEL, _C_END), lambda i: (0, 0)),
            pl.BlockSpec((tm, NSA_WIDTH), lambda i: (i % tiles_per_seq, 0)),
            pl.BlockSpec((tm, NSA_WIDTH), lambda i: (i % tiles_per_seq, 0)),
        ],
        out_specs=[pl.BlockSpec((tm, w), row) for w, _ in outs],
        out_shape=[jax.ShapeDtypeStruct((n, w), dt) for w, dt in outs],
        compiler_params=_cparams(("arbitrary",)),
        name="inproj",
    )(x2, mod3, g_mix, w_pad, cos_t, sin_t)


def _compress_kernel(kc_ref, vc_ref, wkt_ref, wkb_ref, wvt_ref, wvb_ref, pk_ref, pv_ref,
                     w1k_ref, w1v_ref, w2k_ref, w2v_ref, okc_ref, ovc_ref):
    def one(x_ref, wt_ref, wb_ref, pos_ref, w1_ref, w2_ref, o_ref):
        pieces = x_ref[0]
        top = _dot(pieces, wt_ref[...])
        bot = _dot(pieces, wb_ref[...])
        nrow = bot.shape[0]
        bot = pltpu.roll(bot, nrow - 1, 0)
        cpos = _dot(pos_ref[...], w1_ref[...], precision=_HIGHEST)
        w2 = w2_ref[...].astype(bf16)
        outs = []
        for g in range(NSA_KV_GROUPS):
            sl = slice(g * CMP_HIDDEN, (g + 1) * CMP_HIDDEN)
            hid = _gelu(top[:, sl] + bot[:, sl] + cpos)
            outs.append(_dot(hid.astype(bf16), w2))
        o_ref[0] = jnp.concatenate(outs, axis=1).astype(bf16)

    one(kc_ref, wkt_ref, wkb_ref, pk_ref, w1k_ref, w2k_ref, okc_ref)
    one(vc_ref, wvt_ref, wvb_ref, pv_ref, w1v_ref, w2v_ref, ovc_ref)


def _compress(kc3, vc3, wkt, wkb, wvt, wvb, pk, pv, w1k, w1v, w2k, w2v):
    bsz, npieces, width = kc3.shape
    full = lambda a: pl.BlockSpec(a.shape, lambda b: (0,) * a.ndim)
    per_b = pl.BlockSpec((1, npieces, width), lambda b: (b, 0, 0))
    out_b = pl.BlockSpec((1, npieces, KV_WIDTH), lambda b: (b, 0, 0))
    return pl.pallas_call(
        _compress_kernel,
        grid=(bsz,),
        in_specs=[per_b, per_b] + [full(a) for a in (wkt, wkb, wvt, wvb, pk, pv, w1k, w1v, w2k, w2v)],
        out_specs=[out_b, out_b],
        out_shape=[jax.ShapeDtypeStruct((bsz, npieces, KV_WIDTH), bf16)] * 2,
        compiler_params=_cparams(("arbitrary",)),
        name="compress",
    )(kc3, vc3, wkt, wkb, wvt, wvb, pk, pv, w1k, w1v, w2k, w2v)


_TQ = 128
_NSA_CLASSES = 4


def _softmax_parts(s, mask):
    sm = jnp.where(mask, s, _NEG)
    m = jnp.max(sm, axis=-1, keepdims=True)
    p = jnp.where(mask, jnp.exp(sm - m), 0.0)
    den = jnp.maximum(jnp.sum(p, axis=-1, keepdims=True), 1e-30)
    return p, den


def _softmax_plain(s):
    p = jnp.exp(s - jnp.max(s, axis=-1, keepdims=True))
    return p, jnp.sum(p, axis=-1, keepdims=True)


def _nsa_kernel(q_ref, kvsw_ref, kc_ref, vc_ref, gate_ref, selmap_ref, expand_ref, o_ref, *, seq):
    tq = _TQ
    qi = pl.program_id(1)
    n_cls = _NSA_CLASSES
    per_cls = (seq // tq) // n_cls
    for cls in range(n_cls):
        @pl.when((qi >= cls * per_cls) & (qi < (cls + 1) * per_cls))
        def _(cls=cls):
            _nsa_tile(q_ref, kvsw_ref, kc_ref, vc_ref, gate_ref, selmap_ref, expand_ref, o_ref,
                      seq=seq, width=(cls + 1) * (seq // n_cls))


def _nsa_tile(q_ref, kvsw_ref, kc_ref, vc_ref, gate_ref, selmap_ref, expand_ref, o_ref, *, seq, width):
    tq = _TQ
    rows = NSA_GROUP * tq
    n_sel = seq // SEL_BLOCK
    n_cmp_pad = seq // CMP_STRIDE
    t0 = pl.program_id(1) * tq
    q = q_ref[...]
    gates = gate_ref[...]
    tlane = t0 + lax.broadcasted_iota(i32, (1, tq), 1)
    t4 = t0 + (lax.broadcasted_iota(i32, (rows, 1), 0) & (tq - 1))
    wk = WINDOW + tq
    ws = pl.multiple_of(jnp.maximum(t0 - WINDOW, 0), tq)
    ks_all = kvsw_ref[0, 0:width, 0:128]
    vs_all = kvsw_ref[0, 0:width, 128:256]
    kw_all = kvsw_ref[0, pl.ds(ws, wk), 256:384]
    vw_all = kvsw_ref[0, pl.ds(ws, wk), 384:512]
    kc_all = kc_ref[0]
    vc_all = vc_ref[0]
    cend = lax.broadcasted_iota(i32, (1, n_cmp_pad), 1) * CMP_STRIDE + (CMP_BLOCK - 1)
    blk = lax.broadcasted_iota(i32, (n_sel, 1), 0)
    kpos = lax.broadcasted_iota(i32, (1, width), 1)
    kpos_w = ws + lax.broadcasted_iota(i32, (1, wk), 1)
    cur = tlane >> 6
    forced = (blk == 0) | (blk == cur) | (blk == cur - 1)
    causal_blk = blk * SEL_BLOCK <= tlane
    pieces = []
    for g in range(NSA_KV_GROUPS):
        gs = slice(g * HEAD_DIM, (g + 1) * HEAD_DIM)
        qg = jnp.concatenate(
            [q[:, (NSA_GROUP * g + r) * HEAD_DIM:(NSA_GROUP * g + r + 1) * HEAD_DIM] for r in range(NSA_GROUP)], axis=0)
        p_c, den_c = _softmax_parts(_dot_nt(qg, kc_all[:, gs]), cend <= t4)
        o_c = _dot(p_c.astype(bf16), vc_all[:, gs]) / den_c
        pn = p_c / den_c
        pc_sum = pn[0:tq]
        for r in range(1, NSA_GROUP):
            pc_sum = pc_sum + pn[r * tq:(r + 1) * tq]
        imp = lax.dot_general(selmap_ref[...], pc_sum, (((1,), (1,)), ((), ())),
                              preferred_element_type=f32, precision=_HIGHEST)
        imp = jnp.where(forced, imp + FORCE_BONUS, imp)
        imp = jnp.where(causal_blk, imp, -1.0)
        rank = jnp.zeros((n_sel, tq), f32)
        for j in range(n_sel):
            row = imp[j:j + 1, :]
            ahead = jnp.where(row > imp, 1.0, jnp.where(row == imp, jnp.where(blk > j, 1.0, 0.0), 0.0))
            rank = rank + ahead
        bias = jnp.where(rank < float(min(SEL_TOPK, n_sel)), jnp.where(causal_blk, 0.0, _NEG), _NEG).astype(bf16)
        bias_q = lax.dot_general(bias, expand_ref[:, 0:width], (((0,), (0,)), ((), ())),
                                 preferred_element_type=f32)
        bias4 = jnp.concatenate([bias_q] * NSA_GROUP, axis=0)
        p_s, den_s = _softmax_plain(jnp.where(kpos <= t4, _dot_nt(qg, ks_all[:, gs]) + bias4, _NEG))
        o_s = _dot(p_s.astype(bf16), vs_all[:, gs]) / den_s
        in_window = (t4 - kpos_w).astype(jnp.uint32) < WINDOW
        p_w, den_w = _softmax_plain(jnp.where(in_window, _dot_nt(qg, kw_all[:, gs]), _NEG))
        o_w = _dot(p_w.astype(bf16), vw_all[:, gs]) / den_w

        def gcol(br):
            return jnp.concatenate(
                [gates[:, (NSA_GROUP * g + r) * 3 + br:(NSA_GROUP * g + r) * 3 + br + 1] for r in range(NSA_GROUP)], axis=0)

        o = gcol(0) * o_c + gcol(1) * o_s + gcol(2) * o_w
        pieces += [o[r * tq:(r + 1) * tq] for r in range(NSA_GROUP)]
    o_ref[...] = jnp.concatenate(pieces, axis=1).astype(bf16)


def _nsa(q, kvsw3, kc3, vc3, gates, selmap, expand, seq):
    n = q.shape[0]
    bsz = n // seq
    tq = _TQ
    nq = seq // tq
    return pl.pallas_call(
        functools.partial(_nsa_kernel, seq=seq),
        grid=(bsz, nq),
        in_specs=[
            pl.BlockSpec((tq, NSA_WIDTH), lambda b, i: (b * nq + i, 0)),
            pl.BlockSpec((1, seq, 4 * KV_WIDTH), lambda b, i: (b, 0, 0)),
            pl.BlockSpec((1,) + kc3.shape[1:], lambda b, i: (b, 0, 0)),
            pl.BlockSpec((1,) + vc3.shape[1:], lambda b, i: (b, 0, 0)),
            pl.BlockSpec((tq, GATE_PAD), lambda b, i: (b * nq + i, 0)),
            pl.BlockSpec(selmap.shape, lambda b, i: (0, 0)),
            pl.BlockSpec(expand.shape, lambda b, i: (0, 0)),
        ],
        out_specs=pl.BlockSpec((tq, NSA_WIDTH), lambda b, i: (b * nq + i, 0)),
        out_shape=jax.ShapeDtypeStruct((n, NSA_WIDTH), bf16),
        compiler_params=_cparams(("arbitrary", "arbitrary")),
        name="nsa",
    )(q, kvsw3, kc3, vc3, gates, selmap, expand)


def _hgrn_kernel(hq_ref, hf_ref, hi_ref, hg_ref, lbl_ref, ng_ref, o_ref, st_ref, *, seq, layer):
    c = HGRN_CHUNK
    logits = lbl_ref[...]
    e = jnp.exp(logits - jnp.max(logits, axis=0, keepdims=True))
    sm = e / jnp.sum(e, axis=0, keepdims=True)
    lb = sm[0:1]
    for l in range(1, layer + 1):
        lb = lb + sm[l:l + 1]
    st_ref[...] = jnp.zeros_like(st_ref)
    r_i = lax.broadcasted_iota(i32, (c, c), 0)
    c_i = lax.broadcasted_iota(i32, (c, c), 1)
    causal = r_i >= c_i
    r_col = lax.broadcasted_iota(i32, (c, 1), 0)
    ng = ng_ref[...]

    def body(ci, carry):
        r0 = pl.multiple_of(ci * c, c)
        hq = hq_ref[pl.ds(r0, c), :].astype(f32)
        hf = hf_ref[pl.ds(r0, c), :]
        v = hi_ref[pl.ds(r0, c), :]
        hg = hg_ref[pl.ds(r0, c), :].astype(f32)
        f = lb + (1.0 - lb) * jax.nn.sigmoid(hf)
        k = 1.0 - f
        qv = hq * jax.nn.sigmoid(hq)
        bcum = jnp.log(f)
        for d in (1, 2, 4, 8, 16):
            bcum = bcum + jnp.where(r_col >= d, pltpu.roll(bcum, d, 0), 0.0)
        bend = bcum[c - 1:c, :]
        q_dec = (qv * jnp.exp(bcum)).astype(bf16)
        k_inv = (k * jnp.exp(-bcum)).astype(bf16)
        k_end = (k * jnp.exp(bend - bcum)).astype(bf16)
        dec_end = jnp.exp(bend)
        outs = []
        for h in range(HGRN_HEADS):
            sl = slice(h * HGRN_EXPAND, (h + 1) * HGRN_EXPAND)
            a = jnp.where(causal, _dot_nt(q_dec[:, sl], k_inv[:, sl]), 0.0)
            st = st_ref[h]
            o = _dot(a.astype(bf16), v[:, sl]) + _dot_nt(q_dec[:, sl], st.astype(bf16))
            upd = lax.dot_general(v[:, sl], k_end[:, sl], (((0,), (0,)), ((), ())), preferred_element_type=f32)
            st_ref[h] = st * dec_end[:, sl] + upd
            y = _rms(o, ng) * (hg[:, sl] * jax.nn.sigmoid(hg[:, sl]))
            outs.append(y)
        o_ref[pl.ds(r0, c), :] = jnp.concatenate(outs, axis=1).astype(bf16)
        return carry

    lax.fori_loop(0, seq // c, body, 0, unroll=8)


def _hgrn(hq, hf, hi, hg, lb_logits, norm_g, seq, layer):
    n = hq.shape[0]
    bsz = n // seq
    per_b = pl.BlockSpec((seq, HGRN_WIDTH), lambda b: (b, 0))
    return pl.pallas_call(
        functools.partial(_hgrn_kernel, seq=seq, layer=layer),
        grid=(bsz,),
        in_specs=[per_b, per_b, per_b, per_b,
                  pl.BlockSpec(lb_logits.shape, lambda b: (0, 0)),
                  pl.BlockSpec(norm_g.shape, lambda b: (0, 0))],
        out_specs=per_b,
        out_shape=jax.ShapeDtypeStruct((n, HGRN_WIDTH), bf16),
        scratch_shapes=[pltpu.VMEM((HGRN_HEADS, HGRN_HEAD_V, HGRN_EXPAND), f32)],
        compiler_params=_cparams(("arbitrary",)),
        name="hgrn",
    )(hq, hf, hi, hg, lb_logits, norm_g)


def _merge_kernel(on_ref, oh_ref, mg_ref, x_ref, mod_ref, g_ref, wb0_ref, wb1_ref, wo_ref, x1_ref, h2_ref, hb_ref):
    a = _dot(on_ref[...], wb0_ref[...])
    b = _dot(oh_ref[...], wb1_ref[...])
    mg = mg_ref[...].astype(f32)
    y = mg[:, :D_MODEL] * a + mg[:, D_MODEL:] * b
    y2 = _dot(y.astype(bf16), wo_ref[...])
    x1 = x_ref[...] + mod_ref[0, 2:3, :] * y2
    h2 = _rms(x1, g_ref[...]) * (1.0 + mod_ref[0, 4:5, :]) + mod_ref[0, 3:4, :]
    rows = x1.shape[0]
    x1_ref[...] = x1.reshape(rows, SUBLANES, LANES)
    h2_ref[...] = h2.reshape(rows, SUBLANES, LANES)
    hb_ref[...] = h2.astype(bf16)


def _merge(o_nsa, o_hgrn, mg, x2, mod3, g_ffn, wb0, wb1, wo, seq):
    n = x2.shape[0]
    tm = 256
    tiles_per_seq = seq // tm
    row = lambda i: (i, 0)
    full = lambda a: pl.BlockSpec(a.shape, lambda i: (0,) * a.ndim)
    return pl.pallas_call(
        _merge_kernel,
        grid=(n // tm,),
        in_specs=[
            pl.BlockSpec((tm, NSA_WIDTH), row), pl.BlockSpec((tm, HGRN_WIDTH), row),
            pl.BlockSpec((tm, 2 * D_MODEL), row), pl.BlockSpec((tm, D_MODEL), row),
            pl.BlockSpec((1, 6, D_MODEL), lambda i: (i // tiles_per_seq, 0, 0)),
            full(g_ffn), full(wb0), full(wb1), full(wo),
        ],
        out_specs=[pl.BlockSpec((tm, SUBLANES, LANES), lambda i: (i, 0, 0))] * 2 + [pl.BlockSpec((tm, D_MODEL), row)],
        out_shape=[jax.ShapeDtypeStruct((n, SUBLANES, LANES), f32)] * 2 + [jax.ShapeDtypeStruct((n, D_MODEL), bf16)],
        compiler_params=_cparams(("arbitrary",)),
        name="merge",
    )(o_nsa, o_hgrn, mg, x2, mod3, g_ffn, wb0, wb1, wo)


_TR = 256


def _topk_rows(s, k, payload=None):
    n = s.shape[0]
    rowid = lax.broadcasted_iota(i32, s.shape, 0).astype(f32)
    vals, picks = [], []
    for _ in range(k):
        m = jnp.max(s, axis=0, keepdims=True)
        i = jnp.min(jnp.where(s == m, rowid, float(n)), axis=0, keepdims=True)
        hit = rowid == i
        vals.append(m)
        picks.append(i if payload is None else jnp.max(jnp.where(hit, payload, -1.0), axis=0, keepdims=True))
        s = jnp.where(hit, -jnp.inf, s)
    return vals, picks


def _pair_candidates(v1, i1, v2, i2):
    k = len(v1)
    s2 = jnp.concatenate(v2, axis=0)
    e2 = jnp.concatenate(i2, axis=0)
    sub = lax.broadcasted_iota(i32, (SUBLANES, s2.shape[1]), 0)
    comb, cand = [], []
    for a in range(k // 2):
        nb = k // (a + 1)
        rows = -(-nb // SUBLANES) * SUBLANES
        c = v1[a] + s2[0:rows]
        if nb < rows:
            c = jnp.where(sub < nb, c, -jnp.inf)
        comb.append(c)
        cand.append(i1[a] * float(PEER_NKEYS) + e2[0:rows])
    comb.append(jnp.concatenate(v1[k // 2:], axis=0) + s2[0:1])
    cand.append(jnp.concatenate(i1[k // 2:], axis=0) * float(PEER_NKEYS) + e2[0:1])
    return jnp.concatenate(comb, axis=0), jnp.concatenate(cand, axis=0)


def _head_routing(tops1, tops2):
    k = PEER_TOPK
    comb, cand = _pair_candidates(tops1[0], tops1[1], tops2[0], tops2[1])
    tv, picked = _topk_rows(comb, k, payload=cand)
    ex = [jnp.exp(tv[r] - tv[0]) for r in range(k)]
    den = ex[0]
    for r in range(1, k):
        den = den + ex[r]
    return picked, [e / den for e in ex]


def _scores_kernel(h_ref, wq_ref, sk_ref, o_ref):
    q = _dot(h_ref[...], wq_ref[...]).astype(bf16)
    half = PEER_QDIM // 2
    for grp in range(2 * PEER_HEADS):
        o_ref[grp] = _dot_nt(sk_ref[grp], q[:, grp * half:(grp + 1) * half])


def _scores(h2, wq, sk):
    n = h2.shape[0]
    tr = _TR
    return pl.pallas_call(
        _scores_kernel,
        grid=(n // tr,),
        in_specs=[
            pl.BlockSpec((tr, D_MODEL), lambda i: (i, 0)),
            pl.BlockSpec(wq.shape, lambda i: (0, 0)),
            pl.BlockSpec(sk.shape, lambda i: (0, 0, 0)),
        ],
        out_specs=pl.BlockSpec((2 * PEER_HEADS, PEER_NKEYS, tr), lambda i: (0, 0, i)),
        out_shape=jax.ShapeDtypeStruct((2 * PEER_HEADS, PEER_NKEYS, n), f32),
        compiler_params=_cparams(("arbitrary",)),
        name="scores",
    )(h2, wq, sk)


def _route_kernel(s_ref, idx_ref, gate_ref):
    k = PEER_TOPK
    experts, gates = [], []
    for hd in range(PEER_HEADS):
        picked, g = _head_routing(_topk_rows(s_ref[2 * hd], k), _topk_rows(s_ref[2 * hd + 1], k))
        experts += picked
        gates += g
    idx_t = jnp.concatenate(experts, axis=0).astype(i32)
    gate_t = jnp.concatenate(gates, axis=0)
    for blk in range(idx_t.shape[1] // LANES):
        sl = slice(blk * LANES, (blk + 1) * LANES)
        idx_ref[sl, :] = idx_t[:, sl].T
        gate_ref[sl, :] = gate_t[:, sl].T


def _route(scores, n_tok):
    ncol = PEER_HEADS * PEER_TOPK
    return pl.pallas_call(
        _route_kernel,
        grid=(1,),
        in_specs=[pl.BlockSpec((2 * PEER_HEADS, PEER_NKEYS, n_tok), lambda i: (0, 0, 0))],
        out_specs=[pl.BlockSpec((n_tok, ncol), lambda i: (0, 0))] * 2,
        out_shape=[jax.ShapeDtypeStruct((n_tok, ncol), i32), jax.ShapeDtypeStruct((n_tok, ncol), f32)],
        compiler_params=_cparams(("arbitrary",)),
        name="route",
    )(scores)


_TS = 128
_HALF = 64
_TG = 4
_NSLOT = 4
_AHEAD = (_NSLOT - 1) * _TG
_NROW = PEER_HEADS * PEER_TOPK
_ROW_TILES = 2 * D_MODEL // LANES
_TP = 2
_RING = 3
_RING_ROWS = (_RING + 2) * _TS


def _split_bf16(x):
    hi = x.astype(bf16)
    return hi, (x - hi.astype(f32)).astype(bf16)


def _expert_kernel(sc_ref, h_ref, x1_ref, mod_ref, gfin_ref, rsum_ref, rrep_ref, idx01_ref, gate01_ref, uv_ref, o_ref,
                   *scratch):
    step = pl.program_id(0)
    last = pl.num_programs(0) - 1
    bufs = scratch[:_NSLOT]
    out_tiles, idx_stage, gate_stage, idx_rows, gate_ring, idx_ring, sem, aux = scratch[_NSLOT:]
    n_dma = _TG * _NROW
    k = PEER_TOPK
    cur = lax.rem(step, _RING)
    nxt = lax.rem(step + 2, _RING)
    mirror = jnp.where(nxt == 0, _RING * _TS, (_RING + 1) * _TS)

    def row_copy(e, slot, j):
        return pltpu.make_async_copy(uv_ref.at[e], bufs[slot].at[j], sem.at[slot])

    def wait(slot):
        pltpu.make_async_copy(uv_ref.at[pl.ds(0, n_dma)], bufs[slot], sem.at[slot]).wait()

    def ring_copies():
        return (pltpu.make_async_copy(idx_rows, idx_ring.at[pl.ds(nxt * _TS, _TS)], aux.at[0]),
                pltpu.make_async_copy(idx_rows, idx_ring.at[pl.ds(mirror, _TS)], aux.at[1]))

    @pl.when(step == 0)
    def _():
        prime = (pltpu.make_async_copy(idx01_ref, idx_ring.at[pl.ds(0, 2 * _TS)], aux.at[0]),
                 pltpu.make_async_copy(idx01_ref.at[pl.ds(0, _TS)], idx_ring.at[pl.ds(_RING * _TS, _TS)], aux.at[1]),
                 pltpu.make_async_copy(gate01_ref, gate_ring.at[pl.ds(0, 2 * _TS)], aux.at[2]))
        for c in prime:
            c.start()
        for c in prime:
            c.wait()
        for slot in range(_NSLOT - 1):
            def body(t, c, slot=slot):
                for r in range(_NROW):
                    row_copy(idx_ring[slot * _TG + t, r], slot, t * _NROW + r).start(priority=r % 2)
                return c
            lax.fori_loop(0, _TG, body, 0)

    gt_f = mod_ref[0, 5]
    gfin = gfin_ref[...]
    width = _TP * LANES
    r_i = lax.broadcasted_iota(i32, (width, width), 0) // LANES
    c_i = lax.broadcasted_iota(i32, (width, width), 1) // LANES
    lane_sum = jnp.where(r_i == c_i, 1.0, 0.0).astype(bf16)
    eye = jnp.where(lax.broadcasted_iota(i32, (_NROW, LANES), 0) == lax.broadcasted_iota(i32, (_NROW, LANES), 1),
                    1.0, 0.0)

    def evaluate(tok0, slot, t0):
        src = bufs[slot]
        z, vs = [], []
        for dt in range(_TP):
            uv = src[pl.ds((t0 + dt) * _NROW, _NROW)].astype(f32)
            vs.append(uv[:, SUBLANES:2 * SUBLANES, :])
            u = uv[:, 0:SUBLANES, :]
            z.append((u * h_ref[tok0 + dt][None]).reshape(_NROW * SUBLANES, LANES).astype(bf16))
        part = _dot(rsum_ref[...], jnp.concatenate(z, axis=1))
        p_hi, p_lo = _split_bf16(part)
        a = _dot(p_hi, lane_sum) + _dot(p_lo, lane_sum)
        g_diag = jnp.concatenate([eye * gate_ring[pl.ds(cur * _TS + tok0 + dt, 1), :] for dt in range(_TP)], axis=1)
        g = _dot(g_diag.astype(bf16), lane_sum)
        w = (g * _gelu(a)).astype(bf16)
        spread = _dot(rrep_ref[...], w)
        for dt in range(_TP):
            tok = tok0 + dt
            wt = spread[:, dt * LANES:(dt + 1) * LANES].reshape(_NROW, SUBLANES, LANES)
            y = jnp.sum(wt * vs[dt], axis=0)
            x2 = x1_ref[tok] + gt_f * y
            ms = jnp.sum(jnp.sum(x2 * x2, axis=1, keepdims=True), axis=0, keepdims=True) * (1.0 / D_MODEL)
            out_tiles[tok] = x2 * lax.rsqrt(ms + EPS) * gfin

    def half_step(half, carry):
        @pl.when((half == _TS // _HALF - 1) & (step > 0))
        def _():
            for c in ring_copies():
                c.wait()

        tok_h = half * _HALF
        ring_h = cur * _TS + tok_h
        head0 = half * (PEER_HEADS // 2)
        experts, gates, tops = [], [], {}
        pieces = []
        for hh in range(PEER_HEADS // 2):
            for p in range(2):
                def sub_keys(hh=hh, p=p):
                    tops[hh, p] = _topk_rows(sc_ref[(head0 + hh) * 2 + p], k)
                pieces.append(sub_keys)

            def pairs(hh=hh):
                picked, g = _head_routing(tops[hh, 0], tops[hh, 1])
                experts.extend(picked)
                gates.extend(g)
            pieces.append(pairs)

        def stage():
            rows = pl.ds(pl.multiple_of(head0 * k, _HALF), _HALF)
            idx_stage[rows, :] = jnp.concatenate(experts, axis=0).astype(i32)
            gate_stage[rows, :] = jnp.concatenate(gates, axis=0)
        pieces.append(stage)

        for grp in range(_HALF // _TG):
            slot = grp % _NSLOT
            wait(slot)
            if grp < len(pieces):
                pieces[grp]()
            dst = (slot + _NSLOT - 1) % _NSLOT
            for t0 in range(0, _TG, _TP):
                evaluate(tok_h + grp * _TG + t0, slot, t0)
                for t in range(t0, t0 + _TP):
                    for r in range(_NROW):
                        row_copy(idx_ring[ring_h + grp * _TG + t + _AHEAD, r], dst, t * _NROW + r).start(priority=r % 2)
        return carry

    lax.fori_loop(0, _TS // _HALF, half_step, 0)
    o_ref[...] = out_tiles[...].reshape(_TS, D_MODEL)
    idx_rows[...] = idx_stage[...].T
    gate_ring[pl.ds(pl.multiple_of(nxt * _TS, _TS), _TS), :] = gate_stage[...].T
    for c in ring_copies():
        c.start()

    @pl.when(step == last)
    def _():
        for c in ring_copies():
            c.wait()
        for slot in range(_NSLOT - 1):
            wait(slot)


def _experts(scores, idx01, gate01, h3, x13, mod4, gfin, uv, seq):
    n = h3.shape[0]
    n_steps = n // _TS
    steps_per_seq = seq // _TS
    tok3 = lambda i: (i, 0, 0)
    rsum = jnp.asarray(np.repeat(np.eye(_NROW, dtype=np.float32), SUBLANES, axis=1), dtype=bf16)
    buf = pltpu.VMEM((_TG * _NROW, _ROW_TILES, LANES), bf16)
    return pl.pallas_call(
        _expert_kernel,
        grid=(n_steps,),
        in_specs=[
            pl.BlockSpec((2 * PEER_HEADS, PEER_NKEYS, _TS), lambda i: (0, 0, jnp.minimum(i + 2, n_steps - 1))),
            pl.BlockSpec((_TS, SUBLANES, LANES), tok3),
            pl.BlockSpec((_TS, SUBLANES, LANES), tok3),
            pl.BlockSpec((1, 6, SUBLANES, LANES), lambda i: (i // steps_per_seq, 0, 0, 0)),
            pl.BlockSpec((SUBLANES, LANES), lambda i: (0, 0)),
            pl.BlockSpec(rsum.shape, lambda i: (0, 0)),
            pl.BlockSpec(rsum.shape[::-1], lambda i: (0, 0)),
            pl.BlockSpec(memory_space=pl.ANY),
            pl.BlockSpec(memory_space=pl.ANY),
            pl.BlockSpec(memory_space=pl.ANY),
        ],
        out_specs=pl.BlockSpec((_TS, D_MODEL), lambda i: (i, 0)),
        out_shape=jax.ShapeDtypeStruct((n, D_MODEL), f32),
        scratch_shapes=[buf] * _NSLOT + [
            pltpu.VMEM((_TS, SUBLANES, LANES), f32),
            pltpu.VMEM((_NROW, _TS), i32),
            pltpu.VMEM((_NROW, _TS), f32),
            pltpu.VMEM((_TS, _NROW), i32),
            pltpu.VMEM((_RING * _TS, _NROW), f32),
            pltpu.SMEM((_RING_ROWS, _NROW), i32),
            pltpu.SemaphoreType.DMA((_NSLOT,)),
            pltpu.SemaphoreType.DMA((3,)),
        ],
        compiler_params=_cparams(("arbitrary",)),
        name="experts",
    )(scores, h3, x13, mod4, gfin, rsum, rsum.T, idx01, gate01, uv)


def _rope_tables(seq):
    half = HEAD_DIM // 2
    inv = ROPE_THETA ** (-np.arange(half, dtype=np.float32) / half)
    ang = np.arange(seq, dtype=np.float32)[:, None] * inv[None, :].astype(np.float32)
    cos = np.cos(ang).astype(np.float32)
    sin = np.sin(ang).astype(np.float32)
    cos_t = np.tile(np.concatenate([cos, cos], axis=1), (1, NSA_HEADS))
    sin_t = np.tile(np.concatenate([-sin, sin], axis=1), (1, NSA_HEADS))
    return jnp.asarray(cos_t), jnp.asarray(sin_t)


def _selection_map(n_cmp_pad, n_sel):
    r_sel = SEL_BLOCK // CMP_STRIDE
    r_cmp = CMP_BLOCK // CMP_STRIDE
    i = np.arange(n_cmp_pad)[:, None]
    j = np.arange(n_sel)[None, :]
    d = i - r_sel * j
    cnt = np.minimum(d, r_sel - 1) - np.maximum(d - r_cmp + 1, 0) + 1
    cnt = np.clip(cnt, 0, None).astype(np.float32)
    cnt[n_cmp_pad - r_cmp + 1:] = 0.0
    return jnp.asarray(cnt.T)


def _block_expand(n_sel, seq):
    e = (np.arange(seq)[None, :] // SEL_BLOCK == np.arange(n_sel)[:, None]).astype(np.float32)
    return jnp.asarray(e, dtype=bf16)


def _compress_weights(w1):
    eye = jnp.eye(NSA_KV_GROUPS, dtype=w1.dtype)
    out = []
    for part in range(CMP_BLOCK // CMP_STRIDE):
        w = w1[part * CMP_STRIDE * HEAD_DIM:(part + 1) * CMP_STRIDE * HEAD_DIM].reshape(CMP_STRIDE, HEAD_DIM, CMP_HIDDEN)
        big = jnp.einsum('pdc,gh->pgdhc', w, eye).reshape(CMP_STRIDE * KV_WIDTH, NSA_KV_GROUPS * CMP_HIDDEN)
        out.append(big.astype(bf16))
    return out


def kernel(x, c, w_ada, b_ada, g_mix, g_ffn, w_in, cmp_pos_k, cmp_pos_v, w_ck1, w_ck2, w_cv1, w_cv2, hgrn_lb_logits, hgrn_out_norm, w_branch, w_out, w_peer_q, peer_sub_keys, peer_u, peer_v, g_final):
    bsz, seq, d = x.shape
    n = bsz * seq
    depth = w_ada.shape[0]
    assert depth == 1, "single-layer block only"
    n_sel = seq // SEL_BLOCK
    n_pieces = seq // CMP_STRIDE
    cos_t, sin_t = _rope_tables(seq)
    selmap = _selection_map(n_pieces, n_sel)
    expand = _block_expand(n_sel, seq)
    xcur = x.reshape(n, d)
    for l in range(depth):
        mod = _adaln(c, w_ada[l], b_ada[l].reshape(1, 6 * d))
        mod3 = mod.reshape(bsz, 6, d)
        w_pad = jnp.concatenate(
            [w_in[l][:, :_C_GL + 3 * NSA_HEADS], jnp.zeros((d, GATE_PAD - 3 * NSA_HEADS), w_in.dtype),
             w_in[l][:, _C_GL + 3 * NSA_HEADS:]], axis=1).astype(bf16)
        q, kc, vc, kvsw, gates, hq, hf, hi, hg, mg = _inproj(xcur, mod3, g_mix[l].reshape(1, d), w_pad, cos_t, sin_t, seq)
        wkt, wkb = _compress_weights(w_ck1[l])
        wvt, wvb = _compress_weights(w_cv1[l])
        kc_c, vc_c = _compress(
            kc.reshape(bsz, n_pieces, CMP_STRIDE * KV_WIDTH), vc.reshape(bsz, n_pieces, CMP_STRIDE * KV_WIDTH),
            wkt, wkb, wvt, wvb, cmp_pos_k[l].reshape(1, -1), cmp_pos_v[l].reshape(1, -1),
            w_ck1[l], w_cv1[l], w_ck2[l], w_cv2[l])
        o_nsa = _nsa(q, kvsw.reshape(bsz, seq, 4 * KV_WIDTH), kc_c, vc_c, gates, selmap, expand, seq)
        o_hgrn = _hgrn(hq, hf, hi, hg, hgrn_lb_logits, hgrn_out_norm[l].reshape(1, -1), seq, l)
        x1, h2, h2_b = _merge(o_nsa, o_hgrn, mg, xcur, mod3, g_ffn[l].reshape(1, d),
                              w_branch[l, 0].astype(bf16), w_branch[l, 1].astype(bf16), w_out[l].astype(bf16), seq)
        wq = w_peer_q[l].astype(bf16)
        sk = jnp.transpose(peer_sub_keys[l], (1, 0, 2, 3)).reshape(2 * PEER_HEADS, PEER_NKEYS, PEER_QDIM // 2).astype(bf16)
        scores = _scores(h2_b, wq, sk)
        idx01, gate01 = _route(scores, 2 * _TS)
        uv = jnp.concatenate([peer_u[l], peer_v[l]], axis=1).astype(bf16).reshape(-1, _ROW_TILES, LANES)
        xcur = _experts(scores, idx01, gate01, h2, x1, mod.reshape(bsz, 6, SUBLANES, LANES),
                        g_final.reshape(SUBLANES, LANES), uv, seq)
    return xcur.reshape(bsz, seq, d)
```

```python
import functools

import numpy as np
import jax
import jax.numpy as jnp
from jax import lax
from jax.experimental import pallas as pl
from jax.experimental.pallas import tpu as pltpu

f32 = jnp.float32
bf16 = jnp.bfloat16
i32 = jnp.int32
_HIGHEST = lax.Precision.HIGHEST

D_MODEL = 1024
EPS = 1e-6
HEAD_DIM = 64
ROPE_THETA = 10000.0
NSA_HEADS = 8
NSA_KV_GROUPS = 2
NSA_GROUP = NSA_HEADS // NSA_KV_GROUPS
NSA_WIDTH = NSA_HEADS * HEAD_DIM
KV_WIDTH = NSA_KV_GROUPS * HEAD_DIM
CMP_BLOCK = 32
CMP_STRIDE = 16
CMP_HIDDEN = 128
SEL_BLOCK = 64
SEL_TOPK = 16
WINDOW = 512
FORCE_BONUS = 1000.0
HGRN_HEADS = 4
HGRN_EXPAND = 128
HGRN_HEAD_V = 128
HGRN_WIDTH = HGRN_HEADS * HGRN_EXPAND
HGRN_CHUNK = 32
PEER_HEADS = 8
PEER_NKEYS = 128
PEER_QDIM = 256
PEER_TOPK = 16
GATE_PAD = 128

LANES = 128
SUBLANES = 8
VMEM_LIMIT = 56 * 1024 * 1024

_NEG = -1e30


def _cparams(sem):
    return pltpu.CompilerParams(dimension_semantics=sem, vmem_limit_bytes=VMEM_LIMIT)


def _gelu(x):
    return 0.5 * x * (1.0 + jnp.tanh(0.7978845608028654 * (x + 0.044715 * (x * x * x))))


def _dot(a, b, **kw):
    return jnp.dot(a, b, preferred_element_type=f32, **kw)


def _dot_nt(a, b):
    return lax.dot_general(a, b, (((1,), (1,)), ((), ())), preferred_element_type=f32)


def _rms(x, g):
    return x * lax.rsqrt(jnp.mean(x * x, axis=-1, keepdims=True) + EPS) * g


def _adaln_kernel(c_ref, w_ref, b_ref, o_ref):
    c = c_ref[...]
    cs = c * jax.nn.sigmoid(c)
    o_ref[...] = _dot(cs, w_ref[...], precision=_HIGHEST) + b_ref[...]


def _adaln(c, w, b):
    bsz = c.shape[0]
    tn = 512
    return pl.pallas_call(
        _adaln_kernel,
        grid=(6 * D_MODEL // tn,),
        in_specs=[
            pl.BlockSpec((bsz, D_MODEL), lambda j: (0, 0)),
            pl.BlockSpec((D_MODEL, tn), lambda j: (0, j)),
            pl.BlockSpec((1, tn), lambda j: (0, j)),
        ],
        out_specs=pl.BlockSpec((bsz, tn), lambda j: (0, j)),
        out_shape=jax.ShapeDtypeStruct((bsz, 6 * D_MODEL), f32),
        compiler_params=_cparams(("arbitrary",)),
        name="adaln",
    )(c, w, b)


_C_Q = 0
_C_KC = 512
_C_VC = 640
_C_KS = 768
_C_VS = 896
_C_KW = 1024
_C_VW = 1152
_C_GL = 1280
_C_HQ = _C_GL + GATE_PAD
_C_HF = _C_HQ + 512
_C_HI = _C_HF + 512
_C_HG = _C_HI + 512
_C_MG = _C_HG + 512
_C_END = _C_MG + 2 * D_MODEL


def _inproj_kernel(x_ref, mod_ref, g_ref, w_ref, cos_ref, sin_ref,
                   q_ref, kc_ref, vc_ref, kvsw_ref, gate_ref, hq_ref, hf_ref, hi_ref, hg_ref, mg_ref):
    x = x_ref[...]
    sh = mod_ref[0, 0:1, :]
    sc = mod_ref[0, 1:2, :]
    h = (_rms(x, g_ref[...]) * (1.0 + sc) + sh).astype(bf16)

    def mm(c0, c1):
        return _dot(h, w_ref[:, c0:c1])

    cos = cos_ref[...]
    sin = sin_ref[...]

    def rope(a):
        width = a.shape[1]
        first = (lax.broadcasted_iota(i32, a.shape, 1) & (HEAD_DIM - 1)) < (HEAD_DIM // 2)
        partner = jnp.where(first, pltpu.roll(a, width - HEAD_DIM // 2, 1), pltpu.roll(a, HEAD_DIM // 2, 1))
        return a * cos[:, :width] + partner * sin[:, :width]

    q_ref[...] = (rope(mm(_C_Q, _C_KC)) * (HEAD_DIM ** -0.5)).astype(bf16)
    kc_ref[...] = rope(mm(_C_KC, _C_VC)).astype(bf16)
    vc_ref[...] = mm(_C_VC, _C_KS).astype(bf16)
    kvsw_ref[:, 0:128] = rope(mm(_C_KS, _C_VS)).astype(bf16)
    kvsw_ref[:, 128:256] = mm(_C_VS, _C_KW).astype(bf16)
    kvsw_ref[:, 256:384] = rope(mm(_C_KW, _C_VW)).astype(bf16)
    kvsw_ref[:, 384:512] = mm(_C_VW, _C_GL).astype(bf16)
    gate_ref[...] = jax.nn.sigmoid(mm(_C_GL, _C_HQ))
    hq_ref[...] = mm(_C_HQ, _C_HF).astype(bf16)
    hf_ref[...] = mm(_C_HF, _C_HI)
    hi_ref[...] = mm(_C_HI, _C_HG).astype(bf16)
    hg_ref[...] = mm(_C_HG, _C_MG).astype(bf16)
    mg_ref[...] = jax.nn.sigmoid(mm(_C_MG, _C_END)).astype(bf16)


def _inproj(x2, mod3, g_mix, w_pad, cos_t, sin_t, seq):
    n = x2.shape[0]
    tm = 256
    tiles_per_seq = seq // tm
    row = lambda i: (i, 0)
    outs = [
        (NSA_WIDTH, bf16), (KV_WIDTH, bf16), (KV_WIDTH, bf16), (4 * KV_WIDTH, bf16), (GATE_PAD, f32),
        (HGRN_WIDTH, bf16), (HGRN_WIDTH, f32), (HGRN_WIDTH, bf16), (HGRN_WIDTH, bf16), (2 * D_MODEL, bf16),
    ]
    return pl.pallas_call(
        _inproj_kernel,
        grid=(n // tm,),
        in_specs=[
            pl.BlockSpec((tm, D_MODEL), row),
            pl.BlockSpec((1, 6, D_MODEL), lambda i: (i // tiles_per_seq, 0, 0)),
            pl.BlockSpec((1, D_MODEL), lambda i: (0, 0)),
            pl.BlockSpec((D_MODEL, _C_END), lambda i: (0, 0)),
            pl.BlockSpec((tm, NSA_WIDTH), lambda i: (i % tiles_per_seq, 0)),
            pl.BlockSpec((tm, NSA_WIDTH), lambda i: (i % tiles_per_seq, 0)),
        ],
        out_specs=[pl.BlockSpec((tm, w), row) for w, _ in outs],
        out_shape=[jax.ShapeDtypeStruct((n, w), dt) for w, dt in outs],
        compiler_params=_cparams(("arbitrary",)),
        name="inproj",
    )(x2, mod3, g_mix, w_pad, cos_t, sin_t)


def _compress_kernel(kc_ref, vc_ref, wkt_ref, wkb_ref, wvt_ref, wvb_ref, pk_ref, pv_ref,
                     w1k_ref, w1v_ref, w2k_ref, w2v_ref, okc_ref, ovc_ref):
    def one(x_ref, wt_ref, wb_ref, pos_ref, w1_ref, w2_ref, o_ref):
        pieces = x_ref[0]
        top = _dot(pieces, wt_ref[...])
        bot = _dot(pieces, wb_ref[...])
        nrow = bot.shape[0]
        bot = pltpu.roll(bot, nrow - 1, 0)
        cpos = _dot(pos_ref[...], w1_ref[...], precision=_HIGHEST)
        w2 = w2_ref[...].astype(bf16)
        outs = []
        for g in range(NSA_KV_GROUPS):
            sl = slice(g * CMP_HIDDEN, (g + 1) * CMP_HIDDEN)
            hid = _gelu(top[:, sl] + bot[:, sl] + cpos)
            outs.append(_dot(hid.astype(bf16), w2))
        o_ref[0] = jnp.concatenate(outs, axis=1).astype(bf16)

    one(kc_ref, wkt_ref, wkb_ref, pk_ref, w1k_ref, w2k_ref, okc_ref)
    one(vc_ref, wvt_ref, wvb_ref, pv_ref, w1v_ref, w2v_ref, ovc_ref)


def _compress(kc3, vc3, wkt, wkb, wvt, wvb, pk, pv, w1k, w1v, w2k, w2v):
    bsz, npieces, width = kc3.shape
    full = lambda a: pl.BlockSpec(a.shape, lambda b: (0,) * a.ndim)
    per_b = pl.BlockSpec((1, npieces, width), lambda b: (b, 0, 0))
    out_b = pl.BlockSpec((1, npieces, KV_WIDTH), lambda b: (b, 0, 0))
    return pl.pallas_call(
        _compress_kernel,
        grid=(bsz,),
        in_specs=[per_b, per_b] + [full(a) for a in (wkt, wkb, wvt, wvb, pk, pv, w1k, w1v, w2k, w2v)],
        out_specs=[out_b, out_b],
        out_shape=[jax.ShapeDtypeStruct((bsz, npieces, KV_WIDTH), bf16)] * 2,
        compiler_params=_cparams(("arbitrary",)),
        name="compress",
    )(kc3, vc3, wkt, wkb, wvt, wvb, pk, pv, w1k, w1v, w2k, w2v)


_TQ = 128
_NSA_CLASSES = 4


def _softmax_parts(s, mask):
    sm = jnp.where(mask, s, _NEG)
    m = jnp.max(sm, axis=-1, keepdims=True)
    p = jnp.where(mask, jnp.exp(sm - m), 0.0)
    den = jnp.maximum(jnp.sum(p, axis=-1, keepdims=True), 1e-30)
    return p, den


def _softmax_plain(s):
    p = jnp.exp(s - jnp.max(s, axis=-1, keepdims=True))
    return p, jnp.sum(p, axis=-1, keepdims=True)


def _nsa_kernel(q_ref, kvsw_ref, kc_ref, vc_ref, gate_ref, selmap_ref, expand_ref, o_ref, *, seq):
    tq = _TQ
    qi = pl.program_id(1)
    n_cls = _NSA_CLASSES
    per_cls = (seq // tq) // n_cls
    for cls in range(n_cls):
        @pl.when((qi >= cls * per_cls) & (qi < (cls + 1) * per_cls))
        def _(cls=cls):
            _nsa_tile(q_ref, kvsw_ref, kc_ref, vc_ref, gate_ref, selmap_ref, expand_ref, o_ref,
                      seq=seq, width=(cls + 1) * (seq // n_cls))


def _nsa_tile(q_ref, kvsw_ref, kc_ref, vc_ref, gate_ref, selmap_ref, expand_ref, o_ref, *, seq, width):
    tq = _TQ
    rows = NSA_GROUP * tq
    n_sel = seq // SEL_BLOCK
    n_cmp_pad = seq // CMP_STRIDE
    t0 = pl.program_id(1) * tq
    q = q_ref[...]
    gates = gate_ref[...]
    tlane = t0 + lax.broadcasted_iota(i32, (1, tq), 1)
    t4 = t0 + (lax.broadcasted_iota(i32, (rows, 1), 0) & (tq - 1))
    wk = WINDOW + tq
    ws = pl.multiple_of(jnp.maximum(t0 - WINDOW, 0), tq)
    ks_all = kvsw_ref[0, 0:width, 0:128]
    vs_all = kvsw_ref[0, 0:width, 128:256]
    kw_all = kvsw_ref[0, pl.ds(ws, wk), 256:384]
    vw_all = kvsw_ref[0, pl.ds(ws, wk), 384:512]
    kc_all = kc_ref[0]
    vc_all = vc_ref[0]
    cend = lax.broadcasted_iota(i32, (1, n_cmp_pad), 1) * CMP_STRIDE + (CMP_BLOCK - 1)
    blk = lax.broadcasted_iota(i32, (n_sel, 1), 0)
    kpos = lax.broadcasted_iota(i32, (1, width), 1)
    kpos_w = ws + lax.broadcasted_iota(i32, (1, wk), 1)
    cur = tlane >> 6
    forced = (blk == 0) | (blk == cur) | (blk == cur - 1)
    causal_blk = blk * SEL_BLOCK <= tlane
    pieces = []
    for g in range(NSA_KV_GROUPS):
        gs = slice(g * HEAD_DIM, (g + 1) * HEAD_DIM)
        qg = jnp.concatenate(
            [q[:, (NSA_GROUP * g + r) * HEAD_DIM:(NSA_GROUP * g + r + 1) * HEAD_DIM] for r in range(NSA_GROUP)], axis=0)
        p_c, den_c = _softmax_parts(_dot_nt(qg, kc_all[:, gs]), cend <= t4)
        o_c = _dot(p_c.astype(bf16), vc_all[:, gs]) / den_c
        pn = p_c / den_c
        pc_sum = pn[0:tq]
        for r in range(1, NSA_GROUP):
            pc_sum = pc_sum + pn[r * tq:(r + 1) * tq]
        imp = lax.dot_general(selmap_ref[...], pc_sum, (((1,), (1,)), ((), ())),
                              preferred_element_type=f32, precision=_HIGHEST)
        imp = jnp.where(forced, imp + FORCE_BONUS, imp)
        imp = jnp.where(causal_blk, imp, -1.0)
        rank = jnp.zeros((n_sel, tq), f32)
        for j in range(n_sel):
            row = imp[j:j + 1, :]
            ahead = jnp.where(row > imp, 1.0, jnp.where(row == imp, jnp.where(blk > j, 1.0, 0.0), 0.0))
            rank = rank + ahead
        bias = jnp.where(rank < float(min(SEL_TOPK, n_sel)), jnp.where(causal_blk, 0.0, _NEG), _NEG).astype(bf16)
        bias_q = lax.dot_general(bias, expand_ref[:, 0:width], (((0,), (0,)), ((), ())),
                                 preferred_element_type=f32)
        bias4 = jnp.concatenate([bias_q] * NSA_GROUP, axis=0)
        p_s, den_s = _softmax_plain(jnp.where(kpos <= t4, _dot_nt(qg, ks_all[:, gs]) + bias4, _NEG))
        o_s = _dot(p_s.astype(bf16), vs_all[:, gs]) / den_s
        in_window = (t4 - kpos_w).astype(jnp.uint32) < WINDOW
        p_w, den_w = _softmax_plain(jnp.where(in_window, _dot_nt(qg, kw_all[:, gs]), _NEG))
        o_w = _dot(p_w.astype(bf16), vw_all[:, gs]) / den_w

        def gcol(br):
            return jnp.concatenate(
                [gates[:, (NSA_GROUP * g + r) * 3 + br:(NSA_GROUP * g + r) * 3 + br + 1] for r in range(NSA_GROUP)], axis=0)

        o = gcol(0) * o_c + gcol(1) * o_s + gcol(2) * o_w
        pieces += [o[r * tq:(r + 1) * tq] for r in range(NSA_GROUP)]
    o_ref[...] = jnp.concatenate(pieces, axis=1).astype(bf16)


def _nsa(q, kvsw3, kc3, vc3, gates, selmap, expand, seq):
    n = q.shape[0]
    bsz = n // seq
    tq = _TQ
    nq = seq // tq
    return pl.pallas_call(
        functools.partial(_nsa_kernel, seq=seq),
        grid=(bsz, nq),
        in_specs=[
            pl.BlockSpec((tq, NSA_WIDTH), lambda b, i: (b * nq + i, 0)),
            pl.BlockSpec((1, seq, 4 * KV_WIDTH), lambda b, i: (b, 0, 0)),
            pl.BlockSpec((1,) + kc3.shape[1:], lambda b, i: (b, 0, 0)),
            pl.BlockSpec((1,) + vc3.shape[1:], lambda b, i: (b, 0, 0)),
            pl.BlockSpec((tq, GATE_PAD), lambda b, i: (b * nq + i, 0)),
            pl.BlockSpec(selmap.shape, lambda b, i: (0, 0)),
            pl.BlockSpec(expand.shape, lambda b, i: (0, 0)),
        ],
        out_specs=pl.BlockSpec((tq, NSA_WIDTH), lambda b, i: (b * nq + i, 0)),
        out_shape=jax.ShapeDtypeStruct((n, NSA_WIDTH), bf16),
        compiler_params=_cparams(("arbitrary", "arbitrary")),
        name="nsa",
    )(q, kvsw3, kc3, vc3, gates, selmap, expand)


def _hgrn_kernel(hq_ref, hf_ref, hi_ref, hg_ref, lbl_ref, ng_ref, o_ref, st_ref, *, seq, layer):
    c = HGRN_CHUNK
    logits = lbl_ref[...]
    e = jnp.exp(logits - jnp.max(logits, axis=0, keepdims=True))
    sm = e / jnp.sum(e, axis=0, keepdims=True)
    lb = sm[0:1]
    for l in range(1, layer + 1):
        lb = lb + sm[l:l + 1]
    st_ref[...] = jnp.zeros_like(st_ref)
    r_i = lax.broadcasted_iota(i32, (c, c), 0)
    c_i = lax.broadcasted_iota(i32, (c, c), 1)
    causal = r_i >= c_i
    r_col = lax.broadcasted_iota(i32, (c, 1), 0)
    ng = ng_ref[...]

    def body(ci, carry):
        r0 = pl.multiple_of(ci * c, c)
        hq = hq_ref[pl.ds(r0, c), :].astype(f32)
        hf = hf_ref[pl.ds(r0, c), :]
        v = hi_ref[pl.ds(r0, c), :]
        hg = hg_ref[pl.ds(r0, c), :].astype(f32)
        f = lb + (1.0 - lb) * jax.nn.sigmoid(hf)
        k = 1.0 - f
        qv = hq * jax.nn.sigmoid(hq)
        bcum = jnp.log(f)
        for d in (1, 2, 4, 8, 16):
            bcum = bcum + jnp.where(r_col >= d, pltpu.roll(bcum, d, 0), 0.0)
        bend = bcum[c - 1:c, :]
        q_dec = (qv * jnp.exp(bcum)).astype(bf16)
        k_inv = (k * jnp.exp(-bcum)).astype(bf16)
        k_end = (k * jnp.exp(bend - bcum)).astype(bf16)
        dec_end = jnp.exp(bend)
        outs = []
        for h in range(HGRN_HEADS):
            sl = slice(h * HGRN_EXPAND, (h + 1) * HGRN_EXPAND)
            a = jnp.where(causal, _dot_nt(q_dec[:, sl], k_inv[:, sl]), 0.0)
            st = st_ref[h]
            o = _dot(a.astype(bf16), v[:, sl]) + _dot_nt(q_dec[:, sl], st.astype(bf16))
            upd = lax.dot_general(v[:, sl], k_end[:, sl], (((0,), (0,)), ((), ())), preferred_element_type=f32)
            st_ref[h] = st * dec_end[:, sl] + upd
            y = _rms(o, ng) * (hg[:, sl] * jax.nn.sigmoid(hg[:, sl]))
            outs.append(y)
        o_ref[pl.ds(r0, c), :] = jnp.concatenate(outs, axis=1).astype(bf16)
        return carry

    lax.fori_loop(0, seq // c, body, 0, unroll=8)


def _hgrn(hq, hf, hi, hg, lb_logits, norm_g, seq, layer):
    n = hq.shape[0]
    bsz = n // seq
    per_b = pl.BlockSpec((seq, HGRN_WIDTH), lambda b: (b, 0))
    return pl.pallas_call(
        functools.partial(_hgrn_kernel, seq=seq, layer=layer),
        grid=(bsz,),
        in_specs=[per_b, per_b, per_b, per_b,
                  pl.BlockSpec(lb_logits.shape, lambda b: (0, 0)),
                  pl.BlockSpec(norm_g.shape, lambda b: (0, 0))],
        out_specs=per_b,
        out_shape=jax.ShapeDtypeStruct((n, HGRN_WIDTH), bf16),
        scratch_shapes=[pltpu.VMEM((HGRN_HEADS, HGRN_HEAD_V, HGRN_EXPAND), f32)],
        compiler_params=_cparams(("arbitrary",)),
        name="hgrn",
    )(hq, hf, hi, hg, lb_logits, norm_g)


def _merge_kernel(on_ref, oh_ref, mg_ref, x_ref, mod_ref, g_ref, wb0_ref, wb1_ref, wo_ref, x1_ref, h2_ref, hb_ref):
    a = _dot(on_ref[...], wb0_ref[...])
    b = _dot(oh_ref[...], wb1_ref[...])
    mg = mg_ref[...].astype(f32)
    y = mg[:, :D_MODEL] * a + mg[:, D_MODEL:] * b
    y2 = _dot(y.astype(bf16), wo_ref[...])
    x1 = x_ref[...] + mod_ref[0, 2:3, :] * y2
    h2 = _rms(x1, g_ref[...]) * (1.0 + mod_ref[0, 4:5, :]) + mod_ref[0, 3:4, :]
    rows = x1.shape[0]
    x1_ref[...] = x1.reshape(rows, SUBLANES, LANES)
    h2_ref[...] = h2.reshape(rows, SUBLANES, LANES)
    hb_ref[...] = h2.astype(bf16)


def _merge(o_nsa, o_hgrn, mg, x2, mod3, g_ffn, wb0, wb1, wo, seq):
    n = x2.shape[0]
    tm = 256
    tiles_per_seq = seq // tm
    row = lambda i: (i, 0)
    full = lambda a: pl.BlockSpec(a.shape, lambda i: (0,) * a.ndim)
    return pl.pallas_call(
        _merge_kernel,
        grid=(n // tm,),
        in_specs=[
            pl.BlockSpec((tm, NSA_WIDTH), row), pl.BlockSpec((tm, HGRN_WIDTH), row),
            pl.BlockSpec((tm, 2 * D_MODEL), row), pl.BlockSpec((tm, D_MODEL), row),
            pl.BlockSpec((1, 6, D_MODEL), lambda i: (i // tiles_per_seq, 0, 0)),
            full(g_ffn), full(wb0), full(wb1), full(wo),
        ],
        out_specs=[pl.BlockSpec((tm, SUBLANES, LANES), lambda i: (i, 0, 0))] * 2 + [pl.BlockSpec((tm, D_MODEL), row)],
        out_shape=[jax.ShapeDtypeStruct((n, SUBLANES, LANES), f32)] * 2 + [jax.ShapeDtypeStruct((n, D_MODEL), bf16)],
        compiler_params=_cparams(("arbitrary",)),
        name="merge",
    )(o_nsa, o_hgrn, mg, x2, mod3, g_ffn, wb0, wb1, wo)


_TR = 256


def _topk_rows(s, k, payload=None):
    n = s.shape[0]
    rowid = lax.broadcasted_iota(i32, s.shape, 0).astype(f32)
    vals, picks = [], []
    for _ in range(k):
        m = jnp.max(s, axis=0, keepdims=True)
        i = jnp.min(jnp.where(s == m, rowid, float(n)), axis=0, keepdims=True)
        hit = rowid == i
        vals.append(m)
        picks.append(i if payload is None else jnp.max(jnp.where(hit, payload, -1.0), axis=0, keepdims=True))
        s = jnp.where(hit, -jnp.inf, s)
    return vals, picks


def _pair_candidates(v1, i1, v2, i2):
    k = len(v1)
    s2 = jnp.concatenate(v2, axis=0)
    e2 = jnp.concatenate(i2, axis=0)
    sub = lax.broadcasted_iota(i32, (SUBLANES, s2.shape[1]), 0)
    comb, cand = [], []
    for a in range(k // 2):
        nb = k // (a + 1)
        rows = -(-nb // SUBLANES) * SUBLANES
        c = v1[a] + s2[0:rows]
        if nb < rows:
            c = jnp.where(sub < nb, c, -jnp.inf)
        comb.append(c)
        cand.append(i1[a] * float(PEER_NKEYS) + e2[0:rows])
    comb.append(jnp.concatenate(v1[k // 2:], axis=0) + s2[0:1])
    cand.append(jnp.concatenate(i1[k // 2:], axis=0) * float(PEER_NKEYS) + e2[0:1])
    return jnp.concatenate(comb, axis=0), jnp.concatenate(cand, axis=0)


def _head_routing(tops1, tops2):
    k = PEER_TOPK
    comb, cand = _pair_candidates(tops1[0], tops1[1], tops2[0], tops2[1])
    tv, picked = _topk_rows(comb, k, payload=cand)
    ex = [jnp.exp(tv[r] - tv[0]) for r in range(k)]
    den = ex[0]
    for r in range(1, k):
        den = den + ex[r]
    return picked, [e / den for e in ex]


def _scores_kernel(h_ref, wq_ref, sk_ref, o_ref):
    q = _dot(h_ref[...], wq_ref[...]).astype(bf16)
    half = PEER_QDIM // 2
    for grp in range(2 * PEER_HEADS):
        o_ref[grp] = _dot_nt(sk_ref[grp], q[:, grp * half:(grp + 1) * half])


def _scores(h2, wq, sk):
    n = h2.shape[0]
    tr = _TR
    return pl.pallas_call(
        _scores_kernel,
        grid=(n // tr,),
        in_specs=[
            pl.BlockSpec((tr, D_MODEL), lambda i: (i, 0)),
            pl.BlockSpec(wq.shape, lambda i: (0, 0)),
            pl.BlockSpec(sk.shape, lambda i: (0, 0, 0)),
        ],
        out_specs=pl.BlockSpec((2 * PEER_HEADS, PEER_NKEYS, tr), lambda i: (0, 0, i)),
        out_shape=jax.ShapeDtypeStruct((2 * PEER_HEADS, PEER_NKEYS, n), f32),
        compiler_params=_cparams(("arbitrary",)),
        name="scores",
    )(h2, wq, sk)


def _route_kernel(s_ref, idx_ref, gate_ref):
    k = PEER_TOPK
    experts, gates = [], []
    for hd in range(PEER_HEADS):
        picked, g = _head_routing(_topk_rows(s_ref[2 * hd], k), _topk_rows(s_ref[2 * hd + 1], k))
        experts += picked
        gates += g
    idx_t = jnp.concatenate(experts, axis=0).astype(i32)
    gate_t = jnp.concatenate(gates, axis=0)
    for blk in range(idx_t.shape[1] // LANES):
        sl = slice(blk * LANES, (blk + 1) * LANES)
        idx_ref[sl, :] = idx_t[:, sl].T
        gate_ref[sl, :] = gate_t[:, sl].T


def _route(scores, n_tok):
    ncol = PEER_HEADS * PEER_TOPK
    return pl.pallas_call(
        _route_kernel,
        grid=(1,),
        in_specs=[pl.BlockSpec((2 * PEER_HEADS, PEER_NKEYS, n_tok), lambda i: (0, 0, 0))],
        out_specs=[pl.BlockSpec((n_tok, ncol), lambda i: (0, 0))] * 2,
        out_shape=[jax.ShapeDtypeStruct((n_tok, ncol), i32), jax.ShapeDtypeStruct((n_tok, ncol), f32)],
        compiler_params=_cparams(("arbitrary",)),
        name="route",
    )(scores)


_TS = 128
_HALF = 64
_TG = 4
_NSLOT = 4
_AHEAD = (_NSLOT - 1) * _TG
_NROW = PEER_HEADS * PEER_TOPK
_ROW_TILES = 2 * D_MODEL // LANES
_TP = 2
_RING = 3
_RING_ROWS = (_RING + 2) * _TS


def _expert_kernel(sc_ref, h_ref, x1_ref, mod_ref, gfin_ref, rsum_ref, rrep_ref, idx01_ref, gate01_ref, uv_ref, o_ref,
                   *scratch):
    step = pl.program_id(0)
    last = pl.num_programs(0) - 1
    bufs = scratch[:_NSLOT]
    out_tiles, idx_stage, gate_stage, idx_rows, gate_ring, idx_ring, sem, aux = scratch[_NSLOT:]
    n_dma = _TG * _NROW
    k = PEER_TOPK
    cur = lax.rem(step, _RING)
    nxt = lax.rem(step + 2, _RING)
    mirror = jnp.where(nxt == 0, _RING * _TS, (_RING + 1) * _TS)

    def row_copy(e, slot, j):
        return pltpu.make_async_copy(uv_ref.at[e], bufs[slot].at[j], sem.at[slot])

    def wait(slot):
        pltpu.make_async_copy(uv_ref.at[pl.ds(0, n_dma)], bufs[slot], sem.at[slot]).wait()

    def ring_copies():
        return (pltpu.make_async_copy(idx_rows, idx_ring.at[pl.ds(nxt * _TS, _TS)], aux.at[0]),
                pltpu.make_async_copy(idx_rows, idx_ring.at[pl.ds(mirror, _TS)], aux.at[1]))

    @pl.when(step == 0)
    def _():
        prime = (pltpu.make_async_copy(idx01_ref, idx_ring.at[pl.ds(0, 2 * _TS)], aux.at[0]),
                 pltpu.make_async_copy(idx01_ref.at[pl.ds(0, _TS)], idx_ring.at[pl.ds(_RING * _TS, _TS)], aux.at[1]),
                 pltpu.make_async_copy(gate01_ref, gate_ring.at[pl.ds(0, 2 * _TS)], aux.at[2]))
        for c in prime:
            c.start()
        for c in prime:
            c.wait()
        for slot in range(_NSLOT - 1):
            def body(t, c, slot=slot):
                for r in range(_NROW):
                    row_copy(idx_ring[slot * _TG + t, r], slot, t * _NROW + r).start(priority=r % 2)
                return c
            lax.fori_loop(0, _TG, body, 0)

    gt_f = mod_ref[0, 5]
    gfin = gfin_ref[...]
    width = _TP * LANES
    r_i = lax.broadcasted_iota(i32, (width, width), 0) // LANES
    c_i = lax.broadcasted_iota(i32, (width, width), 1) // LANES
    lane_sum = jnp.where(r_i == c_i, 1.0, 0.0).astype(bf16)
    eye = jnp.where(lax.broadcasted_iota(i32, (_NROW, LANES), 0) == lax.broadcasted_iota(i32, (_NROW, LANES), 1),
                    1.0, 0.0)

    def evaluate(tok0, slot, t0):
        src = bufs[slot]
        z, vs = [], []
        for dt in range(_TP):
            uv = src[pl.ds((t0 + dt) * _NROW, _NROW)].astype(f32)
            vs.append(uv[:, SUBLANES:2 * SUBLANES, :])
            u = uv[:, 0:SUBLANES, :]
            z.append((u * h_ref[tok0 + dt][None]).reshape(_NROW * SUBLANES, LANES).astype(bf16))
        part = _dot(rsum_ref[...], jnp.concatenate(z, axis=1))
        a = _dot(part.astype(bf16), lane_sum)
        g_diag = jnp.concatenate([eye * gate_ring[pl.ds(cur * _TS + tok0 + dt, 1), :] for dt in range(_TP)], axis=1)
        g = _dot(g_diag.astype(bf16), lane_sum)
        w = (g * _gelu(a)).astype(bf16)
        spread = _dot(rrep_ref[...], w)
        for dt in range(_TP):
            tok = tok0 + dt
            wt = spread[:, dt * LANES:(dt + 1) * LANES].reshape(_NROW, SUBLANES, LANES)
            y = jnp.sum(wt * vs[dt], axis=0)
            x2 = x1_ref[tok] + gt_f * y
            ms = jnp.sum(jnp.sum(x2 * x2, axis=1, keepdims=True), axis=0, keepdims=True) * (1.0 / D_MODEL)
            out_tiles[tok] = x2 * lax.rsqrt(ms + EPS) * gfin

    def half_step(half, carry):
        @pl.when((half == _TS // _HALF - 1) & (step > 0))
        def _():
            for c in ring_copies():
                c.wait()

        tok_h = half * _HALF
        ring_h = cur * _TS + tok_h
        head0 = half * (PEER_HEADS // 2)
        experts, gates, tops = [], [], {}
        pieces = []
        for hh in range(PEER_HEADS // 2):
            for p in range(2):
                def sub_keys(hh=hh, p=p):
                    tops[hh, p] = _topk_rows(sc_ref[(head0 + hh) * 2 + p], k)
                pieces.append(sub_keys)

            def pairs(hh=hh):
                picked, g = _head_routing(tops[hh, 0], tops[hh, 1])
                experts.extend(picked)
                gates.extend(g)
            pieces.append(pairs)

        def stage():
            rows = pl.ds(pl.multiple_of(head0 * k, _HALF), _HALF)
            idx_stage[rows, :] = jnp.concatenate(experts, axis=0).astype(i32)
            gate_stage[rows, :] = jnp.concatenate(gates, axis=0)
        pieces.append(stage)

        for grp in range(_HALF // _TG):
            slot = grp % _NSLOT
            wait(slot)
            if grp < len(pieces):
                pieces[grp]()
            dst = (slot + _NSLOT - 1) % _NSLOT
            for t0 in range(0, _TG, _TP):
                evaluate(tok_h + grp * _TG + t0, slot, t0)
                for t in range(t0, t0 + _TP):
                    for r in range(_NROW):
                        row_copy(idx_ring[ring_h + grp * _TG + t + _AHEAD, r], dst, t * _NROW + r).start(priority=r % 2)
        return carry

    lax.fori_loop(0, _TS // _HALF, half_step, 0)
    o_ref[...] = out_tiles[...].reshape(_TS, D_MODEL)
    idx_rows[...] = idx_stage[...].T
    gate_ring[pl.ds(pl.multiple_of(nxt * _TS, _TS), _TS), :] = gate_stage[...].T
    for c in ring_copies():
        c.start()

    @pl.when(step == last)
    def _():
        for c in ring_copies():
            c.wait()
        for slot in range(_NSLOT - 1):
            wait(slot)


def _experts(scores, idx01, gate01, h3, x13, mod4, gfin, uv, seq):
    n = h3.shape[0]
    n_steps = n // _TS
    steps_per_seq = seq // _TS
    tok3 = lambda i: (i, 0, 0)
    rsum = jnp.asarray(np.repeat(np.eye(_NROW, dtype=np.float32), SUBLANES, axis=1), dtype=bf16)
    buf = pltpu.VMEM((_TG * _NROW, _ROW_TILES, LANES), bf16)
    return pl.pallas_call(
        _expert_kernel,
        grid=(n_steps,),
        in_specs=[
            pl.BlockSpec((2 * PEER_HEADS, PEER_NKEYS, _TS), lambda i: (0, 0, jnp.minimum(i + 2, n_steps - 1))),
            pl.BlockSpec((_TS, SUBLANES, LANES), tok3),
            pl.BlockSpec((_TS, SUBLANES, LANES), tok3),
            pl.BlockSpec((1, 6, SUBLANES, LANES), lambda i: (i // steps_per_seq, 0, 0, 0)),
            pl.BlockSpec((SUBLANES, LANES), lambda i: (0, 0)),
            pl.BlockSpec(rsum.shape, lambda i: (0, 0)),
            pl.BlockSpec(rsum.shape[::-1], lambda i: (0, 0)),
            pl.BlockSpec(memory_space=pl.ANY),
            pl.BlockSpec(memory_space=pl.ANY),
            pl.BlockSpec(memory_space=pl.ANY),
        ],
        out_specs=pl.BlockSpec((_TS, D_MODEL), lambda i: (i, 0)),
        out_shape=jax.ShapeDtypeStruct((n, D_MODEL), f32),
        scratch_shapes=[buf] * _NSLOT + [
            pltpu.VMEM((_TS, SUBLANES, LANES), f32),
            pltpu.VMEM((_NROW, _TS), i32),
            pltpu.VMEM((_NROW, _TS), f32),
            pltpu.VMEM((_TS, _NROW), i32),
            pltpu.VMEM((_RING * _TS, _NROW), f32),
            pltpu.SMEM((_RING_ROWS, _NROW), i32),
            pltpu.SemaphoreType.DMA((_NSLOT,)),
            pltpu.SemaphoreType.DMA((3,)),
        ],
        compiler_params=_cparams(("arbitrary",)),
        name="experts",
    )(scores, h3, x13, mod4, gfin, rsum, rsum.T, idx01, gate01, uv)


def _rope_tables(seq):
    half = HEAD_DIM // 2
    inv = ROPE_THETA ** (-np.arange(half, dtype=np.float32) / half)
    ang = np.arange(seq, dtype=np.float32)[:, None] * inv[None, :].astype(np.float32)
    cos = np.cos(ang).astype(np.float32)
    sin = np.sin(ang).astype(np.float32)
    cos_t = np.tile(np.concatenate([cos, cos], axis=1), (1, NSA_HEADS))
    sin_t = np.tile(np.concatenate([-sin, sin], axis=1), (1, NSA_HEADS))
    return jnp.asarray(cos_t), jnp.asarray(sin_t)


def _selection_map(n_cmp_pad, n_sel):
    r_sel = SEL_BLOCK // CMP_STRIDE
    r_cmp = CMP_BLOCK // CMP_STRIDE
    i = np.arange(n_cmp_pad)[:, None]
    j = np.arange(n_sel)[None, :]
    d = i - r_sel * j
    cnt = np.minimum(d, r_sel - 1) - np.maximum(d - r_cmp + 1, 0) + 1
    cnt = np.clip(cnt, 0, None).astype(np.float32)
    cnt[n_cmp_pad - r_cmp + 1:] = 0.0
    return jnp.asarray(cnt.T)


def _block_expand(n_sel, seq):
    e = (np.arange(seq)[None, :] // SEL_BLOCK == np.arange(n_sel)[:, None]).astype(np.float32)
    return jnp.asarray(e, dtype=bf16)


def _compress_weights(w1):
    eye = jnp.eye(NSA_KV_GROUPS, dtype=w1.dtype)
    out = []
    for part in range(CMP_BLOCK // CMP_STRIDE):
        w = w1[part * CMP_STRIDE * HEAD_DIM:(part + 1) * CMP_STRIDE * HEAD_DIM].reshape(CMP_STRIDE, HEAD_DIM, CMP_HIDDEN)
        big = jnp.einsum('pdc,gh->pgdhc', w, eye).reshape(CMP_STRIDE * KV_WIDTH, NSA_KV_GROUPS * CMP_HIDDEN)
        out.append(big.astype(bf16))
    return out


def kernel(x, c, w_ada, b_ada, g_mix, g_ffn, w_in, cmp_pos_k, cmp_pos_v, w_ck1, w_ck2, w_cv1, w_cv2, hgrn_lb_logits, hgrn_out_norm, w_branch, w_out, w_peer_q, peer_sub_keys, peer_u, peer_v, g_final):
    bsz, seq, d = x.shape
    n = bsz * seq
    depth = w_ada.shape[0]
    assert depth == 1, "single-layer block only"
    n_sel = seq // SEL_BLOCK
    n_pieces = seq // CMP_STRIDE
    cos_t, sin_t = _rope_tables(seq)
    selmap = _selection_map(n_pieces, n_sel)
    expand = _block_expand(n_sel, seq)
    xcur = x.reshape(n, d)
    for l in range(depth):
        mod = _adaln(c, w_ada[l], b_ada[l].reshape(1, 6 * d))
        mod3 = mod.reshape(bsz, 6, d)
        w_pad = jnp.concatenate(
            [w_in[l][:, :_C_GL + 3 * NSA_HEADS], jnp.zeros((d, GATE_PAD - 3 * NSA_HEADS), w_in.dtype),
             w_in[l][:, _C_GL + 3 * NSA_HEADS:]], axis=1).astype(bf16)
        q, kc, vc, kvsw, gates, hq, hf, hi, hg, mg = _inproj(xcur, mod3, g_mix[l].reshape(1, d), w_pad, cos_t, sin_t, seq)
        wkt, wkb = _compress_weights(w_ck1[l])
        wvt, wvb = _compress_weights(w_cv1[l])
        kc_c, vc_c = _compress(
            kc.reshape(bsz, n_pieces, CMP_STRIDE * KV_WIDTH), vc.reshape(bsz, n_pieces, CMP_STRIDE * KV_WIDTH),
            wkt, wkb, wvt, wvb, cmp_pos_k[l].reshape(1, -1), cmp_pos_v[l].reshape(1, -1),
            w_ck1[l], w_cv1[l], w_ck2[l], w_cv2[l])
        o_nsa = _nsa(q, kvsw.reshape(bsz, seq, 4 * KV_WIDTH), kc_c, vc_c, gates, selmap, expand, seq)
        o_hgrn = _hgrn(hq, hf, hi, hg, hgrn_lb_logits, hgrn_out_norm[l].reshape(1, -1), seq, l)
        x1, h2, h2_b = _merge(o_nsa, o_hgrn, mg, xcur, mod3, g_ffn[l].reshape(1, d),
                              w_branch[l, 0].astype(bf16), w_branch[l, 1].astype(bf16), w_out[l].astype(bf16), seq)
        wq = w_peer_q[l].astype(bf16)
        sk = jnp.transpose(peer_sub_keys[l], (1, 0, 2, 3)).reshape(2 * PEER_HEADS, PEER_NKEYS, PEER_QDIM // 2).astype(bf16)
        scores = _scores(h2_b, wq, sk)
        idx01, gate01 = _route(scores, 2 * _TS)
        uv = jnp.concatenate([peer_u[l], peer_v[l]], axis=1).astype(bf16).reshape(-1, _ROW_TILES, LANES)
        xcur = _experts(scores, idx01, gate01, h2, x1, mod.reshape(bsz, 6, SUBLANES, LANES),
                        g_final.reshape(SUBLANES, LANES), uv, seq)
    return xcur.reshape(bsz, seq, d)
```

```python
import functools

import numpy as np
import jax
import jax.numpy as jnp
from jax import lax
from jax.experimental import pallas as pl
from jax.experimental.pallas import tpu as pltpu

f32 = jnp.float32
bf16 = jnp.bfloat16
i32 = jnp.int32
_HIGHEST = lax.Precision.HIGHEST

D_MODEL = 1024
EPS = 1e-6
HEAD_DIM = 64
ROPE_THETA = 10000.0
NSA_HEADS = 8
NSA_KV_GROUPS = 2
NSA_GROUP = NSA_HEADS // NSA_KV_GROUPS
NSA_WIDTH = NSA_HEADS * HEAD_DIM
KV_WIDTH = NSA_KV_GROUPS * HEAD_DIM
CMP_BLOCK = 32
CMP_STRIDE = 16
CMP_HIDDEN = 128
SEL_BLOCK = 64
SEL_TOPK = 16
WINDOW = 512
FORCE_BONUS = 1000.0
HGRN_HEADS = 4
HGRN_EXPAND = 128
HGRN_HEAD_V = 128
HGRN_WIDTH = HGRN_HEADS * HGRN_EXPAND
HGRN_CHUNK = 32
PEER_HEADS = 8
PEER_NKEYS = 128
PEER_QDIM = 256
PEER_TOPK = 16
GATE_PAD = 128

LANES = 128
SUBLANES = 8
VMEM_LIMIT = 56 * 1024 * 1024

_NEG = -1e30


def _cparams(sem):
    return pltpu.CompilerParams(dimension_semantics=sem, vmem_limit_bytes=VMEM_LIMIT)


def _gelu(x):
    return 0.5 * x * (1.0 + jnp.tanh(0.7978845608028654 * (x + 0.044715 * (x * x * x))))


def _dot(a, b, **kw):
    return jnp.dot(a, b, preferred_element_type=f32, **kw)


def _dot_nt(a, b):
    return lax.dot_general(a, b, (((1,), (1,)), ((), ())), preferred_element_type=f32)


def _rms(x, g):
    return x * lax.rsqrt(jnp.mean(x * x, axis=-1, keepdims=True) + EPS) * g


def _adaln_kernel(c_ref, w_ref, b_ref, o_ref):
    c = c_ref[...]
    cs = c * jax.nn.sigmoid(c)
    o_ref[...] = _dot(cs, w_ref[...], precision=_HIGHEST) + b_ref[...]


def _adaln(c, w, b):
    bsz = c.shape[0]
    tn = 512
    return pl.pallas_call(
        _adaln_kernel,
        grid=(6 * D_MODEL // tn,),
        in_specs=[
            pl.BlockSpec((bsz, D_MODEL), lambda j: (0, 0)),
            pl.BlockSpec((D_MODEL, tn), lambda j: (0, j)),
            pl.BlockSpec((1, tn), lambda j: (0, j)),
        ],
        out_specs=pl.BlockSpec((bsz, tn), lambda j: (0, j)),
        out_shape=jax.ShapeDtypeStruct((bsz, 6 * D_MODEL), f32),
        compiler_params=_cparams(("arbitrary",)),
        name="adaln",
    )(c, w, b)


_C_Q = 0
_C_KC = 512
_C_VC = 640
_C_KS = 768
_C_VS = 896
_C_KW = 1024
_C_VW = 1152
_C_GL = 1280
_C_HQ = _C_GL + GATE_PAD
_C_HF = _C_HQ + 512
_C_HI = _C_HF + 512
_C_HG = _C_HI + 512
_C_MG = _C_HG + 512
_C_END = _C_MG + 2 * D_MODEL


def _inproj_kernel(x_ref, mod_ref, g_ref, w_ref, cos_ref, sin_ref,
                   q_ref, kc_ref, vc_ref, kvsw_ref, gate_ref, hq_ref, hf_ref, hi_ref, hg_ref, mg_ref):
    x = x_ref[...]
    sh = mod_ref[0, 0:1, :]
    sc = mod_ref[0, 1:2, :]
    h = (_rms(x, g_ref[...]) * (1.0 + sc) + sh).astype(bf16)

    def mm(c0, c1):
        return _dot(h, w_ref[:, c0:c1])

    cos = cos_ref[...]
    sin = sin_ref[...]

    def rope(a):
        width = a.shape[1]
        first = (lax.broadcasted_iota(i32, a.shape, 1) & (HEAD_DIM - 1)) < (HEAD_DIM // 2)
        partner = jnp.where(first, pltpu.roll(a, width - HEAD_DIM // 2, 1), pltpu.roll(a, HEAD_DIM // 2, 1))
        return a * cos[:, :width] + partner * sin[:, :width]

    q_ref[...] = (rope(mm(_C_Q, _C_KC)) * (HEAD_DIM ** -0.5)).astype(bf16)
    kc_ref[...] = rope(mm(_C_KC, _C_VC)).astype(bf16)
    vc_ref[...] = mm(_C_VC, _C_KS).astype(bf16)
    kvsw_ref[:, 0:128] = rope(mm(_C_KS, _C_VS)).astype(bf16)
    kvsw_ref[:, 128:256] = mm(_C_VS, _C_KW).astype(bf16)
    kvsw_ref[:, 256:384] = rope(mm(_C_KW, _C_VW)).astype(bf16)
    kvsw_ref[:, 384:512] = mm(_C_VW, _C_GL).astype(bf16)
    gate_ref[...] = jax.nn.sigmoid(mm(_C_GL, _C_HQ))
    hq_ref[...] = mm(_C_HQ, _C_HF).astype(bf16)
    hf_ref[...] = mm(_C_HF, _C_HI)
    hi_ref[...] = mm(_C_HI, _C_HG).astype(bf16)
    hg_ref[...] = mm(_C_HG, _C_MG).astype(bf16)
    mg_ref[...] = jax.nn.sigmoid(mm(_C_MG, _C_END)).astype(bf16)


def _inproj(x2, mod3, g_mix, w_pad, cos_t, sin_t, seq):
    n = x2.shape[0]
    tm = 256
    tiles_per_seq = seq // tm
    row = lambda i: (i, 0)
    outs = [
        (NSA_WIDTH, bf16), (KV_WIDTH, bf16), (KV_WIDTH, bf16), (4 * KV_WIDTH, bf16), (GATE_PAD, f32),
        (HGRN_WIDTH, bf16), (HGRN_WIDTH, f32), (HGRN_WIDTH, bf16), (HGRN_WIDTH, bf16), (2 * D_MODEL, bf16),
    ]
    return pl.pallas_call(
        _inproj_kernel,
        grid=(n // tm,),
        in_specs=[
            pl.BlockSpec((tm, D_MODEL), row),
            pl.BlockSpec((1, 6, D_MODEL), lambda i: (i // tiles_per_seq, 0, 0)),
            pl.BlockSpec((1, D_MODEL), lambda i: (0, 0)),
            pl.BlockSpec((D_MODEL, _C_END), lambda i: (0, 0)),
            pl.BlockSpec((tm, NSA_WIDTH), lambda i: (i % tiles_per_seq, 0)),
            pl.BlockSpec((tm, NSA_WIDTH), lambda i: (i % tiles_per_seq, 0)),
        ],
        out_specs=[pl.BlockSpec((tm, w), row) for w, _ in outs],
        out_shape=[jax.ShapeDtypeStruct((n, w), dt) for w, dt in outs],
        compiler_params=_cparams(("arbitrary",)),
        name="inproj",
    )(x2, mod3, g_mix, w_pad, cos_t, sin_t)


def _compress_kernel(kc_ref, vc_ref, wkt_ref, wkb_ref, wvt_ref, wvb_ref, pk_ref, pv_ref,
                     w1k_ref, w1v_ref, w2k_ref, w2v_ref, okc_ref, ovc_ref):
    def one(x_ref, wt_ref, wb_ref, pos_ref, w1_ref, w2_ref, o_ref):
        pieces = x_ref[0]
        top = _dot(pieces, wt_ref[...])
        bot = _dot(pieces, wb_ref[...])
        nrow = bot.shape[0]
        bot = pltpu.roll(bot, nrow - 1, 0)
        cpos = _dot(pos_ref[...], w1_ref[...], precision=_HIGHEST)
        w2 = w2_ref[...].astype(bf16)
        outs = []
        for g in range(NSA_KV_GROUPS):
            sl = slice(g * CMP_HIDDEN, (g + 1) * CMP_HIDDEN)
            hid = _gelu(top[:, sl] + bot[:, sl] + cpos)
            outs.append(_dot(hid.astype(bf16), w2))
        o_ref[0] = jnp.concatenate(outs, axis=1).astype(bf16)

    one(kc_ref, wkt_ref, wkb_ref, pk_ref, w1k_ref, w2k_ref, okc_ref)
    one(vc_ref, wvt_ref, wvb_ref, pv_ref, w1v_ref, w2v_ref, ovc_ref)


def _compress(kc3, vc3, wkt, wkb, wvt, wvb, pk, pv, w1k, w1v, w2k, w2v):
    bsz, npieces, width = kc3.shape
    full = lambda a: pl.BlockSpec(a.shape, lambda b: (0,) * a.ndim)
    per_b = pl.BlockSpec((1, npieces, width), lambda b: (b, 0, 0))
    out_b = pl.BlockSpec((1, npieces, KV_WIDTH), lambda b: (b, 0, 0))
    return pl.pallas_call(
        _compress_kernel,
        grid=(bsz,),
        in_specs=[per_b, per_b] + [full(a) for a in (wkt, wkb, wvt, wvb, pk, pv, w1k, w1v, w2k, w2v)],
        out_specs=[out_b, out_b],
        out_shape=[jax.ShapeDtypeStruct((bsz, npieces, KV_WIDTH), bf16)] * 2,
        compiler_params=_cparams(("arbitrary",)),
        name="compress",
    )(kc3, vc3, wkt, wkb, wvt, wvb, pk, pv, w1k, w1v, w2k, w2v)


_TQ = 128
_NSA_CLASSES = 4


def _softmax_parts(s, mask):
    sm = jnp.where(mask, s, _NEG)
    m = jnp.max(sm, axis=-1, keepdims=True)
    p = jnp.where(mask, jnp.exp(sm - m), 0.0)
    den = jnp.maximum(jnp.sum(p, axis=-1, keepdims=True), 1e-30)
    return p, den


def _softmax_plain(s):
    p = jnp.exp(s - jnp.max(s, axis=-1, keepdims=True))
    return p, jnp.sum(p, axis=-1, keepdims=True)


def _nsa_kernel(q_ref, kvsw_ref, kc_ref, vc_ref, gate_ref, selmap_ref, expand_ref, o_ref, *, seq):
    tq = _TQ
    qi = pl.program_id(1)
    n_cls = _NSA_CLASSES
    per_cls = (seq // tq) // n_cls
    for cls in range(n_cls):
        @pl.when((qi >= cls * per_cls) & (qi < (cls + 1) * per_cls))
        def _(cls=cls):
            _nsa_tile(q_ref, kvsw_ref, kc_ref, vc_ref, gate_ref, selmap_ref, expand_ref, o_ref,
                      seq=seq, width=(cls + 1) * (seq // n_cls))


def _nsa_tile(q_ref, kvsw_ref, kc_ref, vc_ref, gate_ref, selmap_ref, expand_ref, o_ref, *, seq, width):
    tq = _TQ
    rows = NSA_GROUP * tq
    n_sel = seq // SEL_BLOCK
    n_cmp_pad = seq // CMP_STRIDE
    t0 = pl.program_id(1) * tq
    q = q_ref[...]
    gates = gate_ref[...]
    tlane = t0 + lax.broadcasted_iota(i32, (1, tq), 1)
    t4 = t0 + (lax.broadcasted_iota(i32, (rows, 1), 0) & (tq - 1))
    wk = WINDOW + tq
    ws = pl.multiple_of(jnp.maximum(t0 - WINDOW, 0), tq)
    ks_all = kvsw_ref[0, 0:width, 0:128]
    vs_all = kvsw_ref[0, 0:width, 128:256]
    kw_all = kvsw_ref[0, pl.ds(ws, wk), 256:384]
    vw_all = kvsw_ref[0, pl.ds(ws, wk), 384:512]
    kc_all = kc_ref[0]
    vc_all = vc_ref[0]
    cend = lax.broadcasted_iota(i32, (1, n_cmp_pad), 1) * CMP_STRIDE + (CMP_BLOCK - 1)
    blk = lax.broadcasted_iota(i32, (n_sel, 1), 0)
    kpos = lax.broadcasted_iota(i32, (1, width), 1)
    kpos_w = ws + lax.broadcasted_iota(i32, (1, wk), 1)
    cur = tlane >> 6
    forced = (blk == 0) | (blk == cur) | (blk == cur - 1)
    causal_blk = blk * SEL_BLOCK <= tlane
    pieces = []
    for g in range(NSA_KV_GROUPS):
        gs = slice(g * HEAD_DIM, (g + 1) * HEAD_DIM)
        qg = jnp.concatenate(
            [q[:, (NSA_GROUP * g + r) * HEAD_DIM:(NSA_GROUP * g + r + 1) * HEAD_DIM] for r in range(NSA_GROUP)], axis=0)
        p_c, den_c = _softmax_parts(_dot_nt(qg, kc_all[:, gs]), cend <= t4)
        o_c = _dot(p_c.astype(bf16), vc_all[:, gs]) / den_c
        pn = p_c / den_c
        pc_sum = pn[0:tq]
        for r in range(1, NSA_GROUP):
            pc_sum = pc_sum + pn[r * tq:(r + 1) * tq]
        imp = lax.dot_general(selmap_ref[...], pc_sum, (((1,), (1,)), ((), ())),
                              preferred_element_type=f32, precision=_HIGHEST)
        imp = jnp.where(forced, imp + FORCE_BONUS, imp)
        imp = jnp.where(causal_blk, imp, -1.0)
        rank = jnp.zeros((n_sel, tq), f32)
        for j in range(n_sel):
            row = imp[j:j + 1, :]
            ahead = jnp.where(row > imp, 1.0, jnp.where(row == imp, jnp.where(blk > j, 1.0, 0.0), 0.0))
            rank = rank + ahead
        bias = jnp.where(rank < float(min(SEL_TOPK, n_sel)), jnp.where(causal_blk, 0.0, _NEG), _NEG).astype(bf16)
        bias_q = lax.dot_general(bias, expand_ref[:, 0:width], (((0,), (0,)), ((), ())),
                                 preferred_element_type=f32)
        bias4 = jnp.concatenate([bias_q] * NSA_GROUP, axis=0)
        p_s, den_s = _softmax_plain(jnp.where(kpos <= t4, _dot_nt(qg, ks_all[:, gs]) + bias4, _NEG))
        o_s = _dot(p_s.astype(bf16), vs_all[:, gs]) / den_s
        in_window = (t4 - kpos_w).astype(jnp.uint32) < WINDOW
        p_w, den_w = _softmax_plain(jnp.where(in_window, _dot_nt(qg, kw_all[:, gs]), _NEG))
        o_w = _dot(p_w.astype(bf16), vw_all[:, gs]) / den_w

        def gcol(br):
            return jnp.concatenate(
                [gates[:, (NSA_GROUP * g + r) * 3 + br:(NSA_GROUP * g + r) * 3 + br + 1] for r in range(NSA_GROUP)], axis=0)

        o = gcol(0) * o_c + gcol(1) * o_s + gcol(2) * o_w
        pieces += [o[r * tq:(r + 1) * tq] for r in range(NSA_GROUP)]
    o_ref[...] = jnp.concatenate(pieces, axis=1).astype(bf16)


def _nsa(q, kvsw3, kc3, vc3, gates, selmap, expand, seq):
    n = q.shape[0]
    bsz = n // seq
    tq = _TQ
    nq = seq // tq
    return pl.pallas_call(
        functools.partial(_nsa_kernel, seq=seq),
        grid=(bsz, nq),
        in_specs=[
            pl.BlockSpec((tq, NSA_WIDTH), lambda b, i: (b * nq + i, 0)),
            pl.BlockSpec((1, seq, 4 * KV_WIDTH), lambda b, i: (b, 0, 0)),
            pl.BlockSpec((1,) + kc3.shape[1:], lambda b, i: (b, 0, 0)),
            pl.BlockSpec((1,) + vc3.shape[1:], lambda b, i: (b, 0, 0)),
            pl.BlockSpec((tq, GATE_PAD), lambda b, i: (b * nq + i, 0)),
            pl.BlockSpec(selmap.shape, lambda b, i: (0, 0)),
            pl.BlockSpec(expand.shape, lambda b, i: (0, 0)),
        ],
        out_specs=pl.BlockSpec((tq, NSA_WIDTH), lambda b, i: (b * nq + i, 0)),
        out_shape=jax.ShapeDtypeStruct((n, NSA_WIDTH), bf16),
        compiler_params=_cparams(("arbitrary", "arbitrary")),
        name="nsa",
    )(q, kvsw3, kc3, vc3, gates, selmap, expand)


def _hgrn_kernel(hq_ref, hf_ref, hi_ref, hg_ref, lbl_ref, ng_ref, o_ref, st_ref, *, seq, layer):
    c = HGRN_CHUNK
    logits = lbl_ref[...]
    e = jnp.exp(logits - jnp.max(logits, axis=0, keepdims=True))
    sm = e / jnp.sum(e, axis=0, keepdims=True)
    lb = sm[0:1]
    for l in range(1, layer + 1):
        lb = lb + sm[l:l + 1]
    st_ref[...] = jnp.zeros_like(st_ref)
    r_i = lax.broadcasted_iota(i32, (c, c), 0)
    c_i = lax.broadcasted_iota(i32, (c, c), 1)
    causal = r_i >= c_i
    r_col = lax.broadcasted_iota(i32, (c, 1), 0)
    ng = ng_ref[...]

    def body(ci, carry):
        r0 = pl.multiple_of(ci * c, c)
        hq = hq_ref[pl.ds(r0, c), :].astype(f32)
        hf = hf_ref[pl.ds(r0, c), :]
        v = hi_ref[pl.ds(r0, c), :]
        hg = hg_ref[pl.ds(r0, c), :].astype(f32)
        f = lb + (1.0 - lb) * jax.nn.sigmoid(hf)
        k = 1.0 - f
        qv = hq * jax.nn.sigmoid(hq)
        bcum = jnp.log(f)
        for d in (1, 2, 4, 8, 16):
            bcum = bcum + jnp.where(r_col >= d, pltpu.roll(bcum, d, 0), 0.0)
        bend = bcum[c - 1:c, :]
        q_dec = (qv * jnp.exp(bcum)).astype(bf16)
        k_inv = (k * jnp.exp(-bcum)).astype(bf16)
        k_end = (k * jnp.exp(bend - bcum)).astype(bf16)
        dec_end = jnp.exp(bend)
        outs = []
        for h in range(HGRN_HEADS):
            sl = slice(h * HGRN_EXPAND, (h + 1) * HGRN_EXPAND)
            a = jnp.where(causal, _dot_nt(q_dec[:, sl], k_inv[:, sl]), 0.0)
            st = st_ref[h]
            o = _dot(a.astype(bf16), v[:, sl]) + _dot_nt(q_dec[:, sl], st.astype(bf16))
            upd = lax.dot_general(v[:, sl], k_end[:, sl], (((0,), (0,)), ((), ())), preferred_element_type=f32)
            st_ref[h] = st * dec_end[:, sl] + upd
            y = _rms(o, ng) * (hg[:, sl] * jax.nn.sigmoid(hg[:, sl]))
            outs.append(y)
        o_ref[pl.ds(r0, c), :] = jnp.concatenate(outs, axis=1).astype(bf16)
        return carry

    lax.fori_loop(0, seq // c, body, 0, unroll=8)


def _hgrn(hq, hf, hi, hg, lb_logits, norm_g, seq, layer):
    n = hq.shape[0]
    bsz = n // seq
    per_b = pl.BlockSpec((seq, HGRN_WIDTH), lambda b: (b, 0))
    return pl.pallas_call(
        functools.partial(_hgrn_kernel, seq=seq, layer=layer),
        grid=(bsz,),
        in_specs=[per_b, per_b, per_b, per_b,
                  pl.BlockSpec(lb_logits.shape, lambda b: (0, 0)),
                  pl.BlockSpec(norm_g.shape, lambda b: (0, 0))],
        out_specs=per_b,
        out_shape=jax.ShapeDtypeStruct((n, HGRN_WIDTH), bf16),
        scratch_shapes=[pltpu.VMEM((HGRN_HEADS, HGRN_HEAD_V, HGRN_EXPAND), f32)],
        compiler_params=_cparams(("arbitrary",)),
        name="hgrn",
    )(hq, hf, hi, hg, lb_logits, norm_g)


def _merge_kernel(on_ref, oh_ref, mg_ref, x_ref, mod_ref, g_ref, wb0_ref, wb1_ref, wo_ref, x1_ref, h2_ref, hb_ref):
    a = _dot(on_ref[...], wb0_ref[...])
    b = _dot(oh_ref[...], wb1_ref[...])
    mg = mg_ref[...].astype(f32)
    y = mg[:, :D_MODEL] * a + mg[:, D_MODEL:] * b
    y2 = _dot(y.astype(bf16), wo_ref[...])
    x1 = x_ref[...] + mod_ref[0, 2:3, :] * y2
    h2 = _rms(x1, g_ref[...]) * (1.0 + mod_ref[0, 4:5, :]) + mod_ref[0, 3:4, :]
    rows = x1.shape[0]
    x1_ref[...] = x1.reshape(rows, SUBLANES, LANES)
    h2_ref[...] = h2.reshape(rows, SUBLANES, LANES)
    hb_ref[...] = h2.astype(bf16)


def _merge(o_nsa, o_hgrn, mg, x2, mod3, g_ffn, wb0, wb1, wo, seq):
    n = x2.shape[0]
    tm = 256
    tiles_per_seq = seq // tm
    row = lambda i: (i, 0)
    full = lambda a: pl.BlockSpec(a.shape, lambda i: (0,) * a.ndim)
    return pl.pallas_call(
        _merge_kernel,
        grid=(n // tm,),
        in_specs=[
            pl.BlockSpec((tm, NSA_WIDTH), row), pl.BlockSpec((tm, HGRN_WIDTH), row),
            pl.BlockSpec((tm, 2 * D_MODEL), row), pl.BlockSpec((tm, D_MODEL), row),
            pl.BlockSpec((1, 6, D_MODEL), lambda i: (i // tiles_per_seq, 0, 0)),
            full(g_ffn), full(wb0), full(wb1), full(wo),
        ],
        out_specs=[pl.BlockSpec((tm, SUBLANES, LANES), lambda i: (i, 0, 0))] * 2 + [pl.BlockSpec((tm, D_MODEL), row)],
        out_shape=[jax.ShapeDtypeStruct((n, SUBLANES, LANES), f32)] * 2 + [jax.ShapeDtypeStruct((n, D_MODEL), bf16)],
        compiler_params=_cparams(("arbitrary",)),
        name="merge",
    )(o_nsa, o_hgrn, mg, x2, mod3, g_ffn, wb0, wb1, wo)


_TR = 256


def _topk_rows(s, k, payload=None):
    n = s.shape[0]
    rowid = lax.broadcasted_iota(i32, s.shape, 0).astype(f32)
    vals, picks = [], []
    for _ in range(k):
        m = jnp.max(s, axis=0, keepdims=True)
        i = jnp.min(jnp.where(s == m, rowid, float(n)), axis=0, keepdims=True)
        hit = rowid == i
        vals.append(m)
        picks.append(i if payload is None else jnp.max(jnp.where(hit, payload, -1.0), axis=0, keepdims=True))
        s = jnp.where(hit, -jnp.inf, s)
    return vals, picks


def _pair_candidates(v1, i1, v2, i2):
    k = len(v1)
    s2 = jnp.concatenate(v2, axis=0)
    e2 = jnp.concatenate(i2, axis=0)
    sub = lax.broadcasted_iota(i32, (SUBLANES, s2.shape[1]), 0)
    comb, cand = [], []
    for a in range(k // 2):
        nb = k // (a + 1)
        rows = -(-nb // SUBLANES) * SUBLANES
        c = v1[a] + s2[0:rows]
        if nb < rows:
            c = jnp.where(sub < nb, c, -jnp.inf)
        comb.append(c)
        cand.append(i1[a] * float(PEER_NKEYS) + e2[0:rows])
    comb.append(jnp.concatenate(v1[k // 2:], axis=0) + s2[0:1])
    cand.append(jnp.concatenate(i1[k // 2:], axis=0) * float(PEER_NKEYS) + e2[0:1])
    return jnp.concatenate(comb, axis=0), jnp.concatenate(cand, axis=0)


def _head_routing(tops1, tops2):
    k = PEER_TOPK
    comb, cand = _pair_candidates(tops1[0], tops1[1], tops2[0], tops2[1])
    tv, picked = _topk_rows(comb, k, payload=cand)
    ex = [jnp.exp(tv[r] - tv[0]) for r in range(k)]
    den = ex[0]
    for r in range(1, k):
        den = den + ex[r]
    return picked, [e / den for e in ex]


def _scores_kernel(h_ref, wq_ref, sk_ref, o_ref):
    q = _dot(h_ref[...], wq_ref[...]).astype(bf16)
    half = PEER_QDIM // 2
    for grp in range(2 * PEER_HEADS):
        o_ref[grp] = _dot_nt(sk_ref[grp], q[:, grp * half:(grp + 1) * half])


def _scores(h2, wq, sk):
    n = h2.shape[0]
    tr = _TR
    return pl.pallas_call(
        _scores_kernel,
        grid=(n // tr,),
        in_specs=[
            pl.BlockSpec((tr, D_MODEL), lambda i: (i, 0)),
            pl.BlockSpec(wq.shape, lambda i: (0, 0)),
            pl.BlockSpec(sk.shape, lambda i: (0, 0, 0)),
        ],
        out_specs=pl.BlockSpec((2 * PEER_HEADS, PEER_NKEYS, tr), lambda i: (0, 0, i)),
        out_shape=jax.ShapeDtypeStruct((2 * PEER_HEADS, PEER_NKEYS, n), f32),
        compiler_params=_cparams(("arbitrary",)),
        name="scores",
    )(h2, wq, sk)


def _route_kernel(s_ref, idx_ref, gate_ref):
    k = PEER_TOPK
    experts, gates = [], []
    for hd in range(PEER_HEADS):
        picked, g = _head_routing(_topk_rows(s_ref[2 * hd], k), _topk_rows(s_ref[2 * hd + 1], k))
        experts += picked
        gates += g
    idx_t = jnp.concatenate(experts, axis=0).astype(i32)
    gate_t = jnp.concatenate(gates, axis=0)
    for blk in range(idx_t.shape[1] // LANES):
        sl = slice(blk * LANES, (blk + 1) * LANES)
        idx_ref[sl, :] = idx_t[:, sl].T
        gate_ref[sl, :] = gate_t[:, sl].T


def _route(scores, n_tok):
    ncol = PEER_HEADS * PEER_TOPK
    return pl.pallas_call(
        _route_kernel,
        grid=(1,),
        in_specs=[pl.BlockSpec((2 * PEER_HEADS, PEER_NKEYS, n_tok), lambda i: (0, 0, 0))],
        out_specs=[pl.BlockSpec((n_tok, ncol), lambda i: (0, 0))] * 2,
        out_shape=[jax.ShapeDtypeStruct((n_tok, ncol), i32), jax.ShapeDtypeStruct((n_tok, ncol), f32)],
        compiler_params=_cparams(("arbitrary",)),
        name="route",
    )(scores)


_TS = 128
_HALF = 64
_TG = 4
_NSLOT = 4
_AHEAD = (_NSLOT - 1) * _TG
_NROW = PEER_HEADS * PEER_TOPK
_ROW_TILES = 2 * D_MODEL // LANES
_TP = 2
_RING = 3
_RING_ROWS = (_RING + 2) * _TS


def _expert_kernel(sc_ref, h_ref, x1_ref, mod_ref, gfin_ref, rsum_ref, rrep_ref, idx01_ref, gate01_ref, uv_ref, o_ref,
                   *scratch):
    step = pl.program_id(0)
    last = pl.num_programs(0) - 1
    bufs = scratch[:_NSLOT]
    out_tiles, idx_stage, gate_stage, idx_rows, gate_ring, idx_ring, sem, aux = scratch[_NSLOT:]
    n_dma = _TG * _NROW
    k = PEER_TOPK
    cur = lax.rem(step, _RING)
    nxt = lax.rem(step + 2, _RING)
    mirror = jnp.where(nxt == 0, _RING * _TS, (_RING + 1) * _TS)

    def row_copy(e, slot, j):
        return pltpu.make_async_copy(uv_ref.at[e], bufs[slot].at[j], sem.at[slot])

    def wait(slot):
        pltpu.make_async_copy(uv_ref.at[pl.ds(0, n_dma)], bufs[slot], sem.at[slot]).wait()

    def ring_copies():
        return (pltpu.make_async_copy(idx_rows, idx_ring.at[pl.ds(nxt * _TS, _TS)], aux.at[0]),
                pltpu.make_async_copy(idx_rows, idx_ring.at[pl.ds(mirror, _TS)], aux.at[1]))

    @pl.when(step == 0)
    def _():
        prime = (pltpu.make_async_copy(idx01_ref, idx_ring.at[pl.ds(0, 2 * _TS)], aux.at[0]),
                 pltpu.make_async_copy(idx01_ref.at[pl.ds(0, _TS)], idx_ring.at[pl.ds(_RING * _TS, _TS)], aux.at[1]),
                 pltpu.make_async_copy(gate01_ref, gate_ring.at[pl.ds(0, 2 * _TS)], aux.at[2]))
        for c in prime:
            c.start()
        for c in prime:
            c.wait()
        for slot in range(_NSLOT - 1):
            def body(t, c, slot=slot):
                for r in range(_NROW):
                    row_copy(idx_ring[slot * _TG + t, r], slot, t * _NROW + r).start(priority=r % 2)
                return c
            lax.fori_loop(0, _TG, body, 0)

    gt_f = mod_ref[0, 5]
    gfin = gfin_ref[...]
    width = _TP * LANES
    r_i = lax.broadcasted_iota(i32, (width, width), 0) // LANES
    c_i = lax.broadcasted_iota(i32, (width, width), 1) // LANES
    lane_sum = jnp.where(r_i == c_i, 1.0, 0.0).astype(bf16)
    eye = jnp.where(lax.broadcasted_iota(i32, (_NROW, LANES), 0) == lax.broadcasted_iota(i32, (_NROW, LANES), 1),
                    1.0, 0.0)

    def evaluate(tok0, slot, t0):
        src = bufs[slot]
        z, vs = [], []
        for dt in range(_TP):
            uv = src[pl.ds((t0 + dt) * _NROW, _NROW)].astype(f32)
            vs.append(uv[:, SUBLANES:2 * SUBLANES, :])
            u = uv[:, 0:SUBLANES, :]
            z.append((u * h_ref[tok0 + dt][None]).reshape(_NROW * SUBLANES, LANES).astype(bf16))
        part = _dot(rsum_ref[...], jnp.concatenate(z, axis=1))
        a = _dot(part.astype(bf16), lane_sum)
        g = jnp.concatenate(
            [jnp.broadcast_to(jnp.sum(eye * gate_ring[pl.ds(cur * _TS + tok0 + dt, 1), :], axis=1, keepdims=True),
                              (_NROW, LANES)) for dt in range(_TP)], axis=1)
        w = (g * _gelu(a)).astype(bf16)
        spread = _dot(rrep_ref[...], w)
        for dt in range(_TP):
            tok = tok0 + dt
            wt = spread[:, dt * LANES:(dt + 1) * LANES].reshape(_NROW, SUBLANES, LANES)
            y = jnp.sum(wt * vs[dt], axis=0)
            x2 = x1_ref[tok] + gt_f * y
            ms = jnp.sum(jnp.sum(x2 * x2, axis=1, keepdims=True), axis=0, keepdims=True) * (1.0 / D_MODEL)
            out_tiles[tok] = x2 * lax.rsqrt(ms + EPS) * gfin

    def half_step(half, carry):
        @pl.when((half == _TS // _HALF - 1) & (step > 0))
        def _():
            for c in ring_copies():
                c.wait()

        tok_h = half * _HALF
        ring_h = cur * _TS + tok_h
        head0 = half * (PEER_HEADS // 2)
        experts, gates, tops = [], [], {}
        pieces = []
        for hh in range(PEER_HEADS // 2):
            for p in range(2):
                def sub_keys(hh=hh, p=p):
                    tops[hh, p] = _topk_rows(sc_ref[(head0 + hh) * 2 + p], k)
                pieces.append(sub_keys)

            def pairs(hh=hh):
                picked, g = _head_routing(tops[hh, 0], tops[hh, 1])
                experts.extend(picked)
                gates.extend(g)
            pieces.append(pairs)

        def stage():
            rows = pl.ds(pl.multiple_of(head0 * k, _HALF), _HALF)
            idx_stage[rows, :] = jnp.concatenate(experts, axis=0).astype(i32)
            gate_stage[rows, :] = jnp.concatenate(gates, axis=0)
        pieces.append(stage)

        for grp in range(_HALF // _TG):
            slot = grp % _NSLOT
            wait(slot)
            if grp < len(pieces):
                pieces[grp]()
            dst = (slot + _NSLOT - 1) % _NSLOT
            for t0 in range(0, _TG, _TP):
                evaluate(tok_h + grp * _TG + t0, slot, t0)
                for t in range(t0, t0 + _TP):
                    for r in range(_NROW):
                        row_copy(idx_ring[ring_h + grp * _TG + t + _AHEAD, r], dst, t * _NROW + r).start(priority=r % 2)
        return carry

    lax.fori_loop(0, _TS // _HALF, half_step, 0)
    o_ref[...] = out_tiles[...].reshape(_TS, D_MODEL)
    idx_rows[...] = idx_stage[...].T
    gate_ring[pl.ds(pl.multiple_of(nxt * _TS, _TS), _TS), :] = gate_stage[...].T
    for c in ring_copies():
        c.start()

    @pl.when(step == last)
    def _():
        for c in ring_copies():
            c.wait()
        for slot in range(_NSLOT - 1):
            wait(slot)


def _experts(scores, idx01, gate01, h3, x13, mod4, gfin, uv, seq):
    n = h3.shape[0]
    n_steps = n // _TS
    steps_per_seq = seq // _TS
    tok3 = lambda i: (i, 0, 0)
    rsum = jnp.asarray(np.repeat(np.eye(_NROW, dtype=np.float32), SUBLANES, axis=1), dtype=bf16)
    buf = pltpu.VMEM((_TG * _NROW, _ROW_TILES, LANES), bf16)
    return pl.pallas_call(
        _expert_kernel,
        grid=(n_steps,),
        in_specs=[
            pl.BlockSpec((2 * PEER_HEADS, PEER_NKEYS, _TS), lambda i: (0, 0, jnp.minimum(i + 2, n_steps - 1))),
            pl.BlockSpec((_TS, SUBLANES, LANES), tok3),
            pl.BlockSpec((_TS, SUBLANES, LANES), tok3),
            pl.BlockSpec((1, 6, SUBLANES, LANES), lambda i: (i // steps_per_seq, 0, 0, 0)),
            pl.BlockSpec((SUBLANES, LANES), lambda i: (0, 0)),
            pl.BlockSpec(rsum.shape, lambda i: (0, 0)),
            pl.BlockSpec(rsum.shape[::-1], lambda i: (0, 0)),
            pl.BlockSpec(memory_space=pl.ANY),
            pl.BlockSpec(memory_space=pl.ANY),
            pl.BlockSpec(memory_space=pl.ANY),
        ],
        out_specs=pl.BlockSpec((_TS, D_MODEL), lambda i: (i, 0)),
        out_shape=jax.ShapeDtypeStruct((n, D_MODEL), f32),
        scratch_shapes=[buf] * _NSLOT + [
            pltpu.VMEM((_TS, SUBLANES, LANES), f32),
            pltpu.VMEM((_NROW, _TS), i32),
            pltpu.VMEM((_NROW, _TS), f32),
            pltpu.VMEM((_TS, _NROW), i32),
            pltpu.VMEM((_RING * _TS, _NROW), f32),
            pltpu.SMEM((_RING_ROWS, _NROW), i32),
            pltpu.SemaphoreType.DMA((_NSLOT,)),
            pltpu.SemaphoreType.DMA((3,)),
        ],
        compiler_params=_cparams(("arbitrary",)),
        name="experts",
    )(scores, h3, x13, mod4, gfin, rsum, rsum.T, idx01, gate01, uv)


def _rope_tables(seq):
    half = HEAD_DIM // 2
    inv = ROPE_THETA ** (-np.arange(half, dtype=np.float32) / half)
    ang = np.arange(seq, dtype=np.float32)[:, None] * inv[None, :].astype(np.float32)
    cos = np.cos(ang).astype(np.float32)
    sin = np.sin(ang).astype(np.float32)
    cos_t = np.tile(np.concatenate([cos, cos], axis=1), (1, NSA_HEADS))
    sin_t = np.tile(np.concatenate([-sin, sin], axis=1), (1, NSA_HEADS))
    return jnp.asarray(cos_t), jnp.asarray(sin_t)


def _selection_map(n_cmp_pad, n_sel):
    r_sel = SEL_BLOCK // CMP_STRIDE
    r_cmp = CMP_BLOCK // CMP_STRIDE
    i = np.arange(n_cmp_pad)[:, None]
    j = np.arange(n_sel)[None, :]
    d = i - r_sel * j
    cnt = np.minimum(d, r_sel - 1) - np.maximum(d - r_cmp + 1, 0) + 1
    cnt = np.clip(cnt, 0, None).astype(np.float32)
    cnt[n_cmp_pad - r_cmp + 1:] = 0.0
    return jnp.asarray(cnt.T)


def _block_expand(n_sel, seq):
    e = (np.arange(seq)[None, :] // SEL_BLOCK == np.arange(n_sel)[:, None]).astype(np.float32)
    return jnp.asarray(e, dtype=bf16)


def _compress_weights(w1):
    eye = jnp.eye(NSA_KV_GROUPS, dtype=w1.dtype)
    out = []
    for part in range(CMP_BLOCK // CMP_STRIDE):
        w = w1[part * CMP_STRIDE * HEAD_DIM:(part + 1) * CMP_STRIDE * HEAD_DIM].reshape(CMP_STRIDE, HEAD_DIM, CMP_HIDDEN)
        big = jnp.einsum('pdc,gh->pgdhc', w, eye).reshape(CMP_STRIDE * KV_WIDTH, NSA_KV_GROUPS * CMP_HIDDEN)
        out.append(big.astype(bf16))
    return out


def kernel(x, c, w_ada, b_ada, g_mix, g_ffn, w_in, cmp_pos_k, cmp_pos_v, w_ck1, w_ck2, w_cv1, w_cv2, hgrn_lb_logits, hgrn_out_norm, w_branch, w_out, w_peer_q, peer_sub_keys, peer_u, peer_v, g_final):
    bsz, seq, d = x.shape
    n = bsz * seq
    depth = w_ada.shape[0]
    assert depth == 1, "single-layer block only"
    n_sel = seq // SEL_BLOCK
    n_pieces = seq // CMP_STRIDE
    cos_t, sin_t = _rope_tables(seq)
    selmap = _selection_map(n_pieces, n_sel)
    expand = _block_expand(n_sel, seq)
    xcur = x.reshape(n, d)
    for l in range(depth):
        mod = _adaln(c, w_ada[l], b_ada[l].reshape(1, 6 * d))
        mod3 = mod.reshape(bsz, 6, d)
        w_pad = jnp.concatenate(
            [w_in[l][:, :_C_GL + 3 * NSA_HEADS], jnp.zeros((d, GATE_PAD - 3 * NSA_HEADS), w_in.dtype),
             w_in[l][:, _C_GL + 3 * NSA_HEADS:]], axis=1).astype(bf16)
        q, kc, vc, kvsw, gates, hq, hf, hi, hg, mg = _inproj(xcur, mod3, g_mix[l].reshape(1, d), w_pad, cos_t, sin_t, seq)
        wkt, wkb = _compress_weights(w_ck1[l])
        wvt, wvb = _compress_weights(w_cv1[l])
        kc_c, vc_c = _compress(
            kc.reshape(bsz, n_pieces, CMP_STRIDE * KV_WIDTH), vc.reshape(bsz, n_pieces, CMP_STRIDE * KV_WIDTH),
            wkt, wkb, wvt, wvb, cmp_pos_k[l].reshape(1, -1), cmp_pos_v[l].reshape(1, -1),
            w_ck1[l], w_cv1[l], w_ck2[l], w_cv2[l])
        o_nsa = _nsa(q, kvsw.reshape(bsz, seq, 4 * KV_WIDTH), kc_c, vc_c, gates, selmap, expand, seq)
        o_hgrn = _hgrn(hq, hf, hi, hg, hgrn_lb_logits, hgrn_out_norm[l].reshape(1, -1), seq, l)
        x1, h2, h2_b = _merge(o_nsa, o_hgrn, mg, xcur, mod3, g_ffn[l].reshape(1, d),
                              w_branch[l, 0].astype(bf16), w_branch[l, 1].astype(bf16), w_out[l].astype(bf16), seq)
        wq = w_peer_q[l].astype(bf16)
        sk = jnp.transpose(peer_sub_keys[l], (1, 0, 2, 3)).reshape(2 * PEER_HEADS, PEER_NKEYS, PEER_QDIM // 2).astype(bf16)
        scores = _scores(h2_b, wq, sk)
        idx01, gate01 = _route(scores, 2 * _TS)
        uv = jnp.concatenate([peer_u[l], peer_v[l]], axis=1).astype(bf16).reshape(-1, _ROW_TILES, LANES)
        xcur = _experts(scores, idx01, gate01, h2, x1, mod.reshape(bsz, 6, SUBLANES, LANES),
                        g_final.reshape(SUBLANES, LANES), uv, seq)
    return xcur.reshape(bsz, seq, d)
```

```python
import functools

import numpy as np
import jax
import jax.numpy as jnp
from jax import lax
from jax.experimental import pallas as pl
from jax.experimental.pallas import tpu as pltpu

f32 = jnp.float32
bf16 = jnp.bfloat16
i32 = jnp.int32
_HIGHEST = lax.Precision.HIGHEST

D_MODEL = 1024
EPS = 1e-6
HEAD_DIM = 64
ROPE_THETA = 10000.0
NSA_HEADS = 8
NSA_KV_GROUPS = 2
NSA_GROUP = NSA_HEADS // NSA_KV_GROUPS
NSA_WIDTH = NSA_HEADS * HEAD_DIM
KV_WIDTH = NSA_KV_GROUPS * HEAD_DIM
CMP_BLOCK = 32
CMP_STRIDE = 16
CMP_HIDDEN = 128
SEL_BLOCK = 64
SEL_TOPK = 16
WINDOW = 512
FORCE_BONUS = 1000.0
HGRN_HEADS = 4
HGRN_EXPAND = 128
HGRN_HEAD_V = 128
HGRN_WIDTH = HGRN_HEADS * HGRN_EXPAND
HGRN_CHUNK = 32
PEER_HEADS = 8
PEER_NKEYS = 128
PEER_QDIM = 256
PEER_TOPK = 16
GATE_PAD = 128

LANES = 128
SUBLANES = 8
VMEM_LIMIT = 56 * 1024 * 1024

_NEG = -1e30


def _cparams(sem):
    return pltpu.CompilerParams(dimension_semantics=sem, vmem_limit_bytes=VMEM_LIMIT)


def _gelu(x):
    return 0.5 * x * (1.0 + jnp.tanh(0.7978845608028654 * (x + 0.044715 * (x * x * x))))


def _dot(a, b, **kw):
    return jnp.dot(a, b, preferred_element_type=f32, **kw)


def _dot_nt(a, b):
    return lax.dot_general(a, b, (((1,), (1,)), ((), ())), preferred_element_type=f32)


def _rms(x, g):
    return x * lax.rsqrt(jnp.mean(x * x, axis=-1, keepdims=True) + EPS) * g


def _adaln_kernel(c_ref, w_ref, b_ref, o_ref):
    c = c_ref[...]
    cs = c * jax.nn.sigmoid(c)
    o_ref[...] = _dot(cs, w_ref[...], precision=_HIGHEST) + b_ref[...]


def _adaln(c, w, b):
    bsz = c.shape[0]
    tn = 512
    return pl.pallas_call(
        _adaln_kernel,
        grid=(6 * D_MODEL // tn,),
        in_specs=[
            pl.BlockSpec((bsz, D_MODEL), lambda j: (0, 0)),
            pl.BlockSpec((D_MODEL, tn), lambda j: (0, j)),
            pl.BlockSpec((1, tn), lambda j: (0, j)),
        ],
        out_specs=pl.BlockSpec((bsz, tn), lambda j: (0, j)),
        out_shape=jax.ShapeDtypeStruct((bsz, 6 * D_MODEL), f32),
        compiler_params=_cparams(("arbitrary",)),
        name="adaln",
    )(c, w, b)


_C_Q = 0
_C_KC = 512
_C_VC = 640
_C_KS = 768
_C_VS = 896
_C_KW = 1024
_C_VW = 1152
_C_GL = 1280
_C_HQ = _C_GL + GATE_PAD
_C_HF = _C_HQ + 512
_C_HI = _C_HF + 512
_C_HG = _C_HI + 512
_C_MG = _C_HG + 512
_C_END = _C_MG + 2 * D_MODEL


def _inproj_kernel(x_ref, mod_ref, g_ref, w_ref, cos_ref, sin_ref,
                   q_ref, kc_ref, vc_ref, kvsw_ref, gate_ref, hq_ref, hf_ref, hi_ref, hg_ref, mg_ref):
    x = x_ref[...]
    sh = mod_ref[0, 0:1, :]
    sc = mod_ref[0, 1:2, :]
    h = (_rms(x, g_ref[...]) * (1.0 + sc) + sh).astype(bf16)

    def mm(c0, c1):
        return _dot(h, w_ref[:, c0:c1])

    cos = cos_ref[...]
    sin = sin_ref[...]

    def rope(a):
        width = a.shape[1]
        first = (lax.broadcasted_iota(i32, a.shape, 1) & (HEAD_DIM - 1)) < (HEAD_DIM // 2)
        partner = jnp.where(first, pltpu.roll(a, width - HEAD_DIM // 2, 1), pltpu.roll(a, HEAD_DIM // 2, 1))
        return a * cos[:, :width] + partner * sin[:, :width]

    q_ref[...] = (rope(mm(_C_Q, _C_KC)) * (HEAD_DIM ** -0.5)).astype(bf16)
    kc_ref[...] = rope(mm(_C_KC, _C_VC)).astype(bf16)
    vc_ref[...] = mm(_C_VC, _C_KS).astype(bf16)
    kvsw_ref[:, 0:128] = rope(mm(_C_KS, _C_VS)).astype(bf16)
    kvsw_ref[:, 128:256] = mm(_C_VS, _C_KW).astype(bf16)
    kvsw_ref[:, 256:384] = rope(mm(_C_KW, _C_VW)).astype(bf16)
    kvsw_ref[:, 384:512] = mm(_C_VW, _C_GL).astype(bf16)
    gate_ref[...] = jax.nn.sigmoid(mm(_C_GL, _C_HQ))
    hq_ref[...] = mm(_C_HQ, _C_HF).astype(bf16)
    hf_ref[...] = mm(_C_HF, _C_HI)
    hi_ref[...] = mm(_C_HI, _C_HG).astype(bf16)
    hg_ref[...] = mm(_C_HG, _C_MG).astype(bf16)
    mg_ref[...] = jax.nn.sigmoid(mm(_C_MG, _C_END)).astype(bf16)


def _inproj(x2, mod3, g_mix, w_pad, cos_t, sin_t, seq):
    n = x2.shape[0]
    tm = 256
    tiles_per_seq = seq // tm
    row = lambda i: (i, 0)
    outs = [
        (NSA_WIDTH, bf16), (KV_WIDTH, bf16), (KV_WIDTH, bf16), (4 * KV_WIDTH, bf16), (GATE_PAD, f32),
        (HGRN_WIDTH, bf16), (HGRN_WIDTH, f32), (HGRN_WIDTH, bf16), (HGRN_WIDTH, bf16), (2 * D_MODEL, bf16),
    ]
    return pl.pallas_call(
        _inproj_kernel,
        grid=(n // tm,),
        in_specs=[
            pl.BlockSpec((tm, D_MODEL), row),
            pl.BlockSpec((1, 6, D_MODEL), lambda i: (i // tiles_per_seq, 0, 0)),
            pl.BlockSpec((1, D_MODEL), lambda i: (0, 0)),
            pl.BlockSpec((D_MODEL, _C_END), lambda i: (0, 0)),
            pl.BlockSpec((tm, NSA_WIDTH), lambda i: (i % tiles_per_seq, 0)),
            pl.BlockSpec((tm, NSA_WIDTH), lambda i: (i % tiles_per_seq, 0)),
        ],
        out_specs=[pl.BlockSpec((tm, w), row) for w, _ in outs],
        out_shape=[jax.ShapeDtypeStruct((n, w), dt) for w, dt in outs],
        compiler_params=_cparams(("arbitrary",)),
        name="inproj",
    )(x2, mod3, g_mix, w_pad, cos_t, sin_t)


def _compress_kernel(kc_ref, vc_ref, wkt_ref, wkb_ref, wvt_ref, wvb_ref, pk_ref, pv_ref,
                     w1k_ref, w1v_ref, w2k_ref, w2v_ref, okc_ref, ovc_ref):
    def one(x_ref, wt_ref, wb_ref, pos_ref, w1_ref, w2_ref, o_ref):
        pieces = x_ref[0]
        top = _dot(pieces, wt_ref[...])
        bot = _dot(pieces, wb_ref[...])
        nrow = bot.shape[0]
        bot = pltpu.roll(bot, nrow - 1, 0)
        cpos = _dot(pos_ref[...], w1_ref[...], precision=_HIGHEST)
        w2 = w2_ref[...].astype(bf16)
        outs = []
        for g in range(NSA_KV_GROUPS):
            sl = slice(g * CMP_HIDDEN, (g + 1) * CMP_HIDDEN)
            hid = _gelu(top[:, sl] + bot[:, sl] + cpos)
            outs.append(_dot(hid.astype(bf16), w2))
        o_ref[0] = jnp.concatenate(outs, axis=1).astype(bf16)

    one(kc_ref, wkt_ref, wkb_ref, pk_ref, w1k_ref, w2k_ref, okc_ref)
    one(vc_ref, wvt_ref, wvb_ref, pv_ref, w1v_ref, w2v_ref, ovc_ref)


def _compress(kc3, vc3, wkt, wkb, wvt, wvb, pk, pv, w1k, w1v, w2k, w2v):
    bsz, npieces, width = kc3.shape
    full = lambda a: pl.BlockSpec(a.shape, lambda b: (0,) * a.ndim)
    per_b = pl.BlockSpec((1, npieces, width), lambda b: (b, 0, 0))
    out_b = pl.BlockSpec((1, npieces, KV_WIDTH), lambda b: (b, 0, 0))
    return pl.pallas_call(
        _compress_kernel,
        grid=(bsz,),
        in_specs=[per_b, per_b] + [full(a) for a in (wkt, wkb, wvt, wvb, pk, pv, w1k, w1v, w2k, w2v)],
        out_specs=[out_b, out_b],
        out_shape=[jax.ShapeDtypeStruct((bsz, npieces, KV_WIDTH), bf16)] * 2,
        compiler_params=_cparams(("arbitrary",)),
        name="compress",
    )(kc3, vc3, wkt, wkb, wvt, wvb, pk, pv, w1k, w1v, w2k, w2v)


_TQ = 128
_NSA_CLASSES = 4


def _softmax_parts(s, mask):
    sm = jnp.where(mask, s, _NEG)
    m = jnp.max(sm, axis=-1, keepdims=True)
    p = jnp.where(mask, jnp.exp(sm - m), 0.0)
    den = jnp.maximum(jnp.sum(p, axis=-1, keepdims=True), 1e-30)
    return p, den


def _softmax_plain(s):
    p = jnp.exp(s - jnp.max(s, axis=-1, keepdims=True))
    return p, jnp.sum(p, axis=-1, keepdims=True)


def _nsa_kernel(q_ref, kvsw_ref, kc_ref, vc_ref, gate_ref, selmap_ref, expand_ref, o_ref, *, seq):
    tq = _TQ
    qi = pl.program_id(1)
    n_cls = _NSA_CLASSES
    per_cls = (seq // tq) // n_cls
    for cls in range(n_cls):
        @pl.when((qi >= cls * per_cls) & (qi < (cls + 1) * per_cls))
        def _(cls=cls):
            _nsa_tile(q_ref, kvsw_ref, kc_ref, vc_ref, gate_ref, selmap_ref, expand_ref, o_ref,
                      seq=seq, width=(cls + 1) * (seq // n_cls))


def _nsa_tile(q_ref, kvsw_ref, kc_ref, vc_ref, gate_ref, selmap_ref, expand_ref, o_ref, *, seq, width):
    tq = _TQ
    rows = NSA_GROUP * tq
    n_sel = seq // SEL_BLOCK
    n_cmp_pad = seq // CMP_STRIDE
    t0 = pl.program_id(1) * tq
    q = q_ref[...]
    gates = gate_ref[...]
    tlane = t0 + lax.broadcasted_iota(i32, (1, tq), 1)
    t4 = t0 + (lax.broadcasted_iota(i32, (rows, 1), 0) & (tq - 1))
    wk = WINDOW + tq
    ws = pl.multiple_of(jnp.maximum(t0 - WINDOW, 0), tq)
    ks_all = kvsw_ref[0, 0:width, 0:128]
    vs_all = kvsw_ref[0, 0:width, 128:256]
    kw_all = kvsw_ref[0, pl.ds(ws, wk), 256:384]
    vw_all = kvsw_ref[0, pl.ds(ws, wk), 384:512]
    kc_all = kc_ref[0]
    vc_all = vc_ref[0]
    cend = lax.broadcasted_iota(i32, (1, n_cmp_pad), 1) * CMP_STRIDE + (CMP_BLOCK - 1)
    blk = lax.broadcasted_iota(i32, (n_sel, 1), 0)
    kpos = lax.broadcasted_iota(i32, (1, width), 1)
    kpos_w = ws + lax.broadcasted_iota(i32, (1, wk), 1)
    cur = tlane >> 6
    forced = (blk == 0) | (blk == cur) | (blk == cur - 1)
    causal_blk = blk * SEL_BLOCK <= tlane
    pieces = []
    for g in range(NSA_KV_GROUPS):
        gs = slice(g * HEAD_DIM, (g + 1) * HEAD_DIM)
        qg = jnp.concatenate(
            [q[:, (NSA_GROUP * g + r) * HEAD_DIM:(NSA_GROUP * g + r + 1) * HEAD_DIM] for r in range(NSA_GROUP)], axis=0)
        p_c, den_c = _softmax_parts(_dot_nt(qg, kc_all[:, gs]), cend <= t4)
        o_c = _dot(p_c.astype(bf16), vc_all[:, gs]) / den_c
        pn = p_c / den_c
        pc_sum = pn[0:tq]
        for r in range(1, NSA_GROUP):
            pc_sum = pc_sum + pn[r * tq:(r + 1) * tq]
        imp = lax.dot_general(selmap_ref[...], pc_sum, (((1,), (1,)), ((), ())),
                              preferred_element_type=f32, precision=_HIGHEST)
        imp = jnp.where(forced, imp + FORCE_BONUS, imp)
        imp = jnp.where(causal_blk, imp, -1.0)
        rank = jnp.zeros((n_sel, tq), f32)
        for j in range(n_sel):
            row = imp[j:j + 1, :]
            ahead = jnp.where(row > imp, 1.0, jnp.where(row == imp, jnp.where(blk > j, 1.0, 0.0), 0.0))
            rank = rank + ahead
        bias = jnp.where(rank < float(min(SEL_TOPK, n_sel)), jnp.where(causal_blk, 0.0, _NEG), _NEG).astype(bf16)
        bias_q = lax.dot_general(bias, expand_ref[:, 0:width], (((0,), (0,)), ((), ())),
                                 preferred_element_type=f32)
        bias4 = jnp.concatenate([bias_q] * NSA_GROUP, axis=0)
        p_s, den_s = _softmax_plain(jnp.where(kpos <= t4, _dot_nt(qg, ks_all[:, gs]) + bias4, _NEG))
        o_s = _dot(p_s.astype(bf16), vs_all[:, gs]) / den_s
        in_window = (t4 - kpos_w).astype(jnp.uint32) < WINDOW
        p_w, den_w = _softmax_plain(jnp.where(in_window, _dot_nt(qg, kw_all[:, gs]), _NEG))
        o_w = _dot(p_w.astype(bf16), vw_all[:, gs]) / den_w

        def gcol(br):
            return jnp.concatenate(
                [gates[:, (NSA_GROUP * g + r) * 3 + br:(NSA_GROUP * g + r) * 3 + br + 1] for r in range(NSA_GROUP)], axis=0)

        o = gcol(0) * o_c + gcol(1) * o_s + gcol(2) * o_w
        pieces += [o[r * tq:(r + 1) * tq] for r in range(NSA_GROUP)]
    o_ref[...] = jnp.concatenate(pieces, axis=1).astype(bf16)


def _nsa(q, kvsw3, kc3, vc3, gates, selmap, expand, seq):
    n = q.shape[0]
    bsz = n // seq
    tq = _TQ
    nq = seq // tq
    return pl.pallas_call(
        functools.partial(_nsa_kernel, seq=seq),
        grid=(bsz, nq),
        in_specs=[
            pl.BlockSpec((tq, NSA_WIDTH), lambda b, i: (b * nq + i, 0)),
            pl.BlockSpec((1, seq, 4 * KV_WIDTH), lambda b, i: (b, 0, 0)),
            pl.BlockSpec((1,) + kc3.shape[1:], lambda b, i: (b, 0, 0)),
            pl.BlockSpec((1,) + vc3.shape[1:], lambda b, i: (b, 0, 0)),
            pl.BlockSpec((tq, GATE_PAD), lambda b, i: (b * nq + i, 0)),
            pl.BlockSpec(selmap.shape, lambda b, i: (0, 0)),
            pl.BlockSpec(expand.shape, lambda b, i: (0, 0)),
        ],
        out_specs=pl.BlockSpec((tq, NSA_WIDTH), lambda b, i: (b * nq + i, 0)),
        out_shape=jax.ShapeDtypeStruct((n, NSA_WIDTH), bf16),
        compiler_params=_cparams(("arbitrary", "arbitrary")),
        name="nsa",
    )(q, kvsw3, kc3, vc3, gates, selmap, expand)


def _hgrn_kernel(hq_ref, hf_ref, hi_ref, hg_ref, lbl_ref, ng_ref, o_ref, st_ref, *, seq, layer):
    c = HGRN_CHUNK
    logits = lbl_ref[...]
    e = jnp.exp(logits - jnp.max(logits, axis=0, keepdims=True))
    sm = e / jnp.sum(e, axis=0, keepdims=True)
    lb = sm[0:1]
    for l in range(1, layer + 1):
        lb = lb + sm[l:l + 1]
    st_ref[...] = jnp.zeros_like(st_ref)
    r_i = lax.broadcasted_iota(i32, (c, c), 0)
    c_i = lax.broadcasted_iota(i32, (c, c), 1)
    causal = r_i >= c_i
    r_col = lax.broadcasted_iota(i32, (c, 1), 0)
    ng = ng_ref[...]

    def body(ci, carry):
        r0 = pl.multiple_of(ci * c, c)
        hq = hq_ref[pl.ds(r0, c), :].astype(f32)
        hf = hf_ref[pl.ds(r0, c), :]
        v = hi_ref[pl.ds(r0, c), :]
        hg = hg_ref[pl.ds(r0, c), :].astype(f32)
        f = lb + (1.0 - lb) * jax.nn.sigmoid(hf)
        k = 1.0 - f
        qv = hq * jax.nn.sigmoid(hq)
        bcum = jnp.log(f)
        for d in (1, 2, 4, 8, 16):
            bcum = bcum + jnp.where(r_col >= d, pltpu.roll(bcum, d, 0), 0.0)
        bend = bcum[c - 1:c, :]
        q_dec = (qv * jnp.exp(bcum)).astype(bf16)
        k_inv = (k * jnp.exp(-bcum)).astype(bf16)
        k_end = (k * jnp.exp(bend - bcum)).astype(bf16)
        dec_end = jnp.exp(bend)
        outs = []
        for h in range(HGRN_HEADS):
            sl = slice(h * HGRN_EXPAND, (h + 1) * HGRN_EXPAND)
            a = jnp.where(causal, _dot_nt(q_dec[:, sl], k_inv[:, sl]), 0.0)
            st = st_ref[h]
            o = _dot(a.astype(bf16), v[:, sl]) + _dot_nt(q_dec[:, sl], st.astype(bf16))
            upd = lax.dot_general(v[:, sl], k_end[:, sl], (((0,), (0,)), ((), ())), preferred_element_type=f32)
            st_ref[h] = st * dec_end[:, sl] + upd
            y = _rms(o, ng) * (hg[:, sl] * jax.nn.sigmoid(hg[:, sl]))
            outs.append(y)
        o_ref[pl.ds(r0, c), :] = jnp.concatenate(outs, axis=1).astype(bf16)
        return carry

    lax.fori_loop(0, seq // c, body, 0, unroll=8)


def _hgrn(hq, hf, hi, hg, lb_logits, norm_g, seq, layer):
    n = hq.shape[0]
    bsz = n // seq
    per_b = pl.BlockSpec((seq, HGRN_WIDTH), lambda b: (b, 0))
    return pl.pallas_call(
        functools.partial(_hgrn_kernel, seq=seq, layer=layer),
        grid=(bsz,),
        in_specs=[per_b, per_b, per_b, per_b,
                  pl.BlockSpec(lb_logits.shape, lambda b: (0, 0)),
                  pl.BlockSpec(norm_g.shape, lambda b: (0, 0))],
        out_specs=per_b,
        out_shape=jax.ShapeDtypeStruct((n, HGRN_WIDTH), bf16),
        scratch_shapes=[pltpu.VMEM((HGRN_HEADS, HGRN_HEAD_V, HGRN_EXPAND), f32)],
        compiler_params=_cparams(("arbitrary",)),
        name="hgrn",
    )(hq, hf, hi, hg, lb_logits, norm_g)


def _merge_kernel(on_ref, oh_ref, mg_ref, x_ref, mod_ref, g_ref, wb0_ref, wb1_ref, wo_ref, wq_ref, sk_ref,
                  x1_ref, h2_ref, sc_ref):
    a = _dot(on_ref[...], wb0_ref[...])
    b = _dot(oh_ref[...], wb1_ref[...])
    mg = mg_ref[...].astype(f32)
    y = mg[:, :D_MODEL] * a + mg[:, D_MODEL:] * b
    y2 = _dot(y.astype(bf16), wo_ref[...])
    x1 = x_ref[...] + mod_ref[0, 2:3, :] * y2
    h2 = _rms(x1, g_ref[...]) * (1.0 + mod_ref[0, 4:5, :]) + mod_ref[0, 3:4, :]
    rows = x1.shape[0]
    x1_ref[...] = x1.reshape(rows, SUBLANES, LANES)
    h2_ref[...] = h2.reshape(rows, SUBLANES, LANES)
    q = _dot(h2.astype(bf16), wq_ref[...]).astype(bf16)
    half = PEER_QDIM // 2
    for grp in range(2 * PEER_HEADS):
        sc_ref[grp] = _dot_nt(sk_ref[grp], q[:, grp * half:(grp + 1) * half])


def _merge(o_nsa, o_hgrn, mg, x2, mod3, g_ffn, wb0, wb1, wo, wq, sk, seq):
    n = x2.shape[0]
    tm = 256
    tiles_per_seq = seq // tm
    row = lambda i: (i, 0)
    full = lambda a: pl.BlockSpec(a.shape, lambda i: (0,) * a.ndim)
    return pl.pallas_call(
        _merge_kernel,
        grid=(n // tm,),
        in_specs=[
            pl.BlockSpec((tm, NSA_WIDTH), row), pl.BlockSpec((tm, HGRN_WIDTH), row),
            pl.BlockSpec((tm, 2 * D_MODEL), row), pl.BlockSpec((tm, D_MODEL), row),
            pl.BlockSpec((1, 6, D_MODEL), lambda i: (i // tiles_per_seq, 0, 0)),
            full(g_ffn), full(wb0), full(wb1), full(wo), full(wq), full(sk),
        ],
        out_specs=[pl.BlockSpec((tm, SUBLANES, LANES), lambda i: (i, 0, 0))] * 2
        + [pl.BlockSpec((2 * PEER_HEADS, PEER_NKEYS, tm), lambda i: (0, 0, i))],
        out_shape=[jax.ShapeDtypeStruct((n, SUBLANES, LANES), f32)] * 2
        + [jax.ShapeDtypeStruct((2 * PEER_HEADS, PEER_NKEYS, n), f32)],
        compiler_params=_cparams(("arbitrary",)),
        name="merge",
    )(o_nsa, o_hgrn, mg, x2, mod3, g_ffn, wb0, wb1, wo, wq, sk)


def _topk_rows(s, k, payload=None):
    n = s.shape[0]
    rowid = lax.broadcasted_iota(i32, s.shape, 0).astype(f32)
    vals, picks = [], []
    for _ in range(k):
        m = jnp.max(s, axis=0, keepdims=True)
        i = jnp.min(jnp.where(s == m, rowid, float(n)), axis=0, keepdims=True)
        hit = rowid == i
        vals.append(m)
        picks.append(i if payload is None else jnp.max(jnp.where(hit, payload, -1.0), axis=0, keepdims=True))
        s = jnp.where(hit, -jnp.inf, s)
    return vals, picks


def _pair_candidates(v1, i1, v2, i2):
    k = len(v1)
    s2 = jnp.concatenate(v2, axis=0)
    e2 = jnp.concatenate(i2, axis=0)
    sub = lax.broadcasted_iota(i32, (SUBLANES, s2.shape[1]), 0)
    comb, cand = [], []
    for a in range(k // 2):
        nb = k // (a + 1)
        rows = -(-nb // SUBLANES) * SUBLANES
        c = v1[a] + s2[0:rows]
        if nb < rows:
            c = jnp.where(sub < nb, c, -jnp.inf)
        comb.append(c)
        cand.append(i1[a] * float(PEER_NKEYS) + e2[0:rows])
    comb.append(jnp.concatenate(v1[k // 2:], axis=0) + s2[0:1])
    cand.append(jnp.concatenate(i1[k // 2:], axis=0) * float(PEER_NKEYS) + e2[0:1])
    return jnp.concatenate(comb, axis=0), jnp.concatenate(cand, axis=0)


def _head_routing(tops1, tops2):
    k = PEER_TOPK
    comb, cand = _pair_candidates(tops1[0], tops1[1], tops2[0], tops2[1])
    tv, picked = _topk_rows(comb, k, payload=cand)
    ex = [jnp.exp(tv[r] - tv[0]) for r in range(k)]
    den = ex[0]
    for r in range(1, k):
        den = den + ex[r]
    return picked, [e / den for e in ex]


def _route_kernel(s_ref, idx_ref, gate_ref):
    k = PEER_TOPK
    experts, gates = [], []
    for hd in range(PEER_HEADS):
        picked, g = _head_routing(_topk_rows(s_ref[2 * hd], k), _topk_rows(s_ref[2 * hd + 1], k))
        experts += picked
        gates += g
    idx_t = jnp.concatenate(experts, axis=0).astype(i32)
    gate_t = jnp.concatenate(gates, axis=0)
    for blk in range(idx_t.shape[1] // LANES):
        sl = slice(blk * LANES, (blk + 1) * LANES)
        idx_ref[sl, :] = idx_t[:, sl].T
        gate_ref[sl, :] = gate_t[:, sl].T


def _route(scores, n_tok):
    ncol = PEER_HEADS * PEER_TOPK
    return pl.pallas_call(
        _route_kernel,
        grid=(1,),
        in_specs=[pl.BlockSpec((2 * PEER_HEADS, PEER_NKEYS, n_tok), lambda i: (0, 0, 0))],
        out_specs=[pl.BlockSpec((n_tok, ncol), lambda i: (0, 0))] * 2,
        out_shape=[jax.ShapeDtypeStruct((n_tok, ncol), i32), jax.ShapeDtypeStruct((n_tok, ncol), f32)],
        compiler_params=_cparams(("arbitrary",)),
        name="route",
    )(scores)


_TS = 128
_HALF = 64
_TG = 4
_NSLOT = 4
_AHEAD = (_NSLOT - 1) * _TG
_NROW = PEER_HEADS * PEER_TOPK
_ROW_TILES = 2 * D_MODEL // LANES
_TP = 2
_RING = 3
_RING_ROWS = (_RING + 2) * _TS


def _expert_kernel(sc_ref, h_ref, x1_ref, mod_ref, gfin_ref, rsum_ref, idx01_ref, gate01_ref, uv_ref, o_ref,
                   *scratch):
    step = pl.program_id(0)
    last = pl.num_programs(0) - 1
    bufs = scratch[:_NSLOT]
    out_tiles, w_rows, idx_stage, gate_stage, idx_rows, gate_ring, idx_ring, sem, aux = scratch[_NSLOT:]
    n_dma = _TG * _NROW
    k = PEER_TOPK
    cur = lax.rem(step, _RING)
    nxt = lax.rem(step + 2, _RING)
    mirror = jnp.where(nxt == 0, _RING * _TS, (_RING + 1) * _TS)

    def row_copy(e, slot, j):
        return pltpu.make_async_copy(uv_ref.at[e], bufs[slot].at[j], sem.at[slot])

    def wait(slot):
        pltpu.make_async_copy(uv_ref.at[pl.ds(0, n_dma)], bufs[slot], sem.at[slot]).wait()

    def ring_copies():
        return (pltpu.make_async_copy(idx_rows, idx_ring.at[pl.ds(nxt * _TS, _TS)], aux.at[0]),
                pltpu.make_async_copy(idx_rows, idx_ring.at[pl.ds(mirror, _TS)], aux.at[1]))

    @pl.when(step == 0)
    def _():
        prime = (pltpu.make_async_copy(idx01_ref, idx_ring.at[pl.ds(0, 2 * _TS)], aux.at[0]),
                 pltpu.make_async_copy(idx01_ref.at[pl.ds(0, _TS)], idx_ring.at[pl.ds(_RING * _TS, _TS)], aux.at[1]),
                 pltpu.make_async_copy(gate01_ref, gate_ring.at[pl.ds(0, 2 * _TS)], aux.at[2]))
        for c in prime:
            c.start()
        for c in prime:
            c.wait()
        for slot in range(_NSLOT - 1):
            def body(t, c, slot=slot):
                for r in range(_NROW):
                    row_copy(idx_ring[slot * _TG + t, r], slot, t * _NROW + r).start(priority=r % 2)
                return c
            lax.fori_loop(0, _TG, body, 0)

    gt_f = mod_ref[0, 5]
    gfin = gfin_ref[...]
    eye = jnp.where(lax.broadcasted_iota(i32, (_NROW, LANES), 0) == lax.broadcasted_iota(i32, (_NROW, LANES), 1),
                    1.0, 0.0)

    def evaluate(tok0, slot, t0):
        src = bufs[slot]
        z = []
        for dt in range(_TP):
            u = src[pl.ds((t0 + dt) * _NROW, _NROW)].astype(f32)[:, 0:SUBLANES, :]
            z.append((u * h_ref[tok0 + dt][None]).reshape(_NROW * SUBLANES, LANES).astype(bf16))
        part = _dot(rsum_ref[...], jnp.concatenate(z, axis=1))
        a = jnp.concatenate(
            [jnp.broadcast_to(jnp.sum(part[:, dt * LANES:(dt + 1) * LANES], axis=1, keepdims=True), (_NROW, LANES))
             for dt in range(_TP)], axis=1)
        g = jnp.concatenate(
            [jnp.broadcast_to(jnp.sum(eye * gate_ring[pl.ds(cur * _TS + tok0 + dt, 1), :], axis=1, keepdims=True),
                              (_NROW, LANES)) for dt in range(_TP)], axis=1)
        w_rows[...] = g * _gelu(a)
        for dt in range(_TP):
            tok = tok0 + dt
            acc = [None] * 2
            for r in range(_NROW):
                wr = jnp.broadcast_to(w_rows[r:r + 1, dt * LANES:(dt + 1) * LANES], (SUBLANES, LANES))
                term = wr * src[(t0 + dt) * _NROW + r].astype(f32)[SUBLANES:2 * SUBLANES, :]
                acc[r % 2] = term if acc[r % 2] is None else acc[r % 2] + term
            y = acc[0] + acc[1]
            x2 = x1_ref[tok] + gt_f * y
            ms = jnp.sum(jnp.sum(x2 * x2, axis=1, keepdims=True), axis=0, keepdims=True) * (1.0 / D_MODEL)
            out_tiles[tok] = x2 * lax.rsqrt(ms + EPS) * gfin

    def half_step(half, carry):
        @pl.when((half == _TS // _HALF - 1) & (step > 0))
        def _():
            for c in ring_copies():
                c.wait()

        tok_h = half * _HALF
        ring_h = cur * _TS + tok_h
        head0 = half * (PEER_HEADS // 2)
        experts, gates, tops = [], [], {}
        pieces = []
        for hh in range(PEER_HEADS // 2):
            for p in range(2):
                def sub_keys(hh=hh, p=p):
                    tops[hh, p] = _topk_rows(sc_ref[(head0 + hh) * 2 + p], k)
                pieces.append(sub_keys)

            def pairs(hh=hh):
                picked, g = _head_routing(tops[hh, 0], tops[hh, 1])
                experts.extend(picked)
                gates.extend(g)
            pieces.append(pairs)

        def stage():
            rows = pl.ds(pl.multiple_of(head0 * k, _HALF), _HALF)
            idx_stage[rows, :] = jnp.concatenate(experts, axis=0).astype(i32)
            gate_stage[rows, :] = jnp.concatenate(gates, axis=0)
        pieces.append(stage)

        for grp in range(_HALF // _TG):
            slot = grp % _NSLOT
            wait(slot)
            if grp < len(pieces):
                pieces[grp]()
            dst = (slot + _NSLOT - 1) % _NSLOT
            for t0 in range(0, _TG, _TP):
                evaluate(tok_h + grp * _TG + t0, slot, t0)
                for t in range(t0, t0 + _TP):
                    for r in range(_NROW):
                        row_copy(idx_ring[ring_h + grp * _TG + t + _AHEAD, r], dst, t * _NROW + r).start(priority=r % 2)
        return carry

    lax.fori_loop(0, _TS // _HALF, half_step, 0)
    o_ref[...] = out_tiles[...].reshape(_TS, D_MODEL)
    idx_rows[...] = idx_stage[...].T
    gate_ring[pl.ds(pl.multiple_of(nxt * _TS, _TS), _TS), :] = gate_stage[...].T
    for c in ring_copies():
        c.start()

    @pl.when(step == last)
    def _():
        for c in ring_copies():
            c.wait()
        for slot in range(_NSLOT - 1):
            wait(slot)


def _experts(scores, idx01, gate01, h3, x13, mod4, gfin, uv, seq):
    n = h3.shape[0]
    n_steps = n // _TS
    steps_per_seq = seq // _TS
    tok3 = lambda i: (i, 0, 0)
    rsum = jnp.asarray(np.repeat(np.eye(_NROW, dtype=np.float32), SUBLANES, axis=1), dtype=bf16)
    buf = pltpu.VMEM((_TG * _NROW, _ROW_TILES, LANES), bf16)
    return pl.pallas_call(
        _expert_kernel,
        grid=(n_steps,),
        in_specs=[
            pl.BlockSpec((2 * PEER_HEADS, PEER_NKEYS, _TS), lambda i: (0, 0, jnp.minimum(i + 2, n_steps - 1))),
            pl.BlockSpec((_TS, SUBLANES, LANES), tok3),
            pl.BlockSpec((_TS, SUBLANES, LANES), tok3),
            pl.BlockSpec((1, 6, SUBLANES, LANES), lambda i: (i // steps_per_seq, 0, 0, 0)),
            pl.BlockSpec((SUBLANES, LANES), lambda i: (0, 0)),
            pl.BlockSpec(rsum.shape, lambda i: (0, 0)),
            pl.BlockSpec(memory_space=pl.ANY),
            pl.BlockSpec(memory_space=pl.ANY),
            pl.BlockSpec(memory_space=pl.ANY),
        ],
        out_specs=pl.BlockSpec((_TS, D_MODEL), lambda i: (i, 0)),
        out_shape=jax.ShapeDtypeStruct((n, D_MODEL), f32),
        scratch_shapes=[buf] * _NSLOT + [
            pltpu.VMEM((_TS, SUBLANES, LANES), f32),
            pltpu.VMEM((_NROW, _TP * LANES), f32),
            pltpu.VMEM((_NROW, _TS), i32),
            pltpu.VMEM((_NROW, _TS), f32),
            pltpu.VMEM((_TS, _NROW), i32),
            pltpu.VMEM((_RING * _TS, _NROW), f32),
            pltpu.SMEM((_RING_ROWS, _NROW), i32),
            pltpu.SemaphoreType.DMA((_NSLOT,)),
            pltpu.SemaphoreType.DMA((3,)),
        ],
        compiler_params=_cparams(("arbitrary",)),
        name="experts",
    )(scores, h3, x13, mod4, gfin, rsum, idx01, gate01, uv)


def _rope_tables(seq):
    half = HEAD_DIM // 2
    inv = ROPE_THETA ** (-np.arange(half, dtype=np.float32) / half)
    ang = np.arange(seq, dtype=np.float32)[:, None] * inv[None, :].astype(np.float32)
    cos = np.cos(ang).astype(np.float32)
    sin = np.sin(ang).astype(np.float32)
    cos_t = np.tile(np.concatenate([cos, cos], axis=1), (1, NSA_HEADS))
    sin_t = np.tile(np.concatenate([-sin, sin], axis=1), (1, NSA_HEADS))
    return jnp.asarray(cos_t), jnp.asarray(sin_t)


def _selection_map(n_cmp_pad, n_sel):
    r_sel = SEL_BLOCK // CMP_STRIDE
    r_cmp = CMP_BLOCK // CMP_STRIDE
    i = np.arange(n_cmp_pad)[:, None]
    j = np.arange(n_sel)[None, :]
    d = i - r_sel * j
    cnt = np.minimum(d, r_sel - 1) - np.maximum(d - r_cmp + 1, 0) + 1
    cnt = np.clip(cnt, 0, None).astype(np.float32)
    cnt[n_cmp_pad - r_cmp + 1:] = 0.0
    return jnp.asarray(cnt.T)


def _block_expand(n_sel, seq):
    e = (np.arange(seq)[None, :] // SEL_BLOCK == np.arange(n_sel)[:, None]).astype(np.float32)
    return jnp.asarray(e, dtype=bf16)


def _compress_weights(w1):
    eye = jnp.eye(NSA_KV_GROUPS, dtype=w1.dtype)
    out = []
    for part in range(CMP_BLOCK // CMP_STRIDE):
        w = w1[part * CMP_STRIDE * HEAD_DIM:(part + 1) * CMP_STRIDE * HEAD_DIM].reshape(CMP_STRIDE, HEAD_DIM, CMP_HIDDEN)
        big = jnp.einsum('pdc,gh->pgdhc', w, eye).reshape(CMP_STRIDE * KV_WIDTH, NSA_KV_GROUPS * CMP_HIDDEN)
        out.append(big.astype(bf16))
    return out


def kernel(x, c, w_ada, b_ada, g_mix, g_ffn, w_in, cmp_pos_k, cmp_pos_v, w_ck1, w_ck2, w_cv1, w_cv2, hgrn_lb_logits, hgrn_out_norm, w_branch, w_out, w_peer_q, peer_sub_keys, peer_u, peer_v, g_final):
    bsz, seq, d = x.shape
    n = bsz * seq
    depth = w_ada.shape[0]
    assert depth == 1, "single-layer block only"
    n_sel = seq // SEL_BLOCK
    n_pieces = seq // CMP_STRIDE
    cos_t, sin_t = _rope_tables(seq)
    selmap = _selection_map(n_pieces, n_sel)
    expand = _block_expand(n_sel, seq)
    xcur = x.reshape(n, d)
    for l in range(depth):
        mod = _adaln(c, w_ada[l], b_ada[l].reshape(1, 6 * d))
        mod3 = mod.reshape(bsz, 6, d)
        w_pad = jnp.concatenate(
            [w_in[l][:, :_C_GL + 3 * NSA_HEADS], jnp.zeros((d, GATE_PAD - 3 * NSA_HEADS), w_in.dtype),
             w_in[l][:, _C_GL + 3 * NSA_HEADS:]], axis=1).astype(bf16)
        q, kc, vc, kvsw, gates, hq, hf, hi, hg, mg = _inproj(xcur, mod3, g_mix[l].reshape(1, d), w_pad, cos_t, sin_t, seq)
        wkt, wkb = _compress_weights(w_ck1[l])
        wvt, wvb = _compress_weights(w_cv1[l])
        kc_c, vc_c = _compress(
            kc.reshape(bsz, n_pieces, CMP_STRIDE * KV_WIDTH), vc.reshape(bsz, n_pieces, CMP_STRIDE * KV_WIDTH),
            wkt, wkb, wvt, wvb, cmp_pos_k[l].reshape(1, -1), cmp_pos_v[l].reshape(1, -1),
            w_ck1[l], w_cv1[l], w_ck2[l], w_cv2[l])
        o_nsa = _nsa(q, kvsw.reshape(bsz, seq, 4 * KV_WIDTH), kc_c, vc_c, gates, selmap, expand, seq)
        o_hgrn = _hgrn(hq, hf, hi, hg, hgrn_lb_logits, hgrn_out_norm[l].reshape(1, -1), seq, l)
        wq = w_peer_q[l].astype(bf16)
        sk = jnp.transpose(peer_sub_keys[l], (1, 0, 2, 3)).reshape(2 * PEER_HEADS, PEER_NKEYS, PEER_QDIM // 2).astype(bf16)
        x1, h2, scores = _merge(o_nsa, o_hgrn, mg, xcur, mod3, g_ffn[l].reshape(1, d), w_branch[l, 0].astype(bf16),
                                w_branch[l, 1].astype(bf16), w_out[l].astype(bf16), wq, sk, seq)
        idx01, gate01 = _route(scores, 2 * _TS)
        uv = jnp.concatenate([peer_u[l], peer_v[l]], axis=1).astype(bf16).reshape(-1, _ROW_TILES, LANES)
        xcur = _experts(scores, idx01, gate01, h2, x1, mod.reshape(bsz, 6, SUBLANES, LANES),
                        g_final.reshape(SUBLANES, LANES), uv, seq)
    return xcur.reshape(bsz, seq, d)
```

```python
import functools

import numpy as np
import jax
import jax.numpy as jnp
from jax import lax
from jax.experimental import pallas as pl
from jax.experimental.pallas import tpu as pltpu

f32 = jnp.float32
bf16 = jnp.bfloat16
i32 = jnp.int32
_HIGHEST = lax.Precision.HIGHEST

D_MODEL = 1024
EPS = 1e-6
HEAD_DIM = 64
ROPE_THETA = 10000.0
NSA_HEADS = 8
NSA_KV_GROUPS = 2
NSA_GROUP = NSA_HEADS // NSA_KV_GROUPS
NSA_WIDTH = NSA_HEADS * HEAD_DIM
KV_WIDTH = NSA_KV_GROUPS * HEAD_DIM
CMP_BLOCK = 32
CMP_STRIDE = 16
CMP_HIDDEN = 128
SEL_BLOCK = 64
SEL_TOPK = 16
WINDOW = 512
FORCE_BONUS = 1000.0
HGRN_HEADS = 4
HGRN_EXPAND = 128
HGRN_HEAD_V = 128
HGRN_WIDTH = HGRN_HEADS * HGRN_EXPAND
HGRN_CHUNK = 32
PEER_HEADS = 8
PEER_NKEYS = 128
PEER_QDIM = 256
PEER_TOPK = 16
GATE_PAD = 128

LANES = 128
SUBLANES = 8
VMEM_LIMIT = 56 * 1024 * 1024

_NEG = -1e30


def _cparams(sem):
    return pltpu.CompilerParams(dimension_semantics=sem, vmem_limit_bytes=VMEM_LIMIT)


def _gelu(x):
    return 0.5 * x * (1.0 + jnp.tanh(0.7978845608028654 * (x + 0.044715 * (x * x * x))))


def _dot(a, b, **kw):
    return jnp.dot(a, b, preferred_element_type=f32, **kw)


def _dot_nt(a, b):
    return lax.dot_general(a, b, (((1,), (1,)), ((), ())), preferred_element_type=f32)


def _rms(x, g):
    return x * lax.rsqrt(jnp.mean(x * x, axis=-1, keepdims=True) + EPS) * g


def _adaln_kernel(c_ref, w_ref, b_ref, o_ref):
    c = c_ref[...]
    cs = c * jax.nn.sigmoid(c)
    o_ref[...] = _dot(cs, w_ref[...], precision=_HIGHEST) + b_ref[...]


def _adaln(c, w, b):
    bsz = c.shape[0]
    tn = 512
    return pl.pallas_call(
        _adaln_kernel,
        grid=(6 * D_MODEL // tn,),
        in_specs=[
            pl.BlockSpec((bsz, D_MODEL), lambda j: (0, 0)),
            pl.BlockSpec((D_MODEL, tn), lambda j: (0, j)),
            pl.BlockSpec((1, tn), lambda j: (0, j)),
        ],
        out_specs=pl.BlockSpec((bsz, tn), lambda j: (0, j)),
        out_shape=jax.ShapeDtypeStruct((bsz, 6 * D_MODEL), f32),
        compiler_params=_cparams(("arbitrary",)),
        name="adaln",
    )(c, w, b)


_C_Q = 0
_C_KC = 512
_C_VC = 640
_C_KS = 768
_C_VS = 896
_C_KW = 1024
_C_VW = 1152
_C_GL = 1280
_C_HQ = _C_GL + GATE_PAD
_C_HF = _C_HQ + 512
_C_HI = _C_HF + 512
_C_HG = _C_HI + 512
_C_MG = _C_HG + 512
_C_END = _C_MG + 2 * D_MODEL


def _inproj_kernel(x_ref, mod_ref, g_ref, w_ref, cos_ref, sin_ref,
                   q_ref, kc_ref, vc_ref, kvsw_ref, gate_ref, hq_ref, hf_ref, hi_ref, hg_ref, mg_ref):
    x = x_ref[...]
    sh = mod_ref[0, 0:1, :]
    sc = mod_ref[0, 1:2, :]
    h = (_rms(x, g_ref[...]) * (1.0 + sc) + sh).astype(bf16)

    def mm(c0, c1):
        return _dot(h, w_ref[:, c0:c1])

    cos = cos_ref[...]
    sin = sin_ref[...]

    def rope(a):
        width = a.shape[1]
        first = (lax.broadcasted_iota(i32, a.shape, 1) & (HEAD_DIM - 1)) < (HEAD_DIM // 2)
        partner = jnp.where(first, pltpu.roll(a, width - HEAD_DIM // 2, 1), pltpu.roll(a, HEAD_DIM // 2, 1))
        return a * cos[:, :width] + partner * sin[:, :width]

    q_ref[...] = (rope(mm(_C_Q, _C_KC)) * (HEAD_DIM ** -0.5)).astype(bf16)
    kc_ref[...] = rope(mm(_C_KC, _C_VC)).astype(bf16)
    vc_ref[...] = mm(_C_VC, _C_KS).astype(bf16)
    kvsw_ref[:, 0:128] = rope(mm(_C_KS, _C_VS)).astype(bf16)
    kvsw_ref[:, 128:256] = mm(_C_VS, _C_KW).astype(bf16)
    kvsw_ref[:, 256:384] = rope(mm(_C_KW, _C_VW)).astype(bf16)
    kvsw_ref[:, 384:512] = mm(_C_VW, _C_GL).astype(bf16)
    gate_ref[...] = jax.nn.sigmoid(mm(_C_GL, _C_HQ))
    hq_ref[...] = mm(_C_HQ, _C_HF).astype(bf16)
    hf_ref[...] = mm(_C_HF, _C_HI)
    hi_ref[...] = mm(_C_HI, _C_HG).astype(bf16)
    hg_ref[...] = mm(_C_HG, _C_MG).astype(bf16)
    mg_ref[...] = jax.nn.sigmoid(mm(_C_MG, _C_END)).astype(bf16)


def _inproj(x2, mod3, g_mix, w_pad, cos_t, sin_t, seq):
    n = x2.shape[0]
    tm = 256
    tiles_per_seq = seq // tm
    row = lambda i: (i, 0)
    outs = [
        (NSA_WIDTH, bf16), (KV_WIDTH, bf16), (KV_WIDTH, bf16), (4 * KV_WIDTH, bf16), (GATE_PAD, f32),
        (HGRN_WIDTH, bf16), (HGRN_WIDTH, f32), (HGRN_WIDTH, bf16), (HGRN_WIDTH, bf16), (2 * D_MODEL, bf16),
    ]
    return pl.pallas_call(
        _inproj_kernel,
        grid=(n // tm,),
        in_specs=[
            pl.BlockSpec((tm, D_MODEL), row),
            pl.BlockSpec((1, 6, D_MODEL), lambda i: (i // tiles_per_seq, 0, 0)),
            pl.BlockSpec((1, D_MODEL), lambda i: (0, 0)),
            pl.BlockSpec((D_MODEL, _C_END), lambda i: (0, 0)),
            pl.BlockSpec((tm, NSA_WIDTH), lambda i: (i % tiles_per_seq, 0)),
            pl.BlockSpec((tm, NSA_WIDTH), lambda i: (i % tiles_per_seq, 0)),
        ],
        out_specs=[pl.BlockSpec((tm, w), row) for w, _ in outs],
        out_shape=[jax.ShapeDtypeStruct((n, w), dt) for w, dt in outs],
        compiler_params=_cparams(("arbitrary",)),
        name="inproj",
    )(x2, mod3, g_mix, w_pad, cos_t, sin_t)


def _compress_kernel(kc_ref, vc_ref, wkt_ref, wkb_ref, wvt_ref, wvb_ref, pk_ref, pv_ref,
                     w1k_ref, w1v_ref, w2k_ref, w2v_ref, okc_ref, ovc_ref):
    def one(x_ref, wt_ref, wb_ref, pos_ref, w1_ref, w2_ref, o_ref):
        pieces = x_ref[0]
        top = _dot(pieces, wt_ref[...])
        bot = _dot(pieces, wb_ref[...])
        nrow = bot.shape[0]
        bot = pltpu.roll(bot, nrow - 1, 0)
        cpos = _dot(pos_ref[...], w1_ref[...], precision=_HIGHEST)
        w2 = w2_ref[...].astype(bf16)
        outs = []
        for g in range(NSA_KV_GROUPS):
            sl = slice(g * CMP_HIDDEN, (g + 1) * CMP_HIDDEN)
            hid = _gelu(top[:, sl] + bot[:, sl] + cpos)
            outs.append(_dot(hid.astype(bf16), w2))
        o_ref[0] = jnp.concatenate(outs, axis=1).astype(bf16)

    one(kc_ref, wkt_ref, wkb_ref, pk_ref, w1k_ref, w2k_ref, okc_ref)
    one(vc_ref, wvt_ref, wvb_ref, pv_ref, w1v_ref, w2v_ref, ovc_ref)


def _compress(kc3, vc3, wkt, wkb, wvt, wvb, pk, pv, w1k, w1v, w2k, w2v):
    bsz, npieces, width = kc3.shape
    full = lambda a: pl.BlockSpec(a.shape, lambda b: (0,) * a.ndim)
    per_b = pl.BlockSpec((1, npieces, width), lambda b: (b, 0, 0))
    out_b = pl.BlockSpec((1, npieces, KV_WIDTH), lambda b: (b, 0, 0))
    return pl.pallas_call(
        _compress_kernel,
        grid=(bsz,),
        in_specs=[per_b, per_b] + [full(a) for a in (wkt, wkb, wvt, wvb, pk, pv, w1k, w1v, w2k, w2v)],
        out_specs=[out_b, out_b],
        out_shape=[jax.ShapeDtypeStruct((bsz, npieces, KV_WIDTH), bf16)] * 2,
        compiler_params=_cparams(("arbitrary",)),
        name="compress",
    )(kc3, vc3, wkt, wkb, wvt, wvb, pk, pv, w1k, w1v, w2k, w2v)


_TQ = 128
_NSA_CLASSES = 4


def _softmax_parts(s, mask):
    sm = jnp.where(mask, s, _NEG)
    m = jnp.max(sm, axis=-1, keepdims=True)
    p = jnp.where(mask, jnp.exp(sm - m), 0.0)
    den = jnp.maximum(jnp.sum(p, axis=-1, keepdims=True), 1e-30)
    return p, den


def _softmax_plain(s):
    p = jnp.exp(s - jnp.max(s, axis=-1, keepdims=True))
    return p, jnp.sum(p, axis=-1, keepdims=True)


def _nsa_kernel(q_ref, kvsw_ref, kc_ref, vc_ref, gate_ref, selmap_ref, expand_ref, o_ref, *, seq):
    tq = _TQ
    qi = pl.program_id(1)
    n_cls = _NSA_CLASSES
    per_cls = (seq // tq) // n_cls
    for cls in range(n_cls):
        @pl.when((qi >= cls * per_cls) & (qi < (cls + 1) * per_cls))
        def _(cls=cls):
            _nsa_tile(q_ref, kvsw_ref, kc_ref, vc_ref, gate_ref, selmap_ref, expand_ref, o_ref,
                      seq=seq, width=(cls + 1) * (seq // n_cls))


def _nsa_tile(q_ref, kvsw_ref, kc_ref, vc_ref, gate_ref, selmap_ref, expand_ref, o_ref, *, seq, width):
    tq = _TQ
    rows = NSA_GROUP * tq
    n_sel = seq // SEL_BLOCK
    n_cmp_pad = seq // CMP_STRIDE
    t0 = pl.program_id(1) * tq
    q = q_ref[...]
    gates = gate_ref[...]
    tlane = t0 + lax.broadcasted_iota(i32, (1, tq), 1)
    t4 = t0 + (lax.broadcasted_iota(i32, (rows, 1), 0) & (tq - 1))
    wk = WINDOW + tq
    ws = pl.multiple_of(jnp.maximum(t0 - WINDOW, 0), tq)
    ks_all = kvsw_ref[0, 0:width, 0:128]
    vs_all = kvsw_ref[0, 0:width, 128:256]
    kw_all = kvsw_ref[0, pl.ds(ws, wk), 256:384]
    vw_all = kvsw_ref[0, pl.ds(ws, wk), 384:512]
    kc_all = kc_ref[0]
    vc_all = vc_ref[0]
    cend = lax.broadcasted_iota(i32, (1, n_cmp_pad), 1) * CMP_STRIDE + (CMP_BLOCK - 1)
    blk = lax.broadcasted_iota(i32, (n_sel, 1), 0)
    kpos = lax.broadcasted_iota(i32, (1, width), 1)
    kpos_w = ws + lax.broadcasted_iota(i32, (1, wk), 1)
    cur = tlane >> 6
    forced = (blk == 0) | (blk == cur) | (blk == cur - 1)
    causal_blk = blk * SEL_BLOCK <= tlane
    pieces = []
    for g in range(NSA_KV_GROUPS):
        gs = slice(g * HEAD_DIM, (g + 1) * HEAD_DIM)
        qg = jnp.concatenate(
            [q[:, (NSA_GROUP * g + r) * HEAD_DIM:(NSA_GROUP * g + r + 1) * HEAD_DIM] for r in range(NSA_GROUP)], axis=0)
        p_c, den_c = _softmax_parts(_dot_nt(qg, kc_all[:, gs]), cend <= t4)
        o_c = _dot(p_c.astype(bf16), vc_all[:, gs]) / den_c
        pn = p_c / den_c
        pc_sum = pn[0:tq]
        for r in range(1, NSA_GROUP):
            pc_sum = pc_sum + pn[r * tq:(r + 1) * tq]
        imp = lax.dot_general(selmap_ref[...], pc_sum, (((1,), (1,)), ((), ())),
                              preferred_element_type=f32, precision=_HIGHEST)
        imp = jnp.where(forced, imp + FORCE_BONUS, imp)
        imp = jnp.where(causal_blk, imp, -1.0)
        rank = jnp.zeros((n_sel, tq), f32)
        for j in range(n_sel):
            row = imp[j:j + 1, :]
            ahead = jnp.where(row > imp, 1.0, jnp.where(row == imp, jnp.where(blk > j, 1.0, 0.0), 0.0))
            rank = rank + ahead
        bias = jnp.where(rank < float(min(SEL_TOPK, n_sel)), jnp.where(causal_blk, 0.0, _NEG), _NEG).astype(bf16)
        bias_q = lax.dot_general(bias, expand_ref[:, 0:width], (((0,), (0,)), ((), ())),
                                 preferred_element_type=f32)
        bias4 = jnp.concatenate([bias_q] * NSA_GROUP, axis=0)
        p_s, den_s = _softmax_plain(jnp.where(kpos <= t4, _dot_nt(qg, ks_all[:, gs]) + bias4, _NEG))
        o_s = _dot(p_s.astype(bf16), vs_all[:, gs]) / den_s
        in_window = (t4 - kpos_w).astype(jnp.uint32) < WINDOW
        p_w, den_w = _softmax_plain(jnp.where(in_window, _dot_nt(qg, kw_all[:, gs]), _NEG))
        o_w = _dot(p_w.astype(bf16), vw_all[:, gs]) / den_w

        def gcol(br):
            return jnp.concatenate(
                [gates[:, (NSA_GROUP * g + r) * 3 + br:(NSA_GROUP * g + r) * 3 + br + 1] for r in range(NSA_GROUP)], axis=0)

        o = gcol(0) * o_c + gcol(1) * o_s + gcol(2) * o_w
        pieces += [o[r * tq:(r + 1) * tq] for r in range(NSA_GROUP)]
    o_ref[...] = jnp.concatenate(pieces, axis=1).astype(bf16)


def _nsa(q, kvsw3, kc3, vc3, gates, selmap, expand, seq):
    n = q.shape[0]
    bsz = n // seq
    tq = _TQ
    nq = seq // tq
    return pl.pallas_call(
        functools.partial(_nsa_kernel, seq=seq),
        grid=(bsz, nq),
        in_specs=[
            pl.BlockSpec((tq, NSA_WIDTH), lambda b, i: (b * nq + i, 0)),
            pl.BlockSpec((1, seq, 4 * KV_WIDTH), lambda b, i: (b, 0, 0)),
            pl.BlockSpec((1,) + kc3.shape[1:], lambda b, i: (b, 0, 0)),
            pl.BlockSpec((1,) + vc3.shape[1:], lambda b, i: (b, 0, 0)),
            pl.BlockSpec((tq, GATE_PAD), lambda b, i: (b * nq + i, 0)),
            pl.BlockSpec(selmap.shape, lambda b, i: (0, 0)),
            pl.BlockSpec(expand.shape, lambda b, i: (0, 0)),
        ],
        out_specs=pl.BlockSpec((tq, NSA_WIDTH), lambda b, i: (b * nq + i, 0)),
        out_shape=jax.ShapeDtypeStruct((n, NSA_WIDTH), bf16),
        compiler_params=_cparams(("arbitrary", "arbitrary")),
        name="nsa",
    )(q, kvsw3, kc3, vc3, gates, selmap, expand)


def _hgrn_kernel(hq_ref, hf_ref, hi_ref, hg_ref, lbl_ref, ng_ref, o_ref, st_ref, *, seq, layer):
    c = HGRN_CHUNK
    logits = lbl_ref[...]
    e = jnp.exp(logits - jnp.max(logits, axis=0, keepdims=True))
    sm = e / jnp.sum(e, axis=0, keepdims=True)
    lb = sm[0:1]
    for l in range(1, layer + 1):
        lb = lb + sm[l:l + 1]
    st_ref[...] = jnp.zeros_like(st_ref)
    r_i = lax.broadcasted_iota(i32, (c, c), 0)
    c_i = lax.broadcasted_iota(i32, (c, c), 1)
    causal = r_i >= c_i
    r_col = lax.broadcasted_iota(i32, (c, 1), 0)
    ng = ng_ref[...]

    def body(ci, carry):
        r0 = pl.multiple_of(ci * c, c)
        hq = hq_ref[pl.ds(r0, c), :].astype(f32)
        hf = hf_ref[pl.ds(r0, c), :]
        v = hi_ref[pl.ds(r0, c), :]
        hg = hg_ref[pl.ds(r0, c), :].astype(f32)
        f = lb + (1.0 - lb) * jax.nn.sigmoid(hf)
        k = 1.0 - f
        qv = hq * jax.nn.sigmoid(hq)
        bcum = jnp.log(f)
        for d in (1, 2, 4, 8, 16):
            bcum = bcum + jnp.where(r_col >= d, pltpu.roll(bcum, d, 0), 0.0)
        bend = bcum[c - 1:c, :]
        q_dec = (qv * jnp.exp(bcum)).astype(bf16)
        k_inv = (k * jnp.exp(-bcum)).astype(bf16)
        k_end = (k * jnp.exp(bend - bcum)).astype(bf16)
        dec_end = jnp.exp(bend)
        outs = []
        for h in range(HGRN_HEADS):
            sl = slice(h * HGRN_EXPAND, (h + 1) * HGRN_EXPAND)
            a = jnp.where(causal, _dot_nt(q_dec[:, sl], k_inv[:, sl]), 0.0)
            st = st_ref[h]
            o = _dot(a.astype(bf16), v[:, sl]) + _dot_nt(q_dec[:, sl], st.astype(bf16))
            upd = lax.dot_general(v[:, sl], k_end[:, sl], (((0,), (0,)), ((), ())), preferred_element_type=f32)
            st_ref[h] = st * dec_end[:, sl] + upd
            y = _rms(o, ng) * (hg[:, sl] * jax.nn.sigmoid(hg[:, sl]))
            outs.append(y)
        o_ref[pl.ds(r0, c), :] = jnp.concatenate(outs, axis=1).astype(bf16)
        return carry

    lax.fori_loop(0, seq // c, body, 0, unroll=8)


def _hgrn(hq, hf, hi, hg, lb_logits, norm_g, seq, layer):
    n = hq.shape[0]
    bsz = n // seq
    per_b = pl.BlockSpec((seq, HGRN_WIDTH), lambda b: (b, 0))
    return pl.pallas_call(
        functools.partial(_hgrn_kernel, seq=seq, layer=layer),
        grid=(bsz,),
        in_specs=[per_b, per_b, per_b, per_b,
                  pl.BlockSpec(lb_logits.shape, lambda b: (0, 0)),
                  pl.BlockSpec(norm_g.shape, lambda b: (0, 0))],
        out_specs=per_b,
        out_shape=jax.ShapeDtypeStruct((n, HGRN_WIDTH), bf16),
        scratch_shapes=[pltpu.VMEM((HGRN_HEADS, HGRN_HEAD_V, HGRN_EXPAND), f32)],
        compiler_params=_cparams(("arbitrary",)),
        name="hgrn",
    )(hq, hf, hi, hg, lb_logits, norm_g)


def _merge_kernel(on_ref, oh_ref, mg_ref, x_ref, mod_ref, g_ref, wb0_ref, wb1_ref, wo_ref, wq_ref, sk_ref,
                  x1_ref, h2_ref, sc_ref):
    a = _dot(on_ref[...], wb0_ref[...])
    b = _dot(oh_ref[...], wb1_ref[...])
    mg = mg_ref[...].astype(f32)
    y = mg[:, :D_MODEL] * a + mg[:, D_MODEL:] * b
    y2 = _dot(y.astype(bf16), wo_ref[...])
    x1 = x_ref[...] + mod_ref[0, 2:3, :] * y2
    h2 = _rms(x1, g_ref[...]) * (1.0 + mod_ref[0, 4:5, :]) + mod_ref[0, 3:4, :]
    rows = x1.shape[0]
    x1_ref[...] = x1.reshape(rows, SUBLANES, LANES)
    h2_ref[...] = h2.reshape(rows, SUBLANES, LANES)
    q = _dot(h2.astype(bf16), wq_ref[...]).astype(bf16)
    half = PEER_QDIM // 2
    for grp in range(2 * PEER_HEADS):
        sc_ref[grp] = _dot_nt(sk_ref[grp], q[:, grp * half:(grp + 1) * half])


def _merge(o_nsa, o_hgrn, mg, x2, mod3, g_ffn, wb0, wb1, wo, wq, sk, seq):
    n = x2.shape[0]
    tm = 512
    tiles_per_seq = seq // tm
    row = lambda i: (i, 0)
    full = lambda a: pl.BlockSpec(a.shape, lambda i: (0,) * a.ndim)
    return pl.pallas_call(
        _merge_kernel,
        grid=(n // tm,),
        in_specs=[
            pl.BlockSpec((tm, NSA_WIDTH), row), pl.BlockSpec((tm, HGRN_WIDTH), row),
            pl.BlockSpec((tm, 2 * D_MODEL), row), pl.BlockSpec((tm, D_MODEL), row),
            pl.BlockSpec((1, 6, D_MODEL), lambda i: (i // tiles_per_seq, 0, 0)),
            full(g_ffn), full(wb0), full(wb1), full(wo), full(wq), full(sk),
        ],
        out_specs=[pl.BlockSpec((tm, SUBLANES, LANES), lambda i: (i, 0, 0))] * 2
        + [pl.BlockSpec((2 * PEER_HEADS, PEER_NKEYS, tm), lambda i: (0, 0, i))],
        out_shape=[jax.ShapeDtypeStruct((n, SUBLANES, LANES), f32)] * 2
        + [jax.ShapeDtypeStruct((2 * PEER_HEADS, PEER_NKEYS, n), f32)],
        compiler_params=_cparams(("arbitrary",)),
        name="merge",
    )(o_nsa, o_hgrn, mg, x2, mod3, g_ffn, wb0, wb1, wo, wq, sk)


def _topk_rows(s, k, payload=None):
    n = s.shape[0]
    rowid = lax.broadcasted_iota(i32, s.shape, 0).astype(f32)
    vals, picks = [], []
    for _ in range(k):
        m = jnp.max(s, axis=0, keepdims=True)
        i = jnp.min(jnp.where(s == m, rowid, float(n)), axis=0, keepdims=True)
        hit = rowid == i
        vals.append(m)
        picks.append(i if payload is None else jnp.max(jnp.where(hit, payload, -1.0), axis=0, keepdims=True))
        s = jnp.where(hit, -jnp.inf, s)
    return vals, picks


def _pair_candidates(v1, i1, v2, i2):
    k = len(v1)
    s2 = jnp.concatenate(v2, axis=0)
    e2 = jnp.concatenate(i2, axis=0)
    sub = lax.broadcasted_iota(i32, (SUBLANES, s2.shape[1]), 0)
    comb, cand = [], []
    for a in range(k // 2):
        nb = k // (a + 1)
        rows = -(-nb // SUBLANES) * SUBLANES
        c = v1[a] + s2[0:rows]
        if nb < rows:
            c = jnp.where(sub < nb, c, -jnp.inf)
        comb.append(c)
        cand.append(i1[a] * float(PEER_NKEYS) + e2[0:rows])
    comb.append(jnp.concatenate(v1[k // 2:], axis=0) + s2[0:1])
    cand.append(jnp.concatenate(i1[k // 2:], axis=0) * float(PEER_NKEYS) + e2[0:1])
    return jnp.concatenate(comb, axis=0), jnp.concatenate(cand, axis=0)


def _head_routing(tops1, tops2):
    k = PEER_TOPK
    comb, cand = _pair_candidates(tops1[0], tops1[1], tops2[0], tops2[1])
    tv, picked = _topk_rows(comb, k, payload=cand)
    ex = [jnp.exp(tv[r] - tv[0]) for r in range(k)]
    den = ex[0]
    for r in range(1, k):
        den = den + ex[r]
    return picked, [e / den for e in ex]


def _route_kernel(s_ref, idx_ref, gate_ref):
    k = PEER_TOPK
    experts, gates = [], []
    for hd in range(PEER_HEADS):
        picked, g = _head_routing(_topk_rows(s_ref[2 * hd], k), _topk_rows(s_ref[2 * hd + 1], k))
        experts += picked
        gates += g
    idx_t = jnp.concatenate(experts, axis=0).astype(i32)
    gate_t = jnp.concatenate(gates, axis=0)
    for blk in range(idx_t.shape[1] // LANES):
        sl = slice(blk * LANES, (blk + 1) * LANES)
        idx_ref[sl, :] = idx_t[:, sl].T
        gate_ref[sl, :] = gate_t[:, sl].T


def _route(scores, n_tok):
    ncol = PEER_HEADS * PEER_TOPK
    return pl.pallas_call(
        _route_kernel,
        grid=(1,),
        in_specs=[pl.BlockSpec((2 * PEER_HEADS, PEER_NKEYS, n_tok), lambda i: (0, 0, 0))],
        out_specs=[pl.BlockSpec((n_tok, ncol), lambda i: (0, 0))] * 2,
        out_shape=[jax.ShapeDtypeStruct((n_tok, ncol), i32), jax.ShapeDtypeStruct((n_tok, ncol), f32)],
        compiler_params=_cparams(("arbitrary",)),
        name="route",
    )(scores)


_TS = 128
_HALF = 64
_TG = 4
_NSLOT = 4
_AHEAD = (_NSLOT - 1) * _TG
_NROW = PEER_HEADS * PEER_TOPK
_ROW_TILES = 2 * D_MODEL // LANES
_TP = 2
_RING = 3
_RING_ROWS = (_RING + 2) * _TS


def _expert_kernel(sc_ref, h_ref, x1_ref, mod_ref, gfin_ref, rsum_ref, idx01_ref, gate01_ref, uv_ref, o_ref,
                   *scratch):
    step = pl.program_id(0)
    last = pl.num_programs(0) - 1
    bufs = scratch[:_NSLOT]
    out_tiles, w_rows, idx_stage, gate_stage, idx_rows, gate_ring, idx_ring, sem, aux = scratch[_NSLOT:]
    n_dma = _TG * _NROW
    k = PEER_TOPK
    cur = lax.rem(step, _RING)
    nxt = lax.rem(step + 2, _RING)
    mirror = jnp.where(nxt == 0, _RING * _TS, (_RING + 1) * _TS)

    def row_copy(e, slot, j):
        return pltpu.make_async_copy(uv_ref.at[e], bufs[slot].at[j], sem.at[slot])

    def wait(slot):
        pltpu.make_async_copy(uv_ref.at[pl.ds(0, n_dma)], bufs[slot], sem.at[slot]).wait()

    def ring_copies():
        return (pltpu.make_async_copy(idx_rows, idx_ring.at[pl.ds(nxt * _TS, _TS)], aux.at[0]),
                pltpu.make_async_copy(idx_rows, idx_ring.at[pl.ds(mirror, _TS)], aux.at[1]))

    @pl.when(step == 0)
    def _():
        prime = (pltpu.make_async_copy(idx01_ref, idx_ring.at[pl.ds(0, 2 * _TS)], aux.at[0]),
                 pltpu.make_async_copy(idx01_ref.at[pl.ds(0, _TS)], idx_ring.at[pl.ds(_RING * _TS, _TS)], aux.at[1]),
                 pltpu.make_async_copy(gate01_ref, gate_ring.at[pl.ds(0, 2 * _TS)], aux.at[2]))
        for c in prime:
            c.start()
        for c in prime:
            c.wait()
        for slot in range(_NSLOT - 1):
            def body(t, c, slot=slot):
                for r in range(_NROW):
                    row_copy(idx_ring[slot * _TG + t, r], slot, t * _NROW + r).start(priority=r % 2)
                return c
            lax.fori_loop(0, _TG, body, 0)

    gt_f = mod_ref[0, 5]
    gfin = gfin_ref[...]
    eye = jnp.where(lax.broadcasted_iota(i32, (_NROW, LANES), 0) == lax.broadcasted_iota(i32, (_NROW, LANES), 1),
                    1.0, 0.0)

    def evaluate(tok0, slot, t0):
        src = bufs[slot]
        z = []
        for dt in range(_TP):
            u = src[pl.ds((t0 + dt) * _NROW, _NROW)].astype(f32)[:, 0:SUBLANES, :]
            z.append((u * h_ref[tok0 + dt][None]).reshape(_NROW * SUBLANES, LANES).astype(bf16))
        part = _dot(rsum_ref[...], jnp.concatenate(z, axis=1))
        a = jnp.concatenate(
            [jnp.broadcast_to(jnp.sum(part[:, dt * LANES:(dt + 1) * LANES], axis=1, keepdims=True), (_NROW, LANES))
             for dt in range(_TP)], axis=1)
        g = jnp.concatenate(
            [jnp.broadcast_to(jnp.sum(eye * gate_ring[pl.ds(cur * _TS + tok0 + dt, 1), :], axis=1, keepdims=True),
                              (_NROW, LANES)) for dt in range(_TP)], axis=1)
        w_rows[...] = g * _gelu(a)
        for dt in range(_TP):
            tok = tok0 + dt
            acc = [None] * 2
            for r in range(_NROW):
                wr = jnp.broadcast_to(w_rows[r:r + 1, dt * LANES:(dt + 1) * LANES], (SUBLANES, LANES))
                term = wr * src[(t0 + dt) * _NROW + r].astype(f32)[SUBLANES:2 * SUBLANES, :]
                acc[r % 2] = term if acc[r % 2] is None else acc[r % 2] + term
            y = acc[0] + acc[1]
            x2 = x1_ref[tok] + gt_f * y
            ms = jnp.sum(jnp.sum(x2 * x2, axis=1, keepdims=True), axis=0, keepdims=True) * (1.0 / D_MODEL)
            out_tiles[tok] = x2 * lax.rsqrt(ms + EPS) * gfin

    def half_step(half, carry):
        @pl.when((half == _TS // _HALF - 1) & (step > 0))
        def _():
            for c in ring_copies():
                c.wait()

        tok_h = half * _HALF
        ring_h = cur * _TS + tok_h
        head0 = half * (PEER_HEADS // 2)
        experts, gates, tops = [], [], {}
        pieces = []
        for hh in range(PEER_HEADS // 2):
            for p in range(2):
                def sub_keys(hh=hh, p=p):
                    tops[hh, p] = _topk_rows(sc_ref[(head0 + hh) * 2 + p], k)
                pieces.append(sub_keys)

            def pairs(hh=hh):
                picked, g = _head_routing(tops[hh, 0], tops[hh, 1])
                experts.extend(picked)
                gates.extend(g)
            pieces.append(pairs)

        def stage():
            rows = pl.ds(pl.multiple_of(head0 * k, _HALF), _HALF)
            idx_stage[rows, :] = jnp.concatenate(experts, axis=0).astype(i32)
            gate_stage[rows, :] = jnp.concatenate(gates, axis=0)
        pieces.append(stage)

        for grp in range(_HALF // _TG):
            slot = grp % _NSLOT
            wait(slot)
            if grp < len(pieces):
                pieces[grp]()
            dst = (slot + _NSLOT - 1) % _NSLOT
            for t0 in range(0, _TG, _TP):
                evaluate(tok_h + grp * _TG + t0, slot, t0)
                for t in range(t0, t0 + _TP):
                    for r in range(_NROW):
                        row_copy(idx_ring[ring_h + grp * _TG + t + _AHEAD, r], dst, t * _NROW + r).start(priority=r % 2)
        return carry

    lax.fori_loop(0, _TS // _HALF, half_step, 0)
    o_ref[...] = out_tiles[...].reshape(_TS, D_MODEL)
    idx_rows[...] = idx_stage[...].T
    gate_ring[pl.ds(pl.multiple_of(nxt * _TS, _TS), _TS), :] = gate_stage[...].T
    for c in ring_copies():
        c.start()

    @pl.when(step == last)
    def _():
        for c in ring_copies():
            c.wait()
        for slot in range(_NSLOT - 1):
            wait(slot)


def _experts(scores, idx01, gate01, h3, x13, mod4, gfin, uv, seq):
    n = h3.shape[0]
    n_steps = n // _TS
    steps_per_seq = seq // _TS
    tok3 = lambda i: (i, 0, 0)
    rsum = jnp.asarray(np.repeat(np.eye(_NROW, dtype=np.float32), SUBLANES, axis=1), dtype=bf16)
    buf = pltpu.VMEM((_TG * _NROW, _ROW_TILES, LANES), bf16)
    return pl.pallas_call(
        _expert_kernel,
        grid=(n_steps,),
        in_specs=[
            pl.BlockSpec((2 * PEER_HEADS, PEER_NKEYS, _TS), lambda i: (0, 0, jnp.minimum(i + 2, n_steps - 1))),
            pl.BlockSpec((_TS, SUBLANES, LANES), tok3),
            pl.BlockSpec((_TS, SUBLANES, LANES), tok3),
            pl.BlockSpec((1, 6, SUBLANES, LANES), lambda i: (i // steps_per_seq, 0, 0, 0)),
            pl.BlockSpec((SUBLANES, LANES), lambda i: (0, 0)),
            pl.BlockSpec(rsum.shape, lambda i: (0, 0)),
            pl.BlockSpec(memory_space=pl.ANY),
            pl.BlockSpec(memory_space=pl.ANY),
            pl.BlockSpec(memory_space=pl.ANY),
        ],
        out_specs=pl.BlockSpec((_TS, D_MODEL), lambda i: (i, 0)),
        out_shape=jax.ShapeDtypeStruct((n, D_MODEL), f32),
        scratch_shapes=[buf] * _NSLOT + [
            pltpu.VMEM((_TS, SUBLANES, LANES), f32),
            pltpu.VMEM((_NROW, _TP * LANES), f32),
            pltpu.VMEM((_NROW, _TS), i32),
            pltpu.VMEM((_NROW, _TS), f32),
            pltpu.VMEM((_TS, _NROW), i32),
            pltpu.VMEM((_RING * _TS, _NROW), f32),
            pltpu.SMEM((_RING_ROWS, _NROW), i32),
            pltpu.SemaphoreType.DMA((_NSLOT,)),
            pltpu.SemaphoreType.DMA((3,)),
        ],
        compiler_params=_cparams(("arbitrary",)),
        name="experts",
    )(scores, h3, x13, mod4, gfin, rsum, idx01, gate01, uv)


def _rope_tables(seq):
    half = HEAD_DIM // 2
    inv = ROPE_THETA ** (-np.arange(half, dtype=np.float32) / half)
    ang = np.arange(seq, dtype=np.float32)[:, None] * inv[None, :].astype(np.float32)
    cos = np.cos(ang).astype(np.float32)
    sin = np.sin(ang).astype(np.float32)
    cos_t = np.tile(np.concatenate([cos, cos], axis=1), (1, NSA_HEADS))
    sin_t = np.tile(np.concatenate([-sin, sin], axis=1), (1, NSA_HEADS))
    return jnp.asarray(cos_t), jnp.asarray(sin_t)


def _selection_map(n_cmp_pad, n_sel):
    r_sel = SEL_BLOCK // CMP_STRIDE
    r_cmp = CMP_BLOCK // CMP_STRIDE
    i = np.arange(n_cmp_pad)[:, None]
    j = np.arange(n_sel)[None, :]
    d = i - r_sel * j
    cnt = np.minimum(d, r_sel - 1) - np.maximum(d - r_cmp + 1, 0) + 1
    cnt = np.clip(cnt, 0, None).astype(np.float32)
    cnt[n_cmp_pad - r_cmp + 1:] = 0.0
    return jnp.asarray(cnt.T)


def _block_expand(n_sel, seq):
    e = (np.arange(seq)[None, :] // SEL_BLOCK == np.arange(n_sel)[:, None]).astype(np.float32)
    return jnp.asarray(e, dtype=bf16)


def _compress_weights(w1):
    eye = jnp.eye(NSA_KV_GROUPS, dtype=w1.dtype)
    out = []
    for part in range(CMP_BLOCK // CMP_STRIDE):
        w = w1[part * CMP_STRIDE * HEAD_DIM:(part + 1) * CMP_STRIDE * HEAD_DIM].reshape(CMP_STRIDE, HEAD_DIM, CMP_HIDDEN)
        big = jnp.einsum('pdc,gh->pgdhc', w, eye).reshape(CMP_STRIDE * KV_WIDTH, NSA_KV_GROUPS * CMP_HIDDEN)
        out.append(big.astype(bf16))
    return out


def kernel(x, c, w_ada, b_ada, g_mix, g_ffn, w_in, cmp_pos_k, cmp_pos_v, w_ck1, w_ck2, w_cv1, w_cv2, hgrn_lb_logits, hgrn_out_norm, w_branch, w_out, w_peer_q, peer_sub_keys, peer_u, peer_v, g_final):
    bsz, seq, d = x.shape
    n = bsz * seq
    depth = w_ada.shape[0]
    assert depth == 1, "single-layer block only"
    n_sel = seq // SEL_BLOCK
    n_pieces = seq // CMP_STRIDE
    cos_t, sin_t = _rope_tables(seq)
    selmap = _selection_map(n_pieces, n_sel)
    expand = _block_expand(n_sel, seq)
    xcur = x.reshape(n, d)
    for l in range(depth):
        mod = _adaln(c, w_ada[l], b_ada[l].reshape(1, 6 * d))
        mod3 = mod.reshape(bsz, 6, d)
        w_pad = jnp.concatenate(
            [w_in[l][:, :_C_GL + 3 * NSA_HEADS], jnp.zeros((d, GATE_PAD - 3 * NSA_HEADS), w_in.dtype),
             w_in[l][:, _C_GL + 3 * NSA_HEADS:]], axis=1).astype(bf16)
        q, kc, vc, kvsw, gates, hq, hf, hi, hg, mg = _inproj(xcur, mod3, g_mix[l].reshape(1, d), w_pad, cos_t, sin_t, seq)
        wkt, wkb = _compress_weights(w_ck1[l])
        wvt, wvb = _compress_weights(w_cv1[l])
        kc_c, vc_c = _compress(
            kc.reshape(bsz, n_pieces, CMP_STRIDE * KV_WIDTH), vc.reshape(bsz, n_pieces, CMP_STRIDE * KV_WIDTH),
            wkt, wkb, wvt, wvb, cmp_pos_k[l].reshape(1, -1), cmp_pos_v[l].reshape(1, -1),
            w_ck1[l], w_cv1[l], w_ck2[l], w_cv2[l])
        o_nsa = _nsa(q, kvsw.reshape(bsz, seq, 4 * KV_WIDTH), kc_c, vc_c, gates, selmap, expand, seq)
        o_hgrn = _hgrn(hq, hf, hi, hg, hgrn_lb_logits, hgrn_out_norm[l].reshape(1, -1), seq, l)
        wq = w_peer_q[l].astype(bf16)
        sk = jnp.transpose(peer_sub_keys[l], (1, 0, 2, 3)).reshape(2 * PEER_HEADS, PEER_NKEYS, PEER_QDIM // 2).astype(bf16)
        x1, h2, scores = _merge(o_nsa, o_hgrn, mg, xcur, mod3, g_ffn[l].reshape(1, d), w_branch[l, 0].astype(bf16),
                                w_branch[l, 1].astype(bf16), w_out[l].astype(bf16), wq, sk, seq)
        idx01, gate01 = _route(scores, 2 * _TS)
        uv = jnp.concatenate([peer_u[l], peer_v[l]], axis=1).astype(bf16).reshape(-1, _ROW_TILES, LANES)
        xcur = _experts(scores, idx01, gate01, h2, x1, mod.reshape(bsz, 6, SUBLANES, LANES),
                        g_final.reshape(SUBLANES, LANES), uv, seq)
    return xcur.reshape(bsz, seq, d)
```
